```python
import jax
import jax.numpy as jnp
from jax import lax
import numpy as np

D_MODEL = 1024
BATCH = 2
SEQ = 8192
DEPTH = 1
DEC_BATCH = 128
DEC_SEQ = 8
PAST_LEN = 16384
PAGE_SIZE = 128

N_META = 16
RET_HEADS = 4
RET_DK = 64
RET_DV = 128
RET_CHUNK = 128
RET_THETA = 10000.0
MLA_HEADS = 8
MLA_NOPE = 64
MLA_ROPE = 32
MLA_QK = MLA_NOPE + MLA_ROPE
MLA_V = 64
Q_LORA = 384
KV_LORA = 256
MLA_THETA = 10000.0
MLA_SCALE = MLA_QK ** -0.5
Q_BLOCK = 128
MIX_WIDTH = RET_HEADS * RET_DV + MLA_HEADS * MLA_V
IN_SPLITS = (RET_HEADS * RET_DK, RET_HEADS * RET_DK, RET_HEADS * RET_DV, RET_HEADS * RET_DV, Q_LORA, KV_LORA, MLA_ROPE)
IN_WIDTH = sum(IN_SPLITS)
N_EXPERTS = 32
TOP_K = 4
D_FF = 1024
SWIGLU_LIMIT = 7.0
SWIGLU_ALPHA = 1.702
MOE_BLOCK = 128
EPS = 1e-6
NEG_BIG = -1e30
F32 = jnp.float32

kernel_name = 'hymba_retention_mla_moe_step'


def rms_norm(x, g):
    xf = x.astype(F32)
    y = xf * lax.rsqrt(jnp.mean(xf * xf, axis=-1, keepdims=True) + EPS)
    return (y * g.astype(F32)).astype(x.dtype)


def rope(x, pos, theta):
    half = x.shape[-1] // 2
    inv_freq = theta ** (-jnp.arange(half, dtype=F32) / half)
    ang = pos.astype(F32)[:, None] * inv_freq[None, :]
    cos = jnp.cos(ang)[:, None, :]
    sin = jnp.sin(ang)[:, None, :]
    xf = x.astype(F32)
    x1, x2 = xf[..., :half], xf[..., half:]
    return jnp.concatenate([x1 * cos - x2 * sin, x1 * sin + x2 * cos], axis=-1).astype(x.dtype)


def rope_tail(t, pos):
    return jnp.concatenate([t[..., :MLA_NOPE], rope(t[..., MLA_NOPE:], pos, MLA_THETA)], axis=-1)


def in_projection(xn, w_in):
    z = xn @ w_in
    offsets = np.cumsum(IN_SPLITS)[:-1].tolist()
    return jnp.split(z, offsets, axis=-1)


def retention_qkv(zq, zk, zv, pos):
    lead = zq.shape[:-1]
    q = rope(zq.reshape(*lead, RET_HEADS, RET_DK), pos, RET_THETA)
    k = rope(zk.reshape(*lead, RET_HEADS, RET_DK), pos, RET_THETA) * (RET_DK ** -0.5)
    v = zv.reshape(*lead, RET_HEADS, RET_DV)
    return q, k, v


def retention_chunk(q, k, v, s0, log_gamma):
    q, k, v = q.astype(F32), k.astype(F32), v.astype(F32)
    c = q.shape[1]
    idx = jnp.arange(c, dtype=F32)
    diff = idx[:, None] - idx[None, :]
    decay = jnp.where(diff >= 0, jnp.exp(jnp.maximum(diff, 0.0)[None] * log_gamma[:, None, None]), 0.0)
    scores = jnp.einsum('bihd,bjhd->bhij', q, k) * decay[None]
    inner = jnp.einsum('bhij,bjhe->bihe', scores, v)
    cross = jnp.einsum('bihd,bhde->bihe', q, s0) * jnp.exp((idx + 1.0)[:, None] * log_gamma[None, :])[None, :, :, None]
    k_w = k * jnp.exp((c - 1.0 - idx)[:, None] * log_gamma[None, :])[None, :, :, None]
    s_new = s0 * jnp.exp(c * log_gamma)[None, :, None, None] + jnp.einsum('bjhd,bjhe->bhde', k_w, v)
    return inner + cross, s_new


def retention_prompt(q, k, v, log_gamma):
    b, l = q.shape[:2]
    s0 = jnp.zeros((b, RET_HEADS, RET_DK, RET_DV), F32)
    o_meta, s = retention_chunk(q[:, :N_META], k[:, :N_META], v[:, :N_META], s0, log_gamma)
    n_chunks = (l - N_META) // RET_CHUNK

    def to_chunks(t):
        return t[:, N_META:].reshape(b, n_chunks, RET_CHUNK, *t.shape[2:]).swapaxes(0, 1)

    def step(state, qkv):
        o, state = retention_chunk(qkv[0], qkv[1], qkv[2], state, log_gamma)
        return state, o

    s, o_real = lax.scan(step, s, (to_chunks(q), to_chunks(k), to_chunks(v)))
    o_real = o_real.swapaxes(0, 1).reshape(b, n_chunks * RET_CHUNK, RET_HEADS, RET_DV)
    return jnp.concatenate([o_meta, o_real], axis=1), s


def retention_out(o, zg, gain):
    o = rms_norm(o, gain.reshape(RET_HEADS, RET_DV)).reshape(*zg.shape[:-1], RET_HEADS * RET_DV)
    return jax.nn.silu(zg) * o.astype(zg.dtype)


def mla_queries(zcq, pos, q_a_norm_g, w_q_b, qk_norm_q_g):
    q = rms_norm(zcq, q_a_norm_g) @ w_q_b
    q = q.reshape(*zcq.shape[:-1], MLA_HEADS, MLA_QK)
    return rope_tail(rms_norm(q, qk_norm_q_g), pos)


def mla_keys(ckv, kpe, pos, w_uk, qk_norm_k_g):
    k_nope = (ckv @ w_uk).reshape(*ckv.shape[:-1], MLA_HEADS, MLA_NOPE)
    k_pe = jnp.broadcast_to(kpe[..., None, :], (*kpe.shape[:-1], MLA_HEADS, MLA_ROPE)).astype(k_nope.dtype)
    k = jnp.concatenate([k_nope, k_pe], axis=-1)
    return rope_tail(rms_norm(k, qk_norm_k_g), pos)


def mla_attend_prompt(q, k, v):
    b, l = q.shape[:2]
    n_qb = -(-l // Q_BLOCK)
    lp = n_qb * Q_BLOCK
    qp = jnp.pad(q, ((0, 0), (0, lp - l), (0, 0), (0, 0)))
    qb = qp.reshape(b, n_qb, Q_BLOCK, MLA_HEADS, MLA_QK).swapaxes(0, 1)
    kpos = jnp.arange(l)

    def block(args):
        qblk, bi = args
        qpos = bi * Q_BLOCK + jnp.arange(Q_BLOCK)
        s = jnp.einsum('bqhd,bkhd->bhqk', qblk, k, preferred_element_type=F32) * MLA_SCALE
        s = jnp.where((kpos[None, :] <= qpos[:, None])[None, None], s, NEG_BIG)
        p = jax.nn.softmax(s, axis=-1).astype(v.dtype)
        return jnp.einsum('bhqk,bkhe->bqhe', p, v)

    o = lax.map(block, (qb, jnp.arange(n_qb)))
    return o.swapaxes(0, 1).reshape(b, lp, MLA_HEADS, MLA_V)[:, :l]


def mla_attend_sample(q, ckv_new, kpe_new, cache_latent_l, cache_krope_l, page_table, w_uk, w_uv, qk_norm_k_g):
    s_new = q.shape[1]
    past = page_table.shape[1] * cache_latent_l.shape[1]
    kpos = jnp.arange(past + s_new)
    qpos = past + jnp.arange(s_new)
    w_uv_h = w_uv.reshape(KV_LORA, MLA_HEADS, MLA_V)

    def one(args):
        qs, cn, pn, pt = args
        c_all = jnp.concatenate([cache_latent_l[pt].reshape(past, KV_LORA).astype(cn.dtype), cn], axis=0)
        p_all = jnp.concatenate([cache_krope_l[pt].reshape(past, MLA_ROPE).astype(pn.dtype), pn], axis=0)
        k = mla_keys(c_all, p_all, kpos, w_uk, qk_norm_k_g)
        s = jnp.einsum('qhd,khd->hqk', qs, k, preferred_element_type=F32) * MLA_SCALE
        s = jnp.where((kpos[None, :] <= qpos[:, None])[None], s, NEG_BIG)
        p = jax.nn.softmax(s, axis=-1).astype(c_all.dtype)
        ctx = jnp.einsum('hqk,kr->qhr', p, c_all)
        return jnp.einsum('qhr,rhe->qhe', ctx, w_uv_h)

    return lax.map(one, (q, ckv_new, kpe_new, page_table))


def mixer_inputs(xn, pos, prm):
    zq, zk, zv, zg, zcq, zckv, zkpe = in_projection(xn, prm['w_in'])
    rq, rk, rv = retention_qkv(zq, zk, zv, pos)
    mq = mla_queries(zcq, pos, prm['q_a_norm_g'], prm['w_q_b'], prm['qk_norm_q_g'])
    ckv = rms_norm(zckv, prm['kv_a_norm_g'])
    return rq, rk, rv, zg, mq, ckv, zkpe


def mixer_output(ret_o, zg, mla_o, prm):
    r = retention_out(ret_o, zg, prm['ret_out_g'])
    m = mla_o.reshape(*zg.shape[:-1], MLA_HEADS * MLA_V).astype(zg.dtype)
    return jnp.concatenate([r, m], axis=-1) @ prm['w_out']


def mixer_prompt(xn, pos, prm, log_gamma):
    rq, rk, rv, zg, mq, ckv, kpe = mixer_inputs(xn, pos, prm)
    ret_o, ret_state = retention_prompt(rq, rk, rv, log_gamma)
    k = mla_keys(ckv, kpe, pos, prm['w_uk'], prm['qk_norm_k_g'])
    v = (ckv @ prm['w_uv']).reshape(*ckv.shape[:-1], MLA_HEADS, MLA_V)
    mla_o = mla_attend_prompt(mq, k, v)
    return mixer_output(ret_o, zg, mla_o, prm), ckv, kpe, ret_state.astype(xn.dtype)


def mixer_sample(xn, pos, prm, log_gamma, cache_latent_l, cache_krope_l, state_l, page_table):
    rq, rk, rv, zg, mq, ckv, kpe = mixer_inputs(xn, pos, prm)
    ret_o, ret_state = retention_chunk(rq, rk, rv, state_l.astype(F32), log_gamma)
    mla_o = mla_attend_sample(mq, ckv, kpe, cache_latent_l, cache_krope_l, page_table,
                              prm['w_uk'], prm['w_uv'], prm['qk_norm_k_g'])
    return mixer_output(ret_o, zg, mla_o, prm), ckv, kpe, ret_state.astype(xn.dtype)


def moe_ffn(x, w_router, b_router, w_gate_up, b_gate_up, w_down, b_down):
    shp = x.shape
    xt = x.reshape(-1, shp[-1])
    t = xt.shape[0]
    logits = xt.astype(F32) @ w_router.astype(F32) + b_router.astype(F32)
    top_v, top_i = lax.top_k(logits, TOP_K)
    gates = jax.nn.softmax(top_v, axis=-1)
    a = t * TOP_K
    flat_e = top_i.reshape(a)
    flat_tok = jnp.repeat(jnp.arange(t, dtype=jnp.int32), TOP_K)
    flat_g = gates.reshape(a)
    order = jnp.argsort(flat_e)
    e_s, tok_s, g_s = flat_e[order], flat_tok[order], flat_g[order]
    counts = jnp.bincount(flat_e, length=N_EXPERTS)
    padded = (counts + MOE_BLOCK - 1) // MOE_BLOCK * MOE_BLOCK
    start = jnp.cumsum(counts) - counts
    pstart = jnp.cumsum(padded) - padded
    dest = pstart[e_s] + jnp.arange(a) - start[e_s]
    n_blocks = -(-a // MOE_BLOCK) + N_EXPERTS
    n_slots = n_blocks * MOE_BLOCK
    slot_tok = jnp.zeros((n_slots,), jnp.int32).at[dest].set(tok_s)
    slot_gate = jnp.zeros((n_slots,), F32).at[dest].set(g_s)
    block_e = jnp.minimum(jnp.searchsorted(jnp.cumsum(padded), jnp.arange(n_blocks) * MOE_BLOCK, side='right'),
                          N_EXPERTS - 1)

    def expert_block(args):
        tok, gate, e = args
        xb = xt[tok]
        hgu = xb @ w_gate_up[e] + b_gate_up[e]
        g = jnp.minimum(hgu[:, :D_FF], SWIGLU_LIMIT)
        u = jnp.clip(hgu[:, D_FF:], -SWIGLU_LIMIT, SWIGLU_LIMIT)
        h = (u + 1.0) * (g * jax.nn.sigmoid(SWIGLU_ALPHA * g))
        out = h @ w_down[e] + b_down[e]
        return out.astype(F32) * gate[:, None]

    outs = lax.map(expert_block, (slot_tok.reshape(n_blocks, MOE_BLOCK),
                                  slot_gate.reshape(n_blocks, MOE_BLOCK), block_e))
    y = jnp.zeros((t, shp[-1]), F32).at[slot_tok].add(outs.reshape(n_slots, shp[-1]))
    return y.astype(x.dtype).reshape(shp)


def setup_inputs(seed: int = 0) -> dict:
    key = jax.random.key(seed)
    ks = jax.random.split(key, 32)
    n_pages = PAST_LEN // PAGE_SIZE
    n_used = DEC_BATCH * n_pages
    n_phys = n_used + n_used // 4

    def nrm(k, shape, scale):
        return jax.random.normal(k, shape, F32) * scale

    def gain(k, shape):
        return 1.0 + 0.02 * jax.random.normal(k, shape, F32)

    page_table = jax.random.permutation(ks[5], n_phys)[:n_used].reshape(DEC_BATCH, n_pages).astype(jnp.int32)
    return {
        'x_prompt': nrm(ks[0], (BATCH, SEQ, D_MODEL), 1.0),
        'x_sample': nrm(ks[1], (DEC_BATCH, DEC_SEQ, D_MODEL), 1.0),
        'cache_latent': nrm(ks[2], (DEPTH, n_phys, PAGE_SIZE, KV_LORA), 1.0),
        'cache_krope': nrm(ks[3], (DEPTH, n_phys, PAGE_SIZE, MLA_ROPE), 1.0),
        'state_retention': nrm(ks[4], (DEPTH, DEC_BATCH, RET_HEADS, RET_DK, RET_DV), 1.0),
        'page_table': page_table,
        'meta_tokens': nrm(ks[6], (N_META, D_MODEL), 1.0),
        'norm_attn_g': gain(ks[7], (DEPTH, D_MODEL)),
        'w_in': nrm(ks[8], (DEPTH, D_MODEL, IN_WIDTH), D_MODEL ** -0.5),
        'ret_out_g': gain(ks[9], (DEPTH, RET_HEADS * RET_DV)),
        'q_a_norm_g': gain(ks[10], (DEPTH, Q_LORA)),
        'w_q_b': nrm(ks[11], (DEPTH, Q_LORA, MLA_HEADS * MLA_QK), Q_LORA ** -0.5),
        'kv_a_norm_g': gain(ks[12], (DEPTH, KV_LORA)),
        'w_uk': nrm(ks[13], (DEPTH, KV_LORA, MLA_HEADS * MLA_NOPE), KV_LORA ** -0.5),
        'w_uv': nrm(ks[14], (DEPTH, KV_LORA, MLA_HEADS * MLA_V), KV_LORA ** -0.5),
        'qk_norm_q_g': gain(ks[15], (DEPTH, MLA_QK)),
        'qk_norm_k_g': gain(ks[16], (DEPTH, MLA_QK)),
        'w_out': nrm(ks[17], (DEPTH, MIX_WIDTH, D_MODEL), MIX_WIDTH ** -0.5),
        'norm_ffn_g': gain(ks[18], (DEPTH, D_MODEL)),
        'w_router': nrm(ks[19], (DEPTH, D_MODEL, N_EXPERTS), D_MODEL ** -0.5),
        'b_router': nrm(ks[20], (DEPTH, N_EXPERTS), 0.01),
        'w_gate_up': nrm(ks[21], (DEPTH, N_EXPERTS, D_MODEL, 2 * D_FF), D_MODEL ** -0.5),
        'b_gate_up': nrm(ks[22], (DEPTH, N_EXPERTS, 2 * D_FF), 0.01),
        'w_down': nrm(ks[23], (DEPTH, N_EXPERTS, D_FF, D_MODEL), D_FF ** -0.5),
        'b_down': nrm(ks[24], (DEPTH, N_EXPERTS, D_MODEL), 0.01),
    }


def reference(x_prompt, x_sample, cache_latent, cache_krope, state_retention, page_table, meta_tokens,
              norm_attn_g, w_in, ret_out_g, q_a_norm_g, w_q_b, kv_a_norm_g, w_uk, w_uv,
              qk_norm_q_g, qk_norm_k_g, w_out, norm_ffn_g, w_router, b_router,
              w_gate_up, b_gate_up, w_down, b_down):
    b = x_prompt.shape[0]
    log_gamma = jnp.log1p(-jnp.exp2(-5.0 - jnp.arange(RET_HEADS, dtype=F32)))
    meta = jnp.broadcast_to(meta_tokens.astype(x_prompt.dtype)[None], (b, N_META, D_MODEL))
    hp = jnp.concatenate([meta, x_prompt], axis=1)
    hs = x_sample
    pos_p = jnp.arange(hp.shape[1])
    past = page_table.shape[1] * cache_latent.shape[2]
    pos_s = past + jnp.arange(hs.shape[1])
    lat_p, kpe_p, ret_p, lat_s, kpe_s, ret_s = [], [], [], [], [], []
    for l in range(DEPTH):
        prm = {'w_in': w_in[l], 'ret_out_g': ret_out_g[l], 'q_a_norm_g': q_a_norm_g[l], 'w_q_b': w_q_b[l],
               'kv_a_norm_g': kv_a_norm_g[l], 'w_uk': w_uk[l], 'w_uv': w_uv[l],
               'qk_norm_q_g': qk_norm_q_g[l], 'qk_norm_k_g': qk_norm_k_g[l], 'w_out': w_out[l]}
        mix_p, c_p, k_p, st_p = mixer_prompt(rms_norm(hp, norm_attn_g[l]), pos_p, prm, log_gamma)
        mix_s, c_s, k_s, st_s = mixer_sample(rms_norm(hs, norm_attn_g[l]), pos_s, prm, log_gamma,
                                             cache_latent[l], cache_krope[l], state_retention[l], page_table)
        hp = hp + mix_p
        hs = hs + mix_s
        hp = hp + moe_ffn(rms_norm(hp, norm_ffn_g[l]), w_router[l], b_router[l], w_gate_up[l],
                          b_gate_up[l], w_down[l], b_down[l])
        hs = hs + moe_ffn(rms_norm(hs, norm_ffn_g[l]), w_router[l], b_router[l], w_gate_up[l],
                          b_gate_up[l], w_down[l], b_down[l])
        lat_p.append(c_p)
        kpe_p.append(k_p)
        ret_p.append(st_p)
        lat_s.append(c_s)
        kpe_s.append(k_s)
        ret_s.append(st_s)
    y_prompt = hp[:, N_META:]
    return (y_prompt, hs, jnp.stack(lat_p), jnp.stack(kpe_p), jnp.stack(ret_p),
            jnp.stack(lat_s), jnp.stack(kpe_s), jnp.stack(ret_s))
```

```python
import functools

import jax
import jax.numpy as jnp
from jax import lax
from jax.experimental import pallas as pl
from jax.experimental.pallas import tpu as pltpu

F32 = jnp.float32
BF16 = jnp.bfloat16

D_MODEL = 1024
N_META = 16
RET_HEADS = 4
RET_DK = 64
RET_DV = 128
RET_CHUNK = 128
RET_THETA = 10000.0
MLA_HEADS = 8
MLA_NOPE = 64
MLA_ROPE = 32
MLA_QK = MLA_NOPE + MLA_ROPE
MLA_V = 64
Q_LORA = 384
KV_LORA = 256
MLA_THETA = 10000.0
MLA_SCALE = MLA_QK ** -0.5
N_EXPERTS = 32
TOP_K = 4
D_FF = 1024
SWIGLU_LIMIT = 7.0
SWIGLU_ALPHA = 1.702
EPS = 1e-6
NEG_BIG = -1e30

LANES = 128
HEAD_PAD = LANES
VMEM_LIMIT = 56 * 1024 * 1024
ROW_BLOCK = 256
ATTN_BLOCK = 1024
PAGES_PER_STEP = 8
EXPERT_BLOCK = 256

_OFF_RQ = 0
_OFF_RK = _OFF_RQ + RET_HEADS * LANES
_OFF_RV = _OFF_RK + RET_HEADS * LANES
_OFF_ZG = _OFF_RV + RET_HEADS * RET_DV
_OFF_CQ = _OFF_ZG + RET_HEADS * RET_DV
_OFF_CKV = _OFF_CQ + Q_LORA
_OFF_KPE = _OFF_CKV + KV_LORA
IN_PAD = _OFF_KPE + LANES


def _params(sem):
    return pltpu.CompilerParams(dimension_semantics=sem, vmem_limit_bytes=VMEM_LIMIT)


def _rms(x, g):
    return x * lax.rsqrt(jnp.mean(x * x, axis=-1, keepdims=True) + EPS) * g


def _dot(a, b):
    return jnp.dot(a, b, preferred_element_type=F32)


def _dot_nt(a, b):
    return lax.dot_general(a, b, (((1,), (1,)), ((), ())), preferred_element_type=F32)


def _dot_tn(a, b):
    return lax.dot_general(a, b, (((0,), (0,)), ((), ())), preferred_element_type=F32)


def _front_kernel(x_ref, gattn_ref, win_ref, cr_ref, sr_ref, cm_ref, sma_ref, smb_ref,
                  gqa_ref, wqb_ref, gq_ref, gkv_ref, wuk_ref, gk_ref, wuv_ref,
                  rq_ref, rk_ref, rv_ref, zg_ref, mq_ref, ckv_ref, kpe_ref, k_ref, v_ref):
    tm = x_ref.shape[0]
    xn = _rms(x_ref[...], gattn_ref[...])
    z = _dot(xn.astype(BF16), win_ref[...])

    lane = lax.broadcasted_iota(jnp.int32, (tm, LANES), 1)
    upper = (lane & (RET_DK // 2)) != 0
    cr, sr = cr_ref[...], sr_ref[...]

    def rope_ret(t):
        partner = jnp.where(upper, pltpu.roll(t, RET_DK // 2, 1), pltpu.roll(t, LANES - RET_DK // 2, 1))
        return t * cr + partner * sr

    for h in range(RET_HEADS):
        sl = slice(h * LANES, (h + 1) * LANES)
        rq_ref[:, sl] = rope_ret(z[:, _OFF_RQ + h * LANES:_OFF_RQ + (h + 1) * LANES])
        rk_ref[:, sl] = rope_ret(z[:, _OFF_RK + h * LANES:_OFF_RK + (h + 1) * LANES]) * (RET_DK ** -0.5)
    rv_ref[...] = z[:, _OFF_RV:_OFF_ZG].astype(BF16)
    zg_ref[...] = z[:, _OFF_ZG:_OFF_CQ]

    cm, sma, smb = cm_ref[...], sma_ref[...], smb_ref[...]

    def rope_mla(t):
        half = MLA_ROPE // 2
        return t * cm + pltpu.roll(t, half, 1) * sma + pltpu.roll(t, LANES - half, 1) * smb

    def head_norm(t, g):
        ms = jnp.sum(t * t, axis=-1, keepdims=True) * (1.0 / MLA_QK)
        return t * lax.rsqrt(ms + EPS) * g

    cq = _rms(z[:, _OFF_CQ:_OFF_CKV], gqa_ref[...])
    q = _dot(cq.astype(BF16), wqb_ref[...])
    gq = gq_ref[...]
    for h in range(MLA_HEADS):
        sl = slice(h * HEAD_PAD, (h + 1) * HEAD_PAD)
        mq_ref[:, sl] = (rope_mla(head_norm(q[:, sl], gq)) * MLA_SCALE).astype(BF16)

    ckv = _rms(z[:, _OFF_CKV:_OFF_KPE], gkv_ref[...])
    ckv_ref[...] = ckv
    kpe_slab = z[:, _OFF_KPE:IN_PAD]
    kpe_ref[...] = kpe_slab[:, :MLA_ROPE]
    ckv_b = ckv.astype(BF16)
    kn = _dot(ckv_b, wuk_ref[...])
    gk = gk_ref[...]
    for h in range(MLA_HEADS):
        sl = slice(h * HEAD_PAD, (h + 1) * HEAD_PAD)
        k_ref[:, sl] = rope_mla(head_norm(kn[:, sl] + kpe_slab, gk)).astype(BF16)
    v_ref[...] = _dot(ckv_b, wuv_ref[...]).astype(BF16)


def _front(x, tabs, tab_index, w, tm):
    rows = x.shape[0]
    grid = (rows // tm,)
    row = lambda i: (i, 0)
    const = lambda i: (0, 0)
    tab = lambda i: (tab_index(i), 0)

    def full(a):
        return pl.BlockSpec(a.shape, const)

    in_specs = [pl.BlockSpec((tm, D_MODEL), row), full(w['gattn']), full(w['win'])]
    in_specs += [pl.BlockSpec((tm, LANES), tab)] * 5
    in_specs += [full(w[n]) for n in ('gqa', 'wqb', 'gq', 'gkv', 'wuk', 'gk', 'wuv')]
    widths = [(RET_HEADS * LANES, F32), (RET_HEADS * LANES, F32), (RET_HEADS * RET_DV, BF16),
              (RET_HEADS * RET_DV, F32), (MLA_HEADS * HEAD_PAD, BF16), (KV_LORA, F32), (MLA_ROPE, F32),
              (MLA_HEADS * HEAD_PAD, BF16), (MLA_HEADS * MLA_V, BF16)]
    out_shape = [jax.ShapeDtypeStruct((rows, n), dt) for n, dt in widths]
    out_specs = [pl.BlockSpec((tm, n), row) for n, _ in widths]
    return pl.pallas_call(
        _front_kernel, grid=grid, in_specs=in_specs, out_specs=out_specs, out_shape=out_shape,
        compiler_params=_params(("arbitrary",)), name="front",
    )(x, w['gattn'], w['win'], *tabs, w['gqa'], w['wqb'], w['gq'], w['gkv'], w['wuk'], w['gk'], w['wuv'])


def _ret_gate(o, zg, g):
    on = o * lax.rsqrt(jnp.mean(o * o, axis=-1, keepdims=True) + EPS) * g
    return (zg * jax.nn.sigmoid(zg)) * on


def _ret_prompt_kernel(q_ref, k_ref, v_ref, zg_ref, mk_ref, mv_ref, mcol_ref, dec_ref, rowd_ref, cold_ref,
                       cpow_ref, g_ref, r_ref, s_ref, s_scr):
    c = pl.program_id(1)

    @pl.when(c == 0)
    def _():
        for h in range(RET_HEADS):
            sl = slice(h * LANES, (h + 1) * LANES)
            kw = mk_ref[:, sl] * mcol_ref[:, h:h + 1]
            s_scr[h] = _dot_tn(kw.astype(BF16), mv_ref[:, sl])

    for h in range(RET_HEADS):
        sl = slice(h * LANES, (h + 1) * LANES)
        q = q_ref[:, sl]
        k = k_ref[:, sl]
        v = v_ref[:, sl]
        qb = q.astype(BF16)
        s0 = s_scr[h]
        scores = _dot_nt(qb, k.astype(BF16)) * dec_ref[h]
        inner = _dot(scores.astype(BF16), v)
        cross = _dot(qb, s0.astype(BF16)) * rowd_ref[:, h:h + 1]
        kw = k * cold_ref[:, h:h + 1]
        s_new = s0 * cpow_ref[:, h:h + 1] + _dot_tn(kw.astype(BF16), v)
        s_scr[h] = s_new
        r_ref[:, sl] = _ret_gate(inner + cross, zg_ref[:, sl], g_ref[:, sl]).astype(BF16)

    @pl.when(c == pl.num_programs(1) - 1)
    def _():
        for h in range(RET_HEADS):
            s_ref[0, h] = s_scr[h, :RET_DK, :]


def _ret_prompt(rq, rk, rv, zg, mk, mv, tabs, g, batch, n_chunks):
    cs = RET_CHUNK
    row = lambda b, c: (b * n_chunks + c, 0)
    const2 = lambda b, c: (0, 0)
    w4 = RET_HEADS * LANES
    in_specs = [pl.BlockSpec((cs, w4), row), pl.BlockSpec((cs, w4), row), pl.BlockSpec((cs, w4), row),
                pl.BlockSpec((cs, w4), row),
                pl.BlockSpec((cs, w4), const2), pl.BlockSpec((cs, w4), const2),
                pl.BlockSpec((cs, RET_HEADS), const2),
                pl.BlockSpec((RET_HEADS, cs, cs), lambda b, c: (0, 0, 0)),
                pl.BlockSpec((cs, RET_HEADS), const2), pl.BlockSpec((cs, RET_HEADS), const2),
                pl.BlockSpec((LANES, RET_HEADS), const2), pl.BlockSpec((1, w4), const2)]
    out_shape = [jax.ShapeDtypeStruct((batch * n_chunks * cs, w4), BF16),
                 jax.ShapeDtypeStruct((batch, RET_HEADS, RET_DK, RET_DV), F32)]
    out_specs = [pl.BlockSpec((cs, w4), row),
                 pl.BlockSpec((1, RET_HEADS, RET_DK, RET_DV), lambda b, c: (b, 0, 0, 0))]
    return pl.pallas_call(
        _ret_prompt_kernel, grid=(batch, n_chunks), in_specs=in_specs, out_specs=out_specs, out_shape=out_shape,
        scratch_shapes=[pltpu.VMEM((RET_HEADS, LANES, RET_DV), F32)],
        compiler_params=_params(("arbitrary", "arbitrary")), name="ret_prompt",
    )(rq, rk, rv, zg, mk, mv, tabs['mcol'], tabs['dec'], tabs['rowd'], tabs['cold'], tabs['cpow'], g)


def _ret_sample_kernel(q_ref, k_ref, v_ref, zg_ref, s0_ref, dec_ref, rowd_ref, cold_ref, cpow_ref, g_ref,
                       r_ref, s_ref, *, n_seq, n_tok):
    rows = n_seq * n_tok
    ri = lax.broadcasted_iota(jnp.int32, (rows, 1), 0)
    for h in range(RET_HEADS):
        sl = slice(h * LANES, (h + 1) * LANES)
        q = q_ref[:, sl]
        k = k_ref[:, sl]
        v = v_ref[:, sl]
        qb = q.astype(BF16)
        scores = _dot_nt(qb, k.astype(BF16)) * dec_ref[h]
        inner = _dot(scores.astype(BF16), v)
        kw = k * cold_ref[:, h:h + 1]
        cross = jnp.zeros((rows, RET_DV), F32)
        for s in range(n_seq):
            s0 = s0_ref[s, h]
            mine = (ri >= s * n_tok) & (ri < (s + 1) * n_tok)
            cross = cross + jnp.where(mine, _dot(qb[:, :RET_DK], s0.astype(BF16)), 0.0)
            upd = _dot_tn(jnp.where(mine, kw, 0.0).astype(BF16), v)
            s_ref[s, h] = s0 * cpow_ref[:RET_DK, h:h + 1] + upd[:RET_DK]
        cross = cross * rowd_ref[:, h:h + 1]
        r_ref[:, sl] = _ret_gate(inner + cross, zg_ref[:, sl], g_ref[:, sl]).astype(BF16)


def _ret_sample(rq, rk, rv, zg, state, tabs, g, n_tok):
    n_seq_total = state.shape[0]
    n_seq = LANES // n_tok
    rows = n_seq * n_tok
    w4 = RET_HEADS * LANES
    row = lambda i: (i, 0)
    const = lambda i: (0, 0)
    in_specs = [pl.BlockSpec((rows, w4), row)] * 4
    in_specs += [pl.BlockSpec((n_seq, RET_HEADS, RET_DK, RET_DV), lambda i: (i, 0, 0, 0)),
                 pl.BlockSpec((RET_HEADS, rows, rows), lambda i: (0, 0, 0)),
                 pl.BlockSpec((rows, RET_HEADS), const), pl.BlockSpec((rows, RET_HEADS), const),
                 pl.BlockSpec((LANES, RET_HEADS), const), pl.BlockSpec((1, w4), const)]
    out_shape = [jax.ShapeDtypeStruct((n_seq_total * n_tok, w4), BF16),
                 jax.ShapeDtypeStruct(state.shape, F32)]
    out_specs = [pl.BlockSpec((rows, w4), row),
                 pl.BlockSpec((n_seq, RET_HEADS, RET_DK, RET_DV), lambda i: (i, 0, 0, 0))]
    return pl.pallas_call(
        functools.partial(_ret_sample_kernel, n_seq=n_seq, n_tok=n_tok),
        grid=(n_seq_total // n_seq,), in_specs=in_specs, out_specs=out_specs, out_shape=out_shape,
        compiler_params=_params(("arbitrary",)), name="ret_sample",
    )(rq, rk, rv, zg, state, tabs['dec'], tabs['rowd'], tabs['cold'], tabs['cpow'], g)


def _attn_prompt_kernel(q_ref, k_ref, v_ref, km_ref, vm_ref, o_ref, m_scr, l_scr, acc_scr):
    qi = pl.program_id(2)
    ki = pl.program_id(3)
    tm, tk = q_ref.shape[0], k_ref.shape[0]

    @pl.when(ki == 0)
    def _():
        lane = lax.broadcasted_iota(jnp.int32, (tm, km_ref.shape[0]), 1)
        for hh in range(2):
            sl = slice(hh * HEAD_PAD, (hh + 1) * HEAD_PAD)
            s = jnp.where(lane < N_META, _dot_nt(q_ref[:, sl], km_ref[:, sl]), NEG_BIG)
            m = jnp.max(s, axis=-1, keepdims=True)
            p = jnp.exp(s - m)
            m_scr[hh] = m
            l_scr[hh] = jnp.sum(p, axis=-1, keepdims=True)
            acc_scr[hh] = _dot(p.astype(BF16), vm_ref[...])

    def block(masked):
        if masked:
            rowi = lax.broadcasted_iota(jnp.int32, (tm, tk), 0)
            coli = lax.broadcasted_iota(jnp.int32, (tm, tk), 1)
            keep = coli <= rowi
        for hh in range(2):
            sl = slice(hh * HEAD_PAD, (hh + 1) * HEAD_PAD)
            s = _dot_nt(q_ref[:, sl], k_ref[:, sl])
            if masked:
                s = jnp.where(keep, s, NEG_BIG)
            m_old = m_scr[hh]
            m_new = jnp.maximum(m_old, jnp.max(s, axis=-1, keepdims=True))
            alpha = jnp.exp(m_old - m_new)
            p = jnp.exp(s - m_new)
            m_scr[hh] = m_new
            l_scr[hh] = alpha * l_scr[hh] + jnp.sum(p, axis=-1, keepdims=True)
            acc_scr[hh] = alpha * acc_scr[hh] + _dot(p.astype(BF16), v_ref[...])

    @pl.when(ki < qi)
    def _():
        block(False)

    @pl.when(ki == qi)
    def _():
        block(True)
        lane = lax.broadcasted_iota(jnp.int32, (tm, 2 * MLA_V), 1)
        o0 = acc_scr[0] / l_scr[0]
        o1 = acc_scr[1] / l_scr[1]
        o_ref[...] = jnp.where(lane < MLA_V, o0, o1).astype(BF16)


def _attn_prompt(mq, k, v, km, vm, batch, seq, blk):
    nb = seq // blk
    pairs = MLA_HEADS // 2
    grid = (batch, pairs, nb, nb)
    in_specs = [pl.BlockSpec((blk, 2 * HEAD_PAD), lambda b, h, qi, ki: (b * nb + qi, h)),
                pl.BlockSpec((blk, 2 * HEAD_PAD), lambda b, h, qi, ki: (b * nb + jnp.minimum(ki, qi), h)),
                pl.BlockSpec((blk, 2 * MLA_V), lambda b, h, qi, ki: (b * nb + jnp.minimum(ki, qi), h)),
                pl.BlockSpec((km.shape[0], 2 * HEAD_PAD), lambda b, h, qi, ki: (0, h)),
                pl.BlockSpec((vm.shape[0], 2 * MLA_V), lambda b, h, qi, ki: (0, h))]
    out_specs = pl.BlockSpec((blk, 2 * MLA_V), lambda b, h, qi, ki: (b * nb + qi, h))
    return pl.pallas_call(
        _attn_prompt_kernel, grid=grid, in_specs=in_specs, out_specs=out_specs,
        out_shape=jax.ShapeDtypeStruct((batch * seq, MLA_HEADS * MLA_V), BF16),
        scratch_shapes=[pltpu.VMEM((2, blk, 1), F32), pltpu.VMEM((2, blk, 1), F32),
                        pltpu.VMEM((2, blk, 2 * MLA_V), F32)],
        compiler_params=_params(("arbitrary",) * 4), name="attn_prompt",
    )(mq, k, v, km, vm)


def _absorb_kernel(mq_ref, wabs_ref, qt_ref, qf_ref):
    q = mq_ref[...]
    qf_ref[...] = q.astype(F32)
    for h in range(MLA_HEADS):
        qt_ref[:, h * KV_LORA:(h + 1) * KV_LORA] = _dot(q[:, h * HEAD_PAD:(h + 1) * HEAD_PAD], wabs_ref[h])


def _absorb(mq, wabs, row0_blocks, rows, tm):
    return pl.pallas_call(
        _absorb_kernel, grid=(rows // tm,),
        in_specs=[pl.BlockSpec((tm, MLA_HEADS * HEAD_PAD), lambda i: (row0_blocks + i, 0)),
                  pl.BlockSpec(wabs.shape, lambda i: (0, 0, 0))],
        out_specs=[pl.BlockSpec((tm, MLA_HEADS * KV_LORA), lambda i: (i, 0)),
                   pl.BlockSpec((tm, MLA_HEADS * HEAD_PAD), lambda i: (i, 0))],
        out_shape=[jax.ShapeDtypeStruct((rows, MLA_HEADS * KV_LORA), F32),
                   jax.ShapeDtypeStruct((rows, MLA_HEADS * HEAD_PAD), F32)],
        compiler_params=_params(("arbitrary",)), name="absorb",
    )(mq, wabs)


def _attn_sample_kernel(pt_ref, *refs, n_pages, n_tok, page):
    lat_refs = refs[:n_pages]
    kpe_refs = refs[n_pages:2 * n_pages]
    (cos_ref, sin_ref, qt_ref, qf_ref, cn_ref, kn_ref, cosn_ref, sinn_ref, lw_ref, wuv_ref, gpe_ref,
     o_ref, l_scr, qpe_scr, m_scr, d_scr, acc_scr, kpad_scr, cpad_scr) = refs[2 * n_pages:]
    del pt_ref
    j = pl.program_id(1)
    nq = MLA_HEADS * n_tok
    n_up = MLA_HEADS * MLA_NOPE

    @pl.when(j == 0)
    def _():
        l_scr[:n_up, :] = lw_ref[...]
        qt = qt_ref[...]
        qf = qf_ref[...]
        l_scr[n_up:, :] = jnp.concatenate(
            [qt[:, h * KV_LORA:(h + 1) * KV_LORA] for h in range(MLA_HEADS)], axis=0).astype(BF16)
        lane = lax.broadcasted_iota(jnp.int32, (nq, HEAD_PAD), 1)
        qpe = jnp.concatenate([qf[:, h * HEAD_PAD:(h + 1) * HEAD_PAD] for h in range(MLA_HEADS)], axis=0)
        qpe_scr[...] = jnp.where(lane < MLA_ROPE, qpe, 0.0).astype(BF16)
        m_scr[...] = jnp.full(m_scr.shape, NEG_BIG, F32)
        d_scr[...] = jnp.zeros(d_scr.shape, F32)
        acc_scr[...] = jnp.zeros(acc_scr.shape, F32)
        kpad_scr[...] = jnp.zeros(kpad_scr.shape, F32)

    def process(cb, kpe_t, cos_t, sin_t, keep):
        tk = cb.shape[0]
        big = _dot_nt(l_scr[...], cb)
        k_t = big[:n_up]
        ss = jnp.sum((k_t * k_t).reshape(MLA_HEADS, MLA_NOPE, tk), axis=1)
        ss = ss + jnp.sum(kpe_t * kpe_t, axis=0, keepdims=True)
        rs = lax.rsqrt(ss * (1.0 / MLA_QK) + EPS)
        kg = kpe_t * gpe_ref[...]
        half = MLA_ROPE // 2
        x1, x2 = kg[:half], kg[half:]
        rot = jnp.concatenate([x1 * cos_t - x2 * sin_t, x1 * sin_t + x2 * cos_t,
                               jnp.zeros((HEAD_PAD - MLA_ROPE, tk), F32)], axis=0)
        pe = _dot(qpe_scr[...], rot.astype(BF16))
        rs_rows = jnp.concatenate([jnp.broadcast_to(rs[h:h + 1, :], (n_tok, tk)) for h in range(MLA_HEADS)], axis=0)
        s = (big[n_up:] + pe) * rs_rows
        if keep is not None:
            s = jnp.where(keep, s, NEG_BIG)
        m_old = m_scr[...]
        m_new = jnp.maximum(m_old, jnp.max(s, axis=-1, keepdims=True))
        alpha = jnp.exp(m_old - m_new)
        p = jnp.exp(s - m_new)
        m_scr[...] = m_new
        d_scr[...] = alpha * d_scr[...] + jnp.sum(p, axis=-1, keepdims=True)
        acc_scr[...] = alpha * acc_scr[...] + _dot(p.astype(BF16), cb)

    def transposed_kpe(n_blocks):
        parts = [kpad_scr[b * page:(b + 1) * page, :].T[:MLA_ROPE] for b in range(n_blocks)]
        return parts[0] if n_blocks == 1 else jnp.concatenate(parts, axis=1)

    cb = jnp.concatenate([r[...] for r in lat_refs], axis=0).astype(BF16) if n_pages > 1 else lat_refs[0][...].astype(BF16)
    for b in range(n_pages):
        kpad_scr[b * page:(b + 1) * page, :MLA_ROPE] = kpe_refs[b][...]
    process(cb, transposed_kpe(n_pages), cos_ref[...], sin_ref[...], None)

    @pl.when(j == pl.num_programs(1) - 1)
    def _():
        cpad_scr[...] = jnp.zeros(cpad_scr.shape, F32)
        cpad_scr[:n_tok, :] = cn_ref[...]
        kpad_scr[:page, :] = jnp.zeros((page, LANES), F32)
        kpad_scr[:n_tok, :MLA_ROPE] = kn_ref[...]
        rowi = lax.broadcasted_iota(jnp.int32, (nq, page), 0)
        coli = lax.broadcasted_iota(jnp.int32, (nq, page), 1)
        keep = coli <= (rowi & (n_tok - 1))
        process(cpad_scr[...].astype(BF16), transposed_kpe(1), cosn_ref[...], sinn_ref[...], keep)
        ctx = (acc_scr[...] / d_scr[...]).astype(BF16)
        out = jnp.zeros(o_ref.shape, F32)
        for h in range(MLA_HEADS):
            out = out + _dot(ctx, wuv_ref[h])[h * n_tok:(h + 1) * n_tok]
        o_ref[...] = out


def _attn_sample(page_table, cache_latent, cache_krope, tabs, qt, qf, ckv, kpe, w, n_tok, n_pages_step):
    n_seq, n_pages = page_table.shape
    page = cache_latent.shape[2]
    n_steps = n_pages // n_pages_step
    tk = n_pages_step * page
    nq = MLA_HEADS * n_tok

    def page_spec(width, p):
        return pl.BlockSpec((None, None, page, width),
                            lambda s, j, pt: (0, pt[s, j * n_pages_step + p], 0, 0))

    seq_row = lambda s, j, pt: (s, 0)
    const2 = lambda s, j, pt: (0, 0)
    half = MLA_ROPE // 2
    in_specs = [page_spec(KV_LORA, p) for p in range(n_pages_step)]
    in_specs += [page_spec(MLA_ROPE, p) for p in range(n_pages_step)]
    in_specs += [pl.BlockSpec((half, tk), lambda s, j, pt: (0, j)), pl.BlockSpec((half, tk), lambda s, j, pt: (0, j)),
                 pl.BlockSpec((n_tok, MLA_HEADS * KV_LORA), seq_row),
                 pl.BlockSpec((n_tok, MLA_HEADS * HEAD_PAD), seq_row),
                 pl.BlockSpec((n_tok, KV_LORA), seq_row), pl.BlockSpec((n_tok, MLA_ROPE), seq_row),
                 pl.BlockSpec((half, page), const2), pl.BlockSpec((half, page), const2),
                 pl.BlockSpec(w['uk_t'].shape, const2),
                 pl.BlockSpec(w['uv_blk'].shape, lambda s, j, pt: (0, 0, 0)),
                 pl.BlockSpec((MLA_ROPE, 1), const2)]
    grid_spec = pltpu.PrefetchScalarGridSpec(
        num_scalar_prefetch=1, grid=(n_seq, n_steps), in_specs=in_specs,
        out_specs=pl.BlockSpec((n_tok, MLA_HEADS * MLA_V), seq_row),
        scratch_shapes=[pltpu.VMEM((MLA_HEADS * MLA_NOPE + nq, KV_LORA), BF16),
                        pltpu.VMEM((nq, HEAD_PAD), BF16),
                        pltpu.VMEM((nq, 1), F32), pltpu.VMEM((nq, 1), F32), pltpu.VMEM((nq, KV_LORA), F32),
                        pltpu.VMEM((tk, LANES), F32), pltpu.VMEM((page, KV_LORA), F32)])
    return pl.pallas_call(
        functools.partial(_attn_sample_kernel, n_pages=n_pages_step, n_tok=n_tok, page=page),
        grid_spec=grid_spec, out_shape=jax.ShapeDtypeStruct((n_seq * n_tok, MLA_HEADS * MLA_V), F32),
        compiler_params=_params(("arbitrary", "arbitrary")), name="attn_sample",
    )(page_table, *([cache_latent] * n_pages_step), *([cache_krope] * n_pages_step),
      tabs['cos_t'], tabs['sin_t'], qt, qf, ckv, kpe, tabs['cosn_t'], tabs['sinn_t'],
      w['uk_t'], w['uv_blk'], w['gpe'])


def _post_kernel(r_ref, m_ref, x_ref, wo_ref, g_ref, wrh_ref, wrl_ref, br_ref, h_ref, xn_ref, lg_ref):
    half = wo_ref.shape[0] // 2
    mix = _dot(r_ref[...], wo_ref[:half, :]) + _dot(m_ref[...], wo_ref[half:, :])
    h = x_ref[...] + mix
    h_ref[...] = h
    xn = _rms(h, g_ref[...])
    hi = xn.astype(BF16)
    xn_ref[...] = hi
    lo = (xn - hi.astype(F32)).astype(BF16)
    wrh = wrh_ref[...]
    lg_ref[...] = _dot(hi, wrh) + _dot(hi, wrl_ref[...]) + _dot(lo, wrh) + br_ref[...]


def _post(r, m, x, w, tm):
    rows = x.shape[0]
    row = lambda i: (i, 0)
    const = lambda i: (0, 0)
    in_specs = [pl.BlockSpec((tm, r.shape[1]), row), pl.BlockSpec((tm, m.shape[1]), row),
                pl.BlockSpec((tm, D_MODEL), row)]
    in_specs += [pl.BlockSpec(w[n].shape, const) for n in ('wo', 'gffn', 'wr_hi', 'wr_lo', 'br')]
    out_shape = [jax.ShapeDtypeStruct((rows, D_MODEL), F32), jax.ShapeDtypeStruct((rows, D_MODEL), BF16),
                 jax.ShapeDtypeStruct((rows, LANES), F32)]
    out_specs = [pl.BlockSpec((tm, D_MODEL), row), pl.BlockSpec((tm, D_MODEL), row), pl.BlockSpec((tm, LANES), row)]
    return pl.pallas_call(
        _post_kernel, grid=(rows // tm,), in_specs=in_specs, out_specs=out_specs, out_shape=out_shape,
        compiler_params=_params(("arbitrary",)), name="post",
    )(r, m, x, w['wo'], w['gffn'], w['wr_hi'], w['wr_lo'], w['br'])


def _moe_kernel(be_ref, nu_ref, xs_ref, gate_ref, wgu_ref, bgu_ref, wd_ref, bd_ref, o_ref, wgu_b, wd_b):
    i = pl.program_id(0)
    e = be_ref[i]
    e_prev = be_ref[jnp.maximum(i - 1, 0)]

    @pl.when((i == 0) | (e != e_prev))
    def _():
        wgu_b[...] = wgu_ref[...].astype(BF16)
        wd_b[...] = wd_ref[...].astype(BF16)

    @pl.when(i < nu_ref[0])
    def _():
        hgu = _dot(xs_ref[...], wgu_b[...]) + bgu_ref[...]
        g = jnp.minimum(hgu[:, :D_FF], SWIGLU_LIMIT)
        u = jnp.clip(hgu[:, D_FF:], -SWIGLU_LIMIT, SWIGLU_LIMIT)
        hid = (u + 1.0) * (g * jax.nn.sigmoid(SWIGLU_ALPHA * g))
        out = _dot(hid.astype(BF16), wd_b[...]) + bd_ref[...]
        o_ref[...] = out * gate_ref[...]

    @pl.when(i >= nu_ref[0])
    def _():
        o_ref[...] = jnp.zeros(o_ref.shape, F32)


def _moe_blocks(block_e, n_used, xs, slot_gate, w_gate_up, b_gate_up, w_down, b_down, blk):
    n_blocks = block_e.shape[0]
    row = lambda i, be, nu: (i, 0)
    in_specs = [pl.BlockSpec((blk, D_MODEL), row), pl.BlockSpec((blk, 1), row),
                pl.BlockSpec((None, D_MODEL, 2 * D_FF), lambda i, be, nu: (be[i], 0, 0)),
                pl.BlockSpec((None, 1, 2 * D_FF), lambda i, be, nu: (be[i], 0, 0)),
                pl.BlockSpec((None, D_FF, D_MODEL), lambda i, be, nu: (be[i], 0, 0)),
                pl.BlockSpec((None, 1, D_MODEL), lambda i, be, nu: (be[i], 0, 0))]
    grid_spec = pltpu.PrefetchScalarGridSpec(
        num_scalar_prefetch=2, grid=(n_blocks,), in_specs=in_specs,
        out_specs=pl.BlockSpec((blk, D_MODEL), row),
        scratch_shapes=[pltpu.VMEM((D_MODEL, 2 * D_FF), BF16), pltpu.VMEM((D_FF, D_MODEL), BF16)])
    return pl.pallas_call(
        _moe_kernel, grid_spec=grid_spec, out_shape=jax.ShapeDtypeStruct((n_blocks * blk, D_MODEL), F32),
        compiler_params=_params(("arbitrary",)), name="moe",
    )(block_e, n_used, xs, slot_gate, w_gate_up, b_gate_up[:, None, :], w_down, b_down[:, None, :])


def _moe(h, xn, logits, w_gate_up, b_gate_up, w_down, b_down, blk):
    t = h.shape[0]
    top_v, top_i = lax.top_k(logits[:, :N_EXPERTS], TOP_K)
    gates = jax.nn.softmax(top_v, axis=-1)
    a = t * TOP_K
    flat_e = top_i.reshape(a).astype(jnp.int32)
    order = jnp.argsort(flat_e)
    e_s = flat_e[order]
    counts = jnp.bincount(flat_e, length=N_EXPERTS).astype(jnp.int32)
    padded = (counts + blk - 1) // blk * blk
    start = jnp.cumsum(counts) - counts
    pend = jnp.cumsum(padded)
    pstart = pend - padded
    dest_sorted = pstart[e_s] + jnp.arange(a, dtype=jnp.int32) - start[e_s]
    n_blocks = -(-a // blk) + N_EXPERTS
    n_slots = n_blocks * blk
    dest = jnp.zeros((a,), jnp.int32).at[order].set(dest_sorted)
    slot_tok = jnp.zeros((n_slots,), jnp.int32).at[dest].set(jnp.arange(a, dtype=jnp.int32) // TOP_K)
    slot_gate = jnp.zeros((n_slots,), F32).at[dest].set(gates.reshape(a))
    block_e = jnp.minimum(jnp.searchsorted(pend, jnp.arange(n_blocks, dtype=jnp.int32) * blk, side='right'),
                          N_EXPERTS - 1).astype(jnp.int32)
    n_used = (pend[-1:] // blk).astype(jnp.int32)
    xs = xn[slot_tok]
    outs = _moe_blocks(block_e, n_used, xs, slot_gate[:, None], w_gate_up, b_gate_up, w_down, b_down, blk)
    y = outs[dest.reshape(t, TOP_K)].sum(axis=1)
    return h + y


def _rope_tables(pos):
    pos = pos.astype(F32)[:, None]
    n = pos.shape[0]
    hr = RET_DK // 2
    ang = pos * (RET_THETA ** (-jnp.arange(hr, dtype=F32) / hr))[None, :]
    cos, sin = jnp.cos(ang), jnp.sin(ang)
    zr = jnp.zeros((n, LANES - RET_DK), F32)
    cr = jnp.concatenate([cos, cos, zr], axis=1)
    sr = jnp.concatenate([-sin, sin, zr], axis=1)
    hm = MLA_ROPE // 2
    ang = pos * (MLA_THETA ** (-jnp.arange(hm, dtype=F32) / hm))[None, :]
    cos, sin = jnp.cos(ang), jnp.sin(ang)
    cm = jnp.concatenate([cos, cos, jnp.ones((n, LANES - MLA_ROPE), F32)], axis=1)
    sma = jnp.concatenate([jnp.zeros((n, hm), F32), sin, jnp.zeros((n, LANES - MLA_ROPE), F32)], axis=1)
    smb = jnp.concatenate([-sin, jnp.zeros((n, LANES - hm), F32)], axis=1)
    return [cr, sr, cm, sma, smb]


def _rope_tables_t(pos):
    hm = MLA_ROPE // 2
    ang = pos.astype(F32)[:, None] * (MLA_THETA ** (-jnp.arange(hm, dtype=F32) / hm))[None, :]
    return jnp.cos(ang).T, jnp.sin(ang).T


def _decay_tables(log_gamma, c, n_rep):
    idx = jnp.arange(c, dtype=F32)
    diff = idx[:, None] - idx[None, :]
    decay = jnp.where(diff >= 0, jnp.exp(jnp.maximum(diff, 0.0)[None] * log_gamma[:, None, None]), 0.0)
    if n_rep > 1:
        eye = jnp.eye(n_rep, dtype=F32)
        decay = (eye[None, :, None, :, None] * decay[:, None, :, None, :]).reshape(RET_HEADS, n_rep * c, n_rep * c)
    rowd = jnp.tile(jnp.exp((idx + 1.0)[:, None] * log_gamma[None, :]), (n_rep, 1))
    cold = jnp.tile(jnp.exp((c - 1.0 - idx)[:, None] * log_gamma[None, :]), (n_rep, 1))
    cpow = jnp.broadcast_to(jnp.exp(c * log_gamma)[None, :], (LANES, RET_HEADS))
    return {'dec': decay, 'rowd': rowd, 'cold': cold, 'cpow': cpow}


def _pad_heads(wm, n_heads, width, offset):
    k = wm.shape[0]
    wm = wm.reshape(k, n_heads, width)
    out = jnp.zeros((k, n_heads, LANES), wm.dtype).at[:, :, offset:offset + width].set(wm)
    return out.reshape(k, n_heads * LANES)


def _prep_weights(norm_attn_g, w_in, ret_out_g, q_a_norm_g, w_q_b, kv_a_norm_g, w_uk, w_uv,
                  qk_norm_q_g, qk_norm_k_g, w_out, norm_ffn_g, w_router, b_router):
    splits = [RET_HEADS * RET_DK, RET_HEADS * RET_DK, RET_HEADS * RET_DV, RET_HEADS * RET_DV, Q_LORA, KV_LORA,
              MLA_ROPE]
    offs = [0]
    for s in splits:
        offs.append(offs[-1] + s)
    part = [w_in[:, offs[i]:offs[i + 1]] for i in range(len(splits))]
    kpe_cols = jnp.zeros((D_MODEL, LANES), F32).at[:, :MLA_ROPE].set(part[6])
    win = jnp.concatenate([_pad_heads(part[0], RET_HEADS, RET_DK, 0), _pad_heads(part[1], RET_HEADS, RET_DK, 0),
                           part[2], part[3], part[4], part[5], kpe_cols], axis=1).astype(BF16)

    def mla_cols(wm):
        k = wm.shape[0]
        wm = wm.reshape(k, MLA_HEADS, MLA_QK)
        wm = jnp.concatenate([wm[:, :, MLA_NOPE:], wm[:, :, :MLA_NOPE]], axis=2)
        return _pad_heads(wm.reshape(k, MLA_HEADS * MLA_QK), MLA_HEADS, MLA_QK, 0)

    def mla_gain(g):
        g = jnp.concatenate([g[MLA_NOPE:], g[:MLA_NOPE], jnp.zeros((LANES - MLA_QK,), F32)])
        return g[None, :]

    gk_nope = qk_norm_k_g[:MLA_NOPE]
    uk_heads = w_uk.reshape(KV_LORA, MLA_HEADS, MLA_NOPE)
    wabs = jnp.zeros((MLA_HEADS, HEAD_PAD, KV_LORA), F32).at[:, MLA_ROPE:MLA_ROPE + MLA_NOPE, :].set(
        (uk_heads * gk_nope[None, None, :]).transpose(1, 2, 0))
    head_of_col = jnp.arange(MLA_HEADS * MLA_V) // MLA_V
    uv_blk = jnp.where(head_of_col[None, None, :] == jnp.arange(MLA_HEADS)[:, None, None], w_uv[None], 0.0)
    wr = jnp.zeros((D_MODEL, LANES), F32).at[:, :N_EXPERTS].set(w_router)
    wr_hi = wr.astype(BF16)
    return {
        'gattn': norm_attn_g[None, :], 'win': win, 'gqa': q_a_norm_g[None, :], 'wqb': mla_cols(w_q_b).astype(BF16),
        'gq': mla_gain(qk_norm_q_g), 'gkv': kv_a_norm_g[None, :],
        'wuk': _pad_heads(w_uk, MLA_HEADS, MLA_NOPE, MLA_ROPE).astype(BF16), 'gk': mla_gain(qk_norm_k_g),
        'wuv': w_uv.astype(BF16), 'gret': ret_out_g[None, :],
        'wabs': wabs.astype(BF16), 'uk_t': w_uk.T.astype(BF16), 'uv_blk': uv_blk.astype(BF16),
        'gpe': qk_norm_k_g[MLA_NOPE:, None],
        'wo': w_out.astype(BF16), 'gffn': norm_ffn_g[None, :], 'wr_hi': wr_hi,
        'wr_lo': (wr - wr_hi.astype(F32)).astype(BF16),
        'br': jnp.zeros((1, LANES), F32).at[0, :N_EXPERTS].set(b_router),
    }


def _pad_rows(a, rows):
    return jnp.zeros((rows,) + a.shape[1:], a.dtype).at[:a.shape[0]].set(a)


def _largest_divisor(n, cap):
    d = min(n, cap)
    while n % d:
        d -= 1
    return d


def kernel(x_prompt, x_sample, cache_latent, cache_krope, state_retention, page_table, meta_tokens, norm_attn_g, w_in, ret_out_g, q_a_norm_g, w_q_b, kv_a_norm_g, w_uk, w_uv, qk_norm_q_g, qk_norm_k_g, w_out, norm_ffn_g, w_router, b_router, w_gate_up, b_gate_up, w_down, b_down):
    assert w_in.shape[0] == 1, "single-layer trunk"
    batch, seq, _ = x_prompt.shape
    n_seq, n_tok, _ = x_sample.shape
    n_pages, page = page_table.shape[1], cache_latent.shape[2]
    past = n_pages * page
    assert seq % RET_CHUNK == 0 and LANES % n_tok == 0 and (n_seq * n_tok) % LANES == 0
    w = _prep_weights(norm_attn_g[0], w_in[0], ret_out_g[0], q_a_norm_g[0], w_q_b[0], kv_a_norm_g[0], w_uk[0],
                      w_uv[0], qk_norm_q_g[0], qk_norm_k_g[0], w_out[0], norm_ffn_g[0], w_router[0], b_router[0])
    log_gamma = jnp.log1p(-jnp.exp2(-5.0 - jnp.arange(RET_HEADS, dtype=F32)))

    rows_p = batch * seq
    rows_s = n_seq * n_tok
    tm = _largest_divisor(min(seq, rows_s), ROW_BLOCK)
    nb_seq = seq // tm
    x = jnp.concatenate([x_prompt.reshape(rows_p, D_MODEL), x_sample.reshape(rows_s, D_MODEL)], axis=0)
    pos_rows = jnp.concatenate([N_META + jnp.arange(seq), jnp.tile(past + jnp.arange(n_tok), tm // n_tok)])
    tabs = _rope_tables(pos_rows)
    n_pb = rows_p // tm
    rq, rk, rv, zg, mq, ckv, kpe, k, v = _front(
        x, tabs, lambda i: jnp.where(i < n_pb, i % nb_seq, nb_seq), w, tm)
    _, mrk, mrv, _, _, mckv, mkpe, mk, mv = _front(
        meta_tokens, _rope_tables(jnp.arange(N_META)), lambda i: i, w, N_META)

    dt_p = _decay_tables(log_gamma, RET_CHUNK, 1)
    dt_p['mcol'] = _pad_rows(jnp.exp((N_META - 1.0 - jnp.arange(N_META, dtype=F32))[:, None] * log_gamma[None, :]),
                             RET_CHUNK)
    gret = w['gret']
    r_p, st_p = _ret_prompt(rq, rk, rv, zg, _pad_rows(mrk, RET_CHUNK), _pad_rows(mrv, RET_CHUNK), dt_p, gret,
                            batch, seq // RET_CHUNK)
    dt_s = _decay_tables(log_gamma, n_tok, LANES // n_tok)
    r_s, st_s = _ret_sample(rq[rows_p:], rk[rows_p:], rv[rows_p:], zg[rows_p:], state_retention[0], dt_s, gret, n_tok)

    blk = _largest_divisor(seq, ATTN_BLOCK)
    m_p = _attn_prompt(mq, k, v, _pad_rows(mk, LANES), _pad_rows(mv, LANES), batch, seq, blk)
    tm_s = _largest_divisor(rows_s, ROW_BLOCK)
    qt, qf = _absorb(mq, w['wabs'], rows_p // tm_s, rows_s, tm_s)
    n_pages_step = _largest_divisor(n_pages, PAGES_PER_STEP)
    cos_t, sin_t = _rope_tables_t(jnp.arange(past))
    cosn_t, sinn_t = _rope_tables_t(past + jnp.arange(page))
    tabs_s = {'cos_t': cos_t, 'sin_t': sin_t, 'cosn_t': cosn_t, 'sinn_t': sinn_t}
    m_s = _attn_sample(page_table, cache_latent, cache_krope, tabs_s, qt, qf, ckv[rows_p:], kpe[rows_p:], w,
                       n_tok, n_pages_step)

    r_all = jnp.concatenate([r_p, r_s], axis=0)
    m_all = jnp.concatenate([m_p, m_s.astype(BF16)], axis=0)
    h1, xn2, logits = _post(r_all, m_all, x, w, tm)
    y = _moe(h1, xn2, logits, w_gate_up[0], b_gate_up[0], w_down[0], b_down[0], EXPERT_BLOCK)

    y_prompt = y[:rows_p].reshape(batch, seq, D_MODEL)
    y_sample = y[rows_p:].reshape(n_seq, n_tok, D_MODEL)
    lat_p = jnp.concatenate([jnp.broadcast_to(mckv[None], (batch, N_META, KV_LORA)),
                             ckv[:rows_p].reshape(batch, seq, KV_LORA)], axis=1)[None]
    kpe_p = jnp.concatenate([jnp.broadcast_to(mkpe[None], (batch, N_META, MLA_ROPE)),
                             kpe[:rows_p].reshape(batch, seq, MLA_ROPE)], axis=1)[None]
    return (y_prompt, y_sample, lat_p, kpe_p, st_p[None],
            ckv[rows_p:].reshape(n_seq, n_tok, KV_LORA)[None], kpe[rows_p:].reshape(n_seq, n_tok, MLA_ROPE)[None],
            st_s[None])
```

```python
import functools

import jax
import jax.numpy as jnp
from jax import lax
from jax.experimental import pallas as pl
from jax.experimental.pallas import tpu as pltpu

F32 = jnp.float32
BF16 = jnp.bfloat16

D_MODEL = 1024
N_META = 16
RET_HEADS = 4
RET_DK = 64
RET_DV = 128
RET_CHUNK = 128
RET_THETA = 10000.0
MLA_HEADS = 8
MLA_NOPE = 64
MLA_ROPE = 32
MLA_QK = MLA_NOPE + MLA_ROPE
MLA_V = 64
Q_LORA = 384
KV_LORA = 256
MLA_THETA = 10000.0
MLA_SCALE = MLA_QK ** -0.5
N_EXPERTS = 32
TOP_K = 4
D_FF = 1024
SWIGLU_LIMIT = 7.0
SWIGLU_ALPHA = 1.702
EPS = 1e-6
NEG_BIG = -1e30

LANES = 128
HEAD_PAD = LANES
VMEM_LIMIT = 56 * 1024 * 1024
ROW_BLOCK = 256
ATTN_BLOCK = 1024
PAGES_PER_STEP = 16
PAGES_PER_CHAIN = 4
EXPERT_BLOCK = 256

_OFF_RQ = 0
_OFF_RK = _OFF_RQ + RET_HEADS * LANES
_OFF_RV = _OFF_RK + RET_HEADS * LANES
_OFF_ZG = _OFF_RV + RET_HEADS * RET_DV
_OFF_CQ = _OFF_ZG + RET_HEADS * RET_DV
_OFF_CKV = _OFF_CQ + Q_LORA
_OFF_KPE = _OFF_CKV + KV_LORA
IN_PAD = _OFF_KPE + LANES


def _params(sem):
    return pltpu.CompilerParams(dimension_semantics=sem, vmem_limit_bytes=VMEM_LIMIT)


def _rms(x, g):
    return x * lax.rsqrt(jnp.mean(x * x, axis=-1, keepdims=True) + EPS) * g


def _dot(a, b):
    return jnp.dot(a, b, preferred_element_type=F32)


def _dot_nt(a, b):
    return lax.dot_general(a, b, (((1,), (1,)), ((), ())), preferred_element_type=F32)


def _dot_tn(a, b):
    return lax.dot_general(a, b, (((0,), (0,)), ((), ())), preferred_element_type=F32)


def _front_kernel(x_ref, gattn_ref, win_ref, cr_ref, sr_ref, cm_ref, sma_ref, smb_ref,
                  gqa_ref, wqb_ref, gq_ref, gkv_ref, wuk_ref, gk_ref, wuv_ref,
                  rq_ref, rk_ref, rv_ref, zg_ref, mq_ref, ckv_ref, kpe_ref, k_ref, v_ref):
    tm = x_ref.shape[0]
    xn = _rms(x_ref[...], gattn_ref[...])
    z = _dot(xn.astype(BF16), win_ref[...])

    lane = lax.broadcasted_iota(jnp.int32, (tm, LANES), 1)
    upper = (lane & (RET_DK // 2)) != 0
    cr, sr = cr_ref[...], sr_ref[...]

    def rope_ret(t):
        partner = jnp.where(upper, pltpu.roll(t, RET_DK // 2, 1), pltpu.roll(t, LANES - RET_DK // 2, 1))
        return t * cr + partner * sr

    for h in range(RET_HEADS):
        sl = slice(h * LANES, (h + 1) * LANES)
        rq_ref[:, sl] = rope_ret(z[:, _OFF_RQ + h * LANES:_OFF_RQ + (h + 1) * LANES])
        rk_ref[:, sl] = rope_ret(z[:, _OFF_RK + h * LANES:_OFF_RK + (h + 1) * LANES]) * (RET_DK ** -0.5)
    rv_ref[...] = z[:, _OFF_RV:_OFF_ZG].astype(BF16)
    zg_ref[...] = z[:, _OFF_ZG:_OFF_CQ]

    cm, sma, smb = cm_ref[...], sma_ref[...], smb_ref[...]

    def rope_mla(t):
        half = MLA_ROPE // 2
        return t * cm + pltpu.roll(t, half, 1) * sma + pltpu.roll(t, LANES - half, 1) * smb

    def head_norm(t, g):
        ms = jnp.sum(t * t, axis=-1, keepdims=True) * (1.0 / MLA_QK)
        return t * lax.rsqrt(ms + EPS) * g

    cq = _rms(z[:, _OFF_CQ:_OFF_CKV], gqa_ref[...])
    q = _dot(cq.astype(BF16), wqb_ref[...])
    gq = gq_ref[...]
    for h in range(MLA_HEADS):
        sl = slice(h * HEAD_PAD, (h + 1) * HEAD_PAD)
        mq_ref[:, sl] = (rope_mla(head_norm(q[:, sl], gq)) * MLA_SCALE).astype(BF16)

    ckv = _rms(z[:, _OFF_CKV:_OFF_KPE], gkv_ref[...])
    ckv_ref[...] = ckv
    kpe_slab = z[:, _OFF_KPE:IN_PAD]
    kpe_ref[...] = kpe_slab[:, :MLA_ROPE]
    ckv_b = ckv.astype(BF16)
    kn = _dot(ckv_b, wuk_ref[...])
    gk = gk_ref[...]
    for h in range(MLA_HEADS):
        sl = slice(h * HEAD_PAD, (h + 1) * HEAD_PAD)
        k_ref[:, sl] = rope_mla(head_norm(kn[:, sl] + kpe_slab, gk)).astype(BF16)
    v_ref[...] = _dot(ckv_b, wuv_ref[...]).astype(BF16)


def _front(x, tabs, tab_index, w, tm):
    rows = x.shape[0]
    grid = (rows // tm,)
    row = lambda i: (i, 0)
    const = lambda i: (0, 0)
    tab = lambda i: (tab_index(i), 0)

    def full(a):
        return pl.BlockSpec(a.shape, const)

    in_specs = [pl.BlockSpec((tm, D_MODEL), row), full(w['gattn']), full(w['win'])]
    in_specs += [pl.BlockSpec((tm, LANES), tab)] * 5
    in_specs += [full(w[n]) for n in ('gqa', 'wqb', 'gq', 'gkv', 'wuk', 'gk', 'wuv')]
    widths = [(RET_HEADS * LANES, F32), (RET_HEADS * LANES, F32), (RET_HEADS * RET_DV, BF16),
              (RET_HEADS * RET_DV, F32), (MLA_HEADS * HEAD_PAD, BF16), (KV_LORA, F32), (MLA_ROPE, F32),
              (MLA_HEADS * HEAD_PAD, BF16), (MLA_HEADS * MLA_V, BF16)]
    out_shape = [jax.ShapeDtypeStruct((rows, n), dt) for n, dt in widths]
    out_specs = [pl.BlockSpec((tm, n), row) for n, _ in widths]
    return pl.pallas_call(
        _front_kernel, grid=grid, in_specs=in_specs, out_specs=out_specs, out_shape=out_shape,
        compiler_params=_params(("arbitrary",)), name="front",
    )(x, w['gattn'], w['win'], *tabs, w['gqa'], w['wqb'], w['gq'], w['gkv'], w['wuk'], w['gk'], w['wuv'])


def _ret_gate(o, zg, g):
    on = o * lax.rsqrt(jnp.mean(o * o, axis=-1, keepdims=True) + EPS) * g
    return (zg * jax.nn.sigmoid(zg)) * on


def _ret_prompt_kernel(q_ref, k_ref, v_ref, zg_ref, mk_ref, mv_ref, mcol_ref, dec_ref, rowd_ref, cold_ref,
                       cpow_ref, g_ref, r_ref, s_ref, s_scr):
    c = pl.program_id(1)

    @pl.when(c == 0)
    def _():
        for h in range(RET_HEADS):
            sl = slice(h * LANES, (h + 1) * LANES)
            kw = mk_ref[:, sl] * mcol_ref[:, h:h + 1]
            s_scr[h] = _dot_tn(kw.astype(BF16), mv_ref[:, sl])

    for h in range(RET_HEADS):
        sl = slice(h * LANES, (h + 1) * LANES)
        q = q_ref[:, sl]
        k = k_ref[:, sl]
        v = v_ref[:, sl]
        qb = q.astype(BF16)
        s0 = s_scr[h]
        scores = _dot_nt(qb, k.astype(BF16)) * dec_ref[h]
        inner = _dot(scores.astype(BF16), v)
        cross = _dot(qb, s0.astype(BF16)) * rowd_ref[:, h:h + 1]
        kw = k * cold_ref[:, h:h + 1]
        s_new = s0 * cpow_ref[:, h:h + 1] + _dot_tn(kw.astype(BF16), v)
        s_scr[h] = s_new
        r_ref[:, sl] = _ret_gate(inner + cross, zg_ref[:, sl], g_ref[:, sl]).astype(BF16)

    @pl.when(c == pl.num_programs(1) - 1)
    def _():
        for h in range(RET_HEADS):
            s_ref[0, h] = s_scr[h, :RET_DK, :]


def _ret_prompt(rq, rk, rv, zg, mk, mv, tabs, g, batch, n_chunks):
    cs = RET_CHUNK
    row = lambda b, c: (b * n_chunks + c, 0)
    const2 = lambda b, c: (0, 0)
    w4 = RET_HEADS * LANES
    in_specs = [pl.BlockSpec((cs, w4), row), pl.BlockSpec((cs, w4), row), pl.BlockSpec((cs, w4), row),
                pl.BlockSpec((cs, w4), row),
                pl.BlockSpec((cs, w4), const2), pl.BlockSpec((cs, w4), const2),
                pl.BlockSpec((cs, RET_HEADS), const2),
                pl.BlockSpec((RET_HEADS, cs, cs), lambda b, c: (0, 0, 0)),
                pl.BlockSpec((cs, RET_HEADS), const2), pl.BlockSpec((cs, RET_HEADS), const2),
                pl.BlockSpec((LANES, RET_HEADS), const2), pl.BlockSpec((1, w4), const2)]
    out_shape = [jax.ShapeDtypeStruct((batch * n_chunks * cs, w4), BF16),
                 jax.ShapeDtypeStruct((batch, RET_HEADS, RET_DK, RET_DV), F32)]
    out_specs = [pl.BlockSpec((cs, w4), row),
                 pl.BlockSpec((1, RET_HEADS, RET_DK, RET_DV), lambda b, c: (b, 0, 0, 0))]
    return pl.pallas_call(
        _ret_prompt_kernel, grid=(batch, n_chunks), in_specs=in_specs, out_specs=out_specs, out_shape=out_shape,
        scratch_shapes=[pltpu.VMEM((RET_HEADS, LANES, RET_DV), F32)],
        compiler_params=_params(("arbitrary", "arbitrary")), name="ret_prompt",
    )(rq, rk, rv, zg, mk, mv, tabs['mcol'], tabs['dec'], tabs['rowd'], tabs['cold'], tabs['cpow'], g)


def _ret_sample_kernel(q_ref, k_ref, v_ref, zg_ref, s0_ref, dec_ref, rowd_ref, cold_ref, cpow_ref, g_ref,
                       r_ref, s_ref, *, n_seq, n_tok):
    rows = n_seq * n_tok
    ri = lax.broadcasted_iota(jnp.int32, (rows, 1), 0)
    for h in range(RET_HEADS):
        sl = slice(h * LANES, (h + 1) * LANES)
        q = q_ref[:, sl]
        k = k_ref[:, sl]
        v = v_ref[:, sl]
        qb = q.astype(BF16)
        scores = _dot_nt(qb, k.astype(BF16)) * dec_ref[h]
        inner = _dot(scores.astype(BF16), v)
        kw = k * cold_ref[:, h:h + 1]
        cross = jnp.zeros((rows, RET_DV), F32)
        for s in range(n_seq):
            s0 = s0_ref[s, h]
            mine = (ri >= s * n_tok) & (ri < (s + 1) * n_tok)
            cross = cross + jnp.where(mine, _dot(qb[:, :RET_DK], s0.astype(BF16)), 0.0)
            upd = _dot_tn(jnp.where(mine, kw, 0.0).astype(BF16), v)
            s_ref[s, h] = s0 * cpow_ref[:RET_DK, h:h + 1] + upd[:RET_DK]
        cross = cross * rowd_ref[:, h:h + 1]
        r_ref[:, sl] = _ret_gate(inner + cross, zg_ref[:, sl], g_ref[:, sl]).astype(BF16)


def _ret_sample(rq, rk, rv, zg, state, tabs, g, n_tok):
    n_seq_total = state.shape[0]
    n_seq = LANES // n_tok
    rows = n_seq * n_tok
    w4 = RET_HEADS * LANES
    row = lambda i: (i, 0)
    const = lambda i: (0, 0)
    in_specs = [pl.BlockSpec((rows, w4), row)] * 4
    in_specs += [pl.BlockSpec((n_seq, RET_HEADS, RET_DK, RET_DV), lambda i: (i, 0, 0, 0)),
                 pl.BlockSpec((RET_HEADS, rows, rows), lambda i: (0, 0, 0)),
                 pl.BlockSpec((rows, RET_HEADS), const), pl.BlockSpec((rows, RET_HEADS), const),
                 pl.BlockSpec((LANES, RET_HEADS), const), pl.BlockSpec((1, w4), const)]
    out_shape = [jax.ShapeDtypeStruct((n_seq_total * n_tok, w4), BF16),
                 jax.ShapeDtypeStruct(state.shape, F32)]
    out_specs = [pl.BlockSpec((rows, w4), row),
                 pl.BlockSpec((n_seq, RET_HEADS, RET_DK, RET_DV), lambda i: (i, 0, 0, 0))]
    return pl.pallas_call(
        functools.partial(_ret_sample_kernel, n_seq=n_seq, n_tok=n_tok),
        grid=(n_seq_total // n_seq,), in_specs=in_specs, out_specs=out_specs, out_shape=out_shape,
        compiler_params=_params(("arbitrary",)), name="ret_sample",
    )(rq, rk, rv, zg, state, tabs['dec'], tabs['rowd'], tabs['cold'], tabs['cpow'], g)


def _attn_prompt_kernel(q_ref, k_ref, v_ref, km_ref, vm_ref, o_ref, m_scr, l_scr, acc_scr):
    qi = pl.program_id(2)
    ki = pl.program_id(3)
    tm, tk = q_ref.shape[0], k_ref.shape[0]

    @pl.when(ki == 0)
    def _():
        lane = lax.broadcasted_iota(jnp.int32, (tm, km_ref.shape[0]), 1)
        for hh in range(2):
            sl = slice(hh * HEAD_PAD, (hh + 1) * HEAD_PAD)
            s = jnp.where(lane < N_META, _dot_nt(q_ref[:, sl], km_ref[:, sl]), NEG_BIG)
            m = jnp.max(s, axis=-1, keepdims=True)
            p = jnp.exp(s - m)
            m_scr[hh] = m
            l_scr[hh] = jnp.sum(p, axis=-1, keepdims=True)
            acc_scr[hh] = _dot(p.astype(BF16), vm_ref[...])

    def block(masked):
        if masked:
            rowi = lax.broadcasted_iota(jnp.int32, (tm, tk), 0)
            coli = lax.broadcasted_iota(jnp.int32, (tm, tk), 1)
            keep = coli <= rowi
        for hh in range(2):
            sl = slice(hh * HEAD_PAD, (hh + 1) * HEAD_PAD)
            s = _dot_nt(q_ref[:, sl], k_ref[:, sl])
            if masked:
                s = jnp.where(keep, s, NEG_BIG)
            m_old = m_scr[hh]
            m_new = jnp.maximum(m_old, jnp.max(s, axis=-1, keepdims=True))
            alpha = jnp.exp(m_old - m_new)
            p = jnp.exp(s - m_new)
            m_scr[hh] = m_new
            l_scr[hh] = alpha * l_scr[hh] + jnp.sum(p, axis=-1, keepdims=True)
            acc_scr[hh] = alpha * acc_scr[hh] + _dot(p.astype(BF16), v_ref[...])

    @pl.when(ki < qi)
    def _():
        block(False)

    @pl.when(ki == qi)
    def _():
        block(True)
        lane = lax.broadcasted_iota(jnp.int32, (tm, 2 * MLA_V), 1)
        o0 = acc_scr[0] / l_scr[0]
        o1 = acc_scr[1] / l_scr[1]
        o_ref[...] = jnp.where(lane < MLA_V, o0, o1).astype(BF16)


def _attn_prompt(mq, k, v, km, vm, batch, seq, blk):
    nb = seq // blk
    pairs = MLA_HEADS // 2
    grid = (batch, pairs, nb, nb)
    in_specs = [pl.BlockSpec((blk, 2 * HEAD_PAD), lambda b, h, qi, ki: (b * nb + qi, h)),
                pl.BlockSpec((blk, 2 * HEAD_PAD), lambda b, h, qi, ki: (b * nb + jnp.minimum(ki, qi), h)),
                pl.BlockSpec((blk, 2 * MLA_V), lambda b, h, qi, ki: (b * nb + jnp.minimum(ki, qi), h)),
                pl.BlockSpec((km.shape[0], 2 * HEAD_PAD), lambda b, h, qi, ki: (0, h)),
                pl.BlockSpec((vm.shape[0], 2 * MLA_V), lambda b, h, qi, ki: (0, h))]
    out_specs = pl.BlockSpec((blk, 2 * MLA_V), lambda b, h, qi, ki: (b * nb + qi, h))
    return pl.pallas_call(
        _attn_prompt_kernel, grid=grid, in_specs=in_specs, out_specs=out_specs,
        out_shape=jax.ShapeDtypeStruct((batch * seq, MLA_HEADS * MLA_V), BF16),
        scratch_shapes=[pltpu.VMEM((2, blk, 1), F32), pltpu.VMEM((2, blk, 1), F32),
                        pltpu.VMEM((2, blk, 2 * MLA_V), F32)],
        compiler_params=_params(("arbitrary",) * 4), name="attn_prompt",
    )(mq, k, v, km, vm)


def _absorb_kernel(mq_ref, wabs_ref, qt_ref, qf_ref):
    q = mq_ref[...]
    qf_ref[...] = q.astype(F32)
    for h in range(MLA_HEADS):
        qt_ref[:, h * KV_LORA:(h + 1) * KV_LORA] = _dot(q[:, h * HEAD_PAD:(h + 1) * HEAD_PAD], wabs_ref[h])


def _absorb(mq, wabs, row0_blocks, rows, tm):
    return pl.pallas_call(
        _absorb_kernel, grid=(rows // tm,),
        in_specs=[pl.BlockSpec((tm, MLA_HEADS * HEAD_PAD), lambda i: (row0_blocks + i, 0)),
                  pl.BlockSpec(wabs.shape, lambda i: (0, 0, 0))],
        out_specs=[pl.BlockSpec((tm, MLA_HEADS * KV_LORA), lambda i: (i, 0)),
                   pl.BlockSpec((tm, MLA_HEADS * HEAD_PAD), lambda i: (i, 0))],
        out_shape=[jax.ShapeDtypeStruct((rows, MLA_HEADS * KV_LORA), F32),
                   jax.ShapeDtypeStruct((rows, MLA_HEADS * HEAD_PAD), F32)],
        compiler_params=_params(("arbitrary",)), name="absorb",
    )(mq, wabs)


def _attn_sample_kernel(pt_ref, *refs, n_pages, sub, n_tok, page):
    lat_refs = refs[:n_pages]
    kpe_refs = refs[n_pages:2 * n_pages]
    (cos_ref, sin_ref, qt_ref, qf_ref, cn_ref, kn_ref, cosn_ref, sinn_ref, lw_ref, wuv_ref, gpe_ref,
     o_ref, l_scr, qpe_scr, m_scr, d_scr, acc_scr, kpad_scr, cpad_scr) = refs[2 * n_pages:]
    del pt_ref
    j = pl.program_id(1)
    nq = MLA_HEADS * n_tok
    n_up = MLA_HEADS * MLA_NOPE

    @pl.when(j == 0)
    def _():
        l_scr[:n_up, :] = lw_ref[...]
        qt = qt_ref[...]
        qf = qf_ref[...]
        l_scr[n_up:, :] = jnp.concatenate(
            [qt[:, h * KV_LORA:(h + 1) * KV_LORA] for h in range(MLA_HEADS)], axis=0).astype(BF16)
        lane = lax.broadcasted_iota(jnp.int32, (nq, HEAD_PAD), 1)
        qpe = jnp.concatenate([qf[:, h * HEAD_PAD:(h + 1) * HEAD_PAD] for h in range(MLA_HEADS)], axis=0)
        qpe_scr[...] = jnp.where(lane < MLA_ROPE, qpe, 0.0).astype(BF16)
        m_scr[...] = jnp.full(m_scr.shape, NEG_BIG, F32)
        d_scr[...] = jnp.zeros(d_scr.shape, F32)
        acc_scr[...] = jnp.zeros(acc_scr.shape, F32)

    def scores(cb, kpe_t, cos_t, sin_t):
        tk = cb.shape[0]
        big = _dot_nt(l_scr[...], cb)
        k_t = big[:n_up]
        ss = jnp.sum((k_t * k_t).reshape(MLA_HEADS, MLA_NOPE, tk), axis=1)
        ss = ss + jnp.sum(kpe_t * kpe_t, axis=0, keepdims=True)
        rs = lax.rsqrt(ss * (1.0 / MLA_QK) + EPS)
        kg = kpe_t * gpe_ref[...]
        half = MLA_ROPE // 2
        x1, x2 = kg[:half], kg[half:]
        rot = jnp.concatenate([x1 * cos_t - x2 * sin_t, x1 * sin_t + x2 * cos_t], axis=0)
        pe = _dot(qpe_scr[:, :MLA_ROPE], rot.astype(BF16))
        rs_rows = jnp.concatenate([jnp.broadcast_to(rs[h:h + 1, :], (n_tok, tk)) for h in range(MLA_HEADS)], axis=0)
        return (big[n_up:] + pe) * rs_rows

    def update(s_list, cb_list):
        m_old = m_scr[...]
        m_new = m_old
        for s in s_list:
            m_new = jnp.maximum(m_new, jnp.max(s, axis=-1, keepdims=True))
        alpha = jnp.exp(m_old - m_new)
        d = alpha * d_scr[...]
        acc = alpha * acc_scr[...]
        for s, cb in zip(s_list, cb_list):
            p = jnp.exp(s - m_new)
            d = d + jnp.sum(p, axis=-1, keepdims=True)
            acc = acc + _dot(p.astype(BF16), cb)
        m_scr[...] = m_new
        d_scr[...] = d
        acc_scr[...] = acc

    s_list, cb_list = [], []
    for g in range(n_pages // sub):
        pages = range(g * sub, (g + 1) * sub)
        cb = jnp.concatenate([lat_refs[b][...] for b in pages], axis=0).astype(BF16)
        kpe_t = jnp.concatenate([kpe_refs[b][...] for b in pages], axis=1)
        cols = slice(g * sub * page, (g + 1) * sub * page)
        s_list.append(scores(cb, kpe_t, cos_ref[:, cols], sin_ref[:, cols]))
        cb_list.append(cb)
    update(s_list, cb_list)

    @pl.when(j == pl.num_programs(1) - 1)
    def _():
        cpad_scr[...] = jnp.zeros(cpad_scr.shape, F32)
        cpad_scr[:n_tok, :] = cn_ref[...]
        kpad_scr[...] = jnp.zeros(kpad_scr.shape, F32)
        kpad_scr[:n_tok, :MLA_ROPE] = kn_ref[...]
        rowi = lax.broadcasted_iota(jnp.int32, (nq, page), 0)
        coli = lax.broadcasted_iota(jnp.int32, (nq, page), 1)
        keep = coli <= (rowi & (n_tok - 1))
        cb = cpad_scr[...].astype(BF16)
        s = scores(cb, kpad_scr[...].T[:MLA_ROPE], cosn_ref[...], sinn_ref[...])
        update([jnp.where(keep, s, NEG_BIG)], [cb])
        ctx = (acc_scr[...] / d_scr[...]).astype(BF16)
        out = jnp.zeros(o_ref.shape, F32)
        for h in range(MLA_HEADS):
            out = out + _dot(ctx, wuv_ref[h])[h * n_tok:(h + 1) * n_tok]
        o_ref[...] = out


def _attn_sample(page_table, cache_latent, cache_krope_t, tabs, qt, qf, ckv, kpe, w, n_tok, n_pages_step, sub):
    n_seq, n_pages = page_table.shape
    page = cache_latent.shape[2]
    n_steps = n_pages // n_pages_step
    tk = n_pages_step * page
    nq = MLA_HEADS * n_tok

    def page_spec(shape, p):
        return pl.BlockSpec((None, None) + shape, lambda s, j, pt: (0, pt[s, j * n_pages_step + p], 0, 0))

    seq_row = lambda s, j, pt: (s, 0)
    const2 = lambda s, j, pt: (0, 0)
    half = MLA_ROPE // 2
    in_specs = [page_spec((page, KV_LORA), p) for p in range(n_pages_step)]
    in_specs += [page_spec((MLA_ROPE, page), p) for p in range(n_pages_step)]
    in_specs += [pl.BlockSpec((half, tk), lambda s, j, pt: (0, j)), pl.BlockSpec((half, tk), lambda s, j, pt: (0, j)),
                 pl.BlockSpec((n_tok, MLA_HEADS * KV_LORA), seq_row),
                 pl.BlockSpec((n_tok, MLA_HEADS * HEAD_PAD), seq_row),
                 pl.BlockSpec((n_tok, KV_LORA), seq_row), pl.BlockSpec((n_tok, MLA_ROPE), seq_row),
                 pl.BlockSpec((half, page), const2), pl.BlockSpec((half, page), const2),
                 pl.BlockSpec(w['uk_t'].shape, const2),
                 pl.BlockSpec(w['uv_blk'].shape, lambda s, j, pt: (0, 0, 0)),
                 pl.BlockSpec((MLA_ROPE, 1), const2)]
    grid_spec = pltpu.PrefetchScalarGridSpec(
        num_scalar_prefetch=1, grid=(n_seq, n_steps), in_specs=in_specs,
        out_specs=pl.BlockSpec((n_tok, MLA_HEADS * MLA_V), seq_row),
        scratch_shapes=[pltpu.VMEM((MLA_HEADS * MLA_NOPE + nq, KV_LORA), BF16),
                        pltpu.VMEM((nq, HEAD_PAD), BF16),
                        pltpu.VMEM((nq, 1), F32), pltpu.VMEM((nq, 1), F32), pltpu.VMEM((nq, KV_LORA), F32),
                        pltpu.VMEM((page, LANES), F32), pltpu.VMEM((page, KV_LORA), F32)])
    return pl.pallas_call(
        functools.partial(_attn_sample_kernel, n_pages=n_pages_step, sub=sub, n_tok=n_tok, page=page),
        grid_spec=grid_spec, out_shape=jax.ShapeDtypeStruct((n_seq * n_tok, MLA_HEADS * MLA_V), F32),
        compiler_params=_params(("arbitrary", "arbitrary")), name="attn_sample",
    )(page_table, *([cache_latent] * n_pages_step), *([cache_krope_t] * n_pages_step),
      tabs['cos_t'], tabs['sin_t'], qt, qf, ckv, kpe, tabs['cosn_t'], tabs['sinn_t'],
      w['uk_t'], w['uv_blk'], w['gpe'])


def _post_kernel(r_ref, m_ref, x_ref, wo_ref, g_ref, wrh_ref, wrl_ref, br_ref, h_ref, xn_ref, ti_ref, tg_ref):
    half = wo_ref.shape[0] // 2
    mix = _dot(r_ref[...], wo_ref[:half, :]) + _dot(m_ref[...], wo_ref[half:, :])
    h = x_ref[...] + mix
    h_ref[...] = h
    xn = _rms(h, g_ref[...])
    xn_ref[...] = xn
    hi = xn.astype(BF16)
    lo = (xn - hi.astype(F32)).astype(BF16)
    wrh = wrh_ref[...]
    work = _dot(hi, wrh) + _dot(hi, wrl_ref[...]) + _dot(lo, wrh) + br_ref[...]
    lane = lax.broadcasted_iota(jnp.int32, work.shape, 1).astype(F32)
    idx = jnp.zeros(work.shape, F32)
    val = jnp.zeros(work.shape, F32)
    vmax = None
    denom = None
    for k in range(TOP_K):
        vk = jnp.max(work, axis=-1, keepdims=True)
        ik = jnp.min(jnp.where(work == vk, lane, float(LANES)), axis=-1, keepdims=True)
        work = jnp.where(lane == ik, -jnp.inf, work)
        if k == 0:
            vmax = vk
        ek = jnp.exp(vk - vmax)
        denom = ek if k == 0 else denom + ek
        idx = jnp.where(lane == float(k), ik, idx)
        val = jnp.where(lane == float(k), ek, val)
    ti_ref[...] = idx.astype(jnp.int32)
    tg_ref[...] = val / denom


def _post(r, m, x, w, tm):
    rows = x.shape[0]
    row = lambda i: (i, 0)
    const = lambda i: (0, 0)
    in_specs = [pl.BlockSpec((tm, r.shape[1]), row), pl.BlockSpec((tm, m.shape[1]), row),
                pl.BlockSpec((tm, D_MODEL), row)]
    in_specs += [pl.BlockSpec(w[n].shape, const) for n in ('wo', 'gffn', 'wr_hi', 'wr_lo', 'br')]
    out_shape = [jax.ShapeDtypeStruct((rows, D_MODEL), F32), jax.ShapeDtypeStruct((rows, D_MODEL), F32),
                 jax.ShapeDtypeStruct((rows, LANES), jnp.int32), jax.ShapeDtypeStruct((rows, LANES), F32)]
    out_specs = [pl.BlockSpec((tm, D_MODEL), row), pl.BlockSpec((tm, D_MODEL), row),
                 pl.BlockSpec((tm, LANES), row), pl.BlockSpec((tm, LANES), row)]
    return pl.pallas_call(
        _post_kernel, grid=(rows // tm,), in_specs=in_specs, out_specs=out_specs, out_shape=out_shape,
        compiler_params=_params(("arbitrary",)), name="post",
    )(r, m, x, w['wo'], w['gffn'], w['wr_hi'], w['wr_lo'], w['br'])


def _moe_kernel(be_ref, nu_ref, xs_ref, wgu_ref, bgu_ref, wd_ref, bd_ref, o_ref, wgu_b, wd_b):
    i = pl.program_id(0)
    e = be_ref[i]
    e_prev = be_ref[jnp.maximum(i - 1, 0)]

    @pl.when((i == 0) | (e != e_prev))
    def _():
        wgu_b[...] = wgu_ref[...].astype(BF16)
        wd_b[...] = wd_ref[...].astype(BF16)

    @pl.when(i < nu_ref[0])
    def _():
        hgu = _dot(xs_ref[...].astype(BF16), wgu_b[...]) + bgu_ref[...]
        g = jnp.minimum(hgu[:, :D_FF], SWIGLU_LIMIT)
        u = jnp.clip(hgu[:, D_FF:], -SWIGLU_LIMIT, SWIGLU_LIMIT)
        hid = (u + 1.0) * (g * jax.nn.sigmoid(SWIGLU_ALPHA * g))
        o_ref[...] = _dot(hid.astype(BF16), wd_b[...]) + bd_ref[...]

    @pl.when(i >= nu_ref[0])
    def _():
        o_ref[...] = jnp.zeros(o_ref.shape, F32)


def _moe_blocks(block_e, n_used, xs, w_gate_up, b_gate_up, w_down, b_down, blk):
    n_blocks = block_e.shape[0]
    row = lambda i, be, nu: (i, 0)
    in_specs = [pl.BlockSpec((blk, D_MODEL), row),
                pl.BlockSpec((None, D_MODEL, 2 * D_FF), lambda i, be, nu: (be[i], 0, 0)),
                pl.BlockSpec((None, 1, 2 * D_FF), lambda i, be, nu: (be[i], 0, 0)),
                pl.BlockSpec((None, D_FF, D_MODEL), lambda i, be, nu: (be[i], 0, 0)),
                pl.BlockSpec((None, 1, D_MODEL), lambda i, be, nu: (be[i], 0, 0))]
    grid_spec = pltpu.PrefetchScalarGridSpec(
        num_scalar_prefetch=2, grid=(n_blocks,), in_specs=in_specs,
        out_specs=pl.BlockSpec((blk, D_MODEL), row),
        scratch_shapes=[pltpu.VMEM((D_MODEL, 2 * D_FF), BF16), pltpu.VMEM((D_FF, D_MODEL), BF16)])
    return pl.pallas_call(
        _moe_kernel, grid_spec=grid_spec, out_shape=jax.ShapeDtypeStruct((n_blocks * blk, D_MODEL), F32),
        compiler_params=_params(("arbitrary",)), name="moe",
    )(block_e, n_used, xs, w_gate_up, b_gate_up[:, None, :], w_down, b_down[:, None, :])


def _combine_kernel(h_ref, g_ref, y_ref, o_ref):
    g = g_ref[...]
    acc = y_ref[0] * g[:, 0:1]
    for k in range(1, TOP_K):
        acc = acc + y_ref[k] * g[:, k:k + 1]
    o_ref[...] = h_ref[...] + acc


def _combine(h, gates, y4, block0, rows, tm):
    row = lambda i: (block0 + i, 0)
    return pl.pallas_call(
        _combine_kernel, grid=(rows // tm,),
        in_specs=[pl.BlockSpec((tm, D_MODEL), row), pl.BlockSpec((tm, LANES), row),
                  pl.BlockSpec((TOP_K, tm, D_MODEL), lambda i: (0, block0 + i, 0))],
        out_specs=pl.BlockSpec((tm, D_MODEL), lambda i: (i, 0)),
        out_shape=jax.ShapeDtypeStruct((rows, D_MODEL), F32),
        compiler_params=_params(("arbitrary",)), name="combine",
    )(h, gates, y4)


def _route(top_i, blk):
    t = top_i.shape[0]
    a = t * TOP_K
    flat_e = top_i.reshape(a)
    order = jnp.argsort(flat_e)
    tok_sorted = (order // TOP_K).astype(jnp.int32)
    experts = jnp.arange(N_EXPERTS, dtype=jnp.int32)
    hit = (top_i[:, :, None] == experts[None, None, :]).astype(jnp.int32)
    per_tok = hit.sum(axis=1)
    before = jnp.cumsum(per_tok, axis=0) - per_tok
    counts = per_tok.sum(axis=0)
    padded = (counts + blk - 1) // blk * blk
    start = jnp.cumsum(counts) - counts
    pend = jnp.cumsum(padded)
    pstart = pend - padded
    dest = ((before + pstart[None, :])[:, None, :] * hit).sum(axis=2)
    n_blocks = -(-a // blk) + N_EXPERTS
    block_e = jnp.minimum((pend[None, :] <= (jnp.arange(n_blocks, dtype=jnp.int32) * blk)[:, None]).sum(axis=1),
                          N_EXPERTS - 1).astype(jnp.int32)
    slot = jnp.arange(n_blocks * blk, dtype=jnp.int32)
    slot_e = jnp.repeat(block_e, blk)
    r = slot - pstart[slot_e]
    slot_tok = jnp.where(r < counts[slot_e], tok_sorted[jnp.minimum(start[slot_e] + r, a - 1)], 0)
    n_used = (pend[-1:] // blk).astype(jnp.int32)
    return slot_tok, dest.astype(jnp.int32), block_e, n_used


def _rope_tables(pos):
    pos = pos.astype(F32)[:, None]
    n = pos.shape[0]
    hr = RET_DK // 2
    ang = pos * (RET_THETA ** (-jnp.arange(hr, dtype=F32) / hr))[None, :]
    cos, sin = jnp.cos(ang), jnp.sin(ang)
    zr = jnp.zeros((n, LANES - RET_DK), F32)
    cr = jnp.concatenate([cos, cos, zr], axis=1)
    sr = jnp.concatenate([-sin, sin, zr], axis=1)
    hm = MLA_ROPE // 2
    ang = pos * (MLA_THETA ** (-jnp.arange(hm, dtype=F32) / hm))[None, :]
    cos, sin = jnp.cos(ang), jnp.sin(ang)
    cm = jnp.concatenate([cos, cos, jnp.ones((n, LANES - MLA_ROPE), F32)], axis=1)
    sma = jnp.concatenate([jnp.zeros((n, hm), F32), sin, jnp.zeros((n, LANES - MLA_ROPE), F32)], axis=1)
    smb = jnp.concatenate([-sin, jnp.zeros((n, LANES - hm), F32)], axis=1)
    return [cr, sr, cm, sma, smb]


def _rope_tables_t(pos):
    hm = MLA_ROPE // 2
    ang = pos.astype(F32)[:, None] * (MLA_THETA ** (-jnp.arange(hm, dtype=F32) / hm))[None, :]
    return jnp.cos(ang).T, jnp.sin(ang).T


def _decay_tables(log_gamma, c, n_rep):
    idx = jnp.arange(c, dtype=F32)
    diff = idx[:, None] - idx[None, :]
    decay = jnp.where(diff >= 0, jnp.exp(jnp.maximum(diff, 0.0)[None] * log_gamma[:, None, None]), 0.0)
    if n_rep > 1:
        eye = jnp.eye(n_rep, dtype=F32)
        decay = (eye[None, :, None, :, None] * decay[:, None, :, None, :]).reshape(RET_HEADS, n_rep * c, n_rep * c)
    rowd = jnp.tile(jnp.exp((idx + 1.0)[:, None] * log_gamma[None, :]), (n_rep, 1))
    cold = jnp.tile(jnp.exp((c - 1.0 - idx)[:, None] * log_gamma[None, :]), (n_rep, 1))
    cpow = jnp.broadcast_to(jnp.exp(c * log_gamma)[None, :], (LANES, RET_HEADS))
    return {'dec': decay, 'rowd': rowd, 'cold': cold, 'cpow': cpow}


def _pad_heads(wm, n_heads, width, offset):
    k = wm.shape[0]
    wm = wm.reshape(k, n_heads, width)
    out = jnp.zeros((k, n_heads, LANES), wm.dtype).at[:, :, offset:offset + width].set(wm)
    return out.reshape(k, n_heads * LANES)


def _prep_weights(norm_attn_g, w_in, ret_out_g, q_a_norm_g, w_q_b, kv_a_norm_g, w_uk, w_uv,
                  qk_norm_q_g, qk_norm_k_g, w_out, norm_ffn_g, w_router, b_router):
    splits = [RET_HEADS * RET_DK, RET_HEADS * RET_DK, RET_HEADS * RET_DV, RET_HEADS * RET_DV, Q_LORA, KV_LORA,
              MLA_ROPE]
    offs = [0]
    for s in splits:
        offs.append(offs[-1] + s)
    part = [w_in[:, offs[i]:offs[i + 1]] for i in range(len(splits))]
    kpe_cols = jnp.zeros((D_MODEL, LANES), F32).at[:, :MLA_ROPE].set(part[6])
    win = jnp.concatenate([_pad_heads(part[0], RET_HEADS, RET_DK, 0), _pad_heads(part[1], RET_HEADS, RET_DK, 0),
                           part[2], part[3], part[4], part[5], kpe_cols], axis=1).astype(BF16)

    def mla_cols(wm):
        k = wm.shape[0]
        wm = wm.reshape(k, MLA_HEADS, MLA_QK)
        wm = jnp.concatenate([wm[:, :, MLA_NOPE:], wm[:, :, :MLA_NOPE]], axis=2)
        return _pad_heads(wm.reshape(k, MLA_HEADS * MLA_QK), MLA_HEADS, MLA_QK, 0)

    def mla_gain(g):
        g = jnp.concatenate([g[MLA_NOPE:], g[:MLA_NOPE], jnp.zeros((LANES - MLA_QK,), F32)])
        return g[None, :]

    gk_nope = qk_norm_k_g[:MLA_NOPE]
    uk_heads = w_uk.reshape(KV_LORA, MLA_HEADS, MLA_NOPE)
    wabs = jnp.zeros((MLA_HEADS, HEAD_PAD, KV_LORA), F32).at[:, MLA_ROPE:MLA_ROPE + MLA_NOPE, :].set(
        (uk_heads * gk_nope[None, None, :]).transpose(1, 2, 0))
    head_of_col = jnp.arange(MLA_HEADS * MLA_V) // MLA_V
    uv_blk = jnp.where(head_of_col[None, None, :] == jnp.arange(MLA_HEADS)[:, None, None], w_uv[None], 0.0)
    wr = jnp.zeros((D_MODEL, LANES), F32).at[:, :N_EXPERTS].set(w_router)
    wr_hi = wr.astype(BF16)
    return {
        'gattn': norm_attn_g[None, :], 'win': win, 'gqa': q_a_norm_g[None, :], 'wqb': mla_cols(w_q_b).astype(BF16),
        'gq': mla_gain(qk_norm_q_g), 'gkv': kv_a_norm_g[None, :],
        'wuk': _pad_heads(w_uk, MLA_HEADS, MLA_NOPE, MLA_ROPE).astype(BF16), 'gk': mla_gain(qk_norm_k_g),
        'wuv': w_uv.astype(BF16), 'gret': ret_out_g[None, :],
        'wabs': wabs.astype(BF16), 'uk_t': w_uk.T.astype(BF16), 'uv_blk': uv_blk.astype(BF16),
        'gpe': qk_norm_k_g[MLA_NOPE:, None],
        'wo': w_out.astype(BF16), 'gffn': norm_ffn_g[None, :], 'wr_hi': wr_hi,
        'wr_lo': (wr - wr_hi.astype(F32)).astype(BF16),
        'br': jnp.full((1, LANES), NEG_BIG, F32).at[0, :N_EXPERTS].set(b_router),
    }


def _pad_rows(a, rows):
    return jnp.zeros((rows,) + a.shape[1:], a.dtype).at[:a.shape[0]].set(a)


def _largest_divisor(n, cap):
    d = min(n, cap)
    while n % d:
        d -= 1
    return d


def kernel(x_prompt, x_sample, cache_latent, cache_krope, state_retention, page_table, meta_tokens, norm_attn_g, w_in, ret_out_g, q_a_norm_g, w_q_b, kv_a_norm_g, w_uk, w_uv, qk_norm_q_g, qk_norm_k_g, w_out, norm_ffn_g, w_router, b_router, w_gate_up, b_gate_up, w_down, b_down):
    assert w_in.shape[0] == 1, "single-layer trunk"
    batch, seq, _ = x_prompt.shape
    n_seq, n_tok, _ = x_sample.shape
    n_pages, page = page_table.shape[1], cache_latent.shape[2]
    past = n_pages * page
    assert seq % RET_CHUNK == 0 and LANES % n_tok == 0 and (n_seq * n_tok) % LANES == 0 and page == LANES
    w = _prep_weights(norm_attn_g[0], w_in[0], ret_out_g[0], q_a_norm_g[0], w_q_b[0], kv_a_norm_g[0], w_uk[0],
                      w_uv[0], qk_norm_q_g[0], qk_norm_k_g[0], w_out[0], norm_ffn_g[0], w_router[0], b_router[0])
    log_gamma = jnp.log1p(-jnp.exp2(-5.0 - jnp.arange(RET_HEADS, dtype=F32)))

    rows_p = batch * seq
    rows_s = n_seq * n_tok
    tm = _largest_divisor(min(seq, rows_s), ROW_BLOCK)
    nb_seq = seq // tm
    x = jnp.concatenate([x_prompt.reshape(rows_p, D_MODEL), x_sample.reshape(rows_s, D_MODEL)], axis=0)
    pos_rows = jnp.concatenate([N_META + jnp.arange(seq), jnp.tile(past + jnp.arange(n_tok), tm // n_tok)])
    tabs = _rope_tables(pos_rows)
    n_pb = rows_p // tm
    rq, rk, rv, zg, mq, ckv, kpe, k, v = _front(
        x, tabs, lambda i: jnp.where(i < n_pb, i % nb_seq, nb_seq), w, tm)
    _, mrk, mrv, _, _, mckv, mkpe, mk, mv = _front(
        meta_tokens, _rope_tables(jnp.arange(N_META)), lambda i: i, w, N_META)

    dt_p = _decay_tables(log_gamma, RET_CHUNK, 1)
    dt_p['mcol'] = _pad_rows(jnp.exp((N_META - 1.0 - jnp.arange(N_META, dtype=F32))[:, None] * log_gamma[None, :]),
                             RET_CHUNK)
    gret = w['gret']
    r_p, st_p = _ret_prompt(rq, rk, rv, zg, _pad_rows(mrk, RET_CHUNK), _pad_rows(mrv, RET_CHUNK), dt_p, gret,
                            batch, seq // RET_CHUNK)
    dt_s = _decay_tables(log_gamma, n_tok, LANES // n_tok)
    r_s, st_s = _ret_sample(rq[rows_p:], rk[rows_p:], rv[rows_p:], zg[rows_p:], state_retention[0], dt_s, gret, n_tok)

    blk = _largest_divisor(seq, ATTN_BLOCK)
    m_p = _attn_prompt(mq, k, v, _pad_rows(mk, LANES), _pad_rows(mv, LANES), batch, seq, blk)
    tm_s = _largest_divisor(rows_s, ROW_BLOCK)
    qt, qf = _absorb(mq, w['wabs'], rows_p // tm_s, rows_s, tm_s)
    n_pages_step = _largest_divisor(n_pages, PAGES_PER_STEP)
    cos_t, sin_t = _rope_tables_t(jnp.arange(past))
    cosn_t, sinn_t = _rope_tables_t(past + jnp.arange(page))
    tabs_s = {'cos_t': cos_t, 'sin_t': sin_t, 'cosn_t': cosn_t, 'sinn_t': sinn_t}
    m_s = _attn_sample(page_table, cache_latent, jnp.swapaxes(cache_krope, 2, 3), tabs_s, qt, qf, ckv[rows_p:],
                       kpe[rows_p:], w, n_tok, n_pages_step, _largest_divisor(n_pages_step, PAGES_PER_CHAIN))

    r_all = jnp.concatenate([r_p, r_s], axis=0)
    m_all = jnp.concatenate([m_p, m_s.astype(BF16)], axis=0)
    h1, xn2, top_i, gates = _post(r_all, m_all, x, w, tm)

    slot_tok, dest, block_e, n_used = _route(top_i[:, :TOP_K], EXPERT_BLOCK)
    outs = _moe_blocks(block_e, n_used, xn2[slot_tok], w_gate_up[0], b_gate_up[0], w_down[0], b_down[0],
                       EXPERT_BLOCK)
    y4 = outs[dest.T.reshape(-1)].reshape(TOP_K, rows_p + rows_s, D_MODEL)
    y_prompt = _combine(h1, gates, y4, 0, rows_p, tm).reshape(batch, seq, D_MODEL)
    y_sample = _combine(h1, gates, y4, n_pb, rows_s, tm).reshape(n_seq, n_tok, D_MODEL)
    lat_p = jnp.concatenate([jnp.broadcast_to(mckv[None], (batch, N_META, KV_LORA)),
                             ckv[:rows_p].reshape(batch, seq, KV_LORA)], axis=1)[None]
    kpe_p = jnp.concatenate([jnp.broadcast_to(mkpe[None], (batch, N_META, MLA_ROPE)),
                             kpe[:rows_p].reshape(batch, seq, MLA_ROPE)], axis=1)[None]
    return (y_prompt, y_sample, lat_p, kpe_p, st_p[None],
            ckv[rows_p:].reshape(n_seq, n_tok, KV_LORA)[None], kpe[rows_p:].reshape(n_seq, n_tok, MLA_ROPE)[None],
            st_s[None])
```

```python
import functools

import jax
import jax.numpy as jnp
from jax import lax
from jax.experimental import pallas as pl
from jax.experimental.pallas import tpu as pltpu

F32 = jnp.float32
BF16 = jnp.bfloat16

D_MODEL = 1024
N_META = 16
RET_HEADS = 4
RET_DK = 64
RET_DV = 128
RET_CHUNK = 128
RET_THETA = 10000.0
MLA_HEADS = 8
MLA_NOPE = 64
MLA_ROPE = 32
MLA_QK = MLA_NOPE + MLA_ROPE
MLA_V = 64
Q_LORA = 384
KV_LORA = 256
MLA_THETA = 10000.0
MLA_SCALE = MLA_QK ** -0.5
LOG2E = 1.4426950408889634
N_EXPERTS = 32
TOP_K = 4
D_FF = 1024
SWIGLU_LIMIT = 7.0
SWIGLU_ALPHA = 1.702
EPS = 1e-6
NEG_BIG = -1e30

LANES = 128
HEAD_PAD = LANES
VMEM_LIMIT = 56 * 1024 * 1024
ROW_BLOCK = 256
ATTN_BLOCK = 1024
PAGES_PER_STEP = 16
PAGES_PER_CHAIN = 2
EXPERT_BLOCK = 256

_OFF_RQ = 0
_OFF_RK = _OFF_RQ + RET_HEADS * LANES
_OFF_RV = _OFF_RK + RET_HEADS * LANES
_OFF_ZG = _OFF_RV + RET_HEADS * RET_DV
_OFF_CQ = _OFF_ZG + RET_HEADS * RET_DV
_OFF_CKV = _OFF_CQ + Q_LORA
_OFF_KPE = _OFF_CKV + KV_LORA
IN_PAD = _OFF_KPE + LANES


def _params(sem):
    return pltpu.CompilerParams(dimension_semantics=sem, vmem_limit_bytes=VMEM_LIMIT)


def _rms(x, g):
    return x * lax.rsqrt(jnp.mean(x * x, axis=-1, keepdims=True) + EPS) * g


def _dot(a, b):
    return jnp.dot(a, b, preferred_element_type=F32)


def _dot_nt(a, b):
    return lax.dot_general(a, b, (((1,), (1,)), ((), ())), preferred_element_type=F32)


def _dot_tn(a, b):
    return lax.dot_general(a, b, (((0,), (0,)), ((), ())), preferred_element_type=F32)


def _front_kernel(x_ref, gattn_ref, win_ref, cr_ref, sr_ref, cm_ref, sma_ref, smb_ref,
                  gqa_ref, wqb_ref, gq_ref, gkv_ref, wuk_ref, gk_ref, wuv_ref,
                  rq_ref, rk_ref, rv_ref, zg_ref, mq_ref, ckv_ref, kpe_ref, k_ref, v_ref):
    tm = x_ref.shape[0]
    xn = _rms(x_ref[...], gattn_ref[...])
    z = _dot(xn.astype(BF16), win_ref[...])

    lane = lax.broadcasted_iota(jnp.int32, (tm, LANES), 1)
    upper = (lane & (RET_DK // 2)) != 0
    cr, sr = cr_ref[...], sr_ref[...]

    def rope_ret(t):
        partner = jnp.where(upper, pltpu.roll(t, RET_DK // 2, 1), pltpu.roll(t, LANES - RET_DK // 2, 1))
        return t * cr + partner * sr

    for h in range(RET_HEADS):
        sl = slice(h * LANES, (h + 1) * LANES)
        rq_ref[:, sl] = rope_ret(z[:, _OFF_RQ + h * LANES:_OFF_RQ + (h + 1) * LANES])
        rk_ref[:, sl] = rope_ret(z[:, _OFF_RK + h * LANES:_OFF_RK + (h + 1) * LANES]) * (RET_DK ** -0.5)
    rv_ref[...] = z[:, _OFF_RV:_OFF_ZG].astype(BF16)
    zg_ref[...] = z[:, _OFF_ZG:_OFF_CQ]

    cm, sma, smb = cm_ref[...], sma_ref[...], smb_ref[...]

    def rope_mla(t):
        half = MLA_ROPE // 2
        return t * cm + pltpu.roll(t, half, 1) * sma + pltpu.roll(t, LANES - half, 1) * smb

    def head_norm(t, g):
        ms = jnp.sum(t * t, axis=-1, keepdims=True) * (1.0 / MLA_QK)
        return t * lax.rsqrt(ms + EPS) * g

    cq = _rms(z[:, _OFF_CQ:_OFF_CKV], gqa_ref[...])
    q = _dot(cq.astype(BF16), wqb_ref[...])
    gq = gq_ref[...]
    for h in range(MLA_HEADS):
        sl = slice(h * HEAD_PAD, (h + 1) * HEAD_PAD)
        mq_ref[:, sl] = (rope_mla(head_norm(q[:, sl], gq)) * (MLA_SCALE * LOG2E)).astype(BF16)

    ckv = _rms(z[:, _OFF_CKV:_OFF_KPE], gkv_ref[...])
    ckv_ref[...] = ckv
    kpe_slab = z[:, _OFF_KPE:IN_PAD]
    kpe_ref[...] = kpe_slab[:, :MLA_ROPE]
    ckv_b = ckv.astype(BF16)
    kn = _dot(ckv_b, wuk_ref[...])
    gk = gk_ref[...]
    for h in range(MLA_HEADS):
        sl = slice(h * HEAD_PAD, (h + 1) * HEAD_PAD)
        k_ref[:, sl] = rope_mla(head_norm(kn[:, sl] + kpe_slab, gk)).astype(BF16)
    v_ref[...] = _dot(ckv_b, wuv_ref[...]).astype(BF16)


def _front(x, tabs, tab_index, w, tm):
    rows = x.shape[0]
    grid = (rows // tm,)
    row = lambda i: (i, 0)
    const = lambda i: (0, 0)
    tab = lambda i: (tab_index(i), 0)

    def full(a):
        return pl.BlockSpec(a.shape, const)

    in_specs = [pl.BlockSpec((tm, D_MODEL), row), full(w['gattn']), full(w['win'])]
    in_specs += [pl.BlockSpec((tm, LANES), tab)] * 5
    in_specs += [full(w[n]) for n in ('gqa', 'wqb', 'gq', 'gkv', 'wuk', 'gk', 'wuv')]
    widths = [(RET_HEADS * LANES, F32), (RET_HEADS * LANES, F32), (RET_HEADS * RET_DV, BF16),
              (RET_HEADS * RET_DV, F32), (MLA_HEADS * HEAD_PAD, BF16), (KV_LORA, F32), (MLA_ROPE, F32),
              (MLA_HEADS * HEAD_PAD, BF16), (MLA_HEADS * MLA_V, BF16)]
    out_shape = [jax.ShapeDtypeStruct((rows, n), dt) for n, dt in widths]
    out_specs = [pl.BlockSpec((tm, n), row) for n, _ in widths]
    return pl.pallas_call(
        _front_kernel, grid=grid, in_specs=in_specs, out_specs=out_specs, out_shape=out_shape,
        compiler_params=_params(("arbitrary",)), name="front",
    )(x, w['gattn'], w['win'], *tabs, w['gqa'], w['wqb'], w['gq'], w['gkv'], w['wuk'], w['gk'], w['wuv'])


def _ret_gate(o, zg, g):
    on = o * lax.rsqrt(jnp.mean(o * o, axis=-1, keepdims=True) + EPS) * g
    return (zg * jax.nn.sigmoid(zg)) * on


def _ret_prompt_kernel(q_ref, k_ref, v_ref, zg_ref, mk_ref, mv_ref, mcol_ref, dec_ref, rowd_ref, cold_ref,
                       cpow_ref, g_ref, r_ref, s_ref, s_scr):
    c = pl.program_id(1)

    @pl.when(c == 0)
    def _():
        for h in range(RET_HEADS):
            sl = slice(h * LANES, (h + 1) * LANES)
            kw = mk_ref[:, sl] * mcol_ref[:, h:h + 1]
            s_scr[h] = _dot_tn(kw.astype(BF16), mv_ref[:, sl])

    for h in range(RET_HEADS):
        sl = slice(h * LANES, (h + 1) * LANES)
        q = q_ref[:, sl]
        k = k_ref[:, sl]
        v = v_ref[:, sl]
        qb = q.astype(BF16)
        s0 = s_scr[h]
        scores = _dot_nt(qb, k.astype(BF16)) * dec_ref[h]
        inner = _dot(scores.astype(BF16), v)
        cross = _dot(qb, s0.astype(BF16)) * rowd_ref[:, h:h + 1]
        kw = k * cold_ref[:, h:h + 1]
        s_new = s0 * cpow_ref[:, h:h + 1] + _dot_tn(kw.astype(BF16), v)
        s_scr[h] = s_new
        r_ref[:, sl] = _ret_gate(inner + cross, zg_ref[:, sl], g_ref[:, sl]).astype(BF16)

    @pl.when(c == pl.num_programs(1) - 1)
    def _():
        for h in range(RET_HEADS):
            s_ref[0, h] = s_scr[h, :RET_DK, :]


def _ret_prompt(rq, rk, rv, zg, mk, mv, tabs, g, batch, n_chunks):
    cs = RET_CHUNK
    row = lambda b, c: (b * n_chunks + c, 0)
    const2 = lambda b, c: (0, 0)
    w4 = RET_HEADS * LANES
    in_specs = [pl.BlockSpec((cs, w4), row), pl.BlockSpec((cs, w4), row), pl.BlockSpec((cs, w4), row),
                pl.BlockSpec((cs, w4), row),
                pl.BlockSpec((cs, w4), const2), pl.BlockSpec((cs, w4), const2),
                pl.BlockSpec((cs, RET_HEADS), const2),
                pl.BlockSpec((RET_HEADS, cs, cs), lambda b, c: (0, 0, 0)),
                pl.BlockSpec((cs, RET_HEADS), const2), pl.BlockSpec((cs, RET_HEADS), const2),
                pl.BlockSpec((LANES, RET_HEADS), const2), pl.BlockSpec((1, w4), const2)]
    out_shape = [jax.ShapeDtypeStruct((batch * n_chunks * cs, w4), BF16),
                 jax.ShapeDtypeStruct((batch, RET_HEADS, RET_DK, RET_DV), F32)]
    out_specs = [pl.BlockSpec((cs, w4), row),
                 pl.BlockSpec((1, RET_HEADS, RET_DK, RET_DV), lambda b, c: (b, 0, 0, 0))]
    return pl.pallas_call(
        _ret_prompt_kernel, grid=(batch, n_chunks), in_specs=in_specs, out_specs=out_specs, out_shape=out_shape,
        scratch_shapes=[pltpu.VMEM((RET_HEADS, LANES, RET_DV), F32)],
        compiler_params=_params(("arbitrary", "arbitrary")), name="ret_prompt",
    )(rq, rk, rv, zg, mk, mv, tabs['mcol'], tabs['dec'], tabs['rowd'], tabs['cold'], tabs['cpow'], g)


def _ret_sample_kernel(q_ref, k_ref, v_ref, zg_ref, s0_ref, dec_ref, rowd_ref, cold_ref, cpow_ref, g_ref,
                       r_ref, s_ref, *, n_seq, n_tok):
    rows = n_seq * n_tok
    ri = lax.broadcasted_iota(jnp.int32, (rows, 1), 0)
    for h in range(RET_HEADS):
        sl = slice(h * LANES, (h + 1) * LANES)
        q = q_ref[:, sl]
        k = k_ref[:, sl]
        v = v_ref[:, sl]
        qb = q.astype(BF16)
        scores = _dot_nt(qb, k.astype(BF16)) * dec_ref[h]
        inner = _dot(scores.astype(BF16), v)
        kw = k * cold_ref[:, h:h + 1]
        cross = jnp.zeros((rows, RET_DV), F32)
        for s in range(n_seq):
            s0 = s0_ref[s, h]
            mine = (ri >= s * n_tok) & (ri < (s + 1) * n_tok)
            cross = cross + jnp.where(mine, _dot(qb[:, :RET_DK], s0.astype(BF16)), 0.0)
            upd = _dot_tn(jnp.where(mine, kw, 0.0).astype(BF16), v)
            s_ref[s, h] = s0 * cpow_ref[:RET_DK, h:h + 1] + upd[:RET_DK]
        cross = cross * rowd_ref[:, h:h + 1]
        r_ref[:, sl] = _ret_gate(inner + cross, zg_ref[:, sl], g_ref[:, sl]).astype(BF16)


def _ret_sample(rq, rk, rv, zg, state, tabs, g, n_tok):
    n_seq_total = state.shape[0]
    n_seq = LANES // n_tok
    rows = n_seq * n_tok
    w4 = RET_HEADS * LANES
    row = lambda i: (i, 0)
    const = lambda i: (0, 0)
    in_specs = [pl.BlockSpec((rows, w4), row)] * 4
    in_specs += [pl.BlockSpec((n_seq, RET_HEADS, RET_DK, RET_DV), lambda i: (i, 0, 0, 0)),
                 pl.BlockSpec((RET_HEADS, rows, rows), lambda i: (0, 0, 0)),
                 pl.BlockSpec((rows, RET_HEADS), const), pl.BlockSpec((rows, RET_HEADS), const),
                 pl.BlockSpec((LANES, RET_HEADS), const), pl.BlockSpec((1, w4), const)]
    out_shape = [jax.ShapeDtypeStruct((n_seq_total * n_tok, w4), BF16),
                 jax.ShapeDtypeStruct(state.shape, F32)]
    out_specs = [pl.BlockSpec((rows, w4), row),
                 pl.BlockSpec((n_seq, RET_HEADS, RET_DK, RET_DV), lambda i: (i, 0, 0, 0))]
    return pl.pallas_call(
        functools.partial(_ret_sample_kernel, n_seq=n_seq, n_tok=n_tok),
        grid=(n_seq_total // n_seq,), in_specs=in_specs, out_specs=out_specs, out_shape=out_shape,
        compiler_params=_params(("arbitrary",)), name="ret_sample",
    )(rq, rk, rv, zg, state, tabs['dec'], tabs['rowd'], tabs['cold'], tabs['cpow'], g)


def _attn_prompt_kernel(q_ref, k_ref, v_ref, km_ref, vm_ref, o_ref, m_scr, l_scr, acc_scr, *, sub_tiles):
    qi = pl.program_id(2)
    ki = pl.program_id(3)
    tm, tk = q_ref.shape[0], k_ref.shape[0]

    @pl.when(ki == 0)
    def _():
        lane = lax.broadcasted_iota(jnp.int32, (tm, km_ref.shape[0]), 1)
        for hh in range(2):
            sl = slice(hh * HEAD_PAD, (hh + 1) * HEAD_PAD)
            s = jnp.where(lane < N_META, _dot_nt(q_ref[:, sl], km_ref[:, sl]), NEG_BIG)
            m = jnp.max(s, axis=-1, keepdims=True)
            p = jnp.exp2(s - m)
            m_scr[hh] = jnp.broadcast_to(m, (tm, LANES))
            l_scr[hh] = jnp.broadcast_to(jnp.sum(p, axis=-1, keepdims=True), (tm, LANES))
            acc_scr[hh] = _dot(p.astype(BF16), vm_ref[...])

    def tile(r0, nr, c0, nc, masked):
        rows = pl.ds(r0, nr)
        if masked:
            keep = (lax.broadcasted_iota(jnp.int32, (nr, nc), 1) <= lax.broadcasted_iota(jnp.int32, (nr, nc), 0))
        for hh in range(2):
            sl = slice(hh * HEAD_PAD, (hh + 1) * HEAD_PAD)
            s = _dot_nt(q_ref[rows, sl], k_ref[pl.ds(c0, nc), sl])
            if masked:
                s = jnp.where(keep, s, NEG_BIG)
            m_old = m_scr[hh, rows, :]
            m_new = jnp.maximum(m_old, jnp.max(s, axis=-1, keepdims=True))
            alpha = jnp.exp2(m_old - m_new)
            p = jnp.exp2(s - jnp.tile(m_new, (1, nc // LANES)))
            m_scr[hh, rows, :] = m_new
            l_scr[hh, rows, :] = alpha * l_scr[hh, rows, :] + jnp.sum(p, axis=-1, keepdims=True)
            acc_scr[hh, rows, :] = alpha * acc_scr[hh, rows, :] + _dot(p.astype(BF16), v_ref[pl.ds(c0, nc), :])

    @pl.when(ki < qi)
    def _():
        tile(0, tm, 0, tk, False)

    @pl.when(ki == qi)
    def _():
        ns = sub_tiles
        st = tm // ns
        for a in range(ns):
            if a > 0:
                tile(a * st, st, 0, a * st, False)
            tile(a * st, st, a * st, st, True)
        lane = lax.broadcasted_iota(jnp.int32, (tm, 2 * MLA_V), 1)
        o0 = acc_scr[0] / l_scr[0]
        o1 = acc_scr[1] / l_scr[1]
        o_ref[...] = jnp.where(lane < MLA_V, o0, o1).astype(BF16)


def _attn_prompt(mq, k, v, km, vm, batch, seq, blk):
    nb = seq // blk
    pairs = MLA_HEADS // 2
    grid = (batch, pairs, nb, nb)
    in_specs = [pl.BlockSpec((blk, 2 * HEAD_PAD), lambda b, h, qi, ki: (b * nb + qi, h)),
                pl.BlockSpec((blk, 2 * HEAD_PAD), lambda b, h, qi, ki: (b * nb + jnp.minimum(ki, qi), h)),
                pl.BlockSpec((blk, 2 * MLA_V), lambda b, h, qi, ki: (b * nb + jnp.minimum(ki, qi), h)),
                pl.BlockSpec((km.shape[0], 2 * HEAD_PAD), lambda b, h, qi, ki: (0, h)),
                pl.BlockSpec((vm.shape[0], 2 * MLA_V), lambda b, h, qi, ki: (0, h))]
    out_specs = pl.BlockSpec((blk, 2 * MLA_V), lambda b, h, qi, ki: (b * nb + qi, h))
    return pl.pallas_call(
        functools.partial(_attn_prompt_kernel, sub_tiles=2 if blk % (2 * LANES) == 0 else 1), grid=grid, in_specs=in_specs, out_specs=out_specs,
        out_shape=jax.ShapeDtypeStruct((batch * seq, MLA_HEADS * MLA_V), BF16),
        scratch_shapes=[pltpu.VMEM((2, blk, LANES), F32), pltpu.VMEM((2, blk, LANES), F32),
                        pltpu.VMEM((2, blk, 2 * MLA_V), F32)],
        compiler_params=_params(("arbitrary",) * 4), name="attn_prompt",
    )(mq, k, v, km, vm)


def _absorb_kernel(mq_ref, wabs_ref, qt_ref, qf_ref):
    q = mq_ref[...]
    qf_ref[...] = q.astype(F32)
    for h in range(MLA_HEADS):
        qt_ref[:, h * KV_LORA:(h + 1) * KV_LORA] = _dot(q[:, h * HEAD_PAD:(h + 1) * HEAD_PAD], wabs_ref[h])


def _absorb(mq, wabs, row0_blocks, rows, tm):
    return pl.pallas_call(
        _absorb_kernel, grid=(rows // tm,),
        in_specs=[pl.BlockSpec((tm, MLA_HEADS * HEAD_PAD), lambda i: (row0_blocks + i, 0)),
                  pl.BlockSpec(wabs.shape, lambda i: (0, 0, 0))],
        out_specs=[pl.BlockSpec((tm, MLA_HEADS * KV_LORA), lambda i: (i, 0)),
                   pl.BlockSpec((tm, MLA_HEADS * HEAD_PAD), lambda i: (i, 0))],
        out_shape=[jax.ShapeDtypeStruct((rows, MLA_HEADS * KV_LORA), F32),
                   jax.ShapeDtypeStruct((rows, MLA_HEADS * HEAD_PAD), F32)],
        compiler_params=_params(("arbitrary",)), name="absorb",
    )(mq, wabs)


def _attn_sample_kernel(pt_ref, lat_hbm, kpe_hbm, cos_ref, sin_ref, qt_ref, qf_ref, cn_ref, kn_ref, cosn_ref,
                        sinn_ref, lw_ref, wuv_ref, gpe_ref, o_ref, l_scr, qpe_scr, m_scr, d_scr, acc_scr, kpad_scr,
                        cpad_scr, lat_buf, kpe_buf, sem, *, n_pages, sub, n_tok, page):
    s_id = pl.program_id(0)
    j = pl.program_id(1)
    n_steps = pl.num_programs(1)
    step = s_id * n_steps + j
    last = pl.num_programs(0) * n_steps - 1
    slot = lax.rem(step, 2)
    nq = MLA_HEADS * n_tok
    n_up = MLA_HEADS * MLA_NOPE

    def page_copies(seq, st, sl, p):
        pid = pt_ref[seq, st * n_pages + p]
        return (pltpu.make_async_copy(lat_hbm.at[0, pid], lat_buf.at[sl, p], sem.at[sl, 0]),
                pltpu.make_async_copy(kpe_hbm.at[0, pid], kpe_buf.at[sl, p], sem.at[sl, 1]))

    def start_pages(seq, st, sl):
        for p in range(n_pages):
            for c in page_copies(seq, st, sl, p):
                c.start()

    def wait_pages(seq, st, sl):
        for p in range(n_pages):
            for c in page_copies(seq, st, sl, p):
                c.wait()

    @pl.when(step == 0)
    def _():
        start_pages(0, 0, 0)

    wait_pages(s_id, j, slot)
    wrap = j == n_steps - 1
    nxt_seq = jnp.where(wrap, jnp.where(step == last, 0, s_id + 1), s_id)
    nxt_j = jnp.where(wrap, 0, j + 1)
    start_pages(nxt_seq, nxt_j, 1 - slot)
    lat_refs = [lat_buf.at[slot, p] for p in range(n_pages)]
    kpe_refs = [kpe_buf.at[slot, p] for p in range(n_pages)]

    @pl.when(j == 0)
    def _():
        l_scr[:n_up, :] = lw_ref[...]
        qt = qt_ref[...]
        qf = qf_ref[...]
        l_scr[n_up:, :] = jnp.concatenate(
            [qt[:, h * KV_LORA:(h + 1) * KV_LORA] for h in range(MLA_HEADS)], axis=0).astype(BF16)
        lane = lax.broadcasted_iota(jnp.int32, (nq, HEAD_PAD), 1)
        qpe = jnp.concatenate([qf[:, h * HEAD_PAD:(h + 1) * HEAD_PAD] for h in range(MLA_HEADS)], axis=0)
        qpe_scr[...] = jnp.where(lane < MLA_ROPE, qpe, 0.0).astype(BF16)
        m_scr[...] = jnp.full(m_scr.shape, NEG_BIG, F32)
        d_scr[...] = jnp.zeros(d_scr.shape, F32)
        acc_scr[...] = jnp.zeros(acc_scr.shape, F32)

    def scores(cb, kpe_t, cos_t, sin_t):
        tk = cb.shape[0]
        big = _dot_nt(l_scr[...], cb)
        k_t = big[:n_up]
        ss = jnp.sum((k_t * k_t).reshape(MLA_HEADS, MLA_NOPE, tk), axis=1)
        ss = ss + jnp.sum(kpe_t * kpe_t, axis=0, keepdims=True)
        rs = lax.rsqrt(ss * (1.0 / MLA_QK) + EPS)
        kg = kpe_t * gpe_ref[...]
        half = MLA_ROPE // 2
        x1, x2 = kg[:half], kg[half:]
        rot = jnp.concatenate([x1 * cos_t - x2 * sin_t, x1 * sin_t + x2 * cos_t], axis=0)
        pe = _dot(qpe_scr[:, :MLA_ROPE], rot.astype(BF16))
        rs_rows = jnp.concatenate([jnp.broadcast_to(rs[h:h + 1, :], (n_tok, tk)) for h in range(MLA_HEADS)], axis=0)
        return (big[n_up:] + pe) * rs_rows

    def update(s_list, cb_list):
        m_old = m_scr[...]
        m_new = m_old
        for s in s_list:
            m_new = jnp.maximum(m_new, jnp.max(s, axis=-1, keepdims=True))
        alpha = jnp.exp2(m_old - m_new)
        d = alpha * d_scr[...]
        acc = alpha * acc_scr[...]
        for s, cb in zip(s_list, cb_list):
            p = jnp.exp2(s - m_new)
            d = d + jnp.sum(p, axis=-1, keepdims=True)
            acc = acc + _dot(p.astype(BF16), cb)
        m_scr[...] = m_new
        d_scr[...] = d
        acc_scr[...] = acc

    s_list, cb_list = [], []
    for g in range(n_pages // sub):
        pages = range(g * sub, (g + 1) * sub)
        cb = jnp.concatenate([lat_refs[b][...] for b in pages], axis=0).astype(BF16)
        kpe_t = jnp.concatenate([kpe_refs[b][...] for b in pages], axis=1)
        cols = slice(g * sub * page, (g + 1) * sub * page)
        s_list.append(scores(cb, kpe_t, cos_ref[:, cols], sin_ref[:, cols]))
        cb_list.append(cb)
    update(s_list, cb_list)

    @pl.when(j == pl.num_programs(1) - 1)
    def _():
        cpad_scr[...] = jnp.zeros(cpad_scr.shape, F32)
        cpad_scr[:n_tok, :] = cn_ref[...]
        kpad_scr[...] = jnp.zeros(kpad_scr.shape, F32)
        kpad_scr[:n_tok, :MLA_ROPE] = kn_ref[...]
        rowi = lax.broadcasted_iota(jnp.int32, (nq, page), 0)
        coli = lax.broadcasted_iota(jnp.int32, (nq, page), 1)
        keep = coli <= (rowi & (n_tok - 1))
        cb = cpad_scr[...].astype(BF16)
        s = scores(cb, kpad_scr[...].T[:MLA_ROPE], cosn_ref[...], sinn_ref[...])
        update([jnp.where(keep, s, NEG_BIG)], [cb])
        ctx = (acc_scr[...] / d_scr[...]).astype(BF16)
        out = jnp.zeros(o_ref.shape, F32)
        for h in range(MLA_HEADS):
            out = out + _dot(ctx, wuv_ref[h])[h * n_tok:(h + 1) * n_tok]
        o_ref[...] = out

    @pl.when(step == last)
    def _():
        wait_pages(0, 0, 1 - slot)


def _attn_sample(page_table, cache_latent, cache_krope_t, tabs, qt, qf, ckv, kpe, w, n_tok, n_pages_step, sub):
    n_seq, n_pages = page_table.shape
    page = cache_latent.shape[2]
    n_steps = n_pages // n_pages_step
    tk = n_pages_step * page
    nq = MLA_HEADS * n_tok

    seq_row = lambda s, j, pt: (s, 0)
    const2 = lambda s, j, pt: (0, 0)
    half = MLA_ROPE // 2
    in_specs = [pl.BlockSpec(memory_space=pl.ANY), pl.BlockSpec(memory_space=pl.ANY)]
    in_specs += [pl.BlockSpec((half, tk), lambda s, j, pt: (0, j)), pl.BlockSpec((half, tk), lambda s, j, pt: (0, j)),
                 pl.BlockSpec((n_tok, MLA_HEADS * KV_LORA), seq_row),
                 pl.BlockSpec((n_tok, MLA_HEADS * HEAD_PAD), seq_row),
                 pl.BlockSpec((n_tok, KV_LORA), seq_row), pl.BlockSpec((n_tok, MLA_ROPE), seq_row),
                 pl.BlockSpec((half, page), const2), pl.BlockSpec((half, page), const2),
                 pl.BlockSpec(w['uk_t'].shape, const2),
                 pl.BlockSpec(w['uv_blk'].shape, lambda s, j, pt: (0, 0, 0)),
                 pl.BlockSpec((MLA_ROPE, 1), const2)]
    grid_spec = pltpu.PrefetchScalarGridSpec(
        num_scalar_prefetch=1, grid=(n_seq, n_steps), in_specs=in_specs,
        out_specs=pl.BlockSpec((n_tok, MLA_HEADS * MLA_V), seq_row),
        scratch_shapes=[pltpu.VMEM((MLA_HEADS * MLA_NOPE + nq, KV_LORA), BF16),
                        pltpu.VMEM((nq, HEAD_PAD), BF16),
                        pltpu.VMEM((nq, 1), F32), pltpu.VMEM((nq, 1), F32), pltpu.VMEM((nq, KV_LORA), F32),
                        pltpu.VMEM((page, LANES), F32), pltpu.VMEM((page, KV_LORA), F32),
                        pltpu.VMEM((2, n_pages_step, page, KV_LORA), F32),
                        pltpu.VMEM((2, n_pages_step, MLA_ROPE, page), F32),
                        pltpu.SemaphoreType.DMA((2, 2))])
    return pl.pallas_call(
        functools.partial(_attn_sample_kernel, n_pages=n_pages_step, sub=sub, n_tok=n_tok, page=page),
        grid_spec=grid_spec, out_shape=jax.ShapeDtypeStruct((n_seq * n_tok, MLA_HEADS * MLA_V), F32),
        compiler_params=_params(("arbitrary", "arbitrary")), name="attn_sample",
    )(page_table, cache_latent, cache_krope_t,
      tabs['cos_t'], tabs['sin_t'], qt, qf, ckv, kpe, tabs['cosn_t'], tabs['sinn_t'],
      w['uk_t'], w['uv_blk'], w['gpe'])


def _post_kernel(r_ref, m_ref, x_ref, wo_ref, g_ref, wrh_ref, wrl_ref, br_ref, h_ref, xn_ref, ti_ref, tg_ref):
    half = wo_ref.shape[0] // 2
    mix = _dot(r_ref[...], wo_ref[:half, :]) + _dot(m_ref[...], wo_ref[half:, :])
    h = x_ref[...] + mix
    h_ref[...] = h
    xn = _rms(h, g_ref[...])
    xn_ref[...] = xn
    hi = xn.astype(BF16)
    lo = (xn - hi.astype(F32)).astype(BF16)
    wrh = wrh_ref[...]
    work = _dot(hi, wrh) + _dot(hi, wrl_ref[...]) + _dot(lo, wrh) + br_ref[...]
    lane = lax.broadcasted_iota(jnp.int32, work.shape, 1).astype(F32)
    idx = jnp.zeros(work.shape, F32)
    val = jnp.zeros(work.shape, F32)
    vmax = None
    denom = None
    for k in range(TOP_K):
        vk = jnp.max(work, axis=-1, keepdims=True)
        ik = jnp.min(jnp.where(work == vk, lane, float(LANES)), axis=-1, keepdims=True)
        work = jnp.where(lane == ik, -jnp.inf, work)
        if k == 0:
            vmax = vk
        ek = jnp.exp(vk - vmax)
        denom = ek if k == 0 else denom + ek
        idx = jnp.where(lane == float(k), ik, idx)
        val = jnp.where(lane == float(k), ek, val)
    ti_ref[...] = idx.astype(jnp.int32)
    tg_ref[...] = val / denom


def _post(r, m, x, w, tm):
    rows = x.shape[0]
    row = lambda i: (i, 0)
    const = lambda i: (0, 0)
    in_specs = [pl.BlockSpec((tm, r.shape[1]), row), pl.BlockSpec((tm, m.shape[1]), row),
                pl.BlockSpec((tm, D_MODEL), row)]
    in_specs += [pl.BlockSpec(w[n].shape, const) for n in ('wo', 'gffn', 'wr_hi', 'wr_lo', 'br')]
    out_shape = [jax.ShapeDtypeStruct((rows, D_MODEL), F32), jax.ShapeDtypeStruct((rows, D_MODEL), F32),
                 jax.ShapeDtypeStruct((rows, LANES), jnp.int32), jax.ShapeDtypeStruct((rows, LANES), F32)]
    out_specs = [pl.BlockSpec((tm, D_MODEL), row), pl.BlockSpec((tm, D_MODEL), row),
                 pl.BlockSpec((tm, LANES), row), pl.BlockSpec((tm, LANES), row)]
    return pl.pallas_call(
        _post_kernel, grid=(rows // tm,), in_specs=in_specs, out_specs=out_specs, out_shape=out_shape,
        compiler_params=_params(("arbitrary",)), name="post",
    )(r, m, x, w['wo'], w['gffn'], w['wr_hi'], w['wr_lo'], w['br'])


def _moe_kernel(be_ref, nu_ref, xs_ref, wgu_ref, bgu_ref, wd_ref, bd_ref, o_ref, wgu_b, wd_b):
    i = pl.program_id(0)
    e = be_ref[i]
    e_prev = be_ref[jnp.maximum(i - 1, 0)]

    @pl.when((i == 0) | (e != e_prev))
    def _():
        wgu_b[...] = wgu_ref[...].astype(BF16)
        wd_b[...] = wd_ref[...].astype(BF16)

    @pl.when(i < nu_ref[0])
    def _():
        hgu = _dot(xs_ref[...].astype(BF16), wgu_b[...]) + bgu_ref[...]
        g = jnp.minimum(hgu[:, :D_FF], SWIGLU_LIMIT)
        u = jnp.clip(hgu[:, D_FF:], -SWIGLU_LIMIT, SWIGLU_LIMIT)
        hid = (u + 1.0) * (g * jax.nn.sigmoid(SWIGLU_ALPHA * g))
        o_ref[...] = _dot(hid.astype(BF16), wd_b[...]) + bd_ref[...]

    @pl.when(i >= nu_ref[0])
    def _():
        o_ref[...] = jnp.zeros(o_ref.shape, F32)


def _moe_blocks(block_e, n_used, xs, w_gate_up, b_gate_up, w_down, b_down, blk):
    n_blocks = block_e.shape[0]
    row = lambda i, be, nu: (i, 0)
    in_specs = [pl.BlockSpec((blk, D_MODEL), row),
                pl.BlockSpec((None, D_MODEL, 2 * D_FF), lambda i, be, nu: (be[i], 0, 0)),
                pl.BlockSpec((None, 1, 2 * D_FF), lambda i, be, nu: (be[i], 0, 0)),
                pl.BlockSpec((None, D_FF, D_MODEL), lambda i, be, nu: (be[i], 0, 0)),
                pl.BlockSpec((None, 1, D_MODEL), lambda i, be, nu: (be[i], 0, 0))]
    grid_spec = pltpu.PrefetchScalarGridSpec(
        num_scalar_prefetch=2, grid=(n_blocks,), in_specs=in_specs,
        out_specs=pl.BlockSpec((blk, D_MODEL), row),
        scratch_shapes=[pltpu.VMEM((D_MODEL, 2 * D_FF), BF16), pltpu.VMEM((D_FF, D_MODEL), BF16)])
    return pl.pallas_call(
        _moe_kernel, grid_spec=grid_spec, out_shape=jax.ShapeDtypeStruct((n_blocks * blk, D_MODEL), F32),
        compiler_params=_params(("arbitrary",)), name="moe",
    )(block_e, n_used, xs, w_gate_up, b_gate_up[:, None, :], w_down, b_down[:, None, :])


def _combine_kernel(h_ref, g_ref, y_ref, o_ref):
    g = g_ref[...]
    acc = y_ref[0] * g[:, 0:1]
    for k in range(1, TOP_K):
        acc = acc + y_ref[k] * g[:, k:k + 1]
    o_ref[...] = h_ref[...] + acc


def _combine(h, gates, y4, block0, rows, tm):
    row = lambda i: (block0 + i, 0)
    return pl.pallas_call(
        _combine_kernel, grid=(rows // tm,),
        in_specs=[pl.BlockSpec((tm, D_MODEL), row), pl.BlockSpec((tm, LANES), row),
                  pl.BlockSpec((TOP_K, tm, D_MODEL), lambda i: (0, block0 + i, 0))],
        out_specs=pl.BlockSpec((tm, D_MODEL), lambda i: (i, 0)),
        out_shape=jax.ShapeDtypeStruct((rows, D_MODEL), F32),
        compiler_params=_params(("arbitrary",)), name="combine",
    )(h, gates, y4)


def _route(top_i, blk):
    t = top_i.shape[0]
    a = t * TOP_K
    flat_e = top_i.reshape(a)
    order = jnp.argsort(flat_e)
    tok_sorted = (order // TOP_K).astype(jnp.int32)
    experts = jnp.arange(N_EXPERTS, dtype=jnp.int32)
    hit = (top_i[:, :, None] == experts[None, None, :]).astype(jnp.int32)
    per_tok = hit.sum(axis=1)
    before = jnp.cumsum(per_tok, axis=0) - per_tok
    counts = per_tok.sum(axis=0)
    padded = (counts + blk - 1) // blk * blk
    start = jnp.cumsum(counts) - counts
    pend = jnp.cumsum(padded)
    pstart = pend - padded
    dest = ((before + pstart[None, :])[:, None, :] * hit).sum(axis=2)
    n_blocks = -(-a // blk) + N_EXPERTS
    block_e = jnp.minimum((pend[None, :] <= (jnp.arange(n_blocks, dtype=jnp.int32) * blk)[:, None]).sum(axis=1),
                          N_EXPERTS - 1).astype(jnp.int32)
    slot = jnp.arange(n_blocks * blk, dtype=jnp.int32)
    slot_e = jnp.repeat(block_e, blk)
    r = slot - pstart[slot_e]
    slot_tok = jnp.where(r < counts[slot_e], tok_sorted[jnp.minimum(start[slot_e] + r, a - 1)], 0)
    n_used = (pend[-1:] // blk).astype(jnp.int32)
    return slot_tok, dest.astype(jnp.int32), block_e, n_used


def _rope_tables(pos):
    pos = pos.astype(F32)[:, None]
    n = pos.shape[0]
    hr = RET_DK // 2
    ang = pos * (RET_THETA ** (-jnp.arange(hr, dtype=F32) / hr))[None, :]
    cos, sin = jnp.cos(ang), jnp.sin(ang)
    zr = jnp.zeros((n, LANES - RET_DK), F32)
    cr = jnp.concatenate([cos, cos, zr], axis=1)
    sr = jnp.concatenate([-sin, sin, zr], axis=1)
    hm = MLA_ROPE // 2
    ang = pos * (MLA_THETA ** (-jnp.arange(hm, dtype=F32) / hm))[None, :]
    cos, sin = jnp.cos(ang), jnp.sin(ang)
    cm = jnp.concatenate([cos, cos, jnp.ones((n, LANES - MLA_ROPE), F32)], axis=1)
    sma = jnp.concatenate([jnp.zeros((n, hm), F32), sin, jnp.zeros((n, LANES - MLA_ROPE), F32)], axis=1)
    smb = jnp.concatenate([-sin, jnp.zeros((n, LANES - hm), F32)], axis=1)
    return [cr, sr, cm, sma, smb]


def _rope_tables_t(pos):
    hm = MLA_ROPE // 2
    ang = pos.astype(F32)[:, None] * (MLA_THETA ** (-jnp.arange(hm, dtype=F32) / hm))[None, :]
    return jnp.cos(ang).T, jnp.sin(ang).T


def _decay_tables(log_gamma, c, n_rep):
    idx = jnp.arange(c, dtype=F32)
    diff = idx[:, None] - idx[None, :]
    decay = jnp.where(diff >= 0, jnp.exp(jnp.maximum(diff, 0.0)[None] * log_gamma[:, None, None]), 0.0)
    if n_rep > 1:
        eye = jnp.eye(n_rep, dtype=F32)
        decay = (eye[None, :, None, :, None] * decay[:, None, :, None, :]).reshape(RET_HEADS, n_rep * c, n_rep * c)
    rowd = jnp.tile(jnp.exp((idx + 1.0)[:, None] * log_gamma[None, :]), (n_rep, 1))
    cold = jnp.tile(jnp.exp((c - 1.0 - idx)[:, None] * log_gamma[None, :]), (n_rep, 1))
    cpow = jnp.broadcast_to(jnp.exp(c * log_gamma)[None, :], (LANES, RET_HEADS))
    return {'dec': decay, 'rowd': rowd, 'cold': cold, 'cpow': cpow}


def _pad_heads(wm, n_heads, width, offset):
    k = wm.shape[0]
    wm = wm.reshape(k, n_heads, width)
    out = jnp.zeros((k, n_heads, LANES), wm.dtype).at[:, :, offset:offset + width].set(wm)
    return out.reshape(k, n_heads * LANES)


def _prep_weights(norm_attn_g, w_in, ret_out_g, q_a_norm_g, w_q_b, kv_a_norm_g, w_uk, w_uv,
                  qk_norm_q_g, qk_norm_k_g, w_out, norm_ffn_g, w_router, b_router):
    splits = [RET_HEADS * RET_DK, RET_HEADS * RET_DK, RET_HEADS * RET_DV, RET_HEADS * RET_DV, Q_LORA, KV_LORA,
              MLA_ROPE]
    offs = [0]
    for s in splits:
        offs.append(offs[-1] + s)
    part = [w_in[:, offs[i]:offs[i + 1]] for i in range(len(splits))]
    kpe_cols = jnp.zeros((D_MODEL, LANES), F32).at[:, :MLA_ROPE].set(part[6])
    win = jnp.concatenate([_pad_heads(part[0], RET_HEADS, RET_DK, 0), _pad_heads(part[1], RET_HEADS, RET_DK, 0),
                           part[2], part[3], part[4], part[5], kpe_cols], axis=1).astype(BF16)

    def mla_cols(wm):
        k = wm.shape[0]
        wm = wm.reshape(k, MLA_HEADS, MLA_QK)
        wm = jnp.concatenate([wm[:, :, MLA_NOPE:], wm[:, :, :MLA_NOPE]], axis=2)
        return _pad_heads(wm.reshape(k, MLA_HEADS * MLA_QK), MLA_HEADS, MLA_QK, 0)

    def mla_gain(g):
        g = jnp.concatenate([g[MLA_NOPE:], g[:MLA_NOPE], jnp.zeros((LANES - MLA_QK,), F32)])
        return g[None, :]

    gk_nope = qk_norm_k_g[:MLA_NOPE]
    uk_heads = w_uk.reshape(KV_LORA, MLA_HEADS, MLA_NOPE)
    wabs = jnp.zeros((MLA_HEADS, HEAD_PAD, KV_LORA), F32).at[:, MLA_ROPE:MLA_ROPE + MLA_NOPE, :].set(
        (uk_heads * gk_nope[None, None, :]).transpose(1, 2, 0))
    head_of_col = jnp.arange(MLA_HEADS * MLA_V) // MLA_V
    uv_blk = jnp.where(head_of_col[None, None, :] == jnp.arange(MLA_HEADS)[:, None, None], w_uv[None], 0.0)
    wr = jnp.zeros((D_MODEL, LANES), F32).at[:, :N_EXPERTS].set(w_router)
    wr_hi = wr.astype(BF16)
    return {
        'gattn': norm_attn_g[None, :], 'win': win, 'gqa': q_a_norm_g[None, :], 'wqb': mla_cols(w_q_b).astype(BF16),
        'gq': mla_gain(qk_norm_q_g), 'gkv': kv_a_norm_g[None, :],
        'wuk': _pad_heads(w_uk, MLA_HEADS, MLA_NOPE, MLA_ROPE).astype(BF16), 'gk': mla_gain(qk_norm_k_g),
        'wuv': w_uv.astype(BF16), 'gret': ret_out_g[None, :],
        'wabs': wabs.astype(BF16), 'uk_t': w_uk.T.astype(BF16), 'uv_blk': uv_blk.astype(BF16),
        'gpe': qk_norm_k_g[MLA_NOPE:, None],
        'wo': w_out.astype(BF16), 'gffn': norm_ffn_g[None, :], 'wr_hi': wr_hi,
        'wr_lo': (wr - wr_hi.astype(F32)).astype(BF16),
        'br': jnp.full((1, LANES), NEG_BIG, F32).at[0, :N_EXPERTS].set(b_router),
    }


def _pad_rows(a, rows):
    return jnp.zeros((rows,) + a.shape[1:], a.dtype).at[:a.shape[0]].set(a)


def _largest_divisor(n, cap):
    d = min(n, cap)
    while n % d:
        d -= 1
    return d


def kernel(x_prompt, x_sample, cache_latent, cache_krope, state_retention, page_table, meta_tokens, norm_attn_g, w_in, ret_out_g, q_a_norm_g, w_q_b, kv_a_norm_g, w_uk, w_uv, qk_norm_q_g, qk_norm_k_g, w_out, norm_ffn_g, w_router, b_router, w_gate_up, b_gate_up, w_down, b_down):
    assert w_in.shape[0] == 1, "single-layer trunk"
    batch, seq, _ = x_prompt.shape
    n_seq, n_tok, _ = x_sample.shape
    n_pages, page = page_table.shape[1], cache_latent.shape[2]
    past = n_pages * page
    assert seq % RET_CHUNK == 0 and LANES % n_tok == 0 and (n_seq * n_tok) % LANES == 0 and page == LANES
    w = _prep_weights(norm_attn_g[0], w_in[0], ret_out_g[0], q_a_norm_g[0], w_q_b[0], kv_a_norm_g[0], w_uk[0],
                      w_uv[0], qk_norm_q_g[0], qk_norm_k_g[0], w_out[0], norm_ffn_g[0], w_router[0], b_router[0])
    log_gamma = jnp.log1p(-jnp.exp2(-5.0 - jnp.arange(RET_HEADS, dtype=F32)))

    rows_p = batch * seq
    rows_s = n_seq * n_tok
    tm = _largest_divisor(min(seq, rows_s), ROW_BLOCK)
    nb_seq = seq // tm
    x = jnp.concatenate([x_prompt.reshape(rows_p, D_MODEL), x_sample.reshape(rows_s, D_MODEL)], axis=0)
    pos_rows = jnp.concatenate([N_META + jnp.arange(seq), jnp.tile(past + jnp.arange(n_tok), tm // n_tok)])
    tabs = _rope_tables(pos_rows)
    n_pb = rows_p // tm
    rq, rk, rv, zg, mq, ckv, kpe, k, v = _front(
        x, tabs, lambda i: jnp.where(i < n_pb, i % nb_seq, nb_seq), w, tm)
    _, mrk, mrv, _, _, mckv, mkpe, mk, mv = _front(
        meta_tokens, _rope_tables(jnp.arange(N_META)), lambda i: i, w, N_META)

    dt_p = _decay_tables(log_gamma, RET_CHUNK, 1)
    dt_p['mcol'] = _pad_rows(jnp.exp((N_META - 1.0 - jnp.arange(N_META, dtype=F32))[:, None] * log_gamma[None, :]),
                             RET_CHUNK)
    gret = w['gret']
    r_p, st_p = _ret_prompt(rq, rk, rv, zg, _pad_rows(mrk, RET_CHUNK), _pad_rows(mrv, RET_CHUNK), dt_p, gret,
                            batch, seq // RET_CHUNK)
    dt_s = _decay_tables(log_gamma, n_tok, LANES // n_tok)
    r_s, st_s = _ret_sample(rq[rows_p:], rk[rows_p:], rv[rows_p:], zg[rows_p:], state_retention[0], dt_s, gret, n_tok)

    blk = _largest_divisor(seq, ATTN_BLOCK)
    m_p = _attn_prompt(mq, k, v, _pad_rows(mk, LANES), _pad_rows(mv, LANES), batch, seq, blk)
    tm_s = _largest_divisor(rows_s, ROW_BLOCK)
    qt, qf = _absorb(mq, w['wabs'], rows_p // tm_s, rows_s, tm_s)
    n_pages_step = _largest_divisor(n_pages, PAGES_PER_STEP)
    cos_t, sin_t = _rope_tables_t(jnp.arange(past))
    cosn_t, sinn_t = _rope_tables_t(past + jnp.arange(page))
    tabs_s = {'cos_t': cos_t, 'sin_t': sin_t, 'cosn_t': cosn_t, 'sinn_t': sinn_t}
    m_s = _attn_sample(page_table, cache_latent, jnp.swapaxes(cache_krope, 2, 3), tabs_s, qt, qf, ckv[rows_p:],
                       kpe[rows_p:], w, n_tok, n_pages_step, _largest_divisor(n_pages_step, PAGES_PER_CHAIN))

    r_all = jnp.concatenate([r_p, r_s], axis=0)
    m_all = jnp.concatenate([m_p, m_s.astype(BF16)], axis=0)
    h1, xn2, top_i, gates = _post(r_all, m_all, x, w, tm)

    slot_tok, dest, block_e, n_used = _route(top_i[:, :TOP_K], EXPERT_BLOCK)
    outs = _moe_blocks(block_e, n_used, xn2[slot_tok], w_gate_up[0], b_gate_up[0], w_down[0], b_down[0],
                       EXPERT_BLOCK)
    y4 = outs[dest.T.reshape(-1)].reshape(TOP_K, rows_p + rows_s, D_MODEL)
    y_prompt = _combine(h1, gates, y4, 0, rows_p, tm).reshape(batch, seq, D_MODEL)
    y_sample = _combine(h1, gates, y4, n_pb, rows_s, tm).reshape(n_seq, n_tok, D_MODEL)
    lat_p = jnp.concatenate([jnp.broadcast_to(mckv[None], (batch, N_META, KV_LORA)),
                             ckv[:rows_p].reshape(batch, seq, KV_LORA)], axis=1)[None]
    kpe_p = jnp.concatenate([jnp.broadcast_to(mkpe[None], (batch, N_META, MLA_ROPE)),
                             kpe[:rows_p].reshape(batch, seq, MLA_ROPE)], axis=1)[None]
    return (y_prompt, y_sample, lat_p, kpe_p, st_p[None],
            ckv[rows_p:].reshape(n_seq, n_tok, KV_LORA)[None], kpe[rows_p:].reshape(n_seq, n_tok, MLA_ROPE)[None],
            st_s[None])
```

```python
import functools

import jax
import jax.numpy as jnp
from jax import lax
from jax.experimental import pallas as pl
from jax.experimental.pallas import tpu as pltpu
from jax.experimental.pallas import tpu_sc as plsc

F32 = jnp.float32
BF16 = jnp.bfloat16

D_MODEL = 1024
N_META = 16
RET_HEADS = 4
RET_DK = 64
RET_DV = 128
RET_CHUNK = 128
RET_THETA = 10000.0
MLA_HEADS = 8
MLA_NOPE = 64
MLA_ROPE = 32
MLA_QK = MLA_NOPE + MLA_ROPE
MLA_V = 64
Q_LORA = 384
KV_LORA = 256
MLA_THETA = 10000.0
MLA_SCALE = MLA_QK ** -0.5
LOG2E = 1.4426950408889634
N_EXPERTS = 32
TOP_K = 4
D_FF = 1024
SWIGLU_LIMIT = 7.0
SWIGLU_ALPHA = 1.702
EPS = 1e-6
NEG_BIG = -1e30

LANES = 128
HEAD_PAD = LANES
VMEM_LIMIT = 56 * 1024 * 1024
ROW_BLOCK = 256
ATTN_BLOCK = 1024
PAGES_PER_STEP = 16
PAGES_PER_CHAIN = 2
EXPERT_BLOCK = 256
SC_CORES, SC_SUBCORES = 2, 16
GATHER_WINDOW = 128
GATHER_WIDTH = 256

_OFF_RQ = 0
_OFF_RK = _OFF_RQ + RET_HEADS * LANES
_OFF_RV = _OFF_RK + RET_HEADS * LANES
_OFF_ZG = _OFF_RV + RET_HEADS * RET_DV
_OFF_CQ = _OFF_ZG + RET_HEADS * RET_DV
_OFF_CKV = _OFF_CQ + Q_LORA
_OFF_KPE = _OFF_CKV + KV_LORA
IN_PAD = _OFF_KPE + LANES


def _params(sem):
    return pltpu.CompilerParams(dimension_semantics=sem, vmem_limit_bytes=VMEM_LIMIT)


def _rms(x, g):
    return x * lax.rsqrt(jnp.mean(x * x, axis=-1, keepdims=True) + EPS) * g


def _dot(a, b):
    return jnp.dot(a, b, preferred_element_type=F32)


def _dot_nt(a, b):
    return lax.dot_general(a, b, (((1,), (1,)), ((), ())), preferred_element_type=F32)


def _dot_tn(a, b):
    return lax.dot_general(a, b, (((0,), (0,)), ((), ())), preferred_element_type=F32)


def _front_kernel(x_ref, gattn_ref, win_ref, cr_ref, sr_ref, cm_ref, sma_ref, smb_ref,
                  gqa_ref, wqb_ref, gq_ref, gkv_ref, wuk_ref, gk_ref, wuv_ref,
                  rq_ref, rk_ref, rv_ref, zg_ref, mq_ref, ckv_ref, kpe_ref, k_ref, v_ref):
    tm = x_ref.shape[0]
    xn = _rms(x_ref[...], gattn_ref[...])
    z = _dot(xn.astype(BF16), win_ref[...])

    lane = lax.broadcasted_iota(jnp.int32, (tm, LANES), 1)
    upper = (lane & (RET_DK // 2)) != 0
    cr, sr = cr_ref[...], sr_ref[...]

    def rope_ret(t):
        partner = jnp.where(upper, pltpu.roll(t, RET_DK // 2, 1), pltpu.roll(t, LANES - RET_DK // 2, 1))
        return t * cr + partner * sr

    for h in range(RET_HEADS):
        sl = slice(h * LANES, (h + 1) * LANES)
        rq_ref[:, sl] = rope_ret(z[:, _OFF_RQ + h * LANES:_OFF_RQ + (h + 1) * LANES])
        rk_ref[:, sl] = rope_ret(z[:, _OFF_RK + h * LANES:_OFF_RK + (h + 1) * LANES]) * (RET_DK ** -0.5)
    rv_ref[...] = z[:, _OFF_RV:_OFF_ZG].astype(BF16)
    zg_ref[...] = z[:, _OFF_ZG:_OFF_CQ]

    cm, sma, smb = cm_ref[...], sma_ref[...], smb_ref[...]

    def rope_mla(t):
        half = MLA_ROPE // 2
        return t * cm + pltpu.roll(t, half, 1) * sma + pltpu.roll(t, LANES - half, 1) * smb

    def head_norm(t, g):
        ms = jnp.sum(t * t, axis=-1, keepdims=True) * (1.0 / MLA_QK)
        return t * lax.rsqrt(ms + EPS) * g

    cq = _rms(z[:, _OFF_CQ:_OFF_CKV], gqa_ref[...])
    q = _dot(cq.astype(BF16), wqb_ref[...])
    gq = gq_ref[...]
    for h in range(MLA_HEADS):
        sl = slice(h * HEAD_PAD, (h + 1) * HEAD_PAD)
        mq_ref[:, sl] = (rope_mla(head_norm(q[:, sl], gq)) * (MLA_SCALE * LOG2E)).astype(BF16)

    ckv = _rms(z[:, _OFF_CKV:_OFF_KPE], gkv_ref[...])
    ckv_ref[...] = ckv
    kpe_slab = z[:, _OFF_KPE:IN_PAD]
    kpe_ref[...] = kpe_slab[:, :MLA_ROPE]
    ckv_b = ckv.astype(BF16)
    kn = _dot(ckv_b, wuk_ref[...])
    gk = gk_ref[...]
    for h in range(MLA_HEADS):
        sl = slice(h * HEAD_PAD, (h + 1) * HEAD_PAD)
        k_ref[:, sl] = rope_mla(head_norm(kn[:, sl] + kpe_slab, gk)).astype(BF16)
    v_ref[...] = _dot(ckv_b, wuv_ref[...]).astype(BF16)


def _front(x, tabs, tab_index, w, tm):
    rows = x.shape[0]
    grid = (rows // tm,)
    row = lambda i: (i, 0)
    const = lambda i: (0, 0)
    tab = lambda i: (tab_index(i), 0)

    def full(a):
        return pl.BlockSpec(a.shape, const)

    in_specs = [pl.BlockSpec((tm, D_MODEL), row), full(w['gattn']), full(w['win'])]
    in_specs += [pl.BlockSpec((tm, LANES), tab)] * 5
    in_specs += [full(w[n]) for n in ('gqa', 'wqb', 'gq', 'gkv', 'wuk', 'gk', 'wuv')]
    widths = [(RET_HEADS * LANES, F32), (RET_HEADS * LANES, F32), (RET_HEADS * RET_DV, BF16),
              (RET_HEADS * RET_DV, F32), (MLA_HEADS * HEAD_PAD, BF16), (KV_LORA, F32), (MLA_ROPE, F32),
              (MLA_HEADS * HEAD_PAD, BF16), (MLA_HEADS * MLA_V, BF16)]
    out_shape = [jax.ShapeDtypeStruct((rows, n), dt) for n, dt in widths]
    out_specs = [pl.BlockSpec((tm, n), row) for n, _ in widths]
    return pl.pallas_call(
        _front_kernel, grid=grid, in_specs=in_specs, out_specs=out_specs, out_shape=out_shape,
        compiler_params=_params(("arbitrary",)), name="front",
    )(x, w['gattn'], w['win'], *tabs, w['gqa'], w['wqb'], w['gq'], w['gkv'], w['wuk'], w['gk'], w['wuv'])


def _ret_gate(o, zg, g):
    on = o * lax.rsqrt(jnp.mean(o * o, axis=-1, keepdims=True) + EPS) * g
    return (zg * jax.nn.sigmoid(zg)) * on


def _ret_prompt_kernel(q_ref, k_ref, v_ref, zg_ref, mk_ref, mv_ref, mcol_ref, dec_ref, rowd_ref, cold_ref,
                       cpow_ref, g_ref, r_ref, s_ref, s_scr):
    c = pl.program_id(1)

    @pl.when(c == 0)
    def _():
        for h in range(RET_HEADS):
            sl = slice(h * LANES, (h + 1) * LANES)
            kw = mk_ref[:, sl] * mcol_ref[:, h:h + 1]
            s_scr[h] = _dot_tn(kw.astype(BF16), mv_ref[:, sl])

    for h in range(RET_HEADS):
        sl = slice(h * LANES, (h + 1) * LANES)
        q = q_ref[:, sl]
        k = k_ref[:, sl]
        v = v_ref[:, sl]
        qb = q.astype(BF16)
        s0 = s_scr[h]
        scores = _dot_nt(qb, k.astype(BF16)) * dec_ref[h]
        inner = _dot(scores.astype(BF16), v)
        cross = _dot(qb, s0.astype(BF16)) * rowd_ref[:, h:h + 1]
        kw = k * cold_ref[:, h:h + 1]
        s_new = s0 * cpow_ref[:, h:h + 1] + _dot_tn(kw.astype(BF16), v)
        s_scr[h] = s_new
        r_ref[:, sl] = _ret_gate(inner + cross, zg_ref[:, sl], g_ref[:, sl]).astype(BF16)

    @pl.when(c == pl.num_programs(1) - 1)
    def _():
        for h in range(RET_HEADS):
            s_ref[0, h] = s_scr[h, :RET_DK, :]


def _ret_prompt(rq, rk, rv, zg, mk, mv, tabs, g, batch, n_chunks):
    cs = RET_CHUNK
    row = lambda b, c: (b * n_chunks + c, 0)
    const2 = lambda b, c: (0, 0)
    w4 = RET_HEADS * LANES
    in_specs = [pl.BlockSpec((cs, w4), row), pl.BlockSpec((cs, w4), row), pl.BlockSpec((cs, w4), row),
                pl.BlockSpec((cs, w4), row),
                pl.BlockSpec((cs, w4), const2), pl.BlockSpec((cs, w4), const2),
                pl.BlockSpec((cs, RET_HEADS), const2),
                pl.BlockSpec((RET_HEADS, cs, cs), lambda b, c: (0, 0, 0)),
                pl.BlockSpec((cs, RET_HEADS), const2), pl.BlockSpec((cs, RET_HEADS), const2),
                pl.BlockSpec((LANES, RET_HEADS), const2), pl.BlockSpec((1, w4), const2)]
    out_shape = [jax.ShapeDtypeStruct((batch * n_chunks * cs, w4), BF16),
                 jax.ShapeDtypeStruct((batch, RET_HEADS, RET_DK, RET_DV), F32)]
    out_specs = [pl.BlockSpec((cs, w4), row),
                 pl.BlockSpec((1, RET_HEADS, RET_DK, RET_DV), lambda b, c: (b, 0, 0, 0))]
    return pl.pallas_call(
        _ret_prompt_kernel, grid=(batch, n_chunks), in_specs=in_specs, out_specs=out_specs, out_shape=out_shape,
        scratch_shapes=[pltpu.VMEM((RET_HEADS, LANES, RET_DV), F32)],
        compiler_params=_params(("arbitrary", "arbitrary")), name="ret_prompt",
    )(rq, rk, rv, zg, mk, mv, tabs['mcol'], tabs['dec'], tabs['rowd'], tabs['cold'], tabs['cpow'], g)


def _ret_sample_kernel(q_ref, k_ref, v_ref, zg_ref, s0_ref, dec_ref, rowd_ref, cold_ref, cpow_ref, g_ref,
                       r_ref, s_ref, *, n_seq, n_tok):
    rows = n_seq * n_tok
    ri = lax.broadcasted_iota(jnp.int32, (rows, 1), 0)
    for h in range(RET_HEADS):
        sl = slice(h * LANES, (h + 1) * LANES)
        q = q_ref[:, sl]
        k = k_ref[:, sl]
        v = v_ref[:, sl]
        qb = q.astype(BF16)
        scores = _dot_nt(qb, k.astype(BF16)) * dec_ref[h]
        inner = _dot(scores.astype(BF16), v)
        kw = k * cold_ref[:, h:h + 1]
        cross = jnp.zeros((rows, RET_DV), F32)
        for s in range(n_seq):
            s0 = s0_ref[s, h]
            mine = (ri >= s * n_tok) & (ri < (s + 1) * n_tok)
            cross = cross + jnp.where(mine, _dot(qb[:, :RET_DK], s0.astype(BF16)), 0.0)
            upd = _dot_tn(jnp.where(mine, kw, 0.0).astype(BF16), v)
            s_ref[s, h] = s0 * cpow_ref[:RET_DK, h:h + 1] + upd[:RET_DK]
        cross = cross * rowd_ref[:, h:h + 1]
        r_ref[:, sl] = _ret_gate(inner + cross, zg_ref[:, sl], g_ref[:, sl]).astype(BF16)


def _ret_sample(rq, rk, rv, zg, state, tabs, g, n_tok):
    n_seq_total = state.shape[0]
    n_seq = LANES // n_tok
    rows = n_seq * n_tok
    w4 = RET_HEADS * LANES
    row = lambda i: (i, 0)
    const = lambda i: (0, 0)
    in_specs = [pl.BlockSpec((rows, w4), row)] * 4
    in_specs += [pl.BlockSpec((n_seq, RET_HEADS, RET_DK, RET_DV), lambda i: (i, 0, 0, 0)),
                 pl.BlockSpec((RET_HEADS, rows, rows), lambda i: (0, 0, 0)),
                 pl.BlockSpec((rows, RET_HEADS), const), pl.BlockSpec((rows, RET_HEADS), const),
                 pl.BlockSpec((LANES, RET_HEADS), const), pl.BlockSpec((1, w4), const)]
    out_shape = [jax.ShapeDtypeStruct((n_seq_total * n_tok, w4), BF16),
                 jax.ShapeDtypeStruct(state.shape, F32)]
    out_specs = [pl.BlockSpec((rows, w4), row),
                 pl.BlockSpec((n_seq, RET_HEADS, RET_DK, RET_DV), lambda i: (i, 0, 0, 0))]
    return pl.pallas_call(
        functools.partial(_ret_sample_kernel, n_seq=n_seq, n_tok=n_tok),
        grid=(n_seq_total // n_seq,), in_specs=in_specs, out_specs=out_specs, out_shape=out_shape,
        compiler_params=_params(("arbitrary",)), name="ret_sample",
    )(rq, rk, rv, zg, state, tabs['dec'], tabs['rowd'], tabs['cold'], tabs['cpow'], g)


def _attn_prompt_kernel(q_ref, k_ref, v_ref, km_ref, vm_ref, o_ref, m_scr, l_scr, acc_scr, *, sub_tiles):
    qi = pl.program_id(2)
    ki = pl.program_id(3)
    tm, tk = q_ref.shape[0], k_ref.shape[0]

    @pl.when(ki == 0)
    def _():
        lane = lax.broadcasted_iota(jnp.int32, (tm, km_ref.shape[0]), 1)
        for hh in range(2):
            sl = slice(hh * HEAD_PAD, (hh + 1) * HEAD_PAD)
            s = jnp.where(lane < N_META, _dot_nt(q_ref[:, sl], km_ref[:, sl]), NEG_BIG)
            m = jnp.max(s, axis=-1, keepdims=True)
            p = jnp.exp2(s - m)
            m_scr[hh] = jnp.broadcast_to(m, (tm, LANES))
            l_scr[hh] = jnp.broadcast_to(jnp.sum(p, axis=-1, keepdims=True), (tm, LANES))
            acc_scr[hh] = _dot(p.astype(BF16), vm_ref[...])

    def tile(r0, nr, c0, nc, masked):
        rows = pl.ds(r0, nr)
        if masked:
            keep = (lax.broadcasted_iota(jnp.int32, (nr, nc), 1) <= lax.broadcasted_iota(jnp.int32, (nr, nc), 0))
        for hh in range(2):
            sl = slice(hh * HEAD_PAD, (hh + 1) * HEAD_PAD)
            s = _dot_nt(q_ref[rows, sl], k_ref[pl.ds(c0, nc), sl])
            if masked:
                s = jnp.where(keep, s, NEG_BIG)
            m_old = m_scr[hh, rows, :]
            m_new = jnp.maximum(m_old, jnp.max(s, axis=-1, keepdims=True))
            alpha = jnp.exp2(m_old - m_new)
            p = jnp.exp2(s - jnp.tile(m_new, (1, nc // LANES)))
            m_scr[hh, rows, :] = m_new
            l_scr[hh, rows, :] = alpha * l_scr[hh, rows, :] + jnp.sum(p, axis=-1, keepdims=True)
            acc_scr[hh, rows, :] = alpha * acc_scr[hh, rows, :] + _dot(p.astype(BF16), v_ref[pl.ds(c0, nc), :])

    @pl.when(ki < qi)
    def _():
        tile(0, tm, 0, tk, False)

    @pl.when(ki == qi)
    def _():
        ns = sub_tiles
        st = tm // ns
        for a in range(ns):
            if a > 0:
                tile(a * st, st, 0, a * st, False)
            tile(a * st, st, a * st, st, True)
        lane = lax.broadcasted_iota(jnp.int32, (tm, 2 * MLA_V), 1)
        o0 = acc_scr[0] / l_scr[0]
        o1 = acc_scr[1] / l_scr[1]
        o_ref[...] = jnp.where(lane < MLA_V, o0, o1).astype(BF16)


def _attn_prompt(mq, k, v, km, vm, batch, seq, blk):
    nb = seq // blk
    pairs = MLA_HEADS // 2
    grid = (batch, pairs, nb, nb)
    in_specs = [pl.BlockSpec((blk, 2 * HEAD_PAD), lambda b, h, qi, ki: (b * nb + qi, h)),
                pl.BlockSpec((blk, 2 * HEAD_PAD), lambda b, h, qi, ki: (b * nb + jnp.minimum(ki, qi), h)),
                pl.BlockSpec((blk, 2 * MLA_V), lambda b, h, qi, ki: (b * nb + jnp.minimum(ki, qi), h)),
                pl.BlockSpec((km.shape[0], 2 * HEAD_PAD), lambda b, h, qi, ki: (0, h)),
                pl.BlockSpec((vm.shape[0], 2 * MLA_V), lambda b, h, qi, ki: (0, h))]
    out_specs = pl.BlockSpec((blk, 2 * MLA_V), lambda b, h, qi, ki: (b * nb + qi, h))
    return pl.pallas_call(
        functools.partial(_attn_prompt_kernel, sub_tiles=2 if blk % (2 * LANES) == 0 else 1), grid=grid, in_specs=in_specs, out_specs=out_specs,
        out_shape=jax.ShapeDtypeStruct((batch * seq, MLA_HEADS * MLA_V), BF16),
        scratch_shapes=[pltpu.VMEM((2, blk, LANES), F32), pltpu.VMEM((2, blk, LANES), F32),
                        pltpu.VMEM((2, blk, 2 * MLA_V), F32)],
        compiler_params=_params(("arbitrary",) * 4), name="attn_prompt",
    )(mq, k, v, km, vm)


def _absorb_kernel(mq_ref, wabs_ref, qt_ref, qf_ref):
    q = mq_ref[...]
    qf_ref[...] = q.astype(F32)
    for h in range(MLA_HEADS):
        qt_ref[:, h * KV_LORA:(h + 1) * KV_LORA] = _dot(q[:, h * HEAD_PAD:(h + 1) * HEAD_PAD], wabs_ref[h])


def _absorb(mq, wabs, row0_blocks, rows, tm):
    return pl.pallas_call(
        _absorb_kernel, grid=(rows // tm,),
        in_specs=[pl.BlockSpec((tm, MLA_HEADS * HEAD_PAD), lambda i: (row0_blocks + i, 0)),
                  pl.BlockSpec(wabs.shape, lambda i: (0, 0, 0))],
        out_specs=[pl.BlockSpec((tm, MLA_HEADS * KV_LORA), lambda i: (i, 0)),
                   pl.BlockSpec((tm, MLA_HEADS * HEAD_PAD), lambda i: (i, 0))],
        out_shape=[jax.ShapeDtypeStruct((rows, MLA_HEADS * KV_LORA), F32),
                   jax.ShapeDtypeStruct((rows, MLA_HEADS * HEAD_PAD), F32)],
        compiler_params=_params(("arbitrary",)), name="absorb",
    )(mq, wabs)


def _attn_sample_kernel(pt_ref, lat_hbm, kpe_hbm, cos_ref, sin_ref, qt_ref, qf_ref, cn_ref, kn_ref, cosn_ref,
                        sinn_ref, lw_ref, wuv_ref, gpe_ref, o_ref, l_scr, qpe_scr, m_scr, d_scr, acc_scr, kpad_scr,
                        cpad_scr, lat_buf, kpe_buf, sem, *, n_pages, sub, n_tok, page):
    s_id = pl.program_id(0)
    j = pl.program_id(1)
    n_steps = pl.num_programs(1)
    step = s_id * n_steps + j
    last = pl.num_programs(0) * n_steps - 1
    slot = lax.rem(step, 2)
    nq = MLA_HEADS * n_tok
    n_up = MLA_HEADS * MLA_NOPE

    def page_copies(pid, sl, p):
        return (pltpu.make_async_copy(lat_hbm.at[0, pid], lat_buf.at[sl, p], sem.at[sl, 0]),
                pltpu.make_async_copy(kpe_hbm.at[0, pid], kpe_buf.at[sl, p], sem.at[sl, 1]))

    def start_pages(seq, st, sl, pages=range(n_pages)):
        for p in pages:
            for c in page_copies(pt_ref[seq, st * n_pages + p], sl, p):
                c.start()

    def wait_pages(sl):
        for p in range(n_pages):
            for c in page_copies(0, sl, p):
                c.wait()

    @pl.when(step == 0)
    def _():
        start_pages(0, 0, 0)

    wait_pages(slot)
    wrap = j == n_steps - 1
    nxt_seq = jnp.where(wrap, jnp.where(step == last, 0, s_id + 1), s_id)
    nxt_j = jnp.where(wrap, 0, j + 1)
    lat_refs = [lat_buf.at[slot, p] for p in range(n_pages)]
    kpe_refs = [kpe_buf.at[slot, p] for p in range(n_pages)]

    @pl.when(j == 0)
    def _():
        l_scr[:n_up, :] = lw_ref[...]
        qt = qt_ref[...]
        qf = qf_ref[...]
        l_scr[n_up:, :] = jnp.concatenate(
            [qt[:, h * KV_LORA:(h + 1) * KV_LORA] for h in range(MLA_HEADS)], axis=0).astype(BF16)
        lane = lax.broadcasted_iota(jnp.int32, (nq, HEAD_PAD), 1)
        qpe = jnp.concatenate([qf[:, h * HEAD_PAD:(h + 1) * HEAD_PAD] for h in range(MLA_HEADS)], axis=0)
        qpe_scr[...] = jnp.where(lane < MLA_ROPE, qpe, 0.0).astype(BF16)
        m_scr[...] = jnp.full(m_scr.shape, NEG_BIG, F32)
        d_scr[...] = jnp.zeros(d_scr.shape, F32)
        acc_scr[...] = jnp.zeros(acc_scr.shape, F32)

    def scores(cb, kpe_t, cos_t, sin_t):
        tk = cb.shape[0]
        big = _dot_nt(l_scr[...], cb)
        k_t = big[:n_up]
        ss = jnp.sum((k_t * k_t).reshape(MLA_HEADS, MLA_NOPE, tk), axis=1)
        ss = ss + jnp.sum(kpe_t * kpe_t, axis=0, keepdims=True)
        rs = lax.rsqrt(ss * (1.0 / MLA_QK) + EPS)
        kg = kpe_t * gpe_ref[...]
        half = MLA_ROPE // 2
        x1, x2 = kg[:half], kg[half:]
        rot = jnp.concatenate([x1 * cos_t - x2 * sin_t, x1 * sin_t + x2 * cos_t], axis=0)
        pe = _dot(qpe_scr[:, :MLA_ROPE], rot.astype(BF16))
        rs_rows = jnp.concatenate([jnp.broadcast_to(rs[h:h + 1, :], (n_tok, tk)) for h in range(MLA_HEADS)], axis=0)
        return (big[n_up:] + pe) * rs_rows

    def update(s_list, cb_list):
        m_old = m_scr[...]
        m_new = m_old
        for s in s_list:
            m_new = jnp.maximum(m_new, jnp.max(s, axis=-1, keepdims=True))
        alpha = jnp.exp2(m_old - m_new)
        d = alpha * d_scr[...]
        acc = alpha * acc_scr[...]
        for s, cb in zip(s_list, cb_list):
            p = jnp.exp2(s - m_new)
            d = d + jnp.sum(p, axis=-1, keepdims=True)
            acc = acc + _dot(p.astype(BF16), cb)
        m_scr[...] = m_new
        d_scr[...] = d
        acc_scr[...] = acc

    s_list, cb_list = [], []
    for g in range(n_pages // sub):
        pages = range(g * sub, (g + 1) * sub)
        cb = jnp.concatenate([lat_refs[b][...] for b in pages], axis=0).astype(BF16)
        kpe_t = jnp.concatenate([kpe_refs[b][...] for b in pages], axis=1)
        cols = slice(g * sub * page, (g + 1) * sub * page)
        s_list.append(scores(cb, kpe_t, cos_ref[:, cols], sin_ref[:, cols]))
        cb_list.append(cb)
        start_pages(nxt_seq, nxt_j, 1 - slot, pages)
    update(s_list, cb_list)

    @pl.when(j == pl.num_programs(1) - 1)
    def _():
        cpad_scr[...] = jnp.zeros(cpad_scr.shape, F32)
        cpad_scr[:n_tok, :] = cn_ref[...]
        kpad_scr[...] = jnp.zeros(kpad_scr.shape, F32)
        kpad_scr[:n_tok, :MLA_ROPE] = kn_ref[...]
        rowi = lax.broadcasted_iota(jnp.int32, (nq, page), 0)
        coli = lax.broadcasted_iota(jnp.int32, (nq, page), 1)
        keep = coli <= (rowi & (n_tok - 1))
        cb = cpad_scr[...].astype(BF16)
        s = scores(cb, kpad_scr[...].T[:MLA_ROPE], cosn_ref[...], sinn_ref[...])
        update([jnp.where(keep, s, NEG_BIG)], [cb])
        ctx = (acc_scr[...] / d_scr[...]).astype(BF16)
        out = jnp.zeros(o_ref.shape, F32)
        for h in range(MLA_HEADS):
            out = out + _dot(ctx, wuv_ref[h])[h * n_tok:(h + 1) * n_tok]
        o_ref[...] = out

    @pl.when(step == last)
    def _():
        wait_pages(1 - slot)


def _attn_sample(page_table, cache_latent, cache_krope_t, tabs, qt, qf, ckv, kpe, w, n_tok, n_pages_step, sub):
    n_seq, n_pages = page_table.shape
    page = cache_latent.shape[2]
    n_steps = n_pages // n_pages_step
    tk = n_pages_step * page
    nq = MLA_HEADS * n_tok

    seq_row = lambda s, j, pt: (s, 0)
    const2 = lambda s, j, pt: (0, 0)
    half = MLA_ROPE // 2
    in_specs = [pl.BlockSpec(memory_space=pl.ANY), pl.BlockSpec(memory_space=pl.ANY)]
    in_specs += [pl.BlockSpec((half, tk), lambda s, j, pt: (0, j)), pl.BlockSpec((half, tk), lambda s, j, pt: (0, j)),
                 pl.BlockSpec((n_tok, MLA_HEADS * KV_LORA), seq_row),
                 pl.BlockSpec((n_tok, MLA_HEADS * HEAD_PAD), seq_row),
                 pl.BlockSpec((n_tok, KV_LORA), seq_row), pl.BlockSpec((n_tok, MLA_ROPE), seq_row),
                 pl.BlockSpec((half, page), const2), pl.BlockSpec((half, page), const2),
                 pl.BlockSpec(w['uk_t'].shape, const2),
                 pl.BlockSpec(w['uv_blk'].shape, lambda s, j, pt: (0, 0, 0)),
                 pl.BlockSpec((MLA_ROPE, 1), const2)]
    grid_spec = pltpu.PrefetchScalarGridSpec(
        num_scalar_prefetch=1, grid=(n_seq, n_steps), in_specs=in_specs,
        out_specs=pl.BlockSpec((n_tok, MLA_HEADS * MLA_V), seq_row),
        scratch_shapes=[pltpu.VMEM((MLA_HEADS * MLA_NOPE + nq, KV_LORA), BF16),
                        pltpu.VMEM((nq, HEAD_PAD), BF16),
                        pltpu.VMEM((nq, 1), F32), pltpu.VMEM((nq, 1), F32), pltpu.VMEM((nq, KV_LORA), F32),
                        pltpu.VMEM((page, LANES), F32), pltpu.VMEM((page, KV_LORA), F32),
                        pltpu.VMEM((2, n_pages_step, page, KV_LORA), F32),
                        pltpu.VMEM((2, n_pages_step, MLA_ROPE, page), F32),
                        pltpu.SemaphoreType.DMA((2, 2))])
    return pl.pallas_call(
        functools.partial(_attn_sample_kernel, n_pages=n_pages_step, sub=sub, n_tok=n_tok, page=page),
        grid_spec=grid_spec, out_shape=jax.ShapeDtypeStruct((n_seq * n_tok, MLA_HEADS * MLA_V), F32),
        compiler_params=_params(("arbitrary", "arbitrary")), name="attn_sample",
    )(page_table, cache_latent, cache_krope_t,
      tabs['cos_t'], tabs['sin_t'], qt, qf, ckv, kpe, tabs['cosn_t'], tabs['sinn_t'],
      w['uk_t'], w['uv_blk'], w['gpe'])


def _post_kernel(r_ref, m_ref, x_ref, wo_ref, g_ref, wrh_ref, wrl_ref, br_ref, h_ref, xn_ref, ti_ref, tg_ref):
    half = wo_ref.shape[0] // 2
    mix = _dot(r_ref[...], wo_ref[:half, :]) + _dot(m_ref[...], wo_ref[half:, :])
    h = x_ref[...] + mix
    h_ref[...] = h
    xn = _rms(h, g_ref[...])
    xn_ref[...] = xn
    hi = xn.astype(BF16)
    lo = (xn - hi.astype(F32)).astype(BF16)
    wrh = wrh_ref[...]
    work = _dot(hi, wrh) + _dot(hi, wrl_ref[...]) + _dot(lo, wrh) + br_ref[...]
    lane = lax.broadcasted_iota(jnp.int32, work.shape, 1).astype(F32)
    idx = jnp.zeros(work.shape, F32)
    val = jnp.zeros(work.shape, F32)
    vmax = None
    denom = None
    for k in range(TOP_K):
        vk = jnp.max(work, axis=-1, keepdims=True)
        ik = jnp.min(jnp.where(work == vk, lane, float(LANES)), axis=-1, keepdims=True)
        work = jnp.where(lane == ik, -jnp.inf, work)
        if k == 0:
            vmax = vk
        ek = jnp.exp(vk - vmax)
        denom = ek if k == 0 else denom + ek
        idx = jnp.where(lane == float(k), ik, idx)
        val = jnp.where(lane == float(k), ek, val)
    ti_ref[...] = idx.astype(jnp.int32)
    tg_ref[...] = val / denom


def _post(r, m, x, w, tm):
    rows = x.shape[0]
    row = lambda i: (i, 0)
    const = lambda i: (0, 0)
    in_specs = [pl.BlockSpec((tm, r.shape[1]), row), pl.BlockSpec((tm, m.shape[1]), row),
                pl.BlockSpec((tm, D_MODEL), row)]
    in_specs += [pl.BlockSpec(w[n].shape, const) for n in ('wo', 'gffn', 'wr_hi', 'wr_lo', 'br')]
    out_shape = [jax.ShapeDtypeStruct((rows, D_MODEL), F32), jax.ShapeDtypeStruct((rows, D_MODEL), F32),
                 jax.ShapeDtypeStruct((rows, LANES), jnp.int32), jax.ShapeDtypeStruct((rows, LANES), F32)]
    out_specs = [pl.BlockSpec((tm, D_MODEL), row), pl.BlockSpec((tm, D_MODEL), row),
                 pl.BlockSpec((tm, LANES), row), pl.BlockSpec((tm, LANES), row)]
    return pl.pallas_call(
        _post_kernel, grid=(rows // tm,), in_specs=in_specs, out_specs=out_specs, out_shape=out_shape,
        compiler_params=_params(("arbitrary",)), name="post",
    )(r, m, x, w['wo'], w['gffn'], w['wr_hi'], w['wr_lo'], w['br'])


def _moe_kernel(be_ref, nu_ref, xs_ref, wgu_ref, bgu_ref, wd_ref, bd_ref, o_ref, wgu_b, wd_b):
    i = pl.program_id(0)
    e = be_ref[i]
    e_prev = be_ref[jnp.maximum(i - 1, 0)]

    @pl.when((i == 0) | (e != e_prev))
    def _():
        wgu_b[...] = wgu_ref[...].astype(BF16)
        wd_b[...] = wd_ref[...].astype(BF16)

    @pl.when(i < nu_ref[0])
    def _():
        hgu = _dot(xs_ref[...].astype(BF16), wgu_b[...]) + bgu_ref[...]
        g = jnp.minimum(hgu[:, :D_FF], SWIGLU_LIMIT)
        u = jnp.clip(hgu[:, D_FF:], -SWIGLU_LIMIT, SWIGLU_LIMIT)
        hid = (u + 1.0) * (g * jax.nn.sigmoid(SWIGLU_ALPHA * g))
        o_ref[...] = _dot(hid.astype(BF16), wd_b[...]) + bd_ref[...]

    @pl.when(i >= nu_ref[0])
    def _():
        o_ref[...] = jnp.zeros(o_ref.shape, F32)


def _moe_blocks(block_e, n_used, xs, w_gate_up, b_gate_up, w_down, b_down, blk):
    n_blocks = block_e.shape[0]
    row = lambda i, be, nu: (i, 0)
    in_specs = [pl.BlockSpec((blk, D_MODEL), row),
                pl.BlockSpec((None, D_MODEL, 2 * D_FF), lambda i, be, nu: (be[i], 0, 0)),
                pl.BlockSpec((None, 1, 2 * D_FF), lambda i, be, nu: (be[i], 0, 0)),
                pl.BlockSpec((None, D_FF, D_MODEL), lambda i, be, nu: (be[i], 0, 0)),
                pl.BlockSpec((None, 1, D_MODEL), lambda i, be, nu: (be[i], 0, 0))]
    grid_spec = pltpu.PrefetchScalarGridSpec(
        num_scalar_prefetch=2, grid=(n_blocks,), in_specs=in_specs,
        out_specs=pl.BlockSpec((blk, D_MODEL), row),
        scratch_shapes=[pltpu.VMEM((D_MODEL, 2 * D_FF), BF16), pltpu.VMEM((D_FF, D_MODEL), BF16)])
    return pl.pallas_call(
        _moe_kernel, grid_spec=grid_spec, out_shape=jax.ShapeDtypeStruct((n_blocks * blk, D_MODEL), F32),
        compiler_params=_params(("arbitrary",)), name="moe",
    )(block_e, n_used, xs, w_gate_up, b_gate_up[:, None, :], w_down, b_down[:, None, :])


def _combine_kernel(h_ref, g_ref, y_ref, o_ref):
    g = g_ref[...]
    acc = y_ref[0] * g[:, 0:1]
    for k in range(1, TOP_K):
        acc = acc + y_ref[k] * g[:, k:k + 1]
    o_ref[...] = h_ref[...] + acc


def _combine(h, gates, y4, block0, rows, tm):
    row = lambda i: (block0 + i, 0)
    return pl.pallas_call(
        _combine_kernel, grid=(rows // tm,),
        in_specs=[pl.BlockSpec((tm, D_MODEL), row), pl.BlockSpec((tm, LANES), row),
                  pl.BlockSpec((TOP_K, tm, D_MODEL), lambda i: (0, block0 + i, 0))],
        out_specs=pl.BlockSpec((tm, D_MODEL), lambda i: (i, 0)),
        out_shape=jax.ShapeDtypeStruct((rows, D_MODEL), F32),
        compiler_params=_params(("arbitrary",)), name="combine",
    )(h, gates, y4)


def _gather_rows(table, idx):
    rows, width = table.shape
    split = width // GATHER_WIDTH
    table = table.reshape(rows * split, GATHER_WIDTH)
    idx = (idx[:, None] * split + jnp.arange(split, dtype=jnp.int32)[None, :]).reshape(-1)
    n, d = idx.shape[0], GATHER_WIDTH
    assert n % (GATHER_WINDOW * SC_CORES * SC_SUBCORES) == 0
    mesh = plsc.VectorSubcoreMesh(core_axis_name="core", subcore_axis_name="subcore",
                                  num_cores=SC_CORES, num_subcores=SC_SUBCORES)

    @functools.partial(pl.kernel, out_type=jax.ShapeDtypeStruct((n, d), table.dtype), mesh=mesh)
    def gather(x_hbm, i_hbm, o_hbm):
        def body(i_vmem, o_vmem):
            pltpu.sync_copy(x_hbm.at[i_vmem.at[0]], o_vmem)

        pltpu.emit_pipeline(
            body, grid=(n // GATHER_WINDOW,),
            in_specs=[pl.BlockSpec((1, GATHER_WINDOW), index_map=lambda i: (0, i))],
            out_specs=[pl.BlockSpec((GATHER_WINDOW, d), index_map=lambda i: (i, 0))],
            core_axis_name=("core", "subcore"), dimension_semantics=(pltpu.PARALLEL,),
        )(i_hbm, o_hbm)

    return gather(table, idx.reshape(1, n)).reshape(n // split, width)


def _route(top_i, blk):
    t = top_i.shape[0]
    a = t * TOP_K
    flat_e = top_i.reshape(a)
    order = jnp.argsort(flat_e)
    tok_sorted = (order // TOP_K).astype(jnp.int32)
    experts = jnp.arange(N_EXPERTS, dtype=jnp.int32)
    hit = (top_i[:, :, None] == experts[None, None, :]).astype(jnp.int32)
    per_tok = hit.sum(axis=1)
    before = jnp.cumsum(per_tok, axis=0) - per_tok
    counts = per_tok.sum(axis=0)
    padded = (counts + blk - 1) // blk * blk
    start = jnp.cumsum(counts) - counts
    pend = jnp.cumsum(padded)
    pstart = pend - padded
    dest = ((before + pstart[None, :])[:, None, :] * hit).sum(axis=2)
    n_blocks = -(-a // blk) + N_EXPERTS
    block_e = jnp.minimum((pend[None, :] <= (jnp.arange(n_blocks, dtype=jnp.int32) * blk)[:, None]).sum(axis=1),
                          N_EXPERTS - 1).astype(jnp.int32)
    slot = jnp.arange(n_blocks * blk, dtype=jnp.int32)
    slot_e = jnp.repeat(block_e, blk)
    r = slot - pstart[slot_e]
    slot_tok = jnp.where(r < counts[slot_e], tok_sorted[jnp.minimum(start[slot_e] + r, a - 1)], 0)
    n_used = (pend[-1:] // blk).astype(jnp.int32)
    return slot_tok, dest.astype(jnp.int32), block_e, n_used


def _rope_tables(pos):
    pos = pos.astype(F32)[:, None]
    n = pos.shape[0]
    hr = RET_DK // 2
    ang = pos * (RET_THETA ** (-jnp.arange(hr, dtype=F32) / hr))[None, :]
    cos, sin = jnp.cos(ang), jnp.sin(ang)
    zr = jnp.zeros((n, LANES - RET_DK), F32)
    cr = jnp.concatenate([cos, cos, zr], axis=1)
    sr = jnp.concatenate([-sin, sin, zr], axis=1)
    hm = MLA_ROPE // 2
    ang = pos * (MLA_THETA ** (-jnp.arange(hm, dtype=F32) / hm))[None, :]
    cos, sin = jnp.cos(ang), jnp.sin(ang)
    cm = jnp.concatenate([cos, cos, jnp.ones((n, LANES - MLA_ROPE), F32)], axis=1)
    sma = jnp.concatenate([jnp.zeros((n, hm), F32), sin, jnp.zeros((n, LANES - MLA_ROPE), F32)], axis=1)
    smb = jnp.concatenate([-sin, jnp.zeros((n, LANES - hm), F32)], axis=1)
    return [cr, sr, cm, sma, smb]


def _rope_tables_t(pos):
    hm = MLA_ROPE // 2
    ang = pos.astype(F32)[:, None] * (MLA_THETA ** (-jnp.arange(hm, dtype=F32) / hm))[None, :]
    return jnp.cos(ang).T, jnp.sin(ang).T


def _decay_tables(log_gamma, c, n_rep):
    idx = jnp.arange(c, dtype=F32)
    diff = idx[:, None] - idx[None, :]
    decay = jnp.where(diff >= 0, jnp.exp(jnp.maximum(diff, 0.0)[None] * log_gamma[:, None, None]), 0.0)
    if n_rep > 1:
        eye = jnp.eye(n_rep, dtype=F32)
        decay = (eye[None, :, None, :, None] * decay[:, None, :, None, :]).reshape(RET_HEADS, n_rep * c, n_rep * c)
    rowd = jnp.tile(jnp.exp((idx + 1.0)[:, None] * log_gamma[None, :]), (n_rep, 1))
    cold = jnp.tile(jnp.exp((c - 1.0 - idx)[:, None] * log_gamma[None, :]), (n_rep, 1))
    cpow = jnp.broadcast_to(jnp.exp(c * log_gamma)[None, :], (LANES, RET_HEADS))
    return {'dec': decay, 'rowd': rowd, 'cold': cold, 'cpow': cpow}


def _pad_heads(wm, n_heads, width, offset):
    k = wm.shape[0]
    wm = wm.reshape(k, n_heads, width)
    out = jnp.zeros((k, n_heads, LANES), wm.dtype).at[:, :, offset:offset + width].set(wm)
    return out.reshape(k, n_heads * LANES)


def _prep_weights(norm_attn_g, w_in, ret_out_g, q_a_norm_g, w_q_b, kv_a_norm_g, w_uk, w_uv,
                  qk_norm_q_g, qk_norm_k_g, w_out, norm_ffn_g, w_router, b_router):
    splits = [RET_HEADS * RET_DK, RET_HEADS * RET_DK, RET_HEADS * RET_DV, RET_HEADS * RET_DV, Q_LORA, KV_LORA,
              MLA_ROPE]
    offs = [0]
    for s in splits:
        offs.append(offs[-1] + s)
    part = [w_in[:, offs[i]:offs[i + 1]] for i in range(len(splits))]
    kpe_cols = jnp.zeros((D_MODEL, LANES), F32).at[:, :MLA_ROPE].set(part[6])
    win = jnp.concatenate([_pad_heads(part[0], RET_HEADS, RET_DK, 0), _pad_heads(part[1], RET_HEADS, RET_DK, 0),
                           part[2], part[3], part[4], part[5], kpe_cols], axis=1).astype(BF16)

    def mla_cols(wm):
        k = wm.shape[0]
        wm = wm.reshape(k, MLA_HEADS, MLA_QK)
        wm = jnp.concatenate([wm[:, :, MLA_NOPE:], wm[:, :, :MLA_NOPE]], axis=2)
        return _pad_heads(wm.reshape(k, MLA_HEADS * MLA_QK), MLA_HEADS, MLA_QK, 0)

    def mla_gain(g):
        g = jnp.concatenate([g[MLA_NOPE:], g[:MLA_NOPE], jnp.zeros((LANES - MLA_QK,), F32)])
        return g[None, :]

    gk_nope = qk_norm_k_g[:MLA_NOPE]
    uk_heads = w_uk.reshape(KV_LORA, MLA_HEADS, MLA_NOPE)
    wabs = jnp.zeros((MLA_HEADS, HEAD_PAD, KV_LORA), F32).at[:, MLA_ROPE:MLA_ROPE + MLA_NOPE, :].set(
        (uk_heads * gk_nope[None, None, :]).transpose(1, 2, 0))
    head_of_col = jnp.arange(MLA_HEADS * MLA_V) // MLA_V
    uv_blk = jnp.where(head_of_col[None, None, :] == jnp.arange(MLA_HEADS)[:, None, None], w_uv[None], 0.0)
    wr = jnp.zeros((D_MODEL, LANES), F32).at[:, :N_EXPERTS].set(w_router)
    wr_hi = wr.astype(BF16)
    return {
        'gattn': norm_attn_g[None, :], 'win': win, 'gqa': q_a_norm_g[None, :], 'wqb': mla_cols(w_q_b).astype(BF16),
        'gq': mla_gain(qk_norm_q_g), 'gkv': kv_a_norm_g[None, :],
        'wuk': _pad_heads(w_uk, MLA_HEADS, MLA_NOPE, MLA_ROPE).astype(BF16), 'gk': mla_gain(qk_norm_k_g),
        'wuv': w_uv.astype(BF16), 'gret': ret_out_g[None, :],
        'wabs': wabs.astype(BF16), 'uk_t': w_uk.T.astype(BF16), 'uv_blk': uv_blk.astype(BF16),
        'gpe': qk_norm_k_g[MLA_NOPE:, None],
        'wo': w_out.astype(BF16), 'gffn': norm_ffn_g[None, :], 'wr_hi': wr_hi,
        'wr_lo': (wr - wr_hi.astype(F32)).astype(BF16),
        'br': jnp.full((1, LANES), NEG_BIG, F32).at[0, :N_EXPERTS].set(b_router),
    }


def _pad_rows(a, rows):
    return jnp.zeros((rows,) + a.shape[1:], a.dtype).at[:a.shape[0]].set(a)


def _largest_divisor(n, cap):
    d = min(n, cap)
    while n % d:
        d -= 1
    return d


def kernel(x_prompt, x_sample, cache_latent, cache_krope, state_retention, page_table, meta_tokens, norm_attn_g, w_in, ret_out_g, q_a_norm_g, w_q_b, kv_a_norm_g, w_uk, w_uv, qk_norm_q_g, qk_norm_k_g, w_out, norm_ffn_g, w_router, b_router, w_gate_up, b_gate_up, w_down, b_down):
    assert w_in.shape[0] == 1, "single-layer trunk"
    batch, seq, _ = x_prompt.shape
    n_seq, n_tok, _ = x_sample.shape
    n_pages, page = page_table.shape[1], cache_latent.shape[2]
    past = n_pages * page
    assert seq % RET_CHUNK == 0 and LANES % n_tok == 0 and (n_seq * n_tok) % LANES == 0 and page == LANES
    w = _prep_weights(norm_attn_g[0], w_in[0], ret_out_g[0], q_a_norm_g[0], w_q_b[0], kv_a_norm_g[0], w_uk[0],
                      w_uv[0], qk_norm_q_g[0], qk_norm_k_g[0], w_out[0], norm_ffn_g[0], w_router[0], b_router[0])
    log_gamma = jnp.log1p(-jnp.exp2(-5.0 - jnp.arange(RET_HEADS, dtype=F32)))

    rows_p = batch * seq
    rows_s = n_seq * n_tok
    tm = _largest_divisor(min(seq, rows_s), ROW_BLOCK)
    nb_seq = seq // tm
    x = jnp.concatenate([x_prompt.reshape(rows_p, D_MODEL), x_sample.reshape(rows_s, D_MODEL)], axis=0)
    pos_rows = jnp.concatenate([N_META + jnp.arange(seq), jnp.tile(past + jnp.arange(n_tok), tm // n_tok)])
    tabs = _rope_tables(pos_rows)
    n_pb = rows_p // tm
    rq, rk, rv, zg, mq, ckv, kpe, k, v = _front(
        x, tabs, lambda i: jnp.where(i < n_pb, i % nb_seq, nb_seq), w, tm)
    _, mrk, mrv, _, _, mckv, mkpe, mk, mv = _front(
        meta_tokens, _rope_tables(jnp.arange(N_META)), lambda i: i, w, N_META)

    dt_p = _decay_tables(log_gamma, RET_CHUNK, 1)
    dt_p['mcol'] = _pad_rows(jnp.exp((N_META - 1.0 - jnp.arange(N_META, dtype=F32))[:, None] * log_gamma[None, :]),
                             RET_CHUNK)
    gret = w['gret']
    r_p, st_p = _ret_prompt(rq, rk, rv, zg, _pad_rows(mrk, RET_CHUNK), _pad_rows(mrv, RET_CHUNK), dt_p, gret,
                            batch, seq // RET_CHUNK)
    dt_s = _decay_tables(log_gamma, n_tok, LANES // n_tok)
    r_s, st_s = _ret_sample(rq[rows_p:], rk[rows_p:], rv[rows_p:], zg[rows_p:], state_retention[0], dt_s, gret, n_tok)

    blk = _largest_divisor(seq, ATTN_BLOCK)
    m_p = _attn_prompt(mq, k, v, _pad_rows(mk, LANES), _pad_rows(mv, LANES), batch, seq, blk)
    tm_s = _largest_divisor(rows_s, ROW_BLOCK)
    qt, qf = _absorb(mq, w['wabs'], rows_p // tm_s, rows_s, tm_s)
    n_pages_step = _largest_divisor(n_pages, PAGES_PER_STEP)
    cos_t, sin_t = _rope_tables_t(jnp.arange(past))
    cosn_t, sinn_t = _rope_tables_t(past + jnp.arange(page))
    tabs_s = {'cos_t': cos_t, 'sin_t': sin_t, 'cosn_t': cosn_t, 'sinn_t': sinn_t}
    m_s = _attn_sample(page_table, cache_latent, jnp.swapaxes(cache_krope, 2, 3), tabs_s, qt, qf, ckv[rows_p:],
                       kpe[rows_p:], w, n_tok, n_pages_step, _largest_divisor(n_pages_step, PAGES_PER_CHAIN))

    r_all = jnp.concatenate([r_p, r_s], axis=0)
    m_all = jnp.concatenate([m_p, m_s.astype(BF16)], axis=0)
    h1, xn2, top_i, gates = _post(r_all, m_all, x, w, tm)

    slot_tok, dest, block_e, n_used = _route(top_i[:, :TOP_K], EXPERT_BLOCK)
    outs = _moe_blocks(block_e, n_used, _gather_rows(xn2, slot_tok), w_gate_up[0], b_gate_up[0], w_down[0],
                       b_down[0], EXPERT_BLOCK)
    y4 = _gather_rows(outs, dest.T.reshape(-1)).reshape(TOP_K, rows_p + rows_s, D_MODEL)
    y_prompt = _combine(h1, gates, y4, 0, rows_p, tm).reshape(batch, seq, D_MODEL)
    y_sample = _combine(h1, gates, y4, n_pb, rows_s, tm).reshape(n_seq, n_tok, D_MODEL)
    lat_p = jnp.concatenate([jnp.broadcast_to(mckv[None], (batch, N_META, KV_LORA)),
                             ckv[:rows_p].reshape(batch, seq, KV_LORA)], axis=1)[None]
    kpe_p = jnp.concatenate([jnp.broadcast_to(mkpe[None], (batch, N_META, MLA_ROPE)),
                             kpe[:rows_p].reshape(batch, seq, MLA_ROPE)], axis=1)[None]
    return (y_prompt, y_sample, lat_p, kpe_p, st_p[None],
            ckv[rows_p:].reshape(n_seq, n_tok, KV_LORA)[None], kpe[rows_p:].reshape(n_seq, n_tok, MLA_ROPE)[None],
            st_s[None])
```

```python
import functools

import jax
import jax.numpy as jnp
from jax import lax
from jax.experimental import pallas as pl
from jax.experimental.pallas import tpu as pltpu
from jax.experimental.pallas import tpu_sc as plsc

F32 = jnp.float32
BF16 = jnp.bfloat16

D_MODEL = 1024
N_META = 16
RET_HEADS = 4
RET_DK = 64
RET_DV = 128
RET_CHUNK = 128
RET_THETA = 10000.0
MLA_HEADS = 8
MLA_NOPE = 64
MLA_ROPE = 32
MLA_QK = MLA_NOPE + MLA_ROPE
MLA_V = 64
Q_LORA = 384
KV_LORA = 256
MLA_THETA = 10000.0
MLA_SCALE = MLA_QK ** -0.5
LOG2E = 1.4426950408889634
N_EXPERTS = 32
TOP_K = 4
D_FF = 1024
SWIGLU_LIMIT = 7.0
SWIGLU_ALPHA = 1.702
EPS = 1e-6
NEG_BIG = -1e30

LANES = 128
HEAD_PAD = LANES
VMEM_LIMIT = 56 * 1024 * 1024
ROW_BLOCK = 256
ATTN_BLOCK = 1024
PAGES_PER_STEP = 16
PAGES_PER_CHAIN = 2
EXPERT_BLOCK = 256
SC_CORES, SC_SUBCORES = 2, 16
GATHER_WINDOW = 32

_OFF_RQ = 0
_OFF_RK = _OFF_RQ + RET_HEADS * LANES
_OFF_RV = _OFF_RK + RET_HEADS * LANES
_OFF_ZG = _OFF_RV + RET_HEADS * RET_DV
_OFF_CQ = _OFF_ZG + RET_HEADS * RET_DV
_OFF_CKV = _OFF_CQ + Q_LORA
_OFF_KPE = _OFF_CKV + KV_LORA
IN_PAD = _OFF_KPE + LANES


def _params(sem):
    return pltpu.CompilerParams(dimension_semantics=sem, vmem_limit_bytes=VMEM_LIMIT)


def _rms(x, g):
    return x * lax.rsqrt(jnp.mean(x * x, axis=-1, keepdims=True) + EPS) * g


def _dot(a, b):
    return jnp.dot(a, b, preferred_element_type=F32)


def _dot_nt(a, b):
    return lax.dot_general(a, b, (((1,), (1,)), ((), ())), preferred_element_type=F32)


def _dot_tn(a, b):
    return lax.dot_general(a, b, (((0,), (0,)), ((), ())), preferred_element_type=F32)


def _front_kernel(x_ref, gattn_ref, win_ref, cr_ref, sr_ref, cm_ref, sma_ref, smb_ref,
                  gqa_ref, wqb_ref, gq_ref, gkv_ref, wuk_ref, gk_ref, wuv_ref,
                  rq_ref, rk_ref, rv_ref, zg_ref, mq_ref, ckv_ref, kpe_ref, k_ref, v_ref):
    tm = x_ref.shape[0]
    xn = _rms(x_ref[...], gattn_ref[...])
    z = _dot(xn.astype(BF16), win_ref[...])

    lane = lax.broadcasted_iota(jnp.int32, (tm, LANES), 1)
    upper = (lane & (RET_DK // 2)) != 0
    cr, sr = cr_ref[...], sr_ref[...]

    def rope_ret(t):
        partner = jnp.where(upper, pltpu.roll(t, RET_DK // 2, 1), pltpu.roll(t, LANES - RET_DK // 2, 1))
        return t * cr + partner * sr

    for h in range(RET_HEADS):
        sl = slice(h * LANES, (h + 1) * LANES)
        rq_ref[:, sl] = rope_ret(z[:, _OFF_RQ + h * LANES:_OFF_RQ + (h + 1) * LANES])
        rk_ref[:, sl] = rope_ret(z[:, _OFF_RK + h * LANES:_OFF_RK + (h + 1) * LANES]) * (RET_DK ** -0.5)
    rv_ref[...] = z[:, _OFF_RV:_OFF_ZG].astype(BF16)
    zg_ref[...] = z[:, _OFF_ZG:_OFF_CQ]

    cm, sma, smb = cm_ref[...], sma_ref[...], smb_ref[...]

    def rope_mla(t):
        half = MLA_ROPE // 2
        return t * cm + pltpu.roll(t, half, 1) * sma + pltpu.roll(t, LANES - half, 1) * smb

    def head_norm(t, g):
        ms = jnp.sum(t * t, axis=-1, keepdims=True) * (1.0 / MLA_QK)
        return t * lax.rsqrt(ms + EPS) * g

    cq = _rms(z[:, _OFF_CQ:_OFF_CKV], gqa_ref[...])
    q = _dot(cq.astype(BF16), wqb_ref[...])
    gq = gq_ref[...]
    for h in range(MLA_HEADS):
        sl = slice(h * HEAD_PAD, (h + 1) * HEAD_PAD)
        mq_ref[:, sl] = (rope_mla(head_norm(q[:, sl], gq)) * (MLA_SCALE * LOG2E)).astype(BF16)

    ckv = _rms(z[:, _OFF_CKV:_OFF_KPE], gkv_ref[...])
    ckv_ref[...] = ckv
    kpe_slab = z[:, _OFF_KPE:IN_PAD]
    kpe_ref[...] = kpe_slab[:, :MLA_ROPE]
    ckv_b = ckv.astype(BF16)
    kn = _dot(ckv_b, wuk_ref[...])
    gk = gk_ref[...]
    for h in range(MLA_HEADS):
        sl = slice(h * HEAD_PAD, (h + 1) * HEAD_PAD)
        k_ref[:, sl] = rope_mla(head_norm(kn[:, sl] + kpe_slab, gk)).astype(BF16)
    v_ref[...] = _dot(ckv_b, wuv_ref[...]).astype(BF16)


def _front(x, tabs, tab_index, w, tm):
    rows = x.shape[0]
    grid = (rows // tm,)
    row = lambda i: (i, 0)
    const = lambda i: (0, 0)
    tab = lambda i: (tab_index(i), 0)

    def full(a):
        return pl.BlockSpec(a.shape, const)

    in_specs = [pl.BlockSpec((tm, D_MODEL), row), full(w['gattn']), full(w['win'])]
    in_specs += [pl.BlockSpec((tm, LANES), tab)] * 5
    in_specs += [full(w[n]) for n in ('gqa', 'wqb', 'gq', 'gkv', 'wuk', 'gk', 'wuv')]
    widths = [(RET_HEADS * LANES, F32), (RET_HEADS * LANES, F32), (RET_HEADS * RET_DV, BF16),
              (RET_HEADS * RET_DV, F32), (MLA_HEADS * HEAD_PAD, BF16), (KV_LORA, F32), (MLA_ROPE, F32),
              (MLA_HEADS * HEAD_PAD, BF16), (MLA_HEADS * MLA_V, BF16)]
    out_shape = [jax.ShapeDtypeStruct((rows, n), dt) for n, dt in widths]
    out_specs = [pl.BlockSpec((tm, n), row) for n, _ in widths]
    return pl.pallas_call(
        _front_kernel, grid=grid, in_specs=in_specs, out_specs=out_specs, out_shape=out_shape,
        compiler_params=_params(("arbitrary",)), name="front",
    )(x, w['gattn'], w['win'], *tabs, w['gqa'], w['wqb'], w['gq'], w['gkv'], w['wuk'], w['gk'], w['wuv'])


def _ret_gate(o, zg, g):
    on = o * lax.rsqrt(jnp.mean(o * o, axis=-1, keepdims=True) + EPS) * g
    return (zg * jax.nn.sigmoid(zg)) * on


def _ret_prompt_kernel(q_ref, k_ref, v_ref, zg_ref, mk_ref, mv_ref, mcol_ref, dec_ref, rowd_ref, cold_ref,
                       cpow_ref, g_ref, r_ref, s_ref, s_scr):
    c = pl.program_id(1)

    @pl.when(c == 0)
    def _():
        for h in range(RET_HEADS):
            sl = slice(h * LANES, (h + 1) * LANES)
            kw = mk_ref[:, sl] * mcol_ref[:, h:h + 1]
            s_scr[h] = _dot_tn(kw.astype(BF16), mv_ref[:, sl])

    for h in range(RET_HEADS):
        sl = slice(h * LANES, (h + 1) * LANES)
        q = q_ref[:, sl]
        k = k_ref[:, sl]
        v = v_ref[:, sl]
        qb = q.astype(BF16)
        s0 = s_scr[h]
        scores = _dot_nt(qb, k.astype(BF16)) * dec_ref[h]
        inner = _dot(scores.astype(BF16), v)
        cross = _dot(qb, s0.astype(BF16)) * rowd_ref[:, h:h + 1]
        kw = k * cold_ref[:, h:h + 1]
        s_new = s0 * cpow_ref[:, h:h + 1] + _dot_tn(kw.astype(BF16), v)
        s_scr[h] = s_new
        r_ref[:, sl] = _ret_gate(inner + cross, zg_ref[:, sl], g_ref[:, sl]).astype(BF16)

    @pl.when(c == pl.num_programs(1) - 1)
    def _():
        for h in range(RET_HEADS):
            s_ref[0, h] = s_scr[h, :RET_DK, :]


def _ret_prompt(rq, rk, rv, zg, mk, mv, tabs, g, batch, n_chunks):
    cs = RET_CHUNK
    row = lambda b, c: (b * n_chunks + c, 0)
    const2 = lambda b, c: (0, 0)
    w4 = RET_HEADS * LANES
    in_specs = [pl.BlockSpec((cs, w4), row), pl.BlockSpec((cs, w4), row), pl.BlockSpec((cs, w4), row),
                pl.BlockSpec((cs, w4), row),
                pl.BlockSpec((cs, w4), const2), pl.BlockSpec((cs, w4), const2),
                pl.BlockSpec((cs, RET_HEADS), const2),
                pl.BlockSpec((RET_HEADS, cs, cs), lambda b, c: (0, 0, 0)),
                pl.BlockSpec((cs, RET_HEADS), const2), pl.BlockSpec((cs, RET_HEADS), const2),
                pl.BlockSpec((LANES, RET_HEADS), const2), pl.BlockSpec((1, w4), const2)]
    out_shape = [jax.ShapeDtypeStruct((batch * n_chunks * cs, w4), BF16),
                 jax.ShapeDtypeStruct((batch, RET_HEADS, RET_DK, RET_DV), F32)]
    out_specs = [pl.BlockSpec((cs, w4), row),
                 pl.BlockSpec((1, RET_HEADS, RET_DK, RET_DV), lambda b, c: (b, 0, 0, 0))]
    return pl.pallas_call(
        _ret_prompt_kernel, grid=(batch, n_chunks), in_specs=in_specs, out_specs=out_specs, out_shape=out_shape,
        scratch_shapes=[pltpu.VMEM((RET_HEADS, LANES, RET_DV), F32)],
        compiler_params=_params(("arbitrary", "arbitrary")), name="ret_prompt",
    )(rq, rk, rv, zg, mk, mv, tabs['mcol'], tabs['dec'], tabs['rowd'], tabs['cold'], tabs['cpow'], g)


def _ret_sample_kernel(q_ref, k_ref, v_ref, zg_ref, s0_ref, dec_ref, rowd_ref, cold_ref, cpow_ref, g_ref,
                       r_ref, s_ref, *, n_seq, n_tok):
    rows = n_seq * n_tok
    ri = lax.broadcasted_iota(jnp.int32, (rows, 1), 0)
    for h in range(RET_HEADS):
        sl = slice(h * LANES, (h + 1) * LANES)
        q = q_ref[:, sl]
        k = k_ref[:, sl]
        v = v_ref[:, sl]
        qb = q.astype(BF16)
        scores = _dot_nt(qb, k.astype(BF16)) * dec_ref[h]
        inner = _dot(scores.astype(BF16), v)
        kw = k * cold_ref[:, h:h + 1]
        cross = jnp.zeros((rows, RET_DV), F32)
        for s in range(n_seq):
            s0 = s0_ref[s, h]
            mine = (ri >= s * n_tok) & (ri < (s + 1) * n_tok)
            cross = cross + jnp.where(mine, _dot(qb[:, :RET_DK], s0.astype(BF16)), 0.0)
            upd = _dot_tn(jnp.where(mine, kw, 0.0).astype(BF16), v)
            s_ref[s, h] = s0 * cpow_ref[:RET_DK, h:h + 1] + upd[:RET_DK]
        cross = cross * rowd_ref[:, h:h + 1]
        r_ref[:, sl] = _ret_gate(inner + cross, zg_ref[:, sl], g_ref[:, sl]).astype(BF16)


def _ret_sample(rq, rk, rv, zg, state, tabs, g, n_tok):
    n_seq_total = state.shape[0]
    n_seq = LANES // n_tok
    rows = n_seq * n_tok
    w4 = RET_HEADS * LANES
    row = lambda i: (i, 0)
    const = lambda i: (0, 0)
    in_specs = [pl.BlockSpec((rows, w4), row)] * 4
    in_specs += [pl.BlockSpec((n_seq, RET_HEADS, RET_DK, RET_DV), lambda i: (i, 0, 0, 0)),
                 pl.BlockSpec((RET_HEADS, rows, rows), lambda i: (0, 0, 0)),
                 pl.BlockSpec((rows, RET_HEADS), const), pl.BlockSpec((rows, RET_HEADS), const),
                 pl.BlockSpec((LANES, RET_HEADS), const), pl.BlockSpec((1, w4), const)]
    out_shape = [jax.ShapeDtypeStruct((n_seq_total * n_tok, w4), BF16),
                 jax.ShapeDtypeStruct(state.shape, F32)]
    out_specs = [pl.BlockSpec((rows, w4), row),
                 pl.BlockSpec((n_seq, RET_HEADS, RET_DK, RET_DV), lambda i: (i, 0, 0, 0))]
    return pl.pallas_call(
        functools.partial(_ret_sample_kernel, n_seq=n_seq, n_tok=n_tok),
        grid=(n_seq_total // n_seq,), in_specs=in_specs, out_specs=out_specs, out_shape=out_shape,
        compiler_params=_params(("arbitrary",)), name="ret_sample",
    )(rq, rk, rv, zg, state, tabs['dec'], tabs['rowd'], tabs['cold'], tabs['cpow'], g)


def _attn_prompt_kernel(q_ref, k_ref, v_ref, km_ref, vm_ref, o_ref, m_scr, l_scr, acc_scr, *, sub_tiles):
    qi = pl.program_id(2)
    ki = pl.program_id(3)
    tm, tk = q_ref.shape[0], k_ref.shape[0]

    @pl.when(ki == 0)
    def _():
        lane = lax.broadcasted_iota(jnp.int32, (tm, km_ref.shape[0]), 1)
        for hh in range(2):
            sl = slice(hh * HEAD_PAD, (hh + 1) * HEAD_PAD)
            s = jnp.where(lane < N_META, _dot_nt(q_ref[:, sl], km_ref[:, sl]), NEG_BIG)
            m = jnp.max(s, axis=-1, keepdims=True)
            p = jnp.exp2(s - m)
            m_scr[hh] = jnp.broadcast_to(m, (tm, LANES))
            l_scr[hh] = jnp.broadcast_to(jnp.sum(p, axis=-1, keepdims=True), (tm, LANES))
            acc_scr[hh] = _dot(p.astype(BF16), vm_ref[...])

    def tile(r0, nr, c0, nc, masked):
        rows = pl.ds(r0, nr)
        if masked:
            keep = (lax.broadcasted_iota(jnp.int32, (nr, nc), 1) <= lax.broadcasted_iota(jnp.int32, (nr, nc), 0))
        for hh in range(2):
            sl = slice(hh * HEAD_PAD, (hh + 1) * HEAD_PAD)
            s = _dot_nt(q_ref[rows, sl], k_ref[pl.ds(c0, nc), sl])
            if masked:
                s = jnp.where(keep, s, NEG_BIG)
            m_old = m_scr[hh, rows, :]
            m_new = jnp.maximum(m_old, jnp.max(s, axis=-1, keepdims=True))
            alpha = jnp.exp2(m_old - m_new)
            p = jnp.exp2(s - jnp.tile(m_new, (1, nc // LANES)))
            m_scr[hh, rows, :] = m_new
            l_scr[hh, rows, :] = alpha * l_scr[hh, rows, :] + jnp.sum(p, axis=-1, keepdims=True)
            acc_scr[hh, rows, :] = alpha * acc_scr[hh, rows, :] + _dot(p.astype(BF16), v_ref[pl.ds(c0, nc), :])

    @pl.when(ki < qi)
    def _():
        tile(0, tm, 0, tk, False)

    @pl.when(ki == qi)
    def _():
        ns = sub_tiles
        st = tm // ns
        for a in range(ns):
            if a > 0:
                tile(a * st, st, 0, a * st, False)
            tile(a * st, st, a * st, st, True)
        lane = lax.broadcasted_iota(jnp.int32, (tm, 2 * MLA_V), 1)
        o0 = acc_scr[0] / l_scr[0]
        o1 = acc_scr[1] / l_scr[1]
        o_ref[...] = jnp.where(lane < MLA_V, o0, o1).astype(BF16)


def _attn_prompt(mq, k, v, km, vm, batch, seq, blk):
    nb = seq // blk
    pairs = MLA_HEADS // 2
    grid = (batch, pairs, nb, nb)
    in_specs = [pl.BlockSpec((blk, 2 * HEAD_PAD), lambda b, h, qi, ki: (b * nb + qi, h)),
                pl.BlockSpec((blk, 2 * HEAD_PAD), lambda b, h, qi, ki: (b * nb + jnp.minimum(ki, qi), h)),
                pl.BlockSpec((blk, 2 * MLA_V), lambda b, h, qi, ki: (b * nb + jnp.minimum(ki, qi), h)),
                pl.BlockSpec((km.shape[0], 2 * HEAD_PAD), lambda b, h, qi, ki: (0, h)),
                pl.BlockSpec((vm.shape[0], 2 * MLA_V), lambda b, h, qi, ki: (0, h))]
    out_specs = pl.BlockSpec((blk, 2 * MLA_V), lambda b, h, qi, ki: (b * nb + qi, h))
    return pl.pallas_call(
        functools.partial(_attn_prompt_kernel, sub_tiles=2 if blk % (2 * LANES) == 0 else 1), grid=grid, in_specs=in_specs, out_specs=out_specs,
        out_shape=jax.ShapeDtypeStruct((batch * seq, MLA_HEADS * MLA_V), BF16),
        scratch_shapes=[pltpu.VMEM((2, blk, LANES), F32), pltpu.VMEM((2, blk, LANES), F32),
                        pltpu.VMEM((2, blk, 2 * MLA_V), F32)],
        compiler_params=_params(("arbitrary",) * 4), name="attn_prompt",
    )(mq, k, v, km, vm)


def _absorb_kernel(mq_ref, wabs_ref, qt_ref, qf_ref):
    q = mq_ref[...]
    qf_ref[...] = q.astype(F32)
    for h in range(MLA_HEADS):
        qt_ref[:, h * KV_LORA:(h + 1) * KV_LORA] = _dot(q[:, h * HEAD_PAD:(h + 1) * HEAD_PAD], wabs_ref[h])


def _absorb(mq, wabs, row0_blocks, rows, tm):
    return pl.pallas_call(
        _absorb_kernel, grid=(rows // tm,),
        in_specs=[pl.BlockSpec((tm, MLA_HEADS * HEAD_PAD), lambda i: (row0_blocks + i, 0)),
                  pl.BlockSpec(wabs.shape, lambda i: (0, 0, 0))],
        out_specs=[pl.BlockSpec((tm, MLA_HEADS * KV_LORA), lambda i: (i, 0)),
                   pl.BlockSpec((tm, MLA_HEADS * HEAD_PAD), lambda i: (i, 0))],
        out_shape=[jax.ShapeDtypeStruct((rows, MLA_HEADS * KV_LORA), F32),
                   jax.ShapeDtypeStruct((rows, MLA_HEADS * HEAD_PAD), F32)],
        compiler_params=_params(("arbitrary",)), name="absorb",
    )(mq, wabs)


def _attn_sample_kernel(pt_ref, lat_hbm, kpe_hbm, cos_ref, sin_ref, qt_ref, qf_ref, cn_ref, kn_ref, cosn_ref,
                        sinn_ref, lw_ref, wuv_ref, gpe_ref, o_ref, l_scr, qpe_scr, m_scr, d_scr, acc_scr, kpad_scr,
                        cpad_scr, lat_buf, kpe_buf, sem, *, n_pages, sub, n_tok, page):
    s_id = pl.program_id(0)
    j = pl.program_id(1)
    n_steps = pl.num_programs(1)
    step = s_id * n_steps + j
    last = pl.num_programs(0) * n_steps - 1
    slot = lax.rem(step, 2)
    nq = MLA_HEADS * n_tok
    n_up = MLA_HEADS * MLA_NOPE

    def page_copies(pid, sl, p):
        return (pltpu.make_async_copy(lat_hbm.at[0, pid], lat_buf.at[sl, p], sem.at[sl, 0]),
                pltpu.make_async_copy(kpe_hbm.at[0, pid], kpe_buf.at[sl, p], sem.at[sl, 1]))

    def start_pages(seq, st, sl, pages=range(n_pages)):
        for p in pages:
            for c in page_copies(pt_ref[seq, st * n_pages + p], sl, p):
                c.start()

    def wait_pages(sl):
        for p in range(n_pages):
            for c in page_copies(0, sl, p):
                c.wait()

    @pl.when(step == 0)
    def _():
        start_pages(0, 0, 0)

    wait_pages(slot)
    wrap = j == n_steps - 1
    nxt_seq = jnp.where(wrap, jnp.where(step == last, 0, s_id + 1), s_id)
    nxt_j = jnp.where(wrap, 0, j + 1)
    start_pages(nxt_seq, nxt_j, 1 - slot)
    lat_refs = [lat_buf.at[slot, p] for p in range(n_pages)]
    kpe_refs = [kpe_buf.at[slot, p] for p in range(n_pages)]

    @pl.when(j == 0)
    def _():
        l_scr[:n_up, :] = lw_ref[...]
        qt = qt_ref[...]
        qf = qf_ref[...]
        l_scr[n_up:, :] = jnp.concatenate(
            [qt[:, h * KV_LORA:(h + 1) * KV_LORA] for h in range(MLA_HEADS)], axis=0).astype(BF16)
        lane = lax.broadcasted_iota(jnp.int32, (nq, HEAD_PAD), 1)
        qpe = jnp.concatenate([qf[:, h * HEAD_PAD:(h + 1) * HEAD_PAD] for h in range(MLA_HEADS)], axis=0)
        qpe_scr[...] = jnp.where(lane < MLA_ROPE, qpe, 0.0).astype(BF16)
        m_scr[...] = jnp.full(m_scr.shape, NEG_BIG, F32)
        d_scr[...] = jnp.zeros(d_scr.shape, F32)
        acc_scr[...] = jnp.zeros(acc_scr.shape, F32)

    def scores(cb, kpe_t, cos_t, sin_t):
        tk = cb.shape[0]
        big = _dot_nt(l_scr[...], cb)
        k_t = big[:n_up]
        ss = jnp.sum((k_t * k_t).reshape(MLA_HEADS, MLA_NOPE, tk), axis=1)
        ss = ss + jnp.sum(kpe_t * kpe_t, axis=0, keepdims=True)
        rs = lax.rsqrt(ss * (1.0 / MLA_QK) + EPS)
        kg = kpe_t * gpe_ref[...]
        half = MLA_ROPE // 2
        x1, x2 = kg[:half], kg[half:]
        rot = jnp.concatenate([x1 * cos_t - x2 * sin_t, x1 * sin_t + x2 * cos_t], axis=0)
        pe = _dot(qpe_scr[:, :MLA_ROPE], rot.astype(BF16))
        rs_rows = jnp.concatenate([jnp.broadcast_to(rs[h:h + 1, :], (n_tok, tk)) for h in range(MLA_HEADS)], axis=0)
        return (big[n_up:] + pe) * rs_rows

    def update(s_list, cb_list):
        m_old = m_scr[...]
        m_new = m_old
        for s in s_list:
            m_new = jnp.maximum(m_new, jnp.max(s, axis=-1, keepdims=True))
        alpha = jnp.exp2(m_old - m_new)
        d = alpha * d_scr[...]
        acc = alpha * acc_scr[...]
        for s, cb in zip(s_list, cb_list):
            p = jnp.exp2(s - m_new)
            d = d + jnp.sum(p, axis=-1, keepdims=True)
            acc = acc + _dot(p.astype(BF16), cb)
        m_scr[...] = m_new
        d_scr[...] = d
        acc_scr[...] = acc

    s_list, cb_list = [], []
    for g in range(n_pages // sub):
        pages = range(g * sub, (g + 1) * sub)
        cb = jnp.concatenate([lat_refs[b][...] for b in pages], axis=0).astype(BF16)
        kpe_t = jnp.concatenate([kpe_refs[b][...] for b in pages], axis=1)
        cols = slice(g * sub * page, (g + 1) * sub * page)
        s_list.append(scores(cb, kpe_t, cos_ref[:, cols], sin_ref[:, cols]))
        cb_list.append(cb)
    update(s_list, cb_list)

    @pl.when(j == pl.num_programs(1) - 1)
    def _():
        cpad_scr[...] = jnp.zeros(cpad_scr.shape, F32)
        cpad_scr[:n_tok, :] = cn_ref[...]
        kpad_scr[...] = jnp.zeros(kpad_scr.shape, F32)
        kpad_scr[:n_tok, :MLA_ROPE] = kn_ref[...]
        rowi = lax.broadcasted_iota(jnp.int32, (nq, page), 0)
        coli = lax.broadcasted_iota(jnp.int32, (nq, page), 1)
        keep = coli <= (rowi & (n_tok - 1))
        cb = cpad_scr[...].astype(BF16)
        s = scores(cb, kpad_scr[...].T[:MLA_ROPE], cosn_ref[...], sinn_ref[...])
        update([jnp.where(keep, s, NEG_BIG)], [cb])
        ctx = (acc_scr[...] / d_scr[...]).astype(BF16)
        out = jnp.zeros(o_ref.shape, F32)
        for h in range(MLA_HEADS):
            out = out + _dot(ctx, wuv_ref[h])[h * n_tok:(h + 1) * n_tok]
        o_ref[...] = out

    @pl.when(step == last)
    def _():
        wait_pages(1 - slot)


def _attn_sample(page_table, cache_latent, cache_krope_t, tabs, qt, qf, ckv, kpe, w, n_tok, n_pages_step, sub):
    n_seq, n_pages = page_table.shape
    page = cache_latent.shape[2]
    n_steps = n_pages // n_pages_step
    tk = n_pages_step * page
    nq = MLA_HEADS * n_tok

    seq_row = lambda s, j, pt: (s, 0)
    const2 = lambda s, j, pt: (0, 0)
    half = MLA_ROPE // 2
    in_specs = [pl.BlockSpec(memory_space=pl.ANY), pl.BlockSpec(memory_space=pl.ANY)]
    in_specs += [pl.BlockSpec((half, tk), lambda s, j, pt: (0, j)), pl.BlockSpec((half, tk), lambda s, j, pt: (0, j)),
                 pl.BlockSpec((n_tok, MLA_HEADS * KV_LORA), seq_row),
                 pl.BlockSpec((n_tok, MLA_HEADS * HEAD_PAD), seq_row),
                 pl.BlockSpec((n_tok, KV_LORA), seq_row), pl.BlockSpec((n_tok, MLA_ROPE), seq_row),
                 pl.BlockSpec((half, page), const2), pl.BlockSpec((half, page), const2),
                 pl.BlockSpec(w['uk_t'].shape, const2),
                 pl.BlockSpec(w['uv_blk'].shape, lambda s, j, pt: (0, 0, 0)),
                 pl.BlockSpec((MLA_ROPE, 1), const2)]
    grid_spec = pltpu.PrefetchScalarGridSpec(
        num_scalar_prefetch=1, grid=(n_seq, n_steps), in_specs=in_specs,
        out_specs=pl.BlockSpec((n_tok, MLA_HEADS * MLA_V), seq_row),
        scratch_shapes=[pltpu.VMEM((MLA_HEADS * MLA_NOPE + nq, KV_LORA), BF16),
                        pltpu.VMEM((nq, HEAD_PAD), BF16),
                        pltpu.VMEM((nq, 1), F32), pltpu.VMEM((nq, 1), F32), pltpu.VMEM((nq, KV_LORA), F32),
                        pltpu.VMEM((page, LANES), F32), pltpu.VMEM((page, KV_LORA), F32),
                        pltpu.VMEM((2, n_pages_step, page, KV_LORA), F32),
                        pltpu.VMEM((2, n_pages_step, MLA_ROPE, page), F32),
                        pltpu.SemaphoreType.DMA((2, 2))])
    return pl.pallas_call(
        functools.partial(_attn_sample_kernel, n_pages=n_pages_step, sub=sub, n_tok=n_tok, page=page),
        grid_spec=grid_spec, out_shape=jax.ShapeDtypeStruct((n_seq * n_tok, MLA_HEADS * MLA_V), F32),
        compiler_params=_params(("arbitrary", "arbitrary")), name="attn_sample",
    )(page_table, cache_latent, cache_krope_t,
      tabs['cos_t'], tabs['sin_t'], qt, qf, ckv, kpe, tabs['cosn_t'], tabs['sinn_t'],
      w['uk_t'], w['uv_blk'], w['gpe'])


def _post_kernel(r_ref, m_ref, x_ref, wo_ref, g_ref, wrh_ref, wrl_ref, br_ref, h_ref, xn_ref, ti_ref, tg_ref):
    half = wo_ref.shape[0] // 2
    mix = _dot(r_ref[...], wo_ref[:half, :]) + _dot(m_ref[...], wo_ref[half:, :])
    h = x_ref[...] + mix
    h_ref[...] = h
    xn = _rms(h, g_ref[...])
    xn_ref[...] = xn
    hi = xn.astype(BF16)
    lo = (xn - hi.astype(F32)).astype(BF16)
    wrh = wrh_ref[...]
    work = _dot(hi, wrh) + _dot(hi, wrl_ref[...]) + _dot(lo, wrh) + br_ref[...]
    lane = lax.broadcasted_iota(jnp.int32, work.shape, 1).astype(F32)
    idx = jnp.zeros(work.shape, F32)
    val = jnp.zeros(work.shape, F32)
    vmax = None
    denom = None
    for k in range(TOP_K):
        vk = jnp.max(work, axis=-1, keepdims=True)
        ik = jnp.min(jnp.where(work == vk, lane, float(LANES)), axis=-1, keepdims=True)
        work = jnp.where(lane == ik, -jnp.inf, work)
        if k == 0:
            vmax = vk
        ek = jnp.exp(vk - vmax)
        denom = ek if k == 0 else denom + ek
        idx = jnp.where(lane == float(k), ik, idx)
        val = jnp.where(lane == float(k), ek, val)
    ti_ref[...] = idx.astype(jnp.int32)
    tg_ref[...] = val / denom


def _post(r, m, x, w, tm):
    rows = x.shape[0]
    row = lambda i: (i, 0)
    const = lambda i: (0, 0)
    in_specs = [pl.BlockSpec((tm, r.shape[1]), row), pl.BlockSpec((tm, m.shape[1]), row),
                pl.BlockSpec((tm, D_MODEL), row)]
    in_specs += [pl.BlockSpec(w[n].shape, const) for n in ('wo', 'gffn', 'wr_hi', 'wr_lo', 'br')]
    out_shape = [jax.ShapeDtypeStruct((rows, D_MODEL), F32), jax.ShapeDtypeStruct((rows, D_MODEL), F32),
                 jax.ShapeDtypeStruct((rows, LANES), jnp.int32), jax.ShapeDtypeStruct((rows, LANES), F32)]
    out_specs = [pl.BlockSpec((tm, D_MODEL), row), pl.BlockSpec((tm, D_MODEL), row),
                 pl.BlockSpec((tm, LANES), row), pl.BlockSpec((tm, LANES), row)]
    return pl.pallas_call(
        _post_kernel, grid=(rows // tm,), in_specs=in_specs, out_specs=out_specs, out_shape=out_shape,
        compiler_params=_params(("arbitrary",)), name="post",
    )(r, m, x, w['wo'], w['gffn'], w['wr_hi'], w['wr_lo'], w['br'])


def _moe_kernel(be_ref, nu_ref, xs_ref, wgu_ref, bgu_ref, wd_ref, bd_ref, o_ref, wgu_b, wd_b):
    i = pl.program_id(0)
    e = be_ref[i]
    e_prev = be_ref[jnp.maximum(i - 1, 0)]

    @pl.when((i == 0) | (e != e_prev))
    def _():
        wgu_b[...] = wgu_ref[...].astype(BF16)
        wd_b[...] = wd_ref[...].astype(BF16)

    @pl.when(i < nu_ref[0])
    def _():
        hgu = _dot(xs_ref[...].astype(BF16), wgu_b[...]) + bgu_ref[...]
        g = jnp.minimum(hgu[:, :D_FF], SWIGLU_LIMIT)
        u = jnp.clip(hgu[:, D_FF:], -SWIGLU_LIMIT, SWIGLU_LIMIT)
        hid = (u + 1.0) * (g * jax.nn.sigmoid(SWIGLU_ALPHA * g))
        o_ref[...] = _dot(hid.astype(BF16), wd_b[...]) + bd_ref[...]

    @pl.when(i >= nu_ref[0])
    def _():
        o_ref[...] = jnp.zeros(o_ref.shape, F32)


def _moe_blocks(block_e, n_used, xs, w_gate_up, b_gate_up, w_down, b_down, blk):
    n_blocks = block_e.shape[0]
    row = lambda i, be, nu: (i, 0)
    in_specs = [pl.BlockSpec((blk, D_MODEL), row),
                pl.BlockSpec((None, D_MODEL, 2 * D_FF), lambda i, be, nu: (be[i], 0, 0)),
                pl.BlockSpec((None, 1, 2 * D_FF), lambda i, be, nu: (be[i], 0, 0)),
                pl.BlockSpec((None, D_FF, D_MODEL), lambda i, be, nu: (be[i], 0, 0)),
                pl.BlockSpec((None, 1, D_MODEL), lambda i, be, nu: (be[i], 0, 0))]
    grid_spec = pltpu.PrefetchScalarGridSpec(
        num_scalar_prefetch=2, grid=(n_blocks,), in_specs=in_specs,
        out_specs=pl.BlockSpec((blk, D_MODEL), row),
        scratch_shapes=[pltpu.VMEM((D_MODEL, 2 * D_FF), BF16), pltpu.VMEM((D_FF, D_MODEL), BF16)])
    return pl.pallas_call(
        _moe_kernel, grid_spec=grid_spec, out_shape=jax.ShapeDtypeStruct((n_blocks * blk, D_MODEL), F32),
        compiler_params=_params(("arbitrary",)), name="moe",
    )(block_e, n_used, xs, w_gate_up, b_gate_up[:, None, :], w_down, b_down[:, None, :])


def _combine_kernel(h_ref, g_ref, y_ref, o_ref):
    g = g_ref[...]
    acc = y_ref[0] * g[:, 0:1]
    for k in range(1, TOP_K):
        acc = acc + y_ref[k] * g[:, k:k + 1]
    o_ref[...] = h_ref[...] + acc


def _combine(h, gates, y4, block0, rows, tm):
    row = lambda i: (block0 + i, 0)
    return pl.pallas_call(
        _combine_kernel, grid=(rows // tm,),
        in_specs=[pl.BlockSpec((tm, D_MODEL), row), pl.BlockSpec((tm, LANES), row),
                  pl.BlockSpec((TOP_K, tm, D_MODEL), lambda i: (0, block0 + i, 0))],
        out_specs=pl.BlockSpec((tm, D_MODEL), lambda i: (i, 0)),
        out_shape=jax.ShapeDtypeStruct((rows, D_MODEL), F32),
        compiler_params=_params(("arbitrary",)), name="combine",
    )(h, gates, y4)


def _gather_rows(table, idx):
    n, d = idx.shape[0], table.shape[1]
    n_win = n // GATHER_WINDOW
    assert n % (GATHER_WINDOW * SC_CORES * SC_SUBCORES) == 0
    idx = jnp.zeros((n_win, LANES), jnp.int32).at[:, :GATHER_WINDOW].set(idx.reshape(n_win, GATHER_WINDOW))
    mesh = plsc.VectorSubcoreMesh(core_axis_name="core", subcore_axis_name="subcore",
                                  num_cores=SC_CORES, num_subcores=SC_SUBCORES)

    @functools.partial(pl.kernel, out_type=jax.ShapeDtypeStruct((n, d), table.dtype), mesh=mesh)
    def gather(x_hbm, i_hbm, o_hbm):
        def body(i_vmem, o_vmem):
            pltpu.sync_copy(x_hbm.at[i_vmem.at[0, pl.ds(0, GATHER_WINDOW)]], o_vmem)

        pltpu.emit_pipeline(
            body, grid=(n_win,),
            in_specs=[pl.BlockSpec((1, LANES), index_map=lambda i: (i, 0))],
            out_specs=[pl.BlockSpec((GATHER_WINDOW, d), index_map=lambda i: (i, 0))],
            core_axis_name=("core", "subcore"), dimension_semantics=(pltpu.PARALLEL,),
        )(i_hbm, o_hbm)

    return gather(table, idx)


def _route(top_i, blk):
    t = top_i.shape[0]
    a = t * TOP_K
    flat_e = top_i.reshape(a)
    order = jnp.argsort(flat_e)
    tok_sorted = (order // TOP_K).astype(jnp.int32)
    experts = jnp.arange(N_EXPERTS, dtype=jnp.int32)
    hit = (top_i[:, :, None] == experts[None, None, :]).astype(jnp.int32)
    per_tok = hit.sum(axis=1)
    before = jnp.cumsum(per_tok, axis=0) - per_tok
    counts = per_tok.sum(axis=0)
    padded = (counts + blk - 1) // blk * blk
    start = jnp.cumsum(counts) - counts
    pend = jnp.cumsum(padded)
    pstart = pend - padded
    dest = ((before + pstart[None, :])[:, None, :] * hit).sum(axis=2)
    n_blocks = -(-a // blk) + N_EXPERTS
    block_e = jnp.minimum((pend[None, :] <= (jnp.arange(n_blocks, dtype=jnp.int32) * blk)[:, None]).sum(axis=1),
                          N_EXPERTS - 1).astype(jnp.int32)
    slot = jnp.arange(n_blocks * blk, dtype=jnp.int32)
    slot_e = jnp.repeat(block_e, blk)
    r = slot - pstart[slot_e]
    slot_tok = jnp.where(r < counts[slot_e], tok_sorted[jnp.minimum(start[slot_e] + r, a - 1)], slot % t)
    n_used = (pend[-1:] // blk).astype(jnp.int32)
    return slot_tok, dest.astype(jnp.int32), block_e, n_used


def _rope_tables(pos):
    pos = pos.astype(F32)[:, None]
    n = pos.shape[0]
    hr = RET_DK // 2
    ang = pos * (RET_THETA ** (-jnp.arange(hr, dtype=F32) / hr))[None, :]
    cos, sin = jnp.cos(ang), jnp.sin(ang)
    zr = jnp.zeros((n, LANES - RET_DK), F32)
    cr = jnp.concatenate([cos, cos, zr], axis=1)
    sr = jnp.concatenate([-sin, sin, zr], axis=1)
    hm = MLA_ROPE // 2
    ang = pos * (MLA_THETA ** (-jnp.arange(hm, dtype=F32) / hm))[None, :]
    cos, sin = jnp.cos(ang), jnp.sin(ang)
    cm = jnp.concatenate([cos, cos, jnp.ones((n, LANES - MLA_ROPE), F32)], axis=1)
    sma = jnp.concatenate([jnp.zeros((n, hm), F32), sin, jnp.zeros((n, LANES - MLA_ROPE), F32)], axis=1)
    smb = jnp.concatenate([-sin, jnp.zeros((n, LANES - hm), F32)], axis=1)
    return [cr, sr, cm, sma, smb]


def _rope_tables_t(pos):
    hm = MLA_ROPE // 2
    ang = pos.astype(F32)[:, None] * (MLA_THETA ** (-jnp.arange(hm, dtype=F32) / hm))[None, :]
    return jnp.cos(ang).T, jnp.sin(ang).T


def _decay_tables(log_gamma, c, n_rep):
    idx = jnp.arange(c, dtype=F32)
    diff = idx[:, None] - idx[None, :]
    decay = jnp.where(diff >= 0, jnp.exp(jnp.maximum(diff, 0.0)[None] * log_gamma[:, None, None]), 0.0)
    if n_rep > 1:
        eye = jnp.eye(n_rep, dtype=F32)
        decay = (eye[None, :, None, :, None] * decay[:, None, :, None, :]).reshape(RET_HEADS, n_rep * c, n_rep * c)
    rowd = jnp.tile(jnp.exp((idx + 1.0)[:, None] * log_gamma[None, :]), (n_rep, 1))
    cold = jnp.tile(jnp.exp((c - 1.0 - idx)[:, None] * log_gamma[None, :]), (n_rep, 1))
    cpow = jnp.broadcast_to(jnp.exp(c * log_gamma)[None, :], (LANES, RET_HEADS))
    return {'dec': decay, 'rowd': rowd, 'cold': cold, 'cpow': cpow}


def _pad_heads(wm, n_heads, width, offset):
    k = wm.shape[0]
    wm = wm.reshape(k, n_heads, width)
    out = jnp.zeros((k, n_heads, LANES), wm.dtype).at[:, :, offset:offset + width].set(wm)
    return out.reshape(k, n_heads * LANES)


def _prep_weights(norm_attn_g, w_in, ret_out_g, q_a_norm_g, w_q_b, kv_a_norm_g, w_uk, w_uv,
                  qk_norm_q_g, qk_norm_k_g, w_out, norm_ffn_g, w_router, b_router):
    splits = [RET_HEADS * RET_DK, RET_HEADS * RET_DK, RET_HEADS * RET_DV, RET_HEADS * RET_DV, Q_LORA, KV_LORA,
              MLA_ROPE]
    offs = [0]
    for s in splits:
        offs.append(offs[-1] + s)
    part = [w_in[:, offs[i]:offs[i + 1]] for i in range(len(splits))]
    kpe_cols = jnp.zeros((D_MODEL, LANES), F32).at[:, :MLA_ROPE].set(part[6])
    win = jnp.concatenate([_pad_heads(part[0], RET_HEADS, RET_DK, 0), _pad_heads(part[1], RET_HEADS, RET_DK, 0),
                           part[2], part[3], part[4], part[5], kpe_cols], axis=1).astype(BF16)

    def mla_cols(wm):
        k = wm.shape[0]
        wm = wm.reshape(k, MLA_HEADS, MLA_QK)
        wm = jnp.concatenate([wm[:, :, MLA_NOPE:], wm[:, :, :MLA_NOPE]], axis=2)
        return _pad_heads(wm.reshape(k, MLA_HEADS * MLA_QK), MLA_HEADS, MLA_QK, 0)

    def mla_gain(g):
        g = jnp.concatenate([g[MLA_NOPE:], g[:MLA_NOPE], jnp.zeros((LANES - MLA_QK,), F32)])
        return g[None, :]

    gk_nope = qk_norm_k_g[:MLA_NOPE]
    uk_heads = w_uk.reshape(KV_LORA, MLA_HEADS, MLA_NOPE)
    wabs = jnp.zeros((MLA_HEADS, HEAD_PAD, KV_LORA), F32).at[:, MLA_ROPE:MLA_ROPE + MLA_NOPE, :].set(
        (uk_heads * gk_nope[None, None, :]).transpose(1, 2, 0))
    head_of_col = jnp.arange(MLA_HEADS * MLA_V) // MLA_V
    uv_blk = jnp.where(head_of_col[None, None, :] == jnp.arange(MLA_HEADS)[:, None, None], w_uv[None], 0.0)
    wr = jnp.zeros((D_MODEL, LANES), F32).at[:, :N_EXPERTS].set(w_router)
    wr_hi = wr.astype(BF16)
    return {
        'gattn': norm_attn_g[None, :], 'win': win, 'gqa': q_a_norm_g[None, :], 'wqb': mla_cols(w_q_b).astype(BF16),
        'gq': mla_gain(qk_norm_q_g), 'gkv': kv_a_norm_g[None, :],
        'wuk': _pad_heads(w_uk, MLA_HEADS, MLA_NOPE, MLA_ROPE).astype(BF16), 'gk': mla_gain(qk_norm_k_g),
        'wuv': w_uv.astype(BF16), 'gret': ret_out_g[None, :],
        'wabs': wabs.astype(BF16), 'uk_t': w_uk.T.astype(BF16), 'uv_blk': uv_blk.astype(BF16),
        'gpe': qk_norm_k_g[MLA_NOPE:, None],
        'wo': w_out.astype(BF16), 'gffn': norm_ffn_g[None, :], 'wr_hi': wr_hi,
        'wr_lo': (wr - wr_hi.astype(F32)).astype(BF16),
        'br': jnp.full((1, LANES), NEG_BIG, F32).at[0, :N_EXPERTS].set(b_router),
    }


def _pad_rows(a, rows):
    return jnp.zeros((rows,) + a.shape[1:], a.dtype).at[:a.shape[0]].set(a)


def _largest_divisor(n, cap):
    d = min(n, cap)
    while n % d:
        d -= 1
    return d


def kernel(x_prompt, x_sample, cache_latent, cache_krope, state_retention, page_table, meta_tokens, norm_attn_g, w_in, ret_out_g, q_a_norm_g, w_q_b, kv_a_norm_g, w_uk, w_uv, qk_norm_q_g, qk_norm_k_g, w_out, norm_ffn_g, w_router, b_router, w_gate_up, b_gate_up, w_down, b_down):
    assert w_in.shape[0] == 1, "single-layer trunk"
    batch, seq, _ = x_prompt.shape
    n_seq, n_tok, _ = x_sample.shape
    n_pages, page = page_table.shape[1], cache_latent.shape[2]
    past = n_pages * page
    assert seq % RET_CHUNK == 0 and LANES % n_tok == 0 and (n_seq * n_tok) % LANES == 0 and page == LANES
    w = _prep_weights(norm_attn_g[0], w_in[0], ret_out_g[0], q_a_norm_g[0], w_q_b[0], kv_a_norm_g[0], w_uk[0],
                      w_uv[0], qk_norm_q_g[0], qk_norm_k_g[0], w_out[0], norm_ffn_g[0], w_router[0], b_router[0])
    log_gamma = jnp.log1p(-jnp.exp2(-5.0 - jnp.arange(RET_HEADS, dtype=F32)))

    rows_p = batch * seq
    rows_s = n_seq * n_tok
    tm = _largest_divisor(min(seq, rows_s), ROW_BLOCK)
    nb_seq = seq // tm
    x = jnp.concatenate([x_prompt.reshape(rows_p, D_MODEL), x_sample.reshape(rows_s, D_MODEL)], axis=0)
    pos_rows = jnp.concatenate([N_META + jnp.arange(seq), jnp.tile(past + jnp.arange(n_tok), tm // n_tok)])
    tabs = _rope_tables(pos_rows)
    n_pb = rows_p // tm
    rq, rk, rv, zg, mq, ckv, kpe, k, v = _front(
        x, tabs, lambda i: jnp.where(i < n_pb, i % nb_seq, nb_seq), w, tm)
    _, mrk, mrv, _, _, mckv, mkpe, mk, mv = _front(
        meta_tokens, _rope_tables(jnp.arange(N_META)), lambda i: i, w, N_META)

    dt_p = _decay_tables(log_gamma, RET_CHUNK, 1)
    dt_p['mcol'] = _pad_rows(jnp.exp((N_META - 1.0 - jnp.arange(N_META, dtype=F32))[:, None] * log_gamma[None, :]),
                             RET_CHUNK)
    gret = w['gret']
    r_p, st_p = _ret_prompt(rq, rk, rv, zg, _pad_rows(mrk, RET_CHUNK), _pad_rows(mrv, RET_CHUNK), dt_p, gret,
                            batch, seq // RET_CHUNK)
    dt_s = _decay_tables(log_gamma, n_tok, LANES // n_tok)
    r_s, st_s = _ret_sample(rq[rows_p:], rk[rows_p:], rv[rows_p:], zg[rows_p:], state_retention[0], dt_s, gret, n_tok)

    blk = _largest_divisor(seq, ATTN_BLOCK)
    m_p = _attn_prompt(mq, k, v, _pad_rows(mk, LANES), _pad_rows(mv, LANES), batch, seq, blk)
    tm_s = _largest_divisor(rows_s, ROW_BLOCK)
    qt, qf = _absorb(mq, w['wabs'], rows_p // tm_s, rows_s, tm_s)
    n_pages_step = _largest_divisor(n_pages, PAGES_PER_STEP)
    cos_t, sin_t = _rope_tables_t(jnp.arange(past))
    cosn_t, sinn_t = _rope_tables_t(past + jnp.arange(page))
    tabs_s = {'cos_t': cos_t, 'sin_t': sin_t, 'cosn_t': cosn_t, 'sinn_t': sinn_t}
    m_s = _attn_sample(page_table, cache_latent, jnp.swapaxes(cache_krope, 2, 3), tabs_s, qt, qf, ckv[rows_p:],
                       kpe[rows_p:], w, n_tok, n_pages_step, _largest_divisor(n_pages_step, PAGES_PER_CHAIN))

    r_all = jnp.concatenate([r_p, r_s], axis=0)
    m_all = jnp.concatenate([m_p, m_s.astype(BF16)], axis=0)
    h1, xn2, top_i, gates = _post(r_all, m_all, x, w, tm)

    slot_tok, dest, block_e, n_used = _route(top_i[:, :TOP_K], EXPERT_BLOCK)
    outs = _moe_blocks(block_e, n_used, _gather_rows(xn2, slot_tok), w_gate_up[0], b_gate_up[0], w_down[0],
                       b_down[0], EXPERT_BLOCK)
    y4 = _gather_rows(outs, dest.T.reshape(-1)).reshape(TOP_K, rows_p + rows_s, D_MODEL)
    y_prompt = _combine(h1, gates, y4, 0, rows_p, tm).reshape(batch, seq, D_MODEL)
    y_sample = _combine(h1, gates, y4, n_pb, rows_s, tm).reshape(n_seq, n_tok, D_MODEL)
    lat_p = jnp.concatenate([jnp.broadcast_to(mckv[None], (batch, N_META, KV_LORA)),
                             ckv[:rows_p].reshape(batch, seq, KV_LORA)], axis=1)[None]
    kpe_p = jnp.concatenate([jnp.broadcast_to(mkpe[None], (batch, N_META, MLA_ROPE)),
                             kpe[:rows_p].reshape(batch, seq, MLA_ROPE)], axis=1)[None]
    return (y_prompt, y_sample, lat_p, kpe_p, st_p[None],
            ckv[rows_p:].reshape(n_seq, n_tok, KV_LORA)[None], kpe[rows_p:].reshape(n_seq, n_tok, MLA_ROPE)[None],
            st_s[None])
```

```python
import functools

import jax
import jax.numpy as jnp
from jax import lax
from jax.experimental import pallas as pl
from jax.experimental.pallas import tpu as pltpu
from jax.experimental.pallas import tpu_sc as plsc

F32 = jnp.float32
BF16 = jnp.bfloat16

D_MODEL = 1024
N_META = 16
RET_HEADS = 4
RET_DK = 64
RET_DV = 128
RET_CHUNK = 128
RET_THETA = 10000.0
MLA_HEADS = 8
MLA_NOPE = 64
MLA_ROPE = 32
MLA_QK = MLA_NOPE + MLA_ROPE
MLA_V = 64
Q_LORA = 384
KV_LORA = 256
MLA_THETA = 10000.0
MLA_SCALE = MLA_QK ** -0.5
LOG2E = 1.4426950408889634
N_EXPERTS = 32
TOP_K = 4
D_FF = 1024
SWIGLU_LIMIT = 7.0
SWIGLU_ALPHA = 1.702
EPS = 1e-6
NEG_BIG = -1e30

LANES = 128
HEAD_PAD = LANES
VMEM_LIMIT = 56 * 1024 * 1024
ROW_BLOCK = 256
ATTN_BLOCK = 1024
PAGES_PER_STEP = 64
PAGES_PER_CHAIN = 32
EXPERT_BLOCK = 256
SC_CORES, SC_SUBCORES = 2, 16
GATHER_WINDOW = 32

_OFF_RQ = 0
_OFF_RK = _OFF_RQ + RET_HEADS * LANES
_OFF_RV = _OFF_RK + RET_HEADS * LANES
_OFF_ZG = _OFF_RV + RET_HEADS * RET_DV
_OFF_CQ = _OFF_ZG + RET_HEADS * RET_DV
_OFF_CKV = _OFF_CQ + Q_LORA
_OFF_KPE = _OFF_CKV + KV_LORA
IN_PAD = _OFF_KPE + LANES


def _params(sem):
    return pltpu.CompilerParams(dimension_semantics=sem, vmem_limit_bytes=VMEM_LIMIT)


def _rms(x, g):
    return x * lax.rsqrt(jnp.mean(x * x, axis=-1, keepdims=True) + EPS) * g


def _dot(a, b):
    return jnp.dot(a, b, preferred_element_type=F32)


def _dot_nt(a, b):
    return lax.dot_general(a, b, (((1,), (1,)), ((), ())), preferred_element_type=F32)


def _dot_tn(a, b):
    return lax.dot_general(a, b, (((0,), (0,)), ((), ())), preferred_element_type=F32)


def _front_kernel(x_ref, gattn_ref, win_ref, cr_ref, sr_ref, cm_ref, sma_ref, smb_ref,
                  gqa_ref, wqb_ref, gq_ref, gkv_ref, wuk_ref, gk_ref, wuv_ref,
                  rq_ref, rk_ref, rv_ref, zg_ref, mq_ref, ckv_ref, kpe_ref, k_ref, v_ref):
    tm = x_ref.shape[0]
    xn = _rms(x_ref[...], gattn_ref[...])
    z = _dot(xn.astype(BF16), win_ref[...])

    lane = lax.broadcasted_iota(jnp.int32, (tm, LANES), 1)
    upper = (lane & (RET_DK // 2)) != 0
    cr, sr = cr_ref[...], sr_ref[...]

    def rope_ret(t):
        partner = jnp.where(upper, pltpu.roll(t, RET_DK // 2, 1), pltpu.roll(t, LANES - RET_DK // 2, 1))
        return t * cr + partner * sr

    for h in range(RET_HEADS):
        sl = slice(h * LANES, (h + 1) * LANES)
        rq_ref[:, sl] = rope_ret(z[:, _OFF_RQ + h * LANES:_OFF_RQ + (h + 1) * LANES])
        rk_ref[:, sl] = rope_ret(z[:, _OFF_RK + h * LANES:_OFF_RK + (h + 1) * LANES]) * (RET_DK ** -0.5)
    rv_ref[...] = z[:, _OFF_RV:_OFF_ZG].astype(BF16)
    zg_ref[...] = z[:, _OFF_ZG:_OFF_CQ]

    cm, sma, smb = cm_ref[...], sma_ref[...], smb_ref[...]

    def rope_mla(t):
        half = MLA_ROPE // 2
        return t * cm + pltpu.roll(t, half, 1) * sma + pltpu.roll(t, LANES - half, 1) * smb

    def head_norm(t, g):
        ms = jnp.sum(t * t, axis=-1, keepdims=True) * (1.0 / MLA_QK)
        return t * lax.rsqrt(ms + EPS) * g

    cq = _rms(z[:, _OFF_CQ:_OFF_CKV], gqa_ref[...])
    q = _dot(cq.astype(BF16), wqb_ref[...])
    gq = gq_ref[...]
    for h in range(MLA_HEADS):
        sl = slice(h * HEAD_PAD, (h + 1) * HEAD_PAD)
        mq_ref[:, sl] = (rope_mla(head_norm(q[:, sl], gq)) * (MLA_SCALE * LOG2E)).astype(BF16)

    ckv = _rms(z[:, _OFF_CKV:_OFF_KPE], gkv_ref[...])
    ckv_ref[...] = ckv
    kpe_slab = z[:, _OFF_KPE:IN_PAD]
    kpe_ref[...] = kpe_slab[:, :MLA_ROPE]
    ckv_b = ckv.astype(BF16)
    kn = _dot(ckv_b, wuk_ref[...])
    gk = gk_ref[...]
    for h in range(MLA_HEADS):
        sl = slice(h * HEAD_PAD, (h + 1) * HEAD_PAD)
        k_ref[:, sl] = rope_mla(head_norm(kn[:, sl] + kpe_slab, gk)).astype(BF16)
    v_ref[...] = _dot(ckv_b, wuv_ref[...]).astype(BF16)


def _front(x, tabs, tab_index, w, tm):
    rows = x.shape[0]
    grid = (rows // tm,)
    row = lambda i: (i, 0)
    const = lambda i: (0, 0)
    tab = lambda i: (tab_index(i), 0)

    def full(a):
        return pl.BlockSpec(a.shape, const)

    in_specs = [pl.BlockSpec((tm, D_MODEL), row), full(w['gattn']), full(w['win'])]
    in_specs += [pl.BlockSpec((tm, LANES), tab)] * 5
    in_specs += [full(w[n]) for n in ('gqa', 'wqb', 'gq', 'gkv', 'wuk', 'gk', 'wuv')]
    widths = [(RET_HEADS * LANES, F32), (RET_HEADS * LANES, F32), (RET_HEADS * RET_DV, BF16),
              (RET_HEADS * RET_DV, F32), (MLA_HEADS * HEAD_PAD, BF16), (KV_LORA, F32), (MLA_ROPE, F32),
              (MLA_HEADS * HEAD_PAD, BF16), (MLA_HEADS * MLA_V, BF16)]
    out_shape = [jax.ShapeDtypeStruct((rows, n), dt) for n, dt in widths]
    out_specs = [pl.BlockSpec((tm, n), row) for n, _ in widths]
    return pl.pallas_call(
        _front_kernel, grid=grid, in_specs=in_specs, out_specs=out_specs, out_shape=out_shape,
        compiler_params=_params(("arbitrary",)), name="front",
    )(x, w['gattn'], w['win'], *tabs, w['gqa'], w['wqb'], w['gq'], w['gkv'], w['wuk'], w['gk'], w['wuv'])


def _ret_gate(o, zg, g):
    on = o * lax.rsqrt(jnp.mean(o * o, axis=-1, keepdims=True) + EPS) * g
    return (zg * jax.nn.sigmoid(zg)) * on


def _ret_prompt_kernel(q_ref, k_ref, v_ref, zg_ref, mk_ref, mv_ref, mcol_ref, dec_ref, rowd_ref, cold_ref,
                       cpow_ref, g_ref, r_ref, s_ref, s_scr):
    c = pl.program_id(1)

    @pl.when(c == 0)
    def _():
        for h in range(RET_HEADS):
            sl = slice(h * LANES, (h + 1) * LANES)
            kw = mk_ref[:, sl] * mcol_ref[:, h:h + 1]
            s_scr[h] = _dot_tn(kw.astype(BF16), mv_ref[:, sl])

    for h in range(RET_HEADS):
        sl = slice(h * LANES, (h + 1) * LANES)
        q = q_ref[:, sl]
        k = k_ref[:, sl]
        v = v_ref[:, sl]
        qb = q.astype(BF16)
        s0 = s_scr[h]
        scores = _dot_nt(qb, k.astype(BF16)) * dec_ref[h]
        inner = _dot(scores.astype(BF16), v)
        cross = _dot(qb, s0.astype(BF16)) * rowd_ref[:, h:h + 1]
        kw = k * cold_ref[:, h:h + 1]
        s_new = s0 * cpow_ref[:, h:h + 1] + _dot_tn(kw.astype(BF16), v)
        s_scr[h] = s_new
        r_ref[:, sl] = _ret_gate(inner + cross, zg_ref[:, sl], g_ref[:, sl]).astype(BF16)

    @pl.when(c == pl.num_programs(1) - 1)
    def _():
        for h in range(RET_HEADS):
            s_ref[0, h] = s_scr[h, :RET_DK, :]


def _ret_prompt(rq, rk, rv, zg, mk, mv, tabs, g, batch, n_chunks):
    cs = RET_CHUNK
    row = lambda b, c: (b * n_chunks + c, 0)
    const2 = lambda b, c: (0, 0)
    w4 = RET_HEADS * LANES
    in_specs = [pl.BlockSpec((cs, w4), row), pl.BlockSpec((cs, w4), row), pl.BlockSpec((cs, w4), row),
                pl.BlockSpec((cs, w4), row),
                pl.BlockSpec((cs, w4), const2), pl.BlockSpec((cs, w4), const2),
                pl.BlockSpec((cs, RET_HEADS), const2),
                pl.BlockSpec((RET_HEADS, cs, cs), lambda b, c: (0, 0, 0)),
                pl.BlockSpec((cs, RET_HEADS), const2), pl.BlockSpec((cs, RET_HEADS), const2),
                pl.BlockSpec((LANES, RET_HEADS), const2), pl.BlockSpec((1, w4), const2)]
    out_shape = [jax.ShapeDtypeStruct((batch * n_chunks * cs, w4), BF16),
                 jax.ShapeDtypeStruct((batch, RET_HEADS, RET_DK, RET_DV), F32)]
    out_specs = [pl.BlockSpec((cs, w4), row),
                 pl.BlockSpec((1, RET_HEADS, RET_DK, RET_DV), lambda b, c: (b, 0, 0, 0))]
    return pl.pallas_call(
        _ret_prompt_kernel, grid=(batch, n_chunks), in_specs=in_specs, out_specs=out_specs, out_shape=out_shape,
        scratch_shapes=[pltpu.VMEM((RET_HEADS, LANES, RET_DV), F32)],
        compiler_params=_params(("arbitrary", "arbitrary")), name="ret_prompt",
    )(rq, rk, rv, zg, mk, mv, tabs['mcol'], tabs['dec'], tabs['rowd'], tabs['cold'], tabs['cpow'], g)


def _ret_sample_kernel(q_ref, k_ref, v_ref, zg_ref, s0_ref, dec_ref, rowd_ref, cold_ref, cpow_ref, g_ref,
                       r_ref, s_ref, *, n_seq, n_tok):
    rows = n_seq * n_tok
    ri = lax.broadcasted_iota(jnp.int32, (rows, 1), 0)
    for h in range(RET_HEADS):
        sl = slice(h * LANES, (h + 1) * LANES)
        q = q_ref[:, sl]
        k = k_ref[:, sl]
        v = v_ref[:, sl]
        qb = q.astype(BF16)
        scores = _dot_nt(qb, k.astype(BF16)) * dec_ref[h]
        inner = _dot(scores.astype(BF16), v)
        kw = k * cold_ref[:, h:h + 1]
        cross = jnp.zeros((rows, RET_DV), F32)
        for s in range(n_seq):
            s0 = s0_ref[s, h]
            mine = (ri >= s * n_tok) & (ri < (s + 1) * n_tok)
            cross = cross + jnp.where(mine, _dot(qb[:, :RET_DK], s0.astype(BF16)), 0.0)
            upd = _dot_tn(jnp.where(mine, kw, 0.0).astype(BF16), v)
            s_ref[s, h] = s0 * cpow_ref[:RET_DK, h:h + 1] + upd[:RET_DK]
        cross = cross * rowd_ref[:, h:h + 1]
        r_ref[:, sl] = _ret_gate(inner + cross, zg_ref[:, sl], g_ref[:, sl]).astype(BF16)


def _ret_sample(rq, rk, rv, zg, state, tabs, g, n_tok):
    n_seq_total = state.shape[0]
    n_seq = LANES // n_tok
    rows = n_seq * n_tok
    w4 = RET_HEADS * LANES
    row = lambda i: (i, 0)
    const = lambda i: (0, 0)
    in_specs = [pl.BlockSpec((rows, w4), row)] * 4
    in_specs += [pl.BlockSpec((n_seq, RET_HEADS, RET_DK, RET_DV), lambda i: (i, 0, 0, 0)),
                 pl.BlockSpec((RET_HEADS, rows, rows), lambda i: (0, 0, 0)),
                 pl.BlockSpec((rows, RET_HEADS), const), pl.BlockSpec((rows, RET_HEADS), const),
                 pl.BlockSpec((LANES, RET_HEADS), const), pl.BlockSpec((1, w4), const)]
    out_shape = [jax.ShapeDtypeStruct((n_seq_total * n_tok, w4), BF16),
                 jax.ShapeDtypeStruct(state.shape, F32)]
    out_specs = [pl.BlockSpec((rows, w4), row),
                 pl.BlockSpec((n_seq, RET_HEADS, RET_DK, RET_DV), lambda i: (i, 0, 0, 0))]
    return pl.pallas_call(
        functools.partial(_ret_sample_kernel, n_seq=n_seq, n_tok=n_tok),
        grid=(n_seq_total // n_seq,), in_specs=in_specs, out_specs=out_specs, out_shape=out_shape,
        compiler_params=_params(("arbitrary",)), name="ret_sample",
    )(rq, rk, rv, zg, state, tabs['dec'], tabs['rowd'], tabs['cold'], tabs['cpow'], g)


def _attn_prompt_kernel(q_ref, k_ref, v_ref, km_ref, vm_ref, o_ref, m_scr, l_scr, acc_scr, *, sub_tiles):
    qi = pl.program_id(2)
    ki = pl.program_id(3)
    tm, tk = q_ref.shape[0], k_ref.shape[0]

    @pl.when(ki == 0)
    def _():
        lane = lax.broadcasted_iota(jnp.int32, (tm, km_ref.shape[0]), 1)
        for hh in range(2):
            sl = slice(hh * HEAD_PAD, (hh + 1) * HEAD_PAD)
            s = jnp.where(lane < N_META, _dot_nt(q_ref[:, sl], km_ref[:, sl]), NEG_BIG)
            m = jnp.max(s, axis=-1, keepdims=True)
            p = jnp.exp2(s - m)
            m_scr[hh] = jnp.broadcast_to(m, (tm, LANES))
            l_scr[hh] = jnp.broadcast_to(jnp.sum(p, axis=-1, keepdims=True), (tm, LANES))
            acc_scr[hh] = _dot(p.astype(BF16), vm_ref[...])

    def tile(r0, nr, c0, nc, masked):
        rows = pl.ds(r0, nr)
        if masked:
            keep = (lax.broadcasted_iota(jnp.int32, (nr, nc), 1) <= lax.broadcasted_iota(jnp.int32, (nr, nc), 0))
        for hh in range(2):
            sl = slice(hh * HEAD_PAD, (hh + 1) * HEAD_PAD)
            s = _dot_nt(q_ref[rows, sl], k_ref[pl.ds(c0, nc), sl])
            if masked:
                s = jnp.where(keep, s, NEG_BIG)
            m_old = m_scr[hh, rows, :]
            m_new = jnp.maximum(m_old, jnp.max(s, axis=-1, keepdims=True))
            alpha = jnp.exp2(m_old - m_new)
            p = jnp.exp2(s - jnp.tile(m_new, (1, nc // LANES)))
            m_scr[hh, rows, :] = m_new
            l_scr[hh, rows, :] = alpha * l_scr[hh, rows, :] + jnp.sum(p, axis=-1, keepdims=True)
            acc_scr[hh, rows, :] = alpha * acc_scr[hh, rows, :] + _dot(p.astype(BF16), v_ref[pl.ds(c0, nc), :])

    @pl.when(ki < qi)
    def _():
        tile(0, tm, 0, tk, False)

    @pl.when(ki == qi)
    def _():
        ns = sub_tiles
        st = tm // ns
        for a in range(ns):
            if a > 0:
                tile(a * st, st, 0, a * st, False)
            tile(a * st, st, a * st, st, True)
        lane = lax.broadcasted_iota(jnp.int32, (tm, 2 * MLA_V), 1)
        o0 = acc_scr[0] / l_scr[0]
        o1 = acc_scr[1] / l_scr[1]
        o_ref[...] = jnp.where(lane < MLA_V, o0, o1).astype(BF16)


def _attn_prompt(mq, k, v, km, vm, batch, seq, blk):
    nb = seq // blk
    pairs = MLA_HEADS // 2
    grid = (batch, pairs, nb, nb)
    in_specs = [pl.BlockSpec((blk, 2 * HEAD_PAD), lambda b, h, qi, ki: (b * nb + qi, h)),
                pl.BlockSpec((blk, 2 * HEAD_PAD), lambda b, h, qi, ki: (b * nb + jnp.minimum(ki, qi), h)),
                pl.BlockSpec((blk, 2 * MLA_V), lambda b, h, qi, ki: (b * nb + jnp.minimum(ki, qi), h)),
                pl.BlockSpec((km.shape[0], 2 * HEAD_PAD), lambda b, h, qi, ki: (0, h)),
                pl.BlockSpec((vm.shape[0], 2 * MLA_V), lambda b, h, qi, ki: (0, h))]
    out_specs = pl.BlockSpec((blk, 2 * MLA_V), lambda b, h, qi, ki: (b * nb + qi, h))
    return pl.pallas_call(
        functools.partial(_attn_prompt_kernel, sub_tiles=2 if blk % (2 * LANES) == 0 else 1), grid=grid, in_specs=in_specs, out_specs=out_specs,
        out_shape=jax.ShapeDtypeStruct((batch * seq, MLA_HEADS * MLA_V), BF16),
        scratch_shapes=[pltpu.VMEM((2, blk, LANES), F32), pltpu.VMEM((2, blk, LANES), F32),
                        pltpu.VMEM((2, blk, 2 * MLA_V), F32)],
        compiler_params=_params(("arbitrary",) * 4), name="attn_prompt",
    )(mq, k, v, km, vm)


def _absorb_kernel(mq_ref, wabs_ref, qt_ref, qf_ref):
    q = mq_ref[...]
    qf_ref[...] = q.astype(F32)
    for h in range(MLA_HEADS):
        qt_ref[:, h * KV_LORA:(h + 1) * KV_LORA] = _dot(q[:, h * HEAD_PAD:(h + 1) * HEAD_PAD], wabs_ref[h])


def _absorb(mq, wabs, row0_blocks, rows, tm):
    return pl.pallas_call(
        _absorb_kernel, grid=(rows // tm,),
        in_specs=[pl.BlockSpec((tm, MLA_HEADS * HEAD_PAD), lambda i: (row0_blocks + i, 0)),
                  pl.BlockSpec(wabs.shape, lambda i: (0, 0, 0))],
        out_specs=[pl.BlockSpec((tm, MLA_HEADS * KV_LORA), lambda i: (i, 0)),
                   pl.BlockSpec((tm, MLA_HEADS * HEAD_PAD), lambda i: (i, 0))],
        out_shape=[jax.ShapeDtypeStruct((rows, MLA_HEADS * KV_LORA), F32),
                   jax.ShapeDtypeStruct((rows, MLA_HEADS * HEAD_PAD), F32)],
        compiler_params=_params(("arbitrary",)), name="absorb",
    )(mq, wabs)


def _attn_sample_kernel(pt_ref, lat_hbm, kpe_hbm, cos_ref, sin_ref, qt_ref, qf_ref, cn_ref, kn_ref, cosn_ref,
                        sinn_ref, lw_ref, wuv_ref, gpe_ref, o_ref, l_scr, qpe_scr, m_scr, d_scr, acc_scr, kpad_scr,
                        cpad_scr, lat_buf, kpe_buf, sem, *, n_pages, sub, n_tok, page):
    s_id = pl.program_id(0)
    j = pl.program_id(1)
    n_steps = pl.num_programs(1)
    step = s_id * n_steps + j
    last = pl.num_programs(0) * n_steps - 1
    slot = lax.rem(step, 2)
    nq = MLA_HEADS * n_tok
    n_up = MLA_HEADS * MLA_NOPE

    def page_copies(pid, sl, p):
        return (pltpu.make_async_copy(lat_hbm.at[0, pid], lat_buf.at[sl, p], sem.at[sl, 0]),
                pltpu.make_async_copy(kpe_hbm.at[0, pid], kpe_buf.at[sl, p], sem.at[sl, 1]))

    def start_pages(seq, st, sl, pages=range(n_pages)):
        for p in pages:
            for c in page_copies(pt_ref[seq, st * n_pages + p], sl, p):
                c.start()

    def wait_pages(sl):
        for p in range(n_pages):
            for c in page_copies(0, sl, p):
                c.wait()

    @pl.when(step == 0)
    def _():
        start_pages(0, 0, 0)

    wait_pages(slot)
    wrap = j == n_steps - 1
    nxt_seq = jnp.where(wrap, jnp.where(step == last, 0, s_id + 1), s_id)
    nxt_j = jnp.where(wrap, 0, j + 1)
    start_pages(nxt_seq, nxt_j, 1 - slot)
    lat_refs = [lat_buf.at[slot, p] for p in range(n_pages)]
    kpe_refs = [kpe_buf.at[slot, p] for p in range(n_pages)]

    @pl.when(j == 0)
    def _():
        l_scr[:n_up, :] = lw_ref[...]
        qt = qt_ref[...]
        qf = qf_ref[...]
        l_scr[n_up:, :] = jnp.concatenate(
            [qt[:, h * KV_LORA:(h + 1) * KV_LORA] for h in range(MLA_HEADS)], axis=0).astype(BF16)
        lane = lax.broadcasted_iota(jnp.int32, (nq, HEAD_PAD), 1)
        qpe = jnp.concatenate([qf[:, h * HEAD_PAD:(h + 1) * HEAD_PAD] for h in range(MLA_HEADS)], axis=0)
        qpe_scr[...] = jnp.where(lane < MLA_ROPE, qpe, 0.0).astype(BF16)
        m_scr[...] = jnp.full(m_scr.shape, NEG_BIG, F32)
        d_scr[...] = jnp.zeros(d_scr.shape, F32)
        acc_scr[...] = jnp.zeros(acc_scr.shape, F32)

    def scores(cb, kpe_t, cos_t, sin_t):
        tk = cb.shape[0]
        big = _dot_nt(l_scr[...], cb)
        k_t = big[:n_up]
        ss = jnp.sum((k_t * k_t).reshape(MLA_HEADS, MLA_NOPE, tk), axis=1)
        ss = ss + jnp.sum(kpe_t * kpe_t, axis=0, keepdims=True)
        rs = lax.rsqrt(ss * (1.0 / MLA_QK) + EPS)
        kg = kpe_t * gpe_ref[...]
        half = MLA_ROPE // 2
        x1, x2 = kg[:half], kg[half:]
        rot = jnp.concatenate([x1 * cos_t - x2 * sin_t, x1 * sin_t + x2 * cos_t], axis=0)
        pe = _dot(qpe_scr[:, :MLA_ROPE], rot.astype(BF16))
        rs_rows = jnp.concatenate([jnp.broadcast_to(rs[h:h + 1, :], (n_tok, tk)) for h in range(MLA_HEADS)], axis=0)
        return (big[n_up:] + pe) * rs_rows

    def update(s_list, cb_list):
        m_old = m_scr[...]
        m_new = m_old
        for s in s_list:
            m_new = jnp.maximum(m_new, jnp.max(s, axis=-1, keepdims=True))
        alpha = jnp.exp2(m_old - m_new)
        d = alpha * d_scr[...]
        acc = alpha * acc_scr[...]
        for s, cb in zip(s_list, cb_list):
            p = jnp.exp2(s - m_new)
            d = d + jnp.sum(p, axis=-1, keepdims=True)
            acc = acc + _dot(p.astype(BF16), cb)
        m_scr[...] = m_new
        d_scr[...] = d
        acc_scr[...] = acc

    s_list, cb_list = [], []
    for g in range(n_pages // sub):
        pages = range(g * sub, (g + 1) * sub)
        cb = jnp.concatenate([lat_refs[b][...] for b in pages], axis=0).astype(BF16)
        kpe_t = jnp.concatenate([kpe_refs[b][...] for b in pages], axis=1)
        cols = slice(g * sub * page, (g + 1) * sub * page)
        s_list.append(scores(cb, kpe_t, cos_ref[:, cols], sin_ref[:, cols]))
        cb_list.append(cb)
    update(s_list, cb_list)

    @pl.when(j == pl.num_programs(1) - 1)
    def _():
        cpad_scr[...] = jnp.zeros(cpad_scr.shape, F32)
        cpad_scr[:n_tok, :] = cn_ref[...]
        kpad_scr[...] = jnp.zeros(kpad_scr.shape, F32)
        kpad_scr[:n_tok, :MLA_ROPE] = kn_ref[...]
        rowi = lax.broadcasted_iota(jnp.int32, (nq, page), 0)
        coli = lax.broadcasted_iota(jnp.int32, (nq, page), 1)
        keep = coli <= (rowi & (n_tok - 1))
        cb = cpad_scr[...].astype(BF16)
        s = scores(cb, kpad_scr[...].T[:MLA_ROPE], cosn_ref[...], sinn_ref[...])
        update([jnp.where(keep, s, NEG_BIG)], [cb])
        ctx = (acc_scr[...] / d_scr[...]).astype(BF16)
        out = jnp.zeros(o_ref.shape, F32)
        for h in range(MLA_HEADS):
            out = out + _dot(ctx, wuv_ref[h])[h * n_tok:(h + 1) * n_tok]
        o_ref[...] = out

    @pl.when(step == last)
    def _():
        wait_pages(1 - slot)


def _attn_sample(page_table, cache_latent, cache_krope_t, tabs, qt, qf, ckv, kpe, w, n_tok, n_pages_step, sub):
    n_seq, n_pages = page_table.shape
    page = cache_latent.shape[2]
    n_steps = n_pages // n_pages_step
    tk = n_pages_step * page
    nq = MLA_HEADS * n_tok

    seq_row = lambda s, j, pt: (s, 0)
    const2 = lambda s, j, pt: (0, 0)
    half = MLA_ROPE // 2
    in_specs = [pl.BlockSpec(memory_space=pl.ANY), pl.BlockSpec(memory_space=pl.ANY)]
    in_specs += [pl.BlockSpec((half, tk), lambda s, j, pt: (0, j)), pl.BlockSpec((half, tk), lambda s, j, pt: (0, j)),
                 pl.BlockSpec((n_tok, MLA_HEADS * KV_LORA), seq_row),
                 pl.BlockSpec((n_tok, MLA_HEADS * HEAD_PAD), seq_row),
                 pl.BlockSpec((n_tok, KV_LORA), seq_row), pl.BlockSpec((n_tok, MLA_ROPE), seq_row),
                 pl.BlockSpec((half, page), const2), pl.BlockSpec((half, page), const2),
                 pl.BlockSpec(w['uk_t'].shape, const2),
                 pl.BlockSpec(w['uv_blk'].shape, lambda s, j, pt: (0, 0, 0)),
                 pl.BlockSpec((MLA_ROPE, 1), const2)]
    grid_spec = pltpu.PrefetchScalarGridSpec(
        num_scalar_prefetch=1, grid=(n_seq, n_steps), in_specs=in_specs,
        out_specs=pl.BlockSpec((n_tok, MLA_HEADS * MLA_V), seq_row),
        scratch_shapes=[pltpu.VMEM((MLA_HEADS * MLA_NOPE + nq, KV_LORA), BF16),
                        pltpu.VMEM((nq, HEAD_PAD), BF16),
                        pltpu.VMEM((nq, 1), F32), pltpu.VMEM((nq, 1), F32), pltpu.VMEM((nq, KV_LORA), F32),
                        pltpu.VMEM((page, LANES), F32), pltpu.VMEM((page, KV_LORA), F32),
                        pltpu.VMEM((2, n_pages_step, page, KV_LORA), F32),
                        pltpu.VMEM((2, n_pages_step, MLA_ROPE, page), F32),
                        pltpu.SemaphoreType.DMA((2, 2))])
    return pl.pallas_call(
        functools.partial(_attn_sample_kernel, n_pages=n_pages_step, sub=sub, n_tok=n_tok, page=page),
        grid_spec=grid_spec, out_shape=jax.ShapeDtypeStruct((n_seq * n_tok, MLA_HEADS * MLA_V), F32),
        compiler_params=_params(("arbitrary", "arbitrary")), name="attn_sample",
    )(page_table, cache_latent, cache_krope_t,
      tabs['cos_t'], tabs['sin_t'], qt, qf, ckv, kpe, tabs['cosn_t'], tabs['sinn_t'],
      w['uk_t'], w['uv_blk'], w['gpe'])


def _post_kernel(r_ref, m_ref, x_ref, wo_ref, g_ref, wrh_ref, wrl_ref, br_ref, h_ref, xn_ref, ti_ref, tg_ref):
    half = wo_ref.shape[0] // 2
    mix = _dot(r_ref[...], wo_ref[:half, :]) + _dot(m_ref[...], wo_ref[half:, :])
    h = x_ref[...] + mix
    h_ref[...] = h
    xn = _rms(h, g_ref[...])
    xn_ref[...] = xn
    hi = xn.astype(BF16)
    lo = (xn - hi.astype(F32)).astype(BF16)
    wrh = wrh_ref[...]
    work = _dot(hi, wrh) + _dot(hi, wrl_ref[...]) + _dot(lo, wrh) + br_ref[...]
    lane = lax.broadcasted_iota(jnp.int32, work.shape, 1).astype(F32)
    idx = jnp.zeros(work.shape, F32)
    val = jnp.zeros(work.shape, F32)
    vmax = None
    denom = None
    for k in range(TOP_K):
        vk = jnp.max(work, axis=-1, keepdims=True)
        ik = jnp.min(jnp.where(work == vk, lane, float(LANES)), axis=-1, keepdims=True)
        work = jnp.where(lane == ik, -jnp.inf, work)
        if k == 0:
            vmax = vk
        ek = jnp.exp(vk - vmax)
        denom = ek if k == 0 else denom + ek
        idx = jnp.where(lane == float(k), ik, idx)
        val = jnp.where(lane == float(k), ek, val)
    ti_ref[...] = idx.astype(jnp.int32)
    tg_ref[...] = val / denom


def _post(r, m, x, w, tm):
    rows = x.shape[0]
    row = lambda i: (i, 0)
    const = lambda i: (0, 0)
    in_specs = [pl.BlockSpec((tm, r.shape[1]), row), pl.BlockSpec((tm, m.shape[1]), row),
                pl.BlockSpec((tm, D_MODEL), row)]
    in_specs += [pl.BlockSpec(w[n].shape, const) for n in ('wo', 'gffn', 'wr_hi', 'wr_lo', 'br')]
    out_shape = [jax.ShapeDtypeStruct((rows, D_MODEL), F32), jax.ShapeDtypeStruct((rows, D_MODEL), F32),
                 jax.ShapeDtypeStruct((rows, LANES), jnp.int32), jax.ShapeDtypeStruct((rows, LANES), F32)]
    out_specs = [pl.BlockSpec((tm, D_MODEL), row), pl.BlockSpec((tm, D_MODEL), row),
                 pl.BlockSpec((tm, LANES), row), pl.BlockSpec((tm, LANES), row)]
    return pl.pallas_call(
        _post_kernel, grid=(rows // tm,), in_specs=in_specs, out_specs=out_specs, out_shape=out_shape,
        compiler_params=_params(("arbitrary",)), name="post",
    )(r, m, x, w['wo'], w['gffn'], w['wr_hi'], w['wr_lo'], w['br'])


def _moe_kernel(be_ref, nu_ref, xs_ref, wgu_ref, bgu_ref, wd_ref, bd_ref, o_ref, wgu_b, wd_b):
    i = pl.program_id(0)
    e = be_ref[i]
    e_prev = be_ref[jnp.maximum(i - 1, 0)]

    @pl.when((i == 0) | (e != e_prev))
    def _():
        wgu_b[...] = wgu_ref[...].astype(BF16)
        wd_b[...] = wd_ref[...].astype(BF16)

    @pl.when(i < nu_ref[0])
    def _():
        hgu = _dot(xs_ref[...].astype(BF16), wgu_b[...]) + bgu_ref[...]
        g = jnp.minimum(hgu[:, :D_FF], SWIGLU_LIMIT)
        u = jnp.clip(hgu[:, D_FF:], -SWIGLU_LIMIT, SWIGLU_LIMIT)
        hid = (u + 1.0) * (g * jax.nn.sigmoid(SWIGLU_ALPHA * g))
        o_ref[...] = _dot(hid.astype(BF16), wd_b[...]) + bd_ref[...]

    @pl.when(i >= nu_ref[0])
    def _():
        o_ref[...] = jnp.zeros(o_ref.shape, F32)


def _moe_blocks(block_e, n_used, xs, w_gate_up, b_gate_up, w_down, b_down, blk):
    n_blocks = block_e.shape[0]
    row = lambda i, be, nu: (i, 0)
    in_specs = [pl.BlockSpec((blk, D_MODEL), row),
                pl.BlockSpec((None, D_MODEL, 2 * D_FF), lambda i, be, nu: (be[i], 0, 0)),
                pl.BlockSpec((None, 1, 2 * D_FF), lambda i, be, nu: (be[i], 0, 0)),
                pl.BlockSpec((None, D_FF, D_MODEL), lambda i, be, nu: (be[i], 0, 0)),
                pl.BlockSpec((None, 1, D_MODEL), lambda i, be, nu: (be[i], 0, 0))]
    grid_spec = pltpu.PrefetchScalarGridSpec(
        num_scalar_prefetch=2, grid=(n_blocks,), in_specs=in_specs,
        out_specs=pl.BlockSpec((blk, D_MODEL), row),
        scratch_shapes=[pltpu.VMEM((D_MODEL, 2 * D_FF), BF16), pltpu.VMEM((D_FF, D_MODEL), BF16)])
    return pl.pallas_call(
        _moe_kernel, grid_spec=grid_spec, out_shape=jax.ShapeDtypeStruct((n_blocks * blk, D_MODEL), F32),
        compiler_params=_params(("arbitrary",)), name="moe",
    )(block_e, n_used, xs, w_gate_up, b_gate_up[:, None, :], w_down, b_down[:, None, :])


def _combine_kernel(h_ref, g_ref, y_ref, o_ref):
    g = g_ref[...]
    acc = y_ref[0] * g[:, 0:1]
    for k in range(1, TOP_K):
        acc = acc + y_ref[k] * g[:, k:k + 1]
    o_ref[...] = h_ref[...] + acc


def _combine(h, gates, y4, block0, rows, tm):
    row = lambda i: (block0 + i, 0)
    return pl.pallas_call(
        _combine_kernel, grid=(rows // tm,),
        in_specs=[pl.BlockSpec((tm, D_MODEL), row), pl.BlockSpec((tm, LANES), row),
                  pl.BlockSpec((TOP_K, tm, D_MODEL), lambda i: (0, block0 + i, 0))],
        out_specs=pl.BlockSpec((tm, D_MODEL), lambda i: (i, 0)),
        out_shape=jax.ShapeDtypeStruct((rows, D_MODEL), F32),
        compiler_params=_params(("arbitrary",)), name="combine",
    )(h, gates, y4)


def _gather_rows(table, idx):
    n, d = idx.shape[0], table.shape[1]
    n_win = n // GATHER_WINDOW
    assert n % (GATHER_WINDOW * SC_CORES * SC_SUBCORES) == 0
    idx = jnp.zeros((n_win, LANES), jnp.int32).at[:, :GATHER_WINDOW].set(idx.reshape(n_win, GATHER_WINDOW))
    mesh = plsc.VectorSubcoreMesh(core_axis_name="core", subcore_axis_name="subcore",
                                  num_cores=SC_CORES, num_subcores=SC_SUBCORES)

    @functools.partial(pl.kernel, out_type=jax.ShapeDtypeStruct((n, d), table.dtype), mesh=mesh)
    def gather(x_hbm, i_hbm, o_hbm):
        def body(i_vmem, o_vmem):
            pltpu.sync_copy(x_hbm.at[i_vmem.at[0, pl.ds(0, GATHER_WINDOW)]], o_vmem)

        pltpu.emit_pipeline(
            body, grid=(n_win,),
            in_specs=[pl.BlockSpec((1, LANES), index_map=lambda i: (i, 0))],
            out_specs=[pl.BlockSpec((GATHER_WINDOW, d), index_map=lambda i: (i, 0))],
            core_axis_name=("core", "subcore"), dimension_semantics=(pltpu.PARALLEL,),
        )(i_hbm, o_hbm)

    return gather(table, idx)


def _route(top_i, blk):
    t = top_i.shape[0]
    a = t * TOP_K
    flat_e = top_i.reshape(a)
    assert N_EXPERTS * a < 2 ** 31
    order = jnp.sort(flat_e.astype(jnp.int32) * a + jnp.arange(a, dtype=jnp.int32)) % a
    tok_sorted = (order // TOP_K).astype(jnp.int32)
    experts = jnp.arange(N_EXPERTS, dtype=jnp.int32)
    hit = (top_i[:, :, None] == experts[None, None, :]).astype(jnp.int32)
    per_tok = hit.sum(axis=1)
    before = jnp.cumsum(per_tok, axis=0) - per_tok
    counts = per_tok.sum(axis=0)
    padded = (counts + blk - 1) // blk * blk
    start = jnp.cumsum(counts) - counts
    pend = jnp.cumsum(padded)
    pstart = pend - padded
    dest = ((before + pstart[None, :])[:, None, :] * hit).sum(axis=2)
    n_blocks = -(-a // blk) + N_EXPERTS
    block_e = jnp.minimum((pend[None, :] <= (jnp.arange(n_blocks, dtype=jnp.int32) * blk)[:, None]).sum(axis=1),
                          N_EXPERTS - 1).astype(jnp.int32)
    slot = jnp.arange(n_blocks * blk, dtype=jnp.int32)
    slot_e = jnp.repeat(block_e, blk)
    r = slot - pstart[slot_e]
    slot_tok = jnp.where(r < counts[slot_e], tok_sorted[jnp.minimum(start[slot_e] + r, a - 1)], slot % t)
    n_used = (pend[-1:] // blk).astype(jnp.int32)
    return slot_tok, dest.astype(jnp.int32), block_e, n_used


def _rope_tables(pos):
    pos = pos.astype(F32)[:, None]
    n = pos.shape[0]
    hr = RET_DK // 2
    ang = pos * (RET_THETA ** (-jnp.arange(hr, dtype=F32) / hr))[None, :]
    cos, sin = jnp.cos(ang), jnp.sin(ang)
    zr = jnp.zeros((n, LANES - RET_DK), F32)
    cr = jnp.concatenate([cos, cos, zr], axis=1)
    sr = jnp.concatenate([-sin, sin, zr], axis=1)
    hm = MLA_ROPE // 2
    ang = pos * (MLA_THETA ** (-jnp.arange(hm, dtype=F32) / hm))[None, :]
    cos, sin = jnp.cos(ang), jnp.sin(ang)
    cm = jnp.concatenate([cos, cos, jnp.ones((n, LANES - MLA_ROPE), F32)], axis=1)
    sma = jnp.concatenate([jnp.zeros((n, hm), F32), sin, jnp.zeros((n, LANES - MLA_ROPE), F32)], axis=1)
    smb = jnp.concatenate([-sin, jnp.zeros((n, LANES - hm), F32)], axis=1)
    return [cr, sr, cm, sma, smb]


def _rope_tables_t(pos):
    hm = MLA_ROPE // 2
    ang = pos.astype(F32)[:, None] * (MLA_THETA ** (-jnp.arange(hm, dtype=F32) / hm))[None, :]
    return jnp.cos(ang).T, jnp.sin(ang).T


def _decay_tables(log_gamma, c, n_rep):
    idx = jnp.arange(c, dtype=F32)
    diff = idx[:, None] - idx[None, :]
    decay = jnp.where(diff >= 0, jnp.exp(jnp.maximum(diff, 0.0)[None] * log_gamma[:, None, None]), 0.0)
    if n_rep > 1:
        eye = jnp.eye(n_rep, dtype=F32)
        decay = (eye[None, :, None, :, None] * decay[:, None, :, None, :]).reshape(RET_HEADS, n_rep * c, n_rep * c)
    rowd = jnp.tile(jnp.exp((idx + 1.0)[:, None] * log_gamma[None, :]), (n_rep, 1))
    cold = jnp.tile(jnp.exp((c - 1.0 - idx)[:, None] * log_gamma[None, :]), (n_rep, 1))
    cpow = jnp.broadcast_to(jnp.exp(c * log_gamma)[None, :], (LANES, RET_HEADS))
    return {'dec': decay, 'rowd': rowd, 'cold': cold, 'cpow': cpow}


def _pad_heads(wm, n_heads, width, offset):
    k = wm.shape[0]
    wm = wm.reshape(k, n_heads, width)
    out = jnp.zeros((k, n_heads, LANES), wm.dtype).at[:, :, offset:offset + width].set(wm)
    return out.reshape(k, n_heads * LANES)


def _prep_weights(norm_attn_g, w_in, ret_out_g, q_a_norm_g, w_q_b, kv_a_norm_g, w_uk, w_uv,
                  qk_norm_q_g, qk_norm_k_g, w_out, norm_ffn_g, w_router, b_router):
    splits = [RET_HEADS * RET_DK, RET_HEADS * RET_DK, RET_HEADS * RET_DV, RET_HEADS * RET_DV, Q_LORA, KV_LORA,
              MLA_ROPE]
    offs = [0]
    for s in splits:
        offs.append(offs[-1] + s)
    part = [w_in[:, offs[i]:offs[i + 1]] for i in range(len(splits))]
    kpe_cols = jnp.zeros((D_MODEL, LANES), F32).at[:, :MLA_ROPE].set(part[6])
    win = jnp.concatenate([_pad_heads(part[0], RET_HEADS, RET_DK, 0), _pad_heads(part[1], RET_HEADS, RET_DK, 0),
                           part[2], part[3], part[4], part[5], kpe_cols], axis=1).astype(BF16)

    def mla_cols(wm):
        k = wm.shape[0]
        wm = wm.reshape(k, MLA_HEADS, MLA_QK)
        wm = jnp.concatenate([wm[:, :, MLA_NOPE:], wm[:, :, :MLA_NOPE]], axis=2)
        return _pad_heads(wm.reshape(k, MLA_HEADS * MLA_QK), MLA_HEADS, MLA_QK, 0)

    def mla_gain(g):
        g = jnp.concatenate([g[MLA_NOPE:], g[:MLA_NOPE], jnp.zeros((LANES - MLA_QK,), F32)])
        return g[None, :]

    gk_nope = qk_norm_k_g[:MLA_NOPE]
    uk_heads = w_uk.reshape(KV_LORA, MLA_HEADS, MLA_NOPE)
    wabs = jnp.zeros((MLA_HEADS, HEAD_PAD, KV_LORA), F32).at[:, MLA_ROPE:MLA_ROPE + MLA_NOPE, :].set(
        (uk_heads * gk_nope[None, None, :]).transpose(1, 2, 0))
    head_of_col = jnp.arange(MLA_HEADS * MLA_V) // MLA_V
    uv_blk = jnp.where(head_of_col[None, None, :] == jnp.arange(MLA_HEADS)[:, None, None], w_uv[None], 0.0)
    wr = jnp.zeros((D_MODEL, LANES), F32).at[:, :N_EXPERTS].set(w_router)
    wr_hi = wr.astype(BF16)
    return {
        'gattn': norm_attn_g[None, :], 'win': win, 'gqa': q_a_norm_g[None, :], 'wqb': mla_cols(w_q_b).astype(BF16),
        'gq': mla_gain(qk_norm_q_g), 'gkv': kv_a_norm_g[None, :],
        'wuk': _pad_heads(w_uk, MLA_HEADS, MLA_NOPE, MLA_ROPE).astype(BF16), 'gk': mla_gain(qk_norm_k_g),
        'wuv': w_uv.astype(BF16), 'gret': ret_out_g[None, :],
        'wabs': wabs.astype(BF16), 'uk_t': w_uk.T.astype(BF16), 'uv_blk': uv_blk.astype(BF16),
        'gpe': qk_norm_k_g[MLA_NOPE:, None],
        'wo': w_out.astype(BF16), 'gffn': norm_ffn_g[None, :], 'wr_hi': wr_hi,
        'wr_lo': (wr - wr_hi.astype(F32)).astype(BF16),
        'br': jnp.full((1, LANES), NEG_BIG, F32).at[0, :N_EXPERTS].set(b_router),
    }


def _pad_rows(a, rows):
    return jnp.zeros((rows,) + a.shape[1:], a.dtype).at[:a.shape[0]].set(a)


def _largest_divisor(n, cap):
    d = min(n, cap)
    while n % d:
        d -= 1
    return d


def kernel(x_prompt, x_sample, cache_latent, cache_krope, state_retention, page_table, meta_tokens, norm_attn_g, w_in, ret_out_g, q_a_norm_g, w_q_b, kv_a_norm_g, w_uk, w_uv, qk_norm_q_g, qk_norm_k_g, w_out, norm_ffn_g, w_router, b_router, w_gate_up, b_gate_up, w_down, b_down):
    assert w_in.shape[0] == 1, "single-layer trunk"
    batch, seq, _ = x_prompt.shape
    n_seq, n_tok, _ = x_sample.shape
    n_pages, page = page_table.shape[1], cache_latent.shape[2]
    past = n_pages * page
    assert seq % RET_CHUNK == 0 and LANES % n_tok == 0 and (n_seq * n_tok) % LANES == 0 and page == LANES
    w = _prep_weights(norm_attn_g[0], w_in[0], ret_out_g[0], q_a_norm_g[0], w_q_b[0], kv_a_norm_g[0], w_uk[0],
                      w_uv[0], qk_norm_q_g[0], qk_norm_k_g[0], w_out[0], norm_ffn_g[0], w_router[0], b_router[0])
    log_gamma = jnp.log1p(-jnp.exp2(-5.0 - jnp.arange(RET_HEADS, dtype=F32)))

    rows_p = batch * seq
    rows_s = n_seq * n_tok
    tm = _largest_divisor(min(seq, rows_s), ROW_BLOCK)
    nb_seq = seq // tm
    x = jnp.concatenate([x_prompt.reshape(rows_p, D_MODEL), x_sample.reshape(rows_s, D_MODEL)], axis=0)
    pos_rows = jnp.concatenate([N_META + jnp.arange(seq), jnp.tile(past + jnp.arange(n_tok), tm // n_tok)])
    tabs = _rope_tables(pos_rows)
    n_pb = rows_p // tm
    rq, rk, rv, zg, mq, ckv, kpe, k, v = _front(
        x, tabs, lambda i: jnp.where(i < n_pb, i % nb_seq, nb_seq), w, tm)
    _, mrk, mrv, _, _, mckv, mkpe, mk, mv = _front(
        meta_tokens, _rope_tables(jnp.arange(N_META)), lambda i: i, w, N_META)

    dt_p = _decay_tables(log_gamma, RET_CHUNK, 1)
    dt_p['mcol'] = _pad_rows(jnp.exp((N_META - 1.0 - jnp.arange(N_META, dtype=F32))[:, None] * log_gamma[None, :]),
                             RET_CHUNK)
    gret = w['gret']
    r_p, st_p = _ret_prompt(rq, rk, rv, zg, _pad_rows(mrk, RET_CHUNK), _pad_rows(mrv, RET_CHUNK), dt_p, gret,
                            batch, seq // RET_CHUNK)
    dt_s = _decay_tables(log_gamma, n_tok, LANES // n_tok)
    r_s, st_s = _ret_sample(rq[rows_p:], rk[rows_p:], rv[rows_p:], zg[rows_p:], state_retention[0], dt_s, gret, n_tok)

    blk = _largest_divisor(seq, ATTN_BLOCK)
    m_p = _attn_prompt(mq, k, v, _pad_rows(mk, LANES), _pad_rows(mv, LANES), batch, seq, blk)
    tm_s = _largest_divisor(rows_s, ROW_BLOCK)
    qt, qf = _absorb(mq, w['wabs'], rows_p // tm_s, rows_s, tm_s)
    n_pages_step = _largest_divisor(n_pages, PAGES_PER_STEP)
    cos_t, sin_t = _rope_tables_t(jnp.arange(past))
    cosn_t, sinn_t = _rope_tables_t(past + jnp.arange(page))
    tabs_s = {'cos_t': cos_t, 'sin_t': sin_t, 'cosn_t': cosn_t, 'sinn_t': sinn_t}
    m_s = _attn_sample(page_table, cache_latent, jnp.swapaxes(cache_krope, 2, 3), tabs_s, qt, qf, ckv[rows_p:],
                       kpe[rows_p:], w, n_tok, n_pages_step, _largest_divisor(n_pages_step, PAGES_PER_CHAIN))

    r_all = jnp.concatenate([r_p, r_s], axis=0)
    m_all = jnp.concatenate([m_p, m_s.astype(BF16)], axis=0)
    h1, xn2, top_i, gates = _post(r_all, m_all, x, w, tm)

    slot_tok, dest, block_e, n_used = _route(top_i[:, :TOP_K], EXPERT_BLOCK)
    outs = _moe_blocks(block_e, n_used, _gather_rows(xn2, slot_tok), w_gate_up[0], b_gate_up[0], w_down[0],
                       b_down[0], EXPERT_BLOCK)
    y4 = _gather_rows(outs, dest.T.reshape(-1)).reshape(TOP_K, rows_p + rows_s, D_MODEL)
    y_prompt = _combine(h1, gates, y4, 0, rows_p, tm).reshape(batch, seq, D_MODEL)
    y_sample = _combine(h1, gates, y4, n_pb, rows_s, tm).reshape(n_seq, n_tok, D_MODEL)
    lat_p = jnp.concatenate([jnp.broadcast_to(mckv[None], (batch, N_META, KV_LORA)),
                             ckv[:rows_p].reshape(batch, seq, KV_LORA)], axis=1)[None]
    kpe_p = jnp.concatenate([jnp.broadcast_to(mkpe[None], (batch, N_META, MLA_ROPE)),
                             kpe[:rows_p].reshape(batch, seq, MLA_ROPE)], axis=1)[None]
    return (y_prompt, y_sample, lat_p, kpe_p, st_p[None],
            ckv[rows_p:].reshape(n_seq, n_tok, KV_LORA)[None], kpe[rows_p:].reshape(n_seq, n_tok, MLA_ROPE)[None],
            st_s[None])
```

```python
import functools

import jax
import jax.numpy as jnp
from jax import lax
from jax.experimental import pallas as pl
from jax.experimental.pallas import tpu as pltpu
from jax.experimental.pallas import tpu_sc as plsc

F32 = jnp.float32
BF16 = jnp.bfloat16

D_MODEL = 1024
N_META = 16
RET_HEADS = 4
RET_DK = 64
RET_DV = 128
RET_CHUNK = 128
RET_THETA = 10000.0
MLA_HEADS = 8
MLA_NOPE = 64
MLA_ROPE = 32
MLA_QK = MLA_NOPE + MLA_ROPE
MLA_V = 64
Q_LORA = 384
KV_LORA = 256
MLA_THETA = 10000.0
MLA_SCALE = MLA_QK ** -0.5
LOG2E = 1.4426950408889634
N_EXPERTS = 32
TOP_K = 4
D_FF = 1024
SWIGLU_LIMIT = 7.0
SWIGLU_ALPHA = 1.702
EPS = 1e-6
NEG_BIG = -1e30

LANES = 128
HEAD_PAD = LANES
VMEM_LIMIT = 56 * 1024 * 1024
ROW_BLOCK = 512
ATTN_BLOCK = 1024
PAGES_PER_STEP = 64
PAGES_PER_CHAIN = 32
EXPERT_BLOCK = 512
SC_CORES, SC_SUBCORES = 2, 16
GATHER_WINDOW = 32

_OFF_RQ = 0
_OFF_RK = _OFF_RQ + RET_HEADS * LANES
_OFF_RV = _OFF_RK + RET_HEADS * LANES
_OFF_ZG = _OFF_RV + RET_HEADS * RET_DV
_OFF_CQ = _OFF_ZG + RET_HEADS * RET_DV
_OFF_CKV = _OFF_CQ + Q_LORA
_OFF_KPE = _OFF_CKV + KV_LORA
IN_PAD = _OFF_KPE + LANES


def _params(sem):
    return pltpu.CompilerParams(dimension_semantics=sem, vmem_limit_bytes=VMEM_LIMIT)


def _rms(x, g):
    return x * lax.rsqrt(jnp.mean(x * x, axis=-1, keepdims=True) + EPS) * g


def _dot(a, b):
    return jnp.dot(a, b, preferred_element_type=F32)


def _dot_nt(a, b):
    return lax.dot_general(a, b, (((1,), (1,)), ((), ())), preferred_element_type=F32)


def _dot_tn(a, b):
    return lax.dot_general(a, b, (((0,), (0,)), ((), ())), preferred_element_type=F32)


def _front_kernel(x_ref, gattn_ref, win_ref, cr_ref, sr_ref, cm_ref, sma_ref, smb_ref,
                  gqa_ref, wqb_ref, gq_ref, gkv_ref, wuk_ref, gk_ref, wuv_ref,
                  rq_ref, rk_ref, rv_ref, zg_ref, mq_ref, ckv_ref, kpe_ref, k_ref, v_ref):
    tm = x_ref.shape[0]
    xn = _rms(x_ref[...], gattn_ref[...])
    z = _dot(xn.astype(BF16), win_ref[...])

    lane = lax.broadcasted_iota(jnp.int32, (tm, LANES), 1)
    upper = (lane & (RET_DK // 2)) != 0
    cr, sr = cr_ref[...], sr_ref[...]

    def rope_ret(t):
        partner = jnp.where(upper, pltpu.roll(t, RET_DK // 2, 1), pltpu.roll(t, LANES - RET_DK // 2, 1))
        return t * cr + partner * sr

    for h in range(RET_HEADS):
        sl = slice(h * LANES, (h + 1) * LANES)
        rq_ref[:, sl] = rope_ret(z[:, _OFF_RQ + h * LANES:_OFF_RQ + (h + 1) * LANES])
        rk_ref[:, sl] = rope_ret(z[:, _OFF_RK + h * LANES:_OFF_RK + (h + 1) * LANES]) * (RET_DK ** -0.5)
    rv_ref[...] = z[:, _OFF_RV:_OFF_ZG].astype(BF16)
    zg_ref[...] = z[:, _OFF_ZG:_OFF_CQ]

    cm, sma, smb = cm_ref[...], sma_ref[...], smb_ref[...]

    def rope_mla(t):
        half = MLA_ROPE // 2
        return t * cm + pltpu.roll(t, half, 1) * sma + pltpu.roll(t, LANES - half, 1) * smb

    def head_norm(t, g):
        ms = jnp.sum(t * t, axis=-1, keepdims=True) * (1.0 / MLA_QK)
        return t * lax.rsqrt(ms + EPS) * g

    cq = _rms(z[:, _OFF_CQ:_OFF_CKV], gqa_ref[...])
    q = _dot(cq.astype(BF16), wqb_ref[...])
    gq = gq_ref[...]
    for h in range(MLA_HEADS):
        sl = slice(h * HEAD_PAD, (h + 1) * HEAD_PAD)
        mq_ref[:, sl] = (rope_mla(head_norm(q[:, sl], gq)) * (MLA_SCALE * LOG2E)).astype(BF16)

    ckv = _rms(z[:, _OFF_CKV:_OFF_KPE], gkv_ref[...])
    ckv_ref[...] = ckv
    kpe_slab = z[:, _OFF_KPE:IN_PAD]
    kpe_ref[...] = kpe_slab[:, :MLA_ROPE]
    ckv_b = ckv.astype(BF16)
    kn = _dot(ckv_b, wuk_ref[...])
    gk = gk_ref[...]
    for h in range(MLA_HEADS):
        sl = slice(h * HEAD_PAD, (h + 1) * HEAD_PAD)
        k_ref[:, sl] = rope_mla(head_norm(kn[:, sl] + kpe_slab, gk)).astype(BF16)
    v_ref[...] = _dot(ckv_b, wuv_ref[...]).astype(BF16)


def _front(x, tabs, tab_index, w, tm):
    rows = x.shape[0]
    grid = (rows // tm,)
    row = lambda i: (i, 0)
    const = lambda i: (0, 0)
    tab = lambda i: (tab_index(i), 0)

    def full(a):
        return pl.BlockSpec(a.shape, const)

    in_specs = [pl.BlockSpec((tm, D_MODEL), row), full(w['gattn']), full(w['win'])]
    in_specs += [pl.BlockSpec((tm, LANES), tab)] * 5
    in_specs += [full(w[n]) for n in ('gqa', 'wqb', 'gq', 'gkv', 'wuk', 'gk', 'wuv')]
    widths = [(RET_HEADS * LANES, F32), (RET_HEADS * LANES, F32), (RET_HEADS * RET_DV, BF16),
              (RET_HEADS * RET_DV, F32), (MLA_HEADS * HEAD_PAD, BF16), (KV_LORA, F32), (MLA_ROPE, F32),
              (MLA_HEADS * HEAD_PAD, BF16), (MLA_HEADS * MLA_V, BF16)]
    out_shape = [jax.ShapeDtypeStruct((rows, n), dt) for n, dt in widths]
    out_specs = [pl.BlockSpec((tm, n), row) for n, _ in widths]
    return pl.pallas_call(
        _front_kernel, grid=grid, in_specs=in_specs, out_specs=out_specs, out_shape=out_shape,
        compiler_params=_params(("arbitrary",)), name="front",
    )(x, w['gattn'], w['win'], *tabs, w['gqa'], w['wqb'], w['gq'], w['gkv'], w['wuk'], w['gk'], w['wuv'])


def _ret_gate(o, zg, g):
    on = o * lax.rsqrt(jnp.mean(o * o, axis=-1, keepdims=True) + EPS) * g
    return (zg * jax.nn.sigmoid(zg)) * on


def _ret_prompt_kernel(q_ref, k_ref, v_ref, zg_ref, mk_ref, mv_ref, mcol_ref, dec_ref, rowd_ref, cold_ref,
                       cpow_ref, g_ref, r_ref, s_ref, s_scr):
    c = pl.program_id(1)

    @pl.when(c == 0)
    def _():
        for h in range(RET_HEADS):
            sl = slice(h * LANES, (h + 1) * LANES)
            kw = mk_ref[:, sl] * mcol_ref[:, h:h + 1]
            s_scr[h] = _dot_tn(kw.astype(BF16), mv_ref[:, sl])

    for h in range(RET_HEADS):
        sl = slice(h * LANES, (h + 1) * LANES)
        q = q_ref[:, sl]
        k = k_ref[:, sl]
        v = v_ref[:, sl]
        qb = q.astype(BF16)
        s0 = s_scr[h]
        scores = _dot_nt(qb, k.astype(BF16)) * dec_ref[h]
        inner = _dot(scores.astype(BF16), v)
        cross = _dot(qb, s0.astype(BF16)) * rowd_ref[:, h:h + 1]
        kw = k * cold_ref[:, h:h + 1]
        s_new = s0 * cpow_ref[:, h:h + 1] + _dot_tn(kw.astype(BF16), v)
        s_scr[h] = s_new
        r_ref[:, sl] = _ret_gate(inner + cross, zg_ref[:, sl], g_ref[:, sl]).astype(BF16)

    @pl.when(c == pl.num_programs(1) - 1)
    def _():
        for h in range(RET_HEADS):
            s_ref[0, h] = s_scr[h, :RET_DK, :]


def _ret_prompt(rq, rk, rv, zg, mk, mv, tabs, g, batch, n_chunks):
    cs = RET_CHUNK
    row = lambda b, c: (b * n_chunks + c, 0)
    const2 = lambda b, c: (0, 0)
    w4 = RET_HEADS * LANES
    in_specs = [pl.BlockSpec((cs, w4), row), pl.BlockSpec((cs, w4), row), pl.BlockSpec((cs, w4), row),
                pl.BlockSpec((cs, w4), row),
                pl.BlockSpec((cs, w4), const2), pl.BlockSpec((cs, w4), const2),
                pl.BlockSpec((cs, RET_HEADS), const2),
                pl.BlockSpec((RET_HEADS, cs, cs), lambda b, c: (0, 0, 0)),
                pl.BlockSpec((cs, RET_HEADS), const2), pl.BlockSpec((cs, RET_HEADS), const2),
                pl.BlockSpec((LANES, RET_HEADS), const2), pl.BlockSpec((1, w4), const2)]
    out_shape = [jax.ShapeDtypeStruct((batch * n_chunks * cs, w4), BF16),
                 jax.ShapeDtypeStruct((batch, RET_HEADS, RET_DK, RET_DV), F32)]
    out_specs = [pl.BlockSpec((cs, w4), row),
                 pl.BlockSpec((1, RET_HEADS, RET_DK, RET_DV), lambda b, c: (b, 0, 0, 0))]
    return pl.pallas_call(
        _ret_prompt_kernel, grid=(batch, n_chunks), in_specs=in_specs, out_specs=out_specs, out_shape=out_shape,
        scratch_shapes=[pltpu.VMEM((RET_HEADS, LANES, RET_DV), F32)],
        compiler_params=_params(("arbitrary", "arbitrary")), name="ret_prompt",
    )(rq, rk, rv, zg, mk, mv, tabs['mcol'], tabs['dec'], tabs['rowd'], tabs['cold'], tabs['cpow'], g)


def _ret_sample_kernel(q_ref, k_ref, v_ref, zg_ref, s0_ref, dec_ref, rowd_ref, cold_ref, cpow_ref, g_ref,
                       r_ref, s_ref, *, n_seq, n_tok):
    rows = n_seq * n_tok
    ri = lax.broadcasted_iota(jnp.int32, (rows, 1), 0)
    for h in range(RET_HEADS):
        sl = slice(h * LANES, (h + 1) * LANES)
        q = q_ref[:, sl]
        k = k_ref[:, sl]
        v = v_ref[:, sl]
        qb = q.astype(BF16)
        scores = _dot_nt(qb, k.astype(BF16)) * dec_ref[h]
        inner = _dot(scores.astype(BF16), v)
        kw = k * cold_ref[:, h:h + 1]
        cross = jnp.zeros((rows, RET_DV), F32)
        for s in range(n_seq):
            s0 = s0_ref[s, h]
            mine = (ri >= s * n_tok) & (ri < (s + 1) * n_tok)
            cross = cross + jnp.where(mine, _dot(qb[:, :RET_DK], s0.astype(BF16)), 0.0)
            upd = _dot_tn(jnp.where(mine, kw, 0.0).astype(BF16), v)
            s_ref[s, h] = s0 * cpow_ref[:RET_DK, h:h + 1] + upd[:RET_DK]
        cross = cross * rowd_ref[:, h:h + 1]
        r_ref[:, sl] = _ret_gate(inner + cross, zg_ref[:, sl], g_ref[:, sl]).astype(BF16)


def _ret_sample(rq, rk, rv, zg, state, tabs, g, n_tok):
    n_seq_total = state.shape[0]
    n_seq = LANES // n_tok
    rows = n_seq * n_tok
    w4 = RET_HEADS * LANES
    row = lambda i: (i, 0)
    const = lambda i: (0, 0)
    in_specs = [pl.BlockSpec((rows, w4), row)] * 4
    in_specs += [pl.BlockSpec((n_seq, RET_HEADS, RET_DK, RET_DV), lambda i: (i, 0, 0, 0)),
                 pl.BlockSpec((RET_HEADS, rows, rows), lambda i: (0, 0, 0)),
                 pl.BlockSpec((rows, RET_HEADS), const), pl.BlockSpec((rows, RET_HEADS), const),
                 pl.BlockSpec((LANES, RET_HEADS), const), pl.BlockSpec((1, w4), const)]
    out_shape = [jax.ShapeDtypeStruct((n_seq_total * n_tok, w4), BF16),
                 jax.ShapeDtypeStruct(state.shape, F32)]
    out_specs = [pl.BlockSpec((rows, w4), row),
                 pl.BlockSpec((n_seq, RET_HEADS, RET_DK, RET_DV), lambda i: (i, 0, 0, 0))]
    return pl.pallas_call(
        functools.partial(_ret_sample_kernel, n_seq=n_seq, n_tok=n_tok),
        grid=(n_seq_total // n_seq,), in_specs=in_specs, out_specs=out_specs, out_shape=out_shape,
        compiler_params=_params(("arbitrary",)), name="ret_sample",
    )(rq, rk, rv, zg, state, tabs['dec'], tabs['rowd'], tabs['cold'], tabs['cpow'], g)


def _attn_prompt_kernel(q_ref, k_ref, v_ref, km_ref, vm_ref, o_ref, m_scr, l_scr, acc_scr, *, sub_tiles):
    qi = pl.program_id(2)
    ki = pl.program_id(3)
    tm, tk = q_ref.shape[0], k_ref.shape[0]

    @pl.when(ki == 0)
    def _():
        lane = lax.broadcasted_iota(jnp.int32, (tm, km_ref.shape[0]), 1)
        for hh in range(2):
            sl = slice(hh * HEAD_PAD, (hh + 1) * HEAD_PAD)
            s = jnp.where(lane < N_META, _dot_nt(q_ref[:, sl], km_ref[:, sl]), NEG_BIG)
            m = jnp.max(s, axis=-1, keepdims=True)
            p = jnp.exp2(s - m)
            m_scr[hh] = jnp.broadcast_to(m, (tm, LANES))
            l_scr[hh] = jnp.broadcast_to(jnp.sum(p, axis=-1, keepdims=True), (tm, LANES))
            acc_scr[hh] = _dot(p.astype(BF16), vm_ref[...])

    def tile(r0, nr, c0, nc, masked):
        rows = pl.ds(r0, nr)
        if masked:
            keep = (lax.broadcasted_iota(jnp.int32, (nr, nc), 1) <= lax.broadcasted_iota(jnp.int32, (nr, nc), 0))
        for hh in range(2):
            sl = slice(hh * HEAD_PAD, (hh + 1) * HEAD_PAD)
            s = _dot_nt(q_ref[rows, sl], k_ref[pl.ds(c0, nc), sl])
            if masked:
                s = jnp.where(keep, s, NEG_BIG)
            m_old = m_scr[hh, rows, :]
            m_new = jnp.maximum(m_old, jnp.max(s, axis=-1, keepdims=True))
            alpha = jnp.exp2(m_old - m_new)
            p = jnp.exp2(s - jnp.tile(m_new, (1, nc // LANES)))
            m_scr[hh, rows, :] = m_new
            l_scr[hh, rows, :] = alpha * l_scr[hh, rows, :] + jnp.sum(p, axis=-1, keepdims=True)
            acc_scr[hh, rows, :] = alpha * acc_scr[hh, rows, :] + _dot(p.astype(BF16), v_ref[pl.ds(c0, nc), :])

    @pl.when(ki < qi)
    def _():
        tile(0, tm, 0, tk, False)

    @pl.when(ki == qi)
    def _():
        ns = sub_tiles
        st = tm // ns
        for a in range(ns):
            if a > 0:
                tile(a * st, st, 0, a * st, False)
            tile(a * st, st, a * st, st, True)
        lane = lax.broadcasted_iota(jnp.int32, (tm, 2 * MLA_V), 1)
        o0 = acc_scr[0] / l_scr[0]
        o1 = acc_scr[1] / l_scr[1]
        o_ref[...] = jnp.where(lane < MLA_V, o0, o1).astype(BF16)


def _attn_prompt(mq, k, v, km, vm, batch, seq, blk):
    nb = seq // blk
    pairs = MLA_HEADS // 2
    grid = (batch, pairs, nb, nb)
    in_specs = [pl.BlockSpec((blk, 2 * HEAD_PAD), lambda b, h, qi, ki: (b * nb + qi, h)),
                pl.BlockSpec((blk, 2 * HEAD_PAD), lambda b, h, qi, ki: (b * nb + jnp.minimum(ki, qi), h)),
                pl.BlockSpec((blk, 2 * MLA_V), lambda b, h, qi, ki: (b * nb + jnp.minimum(ki, qi), h)),
                pl.BlockSpec((km.shape[0], 2 * HEAD_PAD), lambda b, h, qi, ki: (0, h)),
                pl.BlockSpec((vm.shape[0], 2 * MLA_V), lambda b, h, qi, ki: (0, h))]
    out_specs = pl.BlockSpec((blk, 2 * MLA_V), lambda b, h, qi, ki: (b * nb + qi, h))
    return pl.pallas_call(
        functools.partial(_attn_prompt_kernel, sub_tiles=2 if blk % (2 * LANES) == 0 else 1), grid=grid, in_specs=in_specs, out_specs=out_specs,
        out_shape=jax.ShapeDtypeStruct((batch * seq, MLA_HEADS * MLA_V), BF16),
        scratch_shapes=[pltpu.VMEM((2, blk, LANES), F32), pltpu.VMEM((2, blk, LANES), F32),
                        pltpu.VMEM((2, blk, 2 * MLA_V), F32)],
        compiler_params=_params(("arbitrary",) * 4), name="attn_prompt",
    )(mq, k, v, km, vm)


def _absorb_kernel(mq_ref, wabs_ref, qt_ref, qf_ref):
    q = mq_ref[...]
    qf_ref[...] = q.astype(F32)
    for h in range(MLA_HEADS):
        qt_ref[:, h * KV_LORA:(h + 1) * KV_LORA] = _dot(q[:, h * HEAD_PAD:(h + 1) * HEAD_PAD], wabs_ref[h])


def _absorb(mq, wabs, row0_blocks, rows, tm):
    return pl.pallas_call(
        _absorb_kernel, grid=(rows // tm,),
        in_specs=[pl.BlockSpec((tm, MLA_HEADS * HEAD_PAD), lambda i: (row0_blocks + i, 0)),
                  pl.BlockSpec(wabs.shape, lambda i: (0, 0, 0))],
        out_specs=[pl.BlockSpec((tm, MLA_HEADS * KV_LORA), lambda i: (i, 0)),
                   pl.BlockSpec((tm, MLA_HEADS * HEAD_PAD), lambda i: (i, 0))],
        out_shape=[jax.ShapeDtypeStruct((rows, MLA_HEADS * KV_LORA), F32),
                   jax.ShapeDtypeStruct((rows, MLA_HEADS * HEAD_PAD), F32)],
        compiler_params=_params(("arbitrary",)), name="absorb",
    )(mq, wabs)


def _attn_sample_kernel(pt_ref, lat_hbm, kpe_hbm, cos_ref, sin_ref, qt_ref, qf_ref, cn_ref, kn_ref, cosn_ref,
                        sinn_ref, lw_ref, wuv_ref, gpe_ref, o_ref, l_scr, qpe_scr, m_scr, d_scr, acc_scr, kpad_scr,
                        cpad_scr, lat_buf, kpe_buf, sem, *, n_pages, sub, n_tok, page):
    s_id = pl.program_id(0)
    j = pl.program_id(1)
    n_steps = pl.num_programs(1)
    step = s_id * n_steps + j
    last = pl.num_programs(0) * n_steps - 1
    slot = lax.rem(step, 2)
    nq = MLA_HEADS * n_tok
    n_up = MLA_HEADS * MLA_NOPE

    def page_copies(pid, sl, p):
        return (pltpu.make_async_copy(lat_hbm.at[0, pid], lat_buf.at[sl, p], sem.at[sl, 0]),
                pltpu.make_async_copy(kpe_hbm.at[0, pid], kpe_buf.at[sl, p], sem.at[sl, 1]))

    def start_pages(seq, st, sl, pages=range(n_pages)):
        for p in pages:
            for c in page_copies(pt_ref[seq, st * n_pages + p], sl, p):
                c.start()

    def wait_pages(sl):
        for p in range(n_pages):
            for c in page_copies(0, sl, p):
                c.wait()

    @pl.when(step == 0)
    def _():
        start_pages(0, 0, 0)

    wait_pages(slot)
    wrap = j == n_steps - 1
    nxt_seq = jnp.where(wrap, jnp.where(step == last, 0, s_id + 1), s_id)
    nxt_j = jnp.where(wrap, 0, j + 1)
    start_pages(nxt_seq, nxt_j, 1 - slot)
    lat_refs = [lat_buf.at[slot, p] for p in range(n_pages)]
    kpe_refs = [kpe_buf.at[slot, p] for p in range(n_pages)]

    @pl.when(j == 0)
    def _():
        l_scr[:n_up, :] = lw_ref[...]
        qt = qt_ref[...]
        qf = qf_ref[...]
        l_scr[n_up:, :] = jnp.concatenate(
            [qt[:, h * KV_LORA:(h + 1) * KV_LORA] for h in range(MLA_HEADS)], axis=0).astype(BF16)
        lane = lax.broadcasted_iota(jnp.int32, (nq, HEAD_PAD), 1)
        qpe = jnp.concatenate([qf[:, h * HEAD_PAD:(h + 1) * HEAD_PAD] for h in range(MLA_HEADS)], axis=0)
        qpe_scr[...] = jnp.where(lane < MLA_ROPE, qpe, 0.0).astype(BF16)
        m_scr[...] = jnp.full(m_scr.shape, NEG_BIG, F32)
        d_scr[...] = jnp.zeros(d_scr.shape, F32)
        acc_scr[...] = jnp.zeros(acc_scr.shape, F32)

    def scores(cb, kpe_t, cos_t, sin_t):
        tk = cb.shape[0]
        big = _dot_nt(l_scr[...], cb)
        k_t = big[:n_up]
        ss = jnp.sum((k_t * k_t).reshape(MLA_HEADS, MLA_NOPE, tk), axis=1)
        ss = ss + jnp.sum(kpe_t * kpe_t, axis=0, keepdims=True)
        rs = lax.rsqrt(ss * (1.0 / MLA_QK) + EPS)
        kg = kpe_t * gpe_ref[...]
        half = MLA_ROPE // 2
        x1, x2 = kg[:half], kg[half:]
        rot = jnp.concatenate([x1 * cos_t - x2 * sin_t, x1 * sin_t + x2 * cos_t], axis=0)
        pe = _dot(qpe_scr[:, :MLA_ROPE], rot.astype(BF16))
        rs_rows = jnp.concatenate([jnp.broadcast_to(rs[h:h + 1, :], (n_tok, tk)) for h in range(MLA_HEADS)], axis=0)
        return (big[n_up:] + pe) * rs_rows

    def update(s_list, cb_list):
        m_old = m_scr[...]
        m_new = m_old
        for s in s_list:
            m_new = jnp.maximum(m_new, jnp.max(s, axis=-1, keepdims=True))
        alpha = jnp.exp2(m_old - m_new)
        d = alpha * d_scr[...]
        acc = alpha * acc_scr[...]
        for s, cb in zip(s_list, cb_list):
            p = jnp.exp2(s - m_new)
            d = d + jnp.sum(p, axis=-1, keepdims=True)
            acc = acc + _dot(p.astype(BF16), cb)
        m_scr[...] = m_new
        d_scr[...] = d
        acc_scr[...] = acc

    s_list, cb_list = [], []
    for g in range(n_pages // sub):
        pages = range(g * sub, (g + 1) * sub)
        cb = jnp.concatenate([lat_refs[b][...] for b in pages], axis=0).astype(BF16)
        kpe_t = jnp.concatenate([kpe_refs[b][...] for b in pages], axis=1)
        cols = slice(g * sub * page, (g + 1) * sub * page)
        s_list.append(scores(cb, kpe_t, cos_ref[:, cols], sin_ref[:, cols]))
        cb_list.append(cb)
    update(s_list, cb_list)

    @pl.when(j == pl.num_programs(1) - 1)
    def _():
        cpad_scr[...] = jnp.zeros(cpad_scr.shape, F32)
        cpad_scr[:n_tok, :] = cn_ref[...]
        kpad_scr[...] = jnp.zeros(kpad_scr.shape, F32)
        kpad_scr[:n_tok, :MLA_ROPE] = kn_ref[...]
        rowi = lax.broadcasted_iota(jnp.int32, (nq, page), 0)
        coli = lax.broadcasted_iota(jnp.int32, (nq, page), 1)
        keep = coli <= (rowi & (n_tok - 1))
        cb = cpad_scr[...].astype(BF16)
        s = scores(cb, kpad_scr[...].T[:MLA_ROPE], cosn_ref[...], sinn_ref[...])
        update([jnp.where(keep, s, NEG_BIG)], [cb])
        ctx = (acc_scr[...] / d_scr[...]).astype(BF16)
        out = jnp.zeros(o_ref.shape, F32)
        for h in range(MLA_HEADS):
            out = out + _dot(ctx, wuv_ref[h])[h * n_tok:(h + 1) * n_tok]
        o_ref[...] = out

    @pl.when(step == last)
    def _():
        wait_pages(1 - slot)


def _attn_sample(page_table, cache_latent, cache_krope_t, tabs, qt, qf, ckv, kpe, w, n_tok, n_pages_step, sub):
    n_seq, n_pages = page_table.shape
    page = cache_latent.shape[2]
    n_steps = n_pages // n_pages_step
    tk = n_pages_step * page
    nq = MLA_HEADS * n_tok

    seq_row = lambda s, j, pt: (s, 0)
    const2 = lambda s, j, pt: (0, 0)
    half = MLA_ROPE // 2
    in_specs = [pl.BlockSpec(memory_space=pl.ANY), pl.BlockSpec(memory_space=pl.ANY)]
    in_specs += [pl.BlockSpec((half, tk), lambda s, j, pt: (0, j)), pl.BlockSpec((half, tk), lambda s, j, pt: (0, j)),
                 pl.BlockSpec((n_tok, MLA_HEADS * KV_LORA), seq_row),
                 pl.BlockSpec((n_tok, MLA_HEADS * HEAD_PAD), seq_row),
                 pl.BlockSpec((n_tok, KV_LORA), seq_row), pl.BlockSpec((n_tok, MLA_ROPE), seq_row),
                 pl.BlockSpec((half, page), const2), pl.BlockSpec((half, page), const2),
                 pl.BlockSpec(w['uk_t'].shape, const2),
                 pl.BlockSpec(w['uv_blk'].shape, lambda s, j, pt: (0, 0, 0)),
                 pl.BlockSpec((MLA_ROPE, 1), const2)]
    grid_spec = pltpu.PrefetchScalarGridSpec(
        num_scalar_prefetch=1, grid=(n_seq, n_steps), in_specs=in_specs,
        out_specs=pl.BlockSpec((n_tok, MLA_HEADS * MLA_V), seq_row),
        scratch_shapes=[pltpu.VMEM((MLA_HEADS * MLA_NOPE + nq, KV_LORA), BF16),
                        pltpu.VMEM((nq, HEAD_PAD), BF16),
                        pltpu.VMEM((nq, 1), F32), pltpu.VMEM((nq, 1), F32), pltpu.VMEM((nq, KV_LORA), F32),
                        pltpu.VMEM((page, LANES), F32), pltpu.VMEM((page, KV_LORA), F32),
                        pltpu.VMEM((2, n_pages_step, page, KV_LORA), F32),
                        pltpu.VMEM((2, n_pages_step, MLA_ROPE, page), F32),
                        pltpu.SemaphoreType.DMA((2, 2))])
    return pl.pallas_call(
        functools.partial(_attn_sample_kernel, n_pages=n_pages_step, sub=sub, n_tok=n_tok, page=page),
        grid_spec=grid_spec, out_shape=jax.ShapeDtypeStruct((n_seq * n_tok, MLA_HEADS * MLA_V), F32),
        compiler_params=_params(("arbitrary", "arbitrary")), name="attn_sample",
    )(page_table, cache_latent, cache_krope_t,
      tabs['cos_t'], tabs['sin_t'], qt, qf, ckv, kpe, tabs['cosn_t'], tabs['sinn_t'],
      w['uk_t'], w['uv_blk'], w['gpe'])


def _post_kernel(r_ref, m_ref, x_ref, wo_ref, g_ref, wrh_ref, wrl_ref, br_ref, h_ref, xn_ref, ti_ref, tg_ref):
    half = wo_ref.shape[0] // 2
    mix = _dot(r_ref[...], wo_ref[:half, :]) + _dot(m_ref[...], wo_ref[half:, :])
    h = x_ref[...] + mix
    h_ref[...] = h
    xn = _rms(h, g_ref[...])
    xn_ref[...] = xn
    hi = xn.astype(BF16)
    lo = (xn - hi.astype(F32)).astype(BF16)
    wrh = wrh_ref[...]
    work = _dot(hi, wrh) + _dot(hi, wrl_ref[...]) + _dot(lo, wrh) + br_ref[...]
    lane = lax.broadcasted_iota(jnp.int32, work.shape, 1).astype(F32)
    idx = jnp.zeros(work.shape, F32)
    val = jnp.zeros(work.shape, F32)
    vmax = None
    denom = None
    for k in range(TOP_K):
        vk = jnp.max(work, axis=-1, keepdims=True)
        ik = jnp.min(jnp.where(work == vk, lane, float(LANES)), axis=-1, keepdims=True)
        work = jnp.where(lane == ik, -jnp.inf, work)
        if k == 0:
            vmax = vk
        ek = jnp.exp(vk - vmax)
        denom = ek if k == 0 else denom + ek
        idx = jnp.where(lane == float(k), ik, idx)
        val = jnp.where(lane == float(k), ek, val)
    ti_ref[...] = idx.astype(jnp.int32)
    tg_ref[...] = val / denom


def _post(r, m, x, w, tm):
    rows = x.shape[0]
    row = lambda i: (i, 0)
    const = lambda i: (0, 0)
    in_specs = [pl.BlockSpec((tm, r.shape[1]), row), pl.BlockSpec((tm, m.shape[1]), row),
                pl.BlockSpec((tm, D_MODEL), row)]
    in_specs += [pl.BlockSpec(w[n].shape, const) for n in ('wo', 'gffn', 'wr_hi', 'wr_lo', 'br')]
    out_shape = [jax.ShapeDtypeStruct((rows, D_MODEL), F32), jax.ShapeDtypeStruct((rows, D_MODEL), F32),
                 jax.ShapeDtypeStruct((rows, LANES), jnp.int32), jax.ShapeDtypeStruct((rows, LANES), F32)]
    out_specs = [pl.BlockSpec((tm, D_MODEL), row), pl.BlockSpec((tm, D_MODEL), row),
                 pl.BlockSpec((tm, LANES), row), pl.BlockSpec((tm, LANES), row)]
    return pl.pallas_call(
        _post_kernel, grid=(rows // tm,), in_specs=in_specs, out_specs=out_specs, out_shape=out_shape,
        compiler_params=_params(("arbitrary",)), name="post",
    )(r, m, x, w['wo'], w['gffn'], w['wr_hi'], w['wr_lo'], w['br'])


def _moe_kernel(be_ref, nu_ref, xs_ref, wgu_ref, bgu_ref, wd_ref, bd_ref, o_ref, wgu_b, wd_b):
    i = pl.program_id(0)
    e = be_ref[i]
    e_prev = be_ref[jnp.maximum(i - 1, 0)]

    @pl.when((i == 0) | (e != e_prev))
    def _():
        wgu_b[...] = wgu_ref[...].astype(BF16)
        wd_b[...] = wd_ref[...].astype(BF16)

    @pl.when(i < nu_ref[0])
    def _():
        hgu = _dot(xs_ref[...].astype(BF16), wgu_b[...]) + bgu_ref[...]
        g = jnp.minimum(hgu[:, :D_FF], SWIGLU_LIMIT)
        u = jnp.clip(hgu[:, D_FF:], -SWIGLU_LIMIT, SWIGLU_LIMIT)
        hid = (u + 1.0) * (g * jax.nn.sigmoid(SWIGLU_ALPHA * g))
        o_ref[...] = _dot(hid.astype(BF16), wd_b[...]) + bd_ref[...]

    @pl.when(i >= nu_ref[0])
    def _():
        o_ref[...] = jnp.zeros(o_ref.shape, F32)


def _moe_blocks(block_e, n_used, xs, w_gate_up, b_gate_up, w_down, b_down, blk):
    n_blocks = block_e.shape[0]
    row = lambda i, be, nu: (i, 0)
    in_specs = [pl.BlockSpec((blk, D_MODEL), row),
                pl.BlockSpec((None, D_MODEL, 2 * D_FF), lambda i, be, nu: (be[i], 0, 0)),
                pl.BlockSpec((None, 1, 2 * D_FF), lambda i, be, nu: (be[i], 0, 0)),
                pl.BlockSpec((None, D_FF, D_MODEL), lambda i, be, nu: (be[i], 0, 0)),
                pl.BlockSpec((None, 1, D_MODEL), lambda i, be, nu: (be[i], 0, 0))]
    grid_spec = pltpu.PrefetchScalarGridSpec(
        num_scalar_prefetch=2, grid=(n_blocks,), in_specs=in_specs,
        out_specs=pl.BlockSpec((blk, D_MODEL), row),
        scratch_shapes=[pltpu.VMEM((D_MODEL, 2 * D_FF), BF16), pltpu.VMEM((D_FF, D_MODEL), BF16)])
    return pl.pallas_call(
        _moe_kernel, grid_spec=grid_spec, out_shape=jax.ShapeDtypeStruct((n_blocks * blk, D_MODEL), F32),
        compiler_params=_params(("arbitrary",)), name="moe",
    )(block_e, n_used, xs, w_gate_up, b_gate_up[:, None, :], w_down, b_down[:, None, :])


def _combine_kernel(h_ref, g_ref, y_ref, o_ref):
    g = g_ref[...]
    acc = y_ref[0] * g[:, 0:1]
    for k in range(1, TOP_K):
        acc = acc + y_ref[k] * g[:, k:k + 1]
    o_ref[...] = h_ref[...] + acc


def _combine(h, gates, y4, block0, rows, tm):
    row = lambda i: (block0 + i, 0)
    return pl.pallas_call(
        _combine_kernel, grid=(rows // tm,),
        in_specs=[pl.BlockSpec((tm, D_MODEL), row), pl.BlockSpec((tm, LANES), row),
                  pl.BlockSpec((TOP_K, tm, D_MODEL), lambda i: (0, block0 + i, 0))],
        out_specs=pl.BlockSpec((tm, D_MODEL), lambda i: (i, 0)),
        out_shape=jax.ShapeDtypeStruct((rows, D_MODEL), F32),
        compiler_params=_params(("arbitrary",)), name="combine",
    )(h, gates, y4)


def _gather_rows(table, idx):
    n, d = idx.shape[0], table.shape[1]
    n_win = n // GATHER_WINDOW
    assert n % (GATHER_WINDOW * SC_CORES * SC_SUBCORES) == 0
    idx = jnp.zeros((n_win, LANES), jnp.int32).at[:, :GATHER_WINDOW].set(idx.reshape(n_win, GATHER_WINDOW))
    mesh = plsc.VectorSubcoreMesh(core_axis_name="core", subcore_axis_name="subcore",
                                  num_cores=SC_CORES, num_subcores=SC_SUBCORES)

    @functools.partial(pl.kernel, out_type=jax.ShapeDtypeStruct((n, d), table.dtype), mesh=mesh)
    def gather(x_hbm, i_hbm, o_hbm):
        def body(i_vmem, o_vmem):
            pltpu.sync_copy(x_hbm.at[i_vmem.at[0, pl.ds(0, GATHER_WINDOW)]], o_vmem)

        pltpu.emit_pipeline(
            body, grid=(n_win,),
            in_specs=[pl.BlockSpec((1, LANES), index_map=lambda i: (i, 0))],
            out_specs=[pl.BlockSpec((GATHER_WINDOW, d), index_map=lambda i: (i, 0))],
            core_axis_name=("core", "subcore"), dimension_semantics=(pltpu.PARALLEL,),
        )(i_hbm, o_hbm)

    return gather(table, idx)


def _route(top_i, blk):
    t = top_i.shape[0]
    a = t * TOP_K
    flat_e = top_i.reshape(a)
    assert N_EXPERTS * a < 2 ** 31
    order = jnp.sort(flat_e.astype(jnp.int32) * a + jnp.arange(a, dtype=jnp.int32)) % a
    tok_sorted = (order // TOP_K).astype(jnp.int32)
    experts = jnp.arange(N_EXPERTS, dtype=jnp.int32)
    hit = (top_i[:, :, None] == experts[None, None, :]).astype(jnp.int32)
    per_tok = hit.sum(axis=1)
    before = jnp.cumsum(per_tok, axis=0) - per_tok
    counts = per_tok.sum(axis=0)
    padded = (counts + blk - 1) // blk * blk
    start = jnp.cumsum(counts) - counts
    pend = jnp.cumsum(padded)
    pstart = pend - padded
    dest = ((before + pstart[None, :])[:, None, :] * hit).sum(axis=2)
    n_blocks = -(-a // blk) + N_EXPERTS
    block_e = jnp.minimum((pend[None, :] <= (jnp.arange(n_blocks, dtype=jnp.int32) * blk)[:, None]).sum(axis=1),
                          N_EXPERTS - 1).astype(jnp.int32)
    slot = jnp.arange(n_blocks * blk, dtype=jnp.int32)
    slot_e = jnp.repeat(block_e, blk)
    r = slot - pstart[slot_e]
    slot_tok = jnp.where(r < counts[slot_e], tok_sorted[jnp.minimum(start[slot_e] + r, a - 1)], slot % t)
    n_used = (pend[-1:] // blk).astype(jnp.int32)
    return slot_tok, dest.astype(jnp.int32), block_e, n_used


def _rope_tables(pos):
    pos = pos.astype(F32)[:, None]
    n = pos.shape[0]
    hr = RET_DK // 2
    ang = pos * (RET_THETA ** (-jnp.arange(hr, dtype=F32) / hr))[None, :]
    cos, sin = jnp.cos(ang), jnp.sin(ang)
    zr = jnp.zeros((n, LANES - RET_DK), F32)
    cr = jnp.concatenate([cos, cos, zr], axis=1)
    sr = jnp.concatenate([-sin, sin, zr], axis=1)
    hm = MLA_ROPE // 2
    ang = pos * (MLA_THETA ** (-jnp.arange(hm, dtype=F32) / hm))[None, :]
    cos, sin = jnp.cos(ang), jnp.sin(ang)
    cm = jnp.concatenate([cos, cos, jnp.ones((n, LANES - MLA_ROPE), F32)], axis=1)
    sma = jnp.concatenate([jnp.zeros((n, hm), F32), sin, jnp.zeros((n, LANES - MLA_ROPE), F32)], axis=1)
    smb = jnp.concatenate([-sin, jnp.zeros((n, LANES - hm), F32)], axis=1)
    return [cr, sr, cm, sma, smb]


def _rope_tables_t(pos):
    hm = MLA_ROPE // 2
    ang = pos.astype(F32)[:, None] * (MLA_THETA ** (-jnp.arange(hm, dtype=F32) / hm))[None, :]
    return jnp.cos(ang).T, jnp.sin(ang).T


def _decay_tables(log_gamma, c, n_rep):
    idx = jnp.arange(c, dtype=F32)
    diff = idx[:, None] - idx[None, :]
    decay = jnp.where(diff >= 0, jnp.exp(jnp.maximum(diff, 0.0)[None] * log_gamma[:, None, None]), 0.0)
    if n_rep > 1:
        eye = jnp.eye(n_rep, dtype=F32)
        decay = (eye[None, :, None, :, None] * decay[:, None, :, None, :]).reshape(RET_HEADS, n_rep * c, n_rep * c)
    rowd = jnp.tile(jnp.exp((idx + 1.0)[:, None] * log_gamma[None, :]), (n_rep, 1))
    cold = jnp.tile(jnp.exp((c - 1.0 - idx)[:, None] * log_gamma[None, :]), (n_rep, 1))
    cpow = jnp.broadcast_to(jnp.exp(c * log_gamma)[None, :], (LANES, RET_HEADS))
    return {'dec': decay, 'rowd': rowd, 'cold': cold, 'cpow': cpow}


def _pad_heads(wm, n_heads, width, offset):
    k = wm.shape[0]
    wm = wm.reshape(k, n_heads, width)
    out = jnp.zeros((k, n_heads, LANES), wm.dtype).at[:, :, offset:offset + width].set(wm)
    return out.reshape(k, n_heads * LANES)


def _prep_weights(norm_attn_g, w_in, ret_out_g, q_a_norm_g, w_q_b, kv_a_norm_g, w_uk, w_uv,
                  qk_norm_q_g, qk_norm_k_g, w_out, norm_ffn_g, w_router, b_router):
    splits = [RET_HEADS * RET_DK, RET_HEADS * RET_DK, RET_HEADS * RET_DV, RET_HEADS * RET_DV, Q_LORA, KV_LORA,
              MLA_ROPE]
    offs = [0]
    for s in splits:
        offs.append(offs[-1] + s)
    part = [w_in[:, offs[i]:offs[i + 1]] for i in range(len(splits))]
    kpe_cols = jnp.zeros((D_MODEL, LANES), F32).at[:, :MLA_ROPE].set(part[6])
    win = jnp.concatenate([_pad_heads(part[0], RET_HEADS, RET_DK, 0), _pad_heads(part[1], RET_HEADS, RET_DK, 0),
                           part[2], part[3], part[4], part[5], kpe_cols], axis=1).astype(BF16)

    def mla_cols(wm):
        k = wm.shape[0]
        wm = wm.reshape(k, MLA_HEADS, MLA_QK)
        wm = jnp.concatenate([wm[:, :, MLA_NOPE:], wm[:, :, :MLA_NOPE]], axis=2)
        return _pad_heads(wm.reshape(k, MLA_HEADS * MLA_QK), MLA_HEADS, MLA_QK, 0)

    def mla_gain(g):
        g = jnp.concatenate([g[MLA_NOPE:], g[:MLA_NOPE], jnp.zeros((LANES - MLA_QK,), F32)])
        return g[None, :]

    gk_nope = qk_norm_k_g[:MLA_NOPE]
    uk_heads = w_uk.reshape(KV_LORA, MLA_HEADS, MLA_NOPE)
    wabs = jnp.zeros((MLA_HEADS, HEAD_PAD, KV_LORA), F32).at[:, MLA_ROPE:MLA_ROPE + MLA_NOPE, :].set(
        (uk_heads * gk_nope[None, None, :]).transpose(1, 2, 0))
    head_of_col = jnp.arange(MLA_HEADS * MLA_V) // MLA_V
    uv_blk = jnp.where(head_of_col[None, None, :] == jnp.arange(MLA_HEADS)[:, None, None], w_uv[None], 0.0)
    wr = jnp.zeros((D_MODEL, LANES), F32).at[:, :N_EXPERTS].set(w_router)
    wr_hi = wr.astype(BF16)
    return {
        'gattn': norm_attn_g[None, :], 'win': win, 'gqa': q_a_norm_g[None, :], 'wqb': mla_cols(w_q_b).astype(BF16),
        'gq': mla_gain(qk_norm_q_g), 'gkv': kv_a_norm_g[None, :],
        'wuk': _pad_heads(w_uk, MLA_HEADS, MLA_NOPE, MLA_ROPE).astype(BF16), 'gk': mla_gain(qk_norm_k_g),
        'wuv': w_uv.astype(BF16), 'gret': ret_out_g[None, :],
        'wabs': wabs.astype(BF16), 'uk_t': w_uk.T.astype(BF16), 'uv_blk': uv_blk.astype(BF16),
        'gpe': qk_norm_k_g[MLA_NOPE:, None],
        'wo': w_out.astype(BF16), 'gffn': norm_ffn_g[None, :], 'wr_hi': wr_hi,
        'wr_lo': (wr - wr_hi.astype(F32)).astype(BF16),
        'br': jnp.full((1, LANES), NEG_BIG, F32).at[0, :N_EXPERTS].set(b_router),
    }


def _pad_rows(a, rows):
    return jnp.zeros((rows,) + a.shape[1:], a.dtype).at[:a.shape[0]].set(a)


def _largest_divisor(n, cap):
    d = min(n, cap)
    while n % d:
        d -= 1
    return d


def kernel(x_prompt, x_sample, cache_latent, cache_krope, state_retention, page_table, meta_tokens, norm_attn_g, w_in, ret_out_g, q_a_norm_g, w_q_b, kv_a_norm_g, w_uk, w_uv, qk_norm_q_g, qk_norm_k_g, w_out, norm_ffn_g, w_router, b_router, w_gate_up, b_gate_up, w_down, b_down):
    assert w_in.shape[0] == 1, "single-layer trunk"
    batch, seq, _ = x_prompt.shape
    n_seq, n_tok, _ = x_sample.shape
    n_pages, page = page_table.shape[1], cache_latent.shape[2]
    past = n_pages * page
    assert seq % RET_CHUNK == 0 and LANES % n_tok == 0 and (n_seq * n_tok) % LANES == 0 and page == LANES
    w = _prep_weights(norm_attn_g[0], w_in[0], ret_out_g[0], q_a_norm_g[0], w_q_b[0], kv_a_norm_g[0], w_uk[0],
                      w_uv[0], qk_norm_q_g[0], qk_norm_k_g[0], w_out[0], norm_ffn_g[0], w_router[0], b_router[0])
    log_gamma = jnp.log1p(-jnp.exp2(-5.0 - jnp.arange(RET_HEADS, dtype=F32)))

    rows_p = batch * seq
    rows_s = n_seq * n_tok
    tm = _largest_divisor(min(seq, rows_s), ROW_BLOCK)
    nb_seq = seq // tm
    x = jnp.concatenate([x_prompt.reshape(rows_p, D_MODEL), x_sample.reshape(rows_s, D_MODEL)], axis=0)
    pos_rows = jnp.concatenate([N_META + jnp.arange(seq), jnp.tile(past + jnp.arange(n_tok), tm // n_tok)])
    tabs = _rope_tables(pos_rows)
    n_pb = rows_p // tm
    rq, rk, rv, zg, mq, ckv, kpe, k, v = _front(
        x, tabs, lambda i: jnp.where(i < n_pb, i % nb_seq, nb_seq), w, tm)
    _, mrk, mrv, _, _, mckv, mkpe, mk, mv = _front(
        meta_tokens, _rope_tables(jnp.arange(N_META)), lambda i: i, w, N_META)

    dt_p = _decay_tables(log_gamma, RET_CHUNK, 1)
    dt_p['mcol'] = _pad_rows(jnp.exp((N_META - 1.0 - jnp.arange(N_META, dtype=F32))[:, None] * log_gamma[None, :]),
                             RET_CHUNK)
    gret = w['gret']
    r_p, st_p = _ret_prompt(rq, rk, rv, zg, _pad_rows(mrk, RET_CHUNK), _pad_rows(mrv, RET_CHUNK), dt_p, gret,
                            batch, seq // RET_CHUNK)
    dt_s = _decay_tables(log_gamma, n_tok, LANES // n_tok)
    r_s, st_s = _ret_sample(rq[rows_p:], rk[rows_p:], rv[rows_p:], zg[rows_p:], state_retention[0], dt_s, gret, n_tok)

    blk = _largest_divisor(seq, ATTN_BLOCK)
    m_p = _attn_prompt(mq, k, v, _pad_rows(mk, LANES), _pad_rows(mv, LANES), batch, seq, blk)
    tm_s = _largest_divisor(rows_s, ROW_BLOCK)
    qt, qf = _absorb(mq, w['wabs'], rows_p // tm_s, rows_s, tm_s)
    n_pages_step = _largest_divisor(n_pages, PAGES_PER_STEP)
    cos_t, sin_t = _rope_tables_t(jnp.arange(past))
    cosn_t, sinn_t = _rope_tables_t(past + jnp.arange(page))
    tabs_s = {'cos_t': cos_t, 'sin_t': sin_t, 'cosn_t': cosn_t, 'sinn_t': sinn_t}
    m_s = _attn_sample(page_table, cache_latent, jnp.swapaxes(cache_krope, 2, 3), tabs_s, qt, qf, ckv[rows_p:],
                       kpe[rows_p:], w, n_tok, n_pages_step, _largest_divisor(n_pages_step, PAGES_PER_CHAIN))

    r_all = jnp.concatenate([r_p, r_s], axis=0)
    m_all = jnp.concatenate([m_p, m_s.astype(BF16)], axis=0)
    h1, xn2, top_i, gates = _post(r_all, m_all, x, w, tm)

    slot_tok, dest, block_e, n_used = _route(top_i[:, :TOP_K], EXPERT_BLOCK)
    outs = _moe_blocks(block_e, n_used, _gather_rows(xn2, slot_tok), w_gate_up[0], b_gate_up[0], w_down[0],
                       b_down[0], EXPERT_BLOCK)
    y4 = _gather_rows(outs, dest.T.reshape(-1)).reshape(TOP_K, rows_p + rows_s, D_MODEL)
    y_prompt = _combine(h1, gates, y4, 0, rows_p, tm).reshape(batch, seq, D_MODEL)
    y_sample = _combine(h1, gates, y4, n_pb, rows_s, tm).reshape(n_seq, n_tok, D_MODEL)
    lat_p = jnp.concatenate([jnp.broadcast_to(mckv[None], (batch, N_META, KV_LORA)),
                             ckv[:rows_p].reshape(batch, seq, KV_LORA)], axis=1)[None]
    kpe_p = jnp.concatenate([jnp.broadcast_to(mkpe[None], (batch, N_META, MLA_ROPE)),
                             kpe[:rows_p].reshape(batch, seq, MLA_ROPE)], axis=1)[None]
    return (y_prompt, y_sample, lat_p, kpe_p, st_p[None],
            ckv[rows_p:].reshape(n_seq, n_tok, KV_LORA)[None], kpe[rows_p:].reshape(n_seq, n_tok, MLA_ROPE)[None],
            st_s[None])
```

```python
import functools

import jax
import jax.numpy as jnp
from jax import lax
from jax.experimental import pallas as pl
from jax.experimental.pallas import tpu as pltpu
from jax.experimental.pallas import tpu_sc as plsc

F32 = jnp.float32
BF16 = jnp.bfloat16

D_MODEL = 1024
N_META = 16
RET_HEADS = 4
RET_DK = 64
RET_DV = 128
RET_CHUNK = 128
RET_THETA = 10000.0
MLA_HEADS = 8
MLA_NOPE = 64
MLA_ROPE = 32
MLA_QK = MLA_NOPE + MLA_ROPE
MLA_V = 64
Q_LORA = 384
KV_LORA = 256
MLA_THETA = 10000.0
MLA_SCALE = MLA_QK ** -0.5
LOG2E = 1.4426950408889634
N_EXPERTS = 32
TOP_K = 4
D_FF = 1024
SWIGLU_LIMIT = 7.0
SWIGLU_ALPHA = 1.702
EPS = 1e-6
NEG_BIG = -1e30

LANES = 128
HEAD_PAD = LANES
VMEM_LIMIT = 56 * 1024 * 1024
ROW_BLOCK = 512
ATTN_BLOCK = 1024
PAGES_PER_STEP = 64
PAGES_PER_CHAIN = 32
EXPERT_BLOCK = 512
SC_CORES, SC_SUBCORES = 2, 16
GATHER_WINDOW = 32

_OFF_RQ = 0
_OFF_RK = _OFF_RQ + RET_HEADS * LANES
_OFF_RV = _OFF_RK + RET_HEADS * LANES
_OFF_ZG = _OFF_RV + RET_HEADS * RET_DV
_OFF_CQ = _OFF_ZG + RET_HEADS * RET_DV
_OFF_CKV = _OFF_CQ + Q_LORA
_OFF_KPE = _OFF_CKV + KV_LORA
IN_PAD = _OFF_KPE + LANES


def _params(sem):
    return pltpu.CompilerParams(dimension_semantics=sem, vmem_limit_bytes=VMEM_LIMIT)


def _rms(x, g):
    return x * lax.rsqrt(jnp.mean(x * x, axis=-1, keepdims=True) + EPS) * g


def _dot(a, b):
    return jnp.dot(a, b, preferred_element_type=F32)


def _dot_nt(a, b):
    return lax.dot_general(a, b, (((1,), (1,)), ((), ())), preferred_element_type=F32)


def _dot_tn(a, b):
    return lax.dot_general(a, b, (((0,), (0,)), ((), ())), preferred_element_type=F32)


def _front_kernel(xa_ref, xb_ref, gattn_ref, win_ref, cr_ref, sr_ref, cm_ref, sma_ref, smb_ref,
                  gqa_ref, wqb_ref, gq_ref, gkv_ref, wuk_ref, gk_ref, wuv_ref,
                  rq_ref, rk_ref, rv_ref, zg_ref, mq_ref, ckv_ref, kpe_ref, k_ref, v_ref, *, n_a):
    tm = xa_ref.shape[0]
    x = jnp.where(pl.program_id(0) < n_a, xa_ref[...], xb_ref[...])
    xn = _rms(x, gattn_ref[...])
    z = _dot(xn.astype(BF16), win_ref[...])

    lane = lax.broadcasted_iota(jnp.int32, (tm, LANES), 1)
    upper = (lane & (RET_DK // 2)) != 0
    cr, sr = cr_ref[...], sr_ref[...]

    def rope_ret(t):
        partner = jnp.where(upper, pltpu.roll(t, RET_DK // 2, 1), pltpu.roll(t, LANES - RET_DK // 2, 1))
        return t * cr + partner * sr

    for h in range(RET_HEADS):
        sl = slice(h * LANES, (h + 1) * LANES)
        rq_ref[:, sl] = rope_ret(z[:, _OFF_RQ + h * LANES:_OFF_RQ + (h + 1) * LANES])
        rk_ref[:, sl] = rope_ret(z[:, _OFF_RK + h * LANES:_OFF_RK + (h + 1) * LANES]) * (RET_DK ** -0.5)
    rv_ref[...] = z[:, _OFF_RV:_OFF_ZG].astype(BF16)
    zg_ref[...] = z[:, _OFF_ZG:_OFF_CQ]

    cm, sma, smb = cm_ref[...], sma_ref[...], smb_ref[...]

    def rope_mla(t):
        half = MLA_ROPE // 2
        return t * cm + pltpu.roll(t, half, 1) * sma + pltpu.roll(t, LANES - half, 1) * smb

    def head_norm(t, g):
        ms = jnp.sum(t * t, axis=-1, keepdims=True) * (1.0 / MLA_QK)
        return t * lax.rsqrt(ms + EPS) * g

    cq = _rms(z[:, _OFF_CQ:_OFF_CKV], gqa_ref[...])
    q = _dot(cq.astype(BF16), wqb_ref[...])
    gq = gq_ref[...]
    for h in range(MLA_HEADS):
        sl = slice(h * HEAD_PAD, (h + 1) * HEAD_PAD)
        mq_ref[:, sl] = (rope_mla(head_norm(q[:, sl], gq)) * (MLA_SCALE * LOG2E)).astype(BF16)

    ckv = _rms(z[:, _OFF_CKV:_OFF_KPE], gkv_ref[...])
    ckv_ref[...] = ckv
    kpe_slab = z[:, _OFF_KPE:IN_PAD]
    kpe_ref[...] = kpe_slab[:, :MLA_ROPE]
    ckv_b = ckv.astype(BF16)
    kn = _dot(ckv_b, wuk_ref[...])
    gk = gk_ref[...]
    for h in range(MLA_HEADS):
        sl = slice(h * HEAD_PAD, (h + 1) * HEAD_PAD)
        k_ref[:, sl] = rope_mla(head_norm(kn[:, sl] + kpe_slab, gk)).astype(BF16)
    v_ref[...] = _dot(ckv_b, wuv_ref[...]).astype(BF16)


def _front(xa, xb, tabs, tab_index, w, tm):
    n_a, n_b = xa.shape[0] // tm, (0 if xb is None else xb.shape[0] // tm)
    xb = xa if xb is None else xb
    rows = (n_a + n_b) * tm
    grid = (n_a + n_b,)
    row = lambda i: (i, 0)
    const = lambda i: (0, 0)
    tab = lambda i: (tab_index(i), 0)

    def full(a):
        return pl.BlockSpec(a.shape, const)

    in_specs = [pl.BlockSpec((tm, D_MODEL), lambda i: (jnp.minimum(i, n_a - 1), 0)),
                pl.BlockSpec((tm, D_MODEL), lambda i: (jnp.maximum(i - n_a, 0), 0)), full(w['gattn']), full(w['win'])]
    in_specs += [pl.BlockSpec((tm, LANES), tab)] * 5
    in_specs += [full(w[n]) for n in ('gqa', 'wqb', 'gq', 'gkv', 'wuk', 'gk', 'wuv')]
    widths = [(RET_HEADS * LANES, F32), (RET_HEADS * LANES, F32), (RET_HEADS * RET_DV, BF16),
              (RET_HEADS * RET_DV, F32), (MLA_HEADS * HEAD_PAD, BF16), (KV_LORA, F32), (MLA_ROPE, F32),
              (MLA_HEADS * HEAD_PAD, BF16), (MLA_HEADS * MLA_V, BF16)]
    out_shape = [jax.ShapeDtypeStruct((rows, n), dt) for n, dt in widths]
    out_specs = [pl.BlockSpec((tm, n), row) for n, _ in widths]
    return pl.pallas_call(
        functools.partial(_front_kernel, n_a=n_a), grid=grid, in_specs=in_specs, out_specs=out_specs,
        out_shape=out_shape, compiler_params=_params(("arbitrary",)), name="front",
    )(xa, xb, w['gattn'], w['win'], *tabs, w['gqa'], w['wqb'], w['gq'], w['gkv'], w['wuk'], w['gk'], w['wuv'])


def _ret_gate(o, zg, g):
    on = o * lax.rsqrt(jnp.mean(o * o, axis=-1, keepdims=True) + EPS) * g
    return (zg * jax.nn.sigmoid(zg)) * on


def _ret_prompt_kernel(q_ref, k_ref, v_ref, zg_ref, mk_ref, mv_ref, mcol_ref, dec_ref, rowd_ref, cold_ref,
                       cpow_ref, g_ref, r_ref, s_ref, s_scr):
    c = pl.program_id(1)

    @pl.when(c == 0)
    def _():
        for h in range(RET_HEADS):
            sl = slice(h * LANES, (h + 1) * LANES)
            kw = mk_ref[:, sl] * mcol_ref[:, h:h + 1]
            s_scr[h] = _dot_tn(kw.astype(BF16), mv_ref[:, sl])

    for h in range(RET_HEADS):
        sl = slice(h * LANES, (h + 1) * LANES)
        q = q_ref[:, sl]
        k = k_ref[:, sl]
        v = v_ref[:, sl]
        qb = q.astype(BF16)
        s0 = s_scr[h]
        scores = _dot_nt(qb, k.astype(BF16)) * dec_ref[h]
        inner = _dot(scores.astype(BF16), v)
        cross = _dot(qb, s0.astype(BF16)) * rowd_ref[:, h:h + 1]
        kw = k * cold_ref[:, h:h + 1]
        s_new = s0 * cpow_ref[:, h:h + 1] + _dot_tn(kw.astype(BF16), v)
        s_scr[h] = s_new
        r_ref[:, sl] = _ret_gate(inner + cross, zg_ref[:, sl], g_ref[:, sl]).astype(BF16)

    @pl.when(c == pl.num_programs(1) - 1)
    def _():
        for h in range(RET_HEADS):
            s_ref[0, h] = s_scr[h, :RET_DK, :]


def _ret_prompt(rq, rk, rv, zg, mk, mv, tabs, g, batch, n_chunks):
    cs = RET_CHUNK
    row = lambda b, c: (b * n_chunks + c, 0)
    const2 = lambda b, c: (0, 0)
    w4 = RET_HEADS * LANES
    in_specs = [pl.BlockSpec((cs, w4), row), pl.BlockSpec((cs, w4), row), pl.BlockSpec((cs, w4), row),
                pl.BlockSpec((cs, w4), row),
                pl.BlockSpec((cs, w4), const2), pl.BlockSpec((cs, w4), const2),
                pl.BlockSpec((cs, RET_HEADS), const2),
                pl.BlockSpec((RET_HEADS, cs, cs), lambda b, c: (0, 0, 0)),
                pl.BlockSpec((cs, RET_HEADS), const2), pl.BlockSpec((cs, RET_HEADS), const2),
                pl.BlockSpec((LANES, RET_HEADS), const2), pl.BlockSpec((1, w4), const2)]
    out_shape = [jax.ShapeDtypeStruct((batch * n_chunks * cs, w4), BF16),
                 jax.ShapeDtypeStruct((batch, RET_HEADS, RET_DK, RET_DV), F32)]
    out_specs = [pl.BlockSpec((cs, w4), row),
                 pl.BlockSpec((1, RET_HEADS, RET_DK, RET_DV), lambda b, c: (b, 0, 0, 0))]
    return pl.pallas_call(
        _ret_prompt_kernel, grid=(batch, n_chunks), in_specs=in_specs, out_specs=out_specs, out_shape=out_shape,
        scratch_shapes=[pltpu.VMEM((RET_HEADS, LANES, RET_DV), F32)],
        compiler_params=_params(("arbitrary", "arbitrary")), name="ret_prompt",
    )(rq, rk, rv, zg, mk, mv, tabs['mcol'], tabs['dec'], tabs['rowd'], tabs['cold'], tabs['cpow'], g)


def _ret_sample_kernel(q_ref, k_ref, v_ref, zg_ref, s0_ref, dec_ref, rowd_ref, cold_ref, cpow_ref, g_ref,
                       r_ref, s_ref, *, n_seq, n_tok):
    rows = n_seq * n_tok
    ri = lax.broadcasted_iota(jnp.int32, (rows, 1), 0)
    for h in range(RET_HEADS):
        sl = slice(h * LANES, (h + 1) * LANES)
        q = q_ref[:, sl]
        k = k_ref[:, sl]
        v = v_ref[:, sl]
        qb = q.astype(BF16)
        scores = _dot_nt(qb, k.astype(BF16)) * dec_ref[h]
        inner = _dot(scores.astype(BF16), v)
        kw = k * cold_ref[:, h:h + 1]
        cross = jnp.zeros((rows, RET_DV), F32)
        for s in range(n_seq):
            s0 = s0_ref[s, h]
            mine = (ri >= s * n_tok) & (ri < (s + 1) * n_tok)
            cross = cross + jnp.where(mine, _dot(qb[:, :RET_DK], s0.astype(BF16)), 0.0)
            upd = _dot_tn(jnp.where(mine, kw, 0.0).astype(BF16), v)
            s_ref[s, h] = s0 * cpow_ref[:RET_DK, h:h + 1] + upd[:RET_DK]
        cross = cross * rowd_ref[:, h:h + 1]
        r_ref[:, sl] = _ret_gate(inner + cross, zg_ref[:, sl], g_ref[:, sl]).astype(BF16)


def _ret_sample(rq, rk, rv, zg, state, tabs, g, n_tok):
    n_seq_total = state.shape[0]
    n_seq = LANES // n_tok
    rows = n_seq * n_tok
    w4 = RET_HEADS * LANES
    row = lambda i: (i, 0)
    const = lambda i: (0, 0)
    in_specs = [pl.BlockSpec((rows, w4), row)] * 4
    in_specs += [pl.BlockSpec((n_seq, RET_HEADS, RET_DK, RET_DV), lambda i: (i, 0, 0, 0)),
                 pl.BlockSpec((RET_HEADS, rows, rows), lambda i: (0, 0, 0)),
                 pl.BlockSpec((rows, RET_HEADS), const), pl.BlockSpec((rows, RET_HEADS), const),
                 pl.BlockSpec((LANES, RET_HEADS), const), pl.BlockSpec((1, w4), const)]
    out_shape = [jax.ShapeDtypeStruct((n_seq_total * n_tok, w4), BF16),
                 jax.ShapeDtypeStruct(state.shape, F32)]
    out_specs = [pl.BlockSpec((rows, w4), row),
                 pl.BlockSpec((n_seq, RET_HEADS, RET_DK, RET_DV), lambda i: (i, 0, 0, 0))]
    return pl.pallas_call(
        functools.partial(_ret_sample_kernel, n_seq=n_seq, n_tok=n_tok),
        grid=(n_seq_total // n_seq,), in_specs=in_specs, out_specs=out_specs, out_shape=out_shape,
        compiler_params=_params(("arbitrary",)), name="ret_sample",
    )(rq, rk, rv, zg, state, tabs['dec'], tabs['rowd'], tabs['cold'], tabs['cpow'], g)


def _attn_prompt_kernel(qi_ref, ki_ref, q_ref, k_ref, v_ref, km_ref, vm_ref, o_ref, m_scr, l_scr, acc_scr, *,
                        sub_tiles):
    qi = qi_ref[pl.program_id(2)]
    ki = ki_ref[pl.program_id(2)]
    tm, tk = q_ref.shape[0], k_ref.shape[0]

    @pl.when(ki == 0)
    def _():
        lane = lax.broadcasted_iota(jnp.int32, (tm, km_ref.shape[0]), 1)
        for hh in range(2):
            sl = slice(hh * HEAD_PAD, (hh + 1) * HEAD_PAD)
            s = jnp.where(lane < N_META, _dot_nt(q_ref[:, sl], km_ref[:, sl]), NEG_BIG)
            m = jnp.max(s, axis=-1, keepdims=True)
            p = jnp.exp2(s - m)
            m_scr[hh] = jnp.broadcast_to(m, (tm, LANES))
            l_scr[hh] = jnp.broadcast_to(jnp.sum(p, axis=-1, keepdims=True), (tm, LANES))
            acc_scr[hh] = _dot(p.astype(BF16), vm_ref[...])

    def tile(r0, nr, c0, nc, masked):
        rows = pl.ds(r0, nr)
        if masked:
            keep = (lax.broadcasted_iota(jnp.int32, (nr, nc), 1) <= lax.broadcasted_iota(jnp.int32, (nr, nc), 0))
        for hh in range(2):
            sl = slice(hh * HEAD_PAD, (hh + 1) * HEAD_PAD)
            s = _dot_nt(q_ref[rows, sl], k_ref[pl.ds(c0, nc), sl])
            if masked:
                s = jnp.where(keep, s, NEG_BIG)
            m_old = m_scr[hh, rows, :]
            m_new = jnp.maximum(m_old, jnp.max(s, axis=-1, keepdims=True))
            alpha = jnp.exp2(m_old - m_new)
            p = jnp.exp2(s - jnp.tile(m_new, (1, nc // LANES)))
            m_scr[hh, rows, :] = m_new
            l_scr[hh, rows, :] = alpha * l_scr[hh, rows, :] + jnp.sum(p, axis=-1, keepdims=True)
            acc_scr[hh, rows, :] = alpha * acc_scr[hh, rows, :] + _dot(p.astype(BF16), v_ref[pl.ds(c0, nc), :])

    @pl.when(ki < qi)
    def _():
        tile(0, tm, 0, tk, False)

    @pl.when(ki == qi)
    def _():
        ns = sub_tiles
        st = tm // ns
        for a in range(ns):
            if a > 0:
                tile(a * st, st, 0, a * st, False)
            tile(a * st, st, a * st, st, True)
        lane = lax.broadcasted_iota(jnp.int32, (tm, 2 * MLA_V), 1)
        o0 = acc_scr[0] / l_scr[0]
        o1 = acc_scr[1] / l_scr[1]
        o_ref[...] = jnp.where(lane < MLA_V, o0, o1).astype(BF16)


def _attn_prompt(mq, k, v, km, vm, batch, seq, blk):
    nb = seq // blk
    pairs = MLA_HEADS // 2
    pairs_qk = [(qi, ki) for qi in range(nb) for ki in range(qi + 1)]
    qi_tab = jnp.asarray([p[0] for p in pairs_qk], jnp.int32)
    ki_tab = jnp.asarray([p[1] for p in pairs_qk], jnp.int32)
    in_specs = [pl.BlockSpec((blk, 2 * HEAD_PAD), lambda b, h, t, qt, kt: (b * nb + qt[t], h)),
                pl.BlockSpec((blk, 2 * HEAD_PAD), lambda b, h, t, qt, kt: (b * nb + kt[t], h)),
                pl.BlockSpec((blk, 2 * MLA_V), lambda b, h, t, qt, kt: (b * nb + kt[t], h)),
                pl.BlockSpec((km.shape[0], 2 * HEAD_PAD), lambda b, h, t, qt, kt: (0, h)),
                pl.BlockSpec((vm.shape[0], 2 * MLA_V), lambda b, h, t, qt, kt: (0, h))]
    grid_spec = pltpu.PrefetchScalarGridSpec(
        num_scalar_prefetch=2, grid=(batch, pairs, len(pairs_qk)), in_specs=in_specs,
        out_specs=pl.BlockSpec((blk, 2 * MLA_V), lambda b, h, t, qt, kt: (b * nb + qt[t], h)),
        scratch_shapes=[pltpu.VMEM((2, blk, LANES), F32), pltpu.VMEM((2, blk, LANES), F32),
                        pltpu.VMEM((2, blk, 2 * MLA_V), F32)])
    return pl.pallas_call(
        functools.partial(_attn_prompt_kernel, sub_tiles=2 if blk % (2 * LANES) == 0 else 1), grid_spec=grid_spec,
        out_shape=jax.ShapeDtypeStruct((batch * seq, MLA_HEADS * MLA_V), BF16),
        compiler_params=_params(("arbitrary",) * 3), name="attn_prompt",
    )(qi_tab, ki_tab, mq, k, v, km, vm)


def _absorb_kernel(mq_ref, wabs_ref, qt_ref, qf_ref):
    q = mq_ref[...]
    qf_ref[...] = q.astype(F32)
    for h in range(MLA_HEADS):
        qt_ref[:, h * KV_LORA:(h + 1) * KV_LORA] = _dot(q[:, h * HEAD_PAD:(h + 1) * HEAD_PAD], wabs_ref[h])


def _absorb(mq, wabs, row0_blocks, rows, tm):
    return pl.pallas_call(
        _absorb_kernel, grid=(rows // tm,),
        in_specs=[pl.BlockSpec((tm, MLA_HEADS * HEAD_PAD), lambda i: (row0_blocks + i, 0)),
                  pl.BlockSpec(wabs.shape, lambda i: (0, 0, 0))],
        out_specs=[pl.BlockSpec((tm, MLA_HEADS * KV_LORA), lambda i: (i, 0)),
                   pl.BlockSpec((tm, MLA_HEADS * HEAD_PAD), lambda i: (i, 0))],
        out_shape=[jax.ShapeDtypeStruct((rows, MLA_HEADS * KV_LORA), F32),
                   jax.ShapeDtypeStruct((rows, MLA_HEADS * HEAD_PAD), F32)],
        compiler_params=_params(("arbitrary",)), name="absorb",
    )(mq, wabs)


def _attn_sample_kernel(pt_ref, lat_hbm, kpe_hbm, cos_ref, sin_ref, qt_ref, qf_ref, cn_ref, kn_ref, cosn_ref,
                        sinn_ref, lw_ref, wuv_ref, gpe_ref, o_ref, l_scr, qpe_scr, m_scr, d_scr, acc_scr, kpad_scr,
                        cpad_scr, lat_buf, kpe_buf, sem, *, n_pages, sub, n_tok, page):
    s_id = pl.program_id(0)
    j = pl.program_id(1)
    n_steps = pl.num_programs(1)
    step = s_id * n_steps + j
    last = pl.num_programs(0) * n_steps - 1
    slot = lax.rem(step, 2)
    nq = MLA_HEADS * n_tok
    n_up = MLA_HEADS * MLA_NOPE

    def page_copies(pid, sl, p):
        return (pltpu.make_async_copy(lat_hbm.at[0, pid], lat_buf.at[sl, p], sem.at[sl, 0]),
                pltpu.make_async_copy(kpe_hbm.at[0, pid], kpe_buf.at[sl, p], sem.at[sl, 1]))

    def start_pages(seq, st, sl, pages=range(n_pages)):
        for p in pages:
            for c in page_copies(pt_ref[seq, st * n_pages + p], sl, p):
                c.start()

    def wait_pages(sl):
        for p in range(n_pages):
            for c in page_copies(0, sl, p):
                c.wait()

    @pl.when(step == 0)
    def _():
        start_pages(0, 0, 0)

    wait_pages(slot)
    wrap = j == n_steps - 1
    nxt_seq = jnp.where(wrap, jnp.where(step == last, 0, s_id + 1), s_id)
    nxt_j = jnp.where(wrap, 0, j + 1)
    start_pages(nxt_seq, nxt_j, 1 - slot)
    lat_refs = [lat_buf.at[slot, p] for p in range(n_pages)]
    kpe_refs = [kpe_buf.at[slot, p] for p in range(n_pages)]

    @pl.when(j == 0)
    def _():
        l_scr[:n_up, :] = lw_ref[...]
        qt = qt_ref[...]
        qf = qf_ref[...]
        l_scr[n_up:, :] = jnp.concatenate(
            [qt[:, h * KV_LORA:(h + 1) * KV_LORA] for h in range(MLA_HEADS)], axis=0).astype(BF16)
        lane = lax.broadcasted_iota(jnp.int32, (nq, HEAD_PAD), 1)
        qpe = jnp.concatenate([qf[:, h * HEAD_PAD:(h + 1) * HEAD_PAD] for h in range(MLA_HEADS)], axis=0)
        qpe_scr[...] = jnp.where(lane < MLA_ROPE, qpe, 0.0).astype(BF16)
        m_scr[...] = jnp.full(m_scr.shape, NEG_BIG, F32)
        d_scr[...] = jnp.zeros(d_scr.shape, F32)
        acc_scr[...] = jnp.zeros(acc_scr.shape, F32)

    def scores(cb, kpe_t, cos_t, sin_t):
        tk = cb.shape[0]
        big = _dot_nt(l_scr[...], cb)
        k_t = big[:n_up]
        ss = jnp.sum((k_t * k_t).reshape(MLA_HEADS, MLA_NOPE, tk), axis=1)
        ss = ss + jnp.sum(kpe_t * kpe_t, axis=0, keepdims=True)
        rs = lax.rsqrt(ss * (1.0 / MLA_QK) + EPS)
        kg = kpe_t * gpe_ref[...]
        half = MLA_ROPE // 2
        x1, x2 = kg[:half], kg[half:]
        rot = jnp.concatenate([x1 * cos_t - x2 * sin_t, x1 * sin_t + x2 * cos_t], axis=0)
        pe = _dot(qpe_scr[:, :MLA_ROPE], rot.astype(BF16))
        rs_rows = jnp.concatenate([jnp.broadcast_to(rs[h:h + 1, :], (n_tok, tk)) for h in range(MLA_HEADS)], axis=0)
        return (big[n_up:] + pe) * rs_rows

    def update(s_list, cb_list):
        m_old = m_scr[...]
        m_new = m_old
        for s in s_list:
            m_new = jnp.maximum(m_new, jnp.max(s, axis=-1, keepdims=True))
        alpha = jnp.exp2(m_old - m_new)
        d = alpha * d_scr[...]
        acc = alpha * acc_scr[...]
        for s, cb in zip(s_list, cb_list):
            p = jnp.exp2(s - m_new)
            d = d + jnp.sum(p, axis=-1, keepdims=True)
            acc = acc + _dot(p.astype(BF16), cb)
        m_scr[...] = m_new
        d_scr[...] = d
        acc_scr[...] = acc

    s_list, cb_list = [], []
    for g in range(n_pages // sub):
        pages = range(g * sub, (g + 1) * sub)
        cb = jnp.concatenate([lat_refs[b][...] for b in pages], axis=0).astype(BF16)
        kpe_t = jnp.concatenate([kpe_refs[b][...] for b in pages], axis=1)
        cols = slice(g * sub * page, (g + 1) * sub * page)
        s_list.append(scores(cb, kpe_t, cos_ref[:, cols], sin_ref[:, cols]))
        cb_list.append(cb)
    update(s_list, cb_list)

    @pl.when(j == pl.num_programs(1) - 1)
    def _():
        cpad_scr[...] = jnp.zeros(cpad_scr.shape, F32)
        cpad_scr[:n_tok, :] = cn_ref[...]
        kpad_scr[...] = jnp.zeros(kpad_scr.shape, F32)
        kpad_scr[:n_tok, :MLA_ROPE] = kn_ref[...]
        rowi = lax.broadcasted_iota(jnp.int32, (nq, page), 0)
        coli = lax.broadcasted_iota(jnp.int32, (nq, page), 1)
        keep = coli <= (rowi & (n_tok - 1))
        cb = cpad_scr[...].astype(BF16)
        s = scores(cb, kpad_scr[...].T[:MLA_ROPE], cosn_ref[...], sinn_ref[...])
        update([jnp.where(keep, s, NEG_BIG)], [cb])
        ctx = (acc_scr[...] / d_scr[...]).astype(BF16)
        out = jnp.zeros(o_ref.shape, F32)
        for h in range(MLA_HEADS):
            out = out + _dot(ctx, wuv_ref[h])[h * n_tok:(h + 1) * n_tok]
        o_ref[...] = out

    @pl.when(step == last)
    def _():
        wait_pages(1 - slot)


def _attn_sample(page_table, cache_latent, cache_krope_t, tabs, qt, qf, ckv, kpe, w, n_tok, n_pages_step, sub):
    n_seq, n_pages = page_table.shape
    page = cache_latent.shape[2]
    n_steps = n_pages // n_pages_step
    tk = n_pages_step * page
    nq = MLA_HEADS * n_tok

    seq_row = lambda s, j, pt: (s, 0)
    const2 = lambda s, j, pt: (0, 0)
    half = MLA_ROPE // 2
    in_specs = [pl.BlockSpec(memory_space=pl.ANY), pl.BlockSpec(memory_space=pl.ANY)]
    in_specs += [pl.BlockSpec((half, tk), lambda s, j, pt: (0, j)), pl.BlockSpec((half, tk), lambda s, j, pt: (0, j)),
                 pl.BlockSpec((n_tok, MLA_HEADS * KV_LORA), seq_row),
                 pl.BlockSpec((n_tok, MLA_HEADS * HEAD_PAD), seq_row),
                 pl.BlockSpec((n_tok, KV_LORA), seq_row), pl.BlockSpec((n_tok, MLA_ROPE), seq_row),
                 pl.BlockSpec((half, page), const2), pl.BlockSpec((half, page), const2),
                 pl.BlockSpec(w['uk_t'].shape, const2),
                 pl.BlockSpec(w['uv_blk'].shape, lambda s, j, pt: (0, 0, 0)),
                 pl.BlockSpec((MLA_ROPE, 1), const2)]
    grid_spec = pltpu.PrefetchScalarGridSpec(
        num_scalar_prefetch=1, grid=(n_seq, n_steps), in_specs=in_specs,
        out_specs=pl.BlockSpec((n_tok, MLA_HEADS * MLA_V), seq_row),
        scratch_shapes=[pltpu.VMEM((MLA_HEADS * MLA_NOPE + nq, KV_LORA), BF16),
                        pltpu.VMEM((nq, HEAD_PAD), BF16),
                        pltpu.VMEM((nq, 1), F32), pltpu.VMEM((nq, 1), F32), pltpu.VMEM((nq, KV_LORA), F32),
                        pltpu.VMEM((page, LANES), F32), pltpu.VMEM((page, KV_LORA), F32),
                        pltpu.VMEM((2, n_pages_step, page, KV_LORA), F32),
                        pltpu.VMEM((2, n_pages_step, MLA_ROPE, page), F32),
                        pltpu.SemaphoreType.DMA((2, 2))])
    return pl.pallas_call(
        functools.partial(_attn_sample_kernel, n_pages=n_pages_step, sub=sub, n_tok=n_tok, page=page),
        grid_spec=grid_spec, out_shape=jax.ShapeDtypeStruct((n_seq * n_tok, MLA_HEADS * MLA_V), F32),
        compiler_params=_params(("arbitrary", "arbitrary")), name="attn_sample",
    )(page_table, cache_latent, cache_krope_t,
      tabs['cos_t'], tabs['sin_t'], qt, qf, ckv, kpe, tabs['cosn_t'], tabs['sinn_t'],
      w['uk_t'], w['uv_blk'], w['gpe'])


def _post_kernel(ra_ref, rb_ref, ma_ref, mb_ref, xa_ref, xb_ref, wo_ref, g_ref, wrh_ref, wrl_ref, br_ref,
                 h_ref, xn_ref, ti_ref, tg_ref, *, n_a):
    first = pl.program_id(0) < n_a
    r = jnp.where(first, ra_ref[...], rb_ref[...])
    m = jnp.where(first, ma_ref[...], mb_ref[...].astype(BF16))
    half = wo_ref.shape[0] // 2
    mix = _dot(r, wo_ref[:half, :]) + _dot(m, wo_ref[half:, :])
    h = jnp.where(first, xa_ref[...], xb_ref[...]) + mix
    h_ref[...] = h
    xn = _rms(h, g_ref[...])
    xn_ref[...] = xn
    hi = xn.astype(BF16)
    lo = (xn - hi.astype(F32)).astype(BF16)
    wrh = wrh_ref[...]
    work = _dot(hi, wrh) + _dot(hi, wrl_ref[...]) + _dot(lo, wrh) + br_ref[...]
    lane = lax.broadcasted_iota(jnp.int32, work.shape, 1).astype(F32)
    idx = jnp.zeros(work.shape, F32)
    val = jnp.zeros(work.shape, F32)
    vmax = None
    denom = None
    for k in range(TOP_K):
        vk = jnp.max(work, axis=-1, keepdims=True)
        ik = jnp.min(jnp.where(work == vk, lane, float(LANES)), axis=-1, keepdims=True)
        work = jnp.where(lane == ik, -jnp.inf, work)
        if k == 0:
            vmax = vk
        ek = jnp.exp(vk - vmax)
        denom = ek if k == 0 else denom + ek
        idx = jnp.where(lane == float(k), ik, idx)
        val = jnp.where(lane == float(k), ek, val)
    ti_ref[...] = idx.astype(jnp.int32)
    tg_ref[...] = val / denom


def _post(r_pair, m_pair, x_pair, w, tm):
    n_a, n_b = x_pair[0].shape[0] // tm, x_pair[1].shape[0] // tm
    rows = (n_a + n_b) * tm
    row = lambda i: (i, 0)
    const = lambda i: (0, 0)
    in_specs = []
    for a, b in (r_pair, m_pair, x_pair):
        in_specs += [pl.BlockSpec((tm, a.shape[1]), lambda i: (jnp.minimum(i, n_a - 1), 0)),
                     pl.BlockSpec((tm, b.shape[1]), lambda i: (jnp.maximum(i - n_a, 0), 0))]
    in_specs += [pl.BlockSpec(w[n].shape, const) for n in ('wo', 'gffn', 'wr_hi', 'wr_lo', 'br')]
    out_shape = [jax.ShapeDtypeStruct((rows, D_MODEL), F32), jax.ShapeDtypeStruct((rows, D_MODEL), F32),
                 jax.ShapeDtypeStruct((rows, LANES), jnp.int32), jax.ShapeDtypeStruct((rows, LANES), F32)]
    out_specs = [pl.BlockSpec((tm, D_MODEL), row), pl.BlockSpec((tm, D_MODEL), row),
                 pl.BlockSpec((tm, LANES), row), pl.BlockSpec((tm, LANES), row)]
    return pl.pallas_call(
        functools.partial(_post_kernel, n_a=n_a), grid=(rows // tm,), in_specs=in_specs, out_specs=out_specs,
        out_shape=out_shape, compiler_params=_params(("arbitrary",)), name="post",
    )(*r_pair, *m_pair, *x_pair, w['wo'], w['gffn'], w['wr_hi'], w['wr_lo'], w['br'])


def _moe_kernel(be_ref, nu_ref, xs_ref, wgu_ref, bgu_ref, wd_ref, bd_ref, o_ref, wgu_b, wd_b):
    i = pl.program_id(0)
    e = be_ref[i]
    e_prev = be_ref[jnp.maximum(i - 1, 0)]

    @pl.when((i == 0) | (e != e_prev))
    def _():
        wgu_b[...] = wgu_ref[...].astype(BF16)
        wd_b[...] = wd_ref[...].astype(BF16)

    @pl.when(i < nu_ref[0])
    def _():
        hgu = _dot(xs_ref[...].astype(BF16), wgu_b[...]) + bgu_ref[...]
        g = jnp.minimum(hgu[:, :D_FF], SWIGLU_LIMIT)
        u = jnp.clip(hgu[:, D_FF:], -SWIGLU_LIMIT, SWIGLU_LIMIT)
        hid = (u + 1.0) * (g * jax.nn.sigmoid(SWIGLU_ALPHA * g))
        o_ref[...] = _dot(hid.astype(BF16), wd_b[...]) + bd_ref[...]

    @pl.when(i >= nu_ref[0])
    def _():
        o_ref[...] = jnp.zeros(o_ref.shape, F32)


def _moe_blocks(block_e, n_used, xs, w_gate_up, b_gate_up, w_down, b_down, blk):
    n_blocks = block_e.shape[0]
    row = lambda i, be, nu: (i, 0)
    in_specs = [pl.BlockSpec((blk, D_MODEL), row),
                pl.BlockSpec((None, D_MODEL, 2 * D_FF), lambda i, be, nu: (be[i], 0, 0)),
                pl.BlockSpec((None, 1, 2 * D_FF), lambda i, be, nu: (be[i], 0, 0)),
                pl.BlockSpec((None, D_FF, D_MODEL), lambda i, be, nu: (be[i], 0, 0)),
                pl.BlockSpec((None, 1, D_MODEL), lambda i, be, nu: (be[i], 0, 0))]
    grid_spec = pltpu.PrefetchScalarGridSpec(
        num_scalar_prefetch=2, grid=(n_blocks,), in_specs=in_specs,
        out_specs=pl.BlockSpec((blk, D_MODEL), row),
        scratch_shapes=[pltpu.VMEM((D_MODEL, 2 * D_FF), BF16), pltpu.VMEM((D_FF, D_MODEL), BF16)])
    return pl.pallas_call(
        _moe_kernel, grid_spec=grid_spec, out_shape=jax.ShapeDtypeStruct((n_blocks * blk, D_MODEL), F32),
        compiler_params=_params(("arbitrary",)), name="moe",
    )(block_e, n_used, xs, w_gate_up, b_gate_up[:, None, :], w_down, b_down[:, None, :])


def _combine_kernel(h_ref, g_ref, y_ref, o_ref):
    g = g_ref[...]
    acc = y_ref[0] * g[:, 0:1]
    for k in range(1, TOP_K):
        acc = acc + y_ref[k] * g[:, k:k + 1]
    o_ref[...] = h_ref[...] + acc


def _combine(h, gates, y4, block0, rows, tm):
    row = lambda i: (block0 + i, 0)
    return pl.pallas_call(
        _combine_kernel, grid=(rows // tm,),
        in_specs=[pl.BlockSpec((tm, D_MODEL), row), pl.BlockSpec((tm, LANES), row),
                  pl.BlockSpec((TOP_K, tm, D_MODEL), lambda i: (0, block0 + i, 0))],
        out_specs=pl.BlockSpec((tm, D_MODEL), lambda i: (i, 0)),
        out_shape=jax.ShapeDtypeStruct((rows, D_MODEL), F32),
        compiler_params=_params(("arbitrary",)), name="combine",
    )(h, gates, y4)


def _gather_rows(table, idx):
    n, d = idx.shape[0], table.shape[1]
    n_win = n // GATHER_WINDOW
    assert n % (GATHER_WINDOW * SC_CORES * SC_SUBCORES) == 0
    idx = jnp.zeros((n_win, LANES), jnp.int32).at[:, :GATHER_WINDOW].set(idx.reshape(n_win, GATHER_WINDOW))
    mesh = plsc.VectorSubcoreMesh(core_axis_name="core", subcore_axis_name="subcore",
                                  num_cores=SC_CORES, num_subcores=SC_SUBCORES)

    @functools.partial(pl.kernel, out_type=jax.ShapeDtypeStruct((n, d), table.dtype), mesh=mesh)
    def gather(x_hbm, i_hbm, o_hbm):
        def body(i_vmem, o_vmem):
            pltpu.sync_copy(x_hbm.at[i_vmem.at[0, pl.ds(0, GATHER_WINDOW)]], o_vmem)

        pltpu.emit_pipeline(
            body, grid=(n_win,),
            in_specs=[pl.BlockSpec((1, LANES), index_map=lambda i: (i, 0))],
            out_specs=[pl.BlockSpec((GATHER_WINDOW, d), index_map=lambda i: (i, 0))],
            core_axis_name=("core", "subcore"), dimension_semantics=(pltpu.PARALLEL,),
        )(i_hbm, o_hbm)

    return gather(table, idx)


def _route(top_i, blk):
    t = top_i.shape[0]
    a = t * TOP_K
    flat_e = top_i.reshape(a)
    assert N_EXPERTS * a < 2 ** 31
    order = jnp.sort(flat_e.astype(jnp.int32) * a + jnp.arange(a, dtype=jnp.int32)) % a
    tok_sorted = (order // TOP_K).astype(jnp.int32)
    experts = jnp.arange(N_EXPERTS, dtype=jnp.int32)
    hit = (top_i[:, :, None] == experts[None, None, :]).astype(jnp.int32)
    per_tok = hit.sum(axis=1)
    before = jnp.cumsum(per_tok, axis=0) - per_tok
    counts = per_tok.sum(axis=0)
    padded = (counts + blk - 1) // blk * blk
    start = jnp.cumsum(counts) - counts
    pend = jnp.cumsum(padded)
    pstart = pend - padded
    dest = ((before + pstart[None, :])[:, None, :] * hit).sum(axis=2)
    n_blocks = -(-a // blk) + N_EXPERTS
    block_e = jnp.minimum((pend[None, :] <= (jnp.arange(n_blocks, dtype=jnp.int32) * blk)[:, None]).sum(axis=1),
                          N_EXPERTS - 1).astype(jnp.int32)
    slot = jnp.arange(n_blocks * blk, dtype=jnp.int32)
    slot_e = jnp.repeat(block_e, blk)
    r = slot - pstart[slot_e]
    slot_tok = jnp.where(r < counts[slot_e], tok_sorted[jnp.minimum(start[slot_e] + r, a - 1)], slot % t)
    n_used = (pend[-1:] // blk).astype(jnp.int32)
    return slot_tok, dest.astype(jnp.int32), block_e, n_used


def _rope_tables(pos):
    pos = pos.astype(F32)[:, None]
    n = pos.shape[0]
    hr = RET_DK // 2
    ang = pos * (RET_THETA ** (-jnp.arange(hr, dtype=F32) / hr))[None, :]
    cos, sin = jnp.cos(ang), jnp.sin(ang)
    zr = jnp.zeros((n, LANES - RET_DK), F32)
    cr = jnp.concatenate([cos, cos, zr], axis=1)
    sr = jnp.concatenate([-sin, sin, zr], axis=1)
    hm = MLA_ROPE // 2
    ang = pos * (MLA_THETA ** (-jnp.arange(hm, dtype=F32) / hm))[None, :]
    cos, sin = jnp.cos(ang), jnp.sin(ang)
    cm = jnp.concatenate([cos, cos, jnp.ones((n, LANES - MLA_ROPE), F32)], axis=1)
    sma = jnp.concatenate([jnp.zeros((n, hm), F32), sin, jnp.zeros((n, LANES - MLA_ROPE), F32)], axis=1)
    smb = jnp.concatenate([-sin, jnp.zeros((n, LANES - hm), F32)], axis=1)
    return [cr, sr, cm, sma, smb]


def _rope_tables_t(pos):
    hm = MLA_ROPE // 2
    ang = pos.astype(F32)[:, None] * (MLA_THETA ** (-jnp.arange(hm, dtype=F32) / hm))[None, :]
    return jnp.cos(ang).T, jnp.sin(ang).T


def _decay_tables(log_gamma, c, n_rep):
    idx = jnp.arange(c, dtype=F32)
    diff = idx[:, None] - idx[None, :]
    decay = jnp.where(diff >= 0, jnp.exp(jnp.maximum(diff, 0.0)[None] * log_gamma[:, None, None]), 0.0)
    if n_rep > 1:
        eye = jnp.eye(n_rep, dtype=F32)
        decay = (eye[None, :, None, :, None] * decay[:, None, :, None, :]).reshape(RET_HEADS, n_rep * c, n_rep * c)
    rowd = jnp.tile(jnp.exp((idx + 1.0)[:, None] * log_gamma[None, :]), (n_rep, 1))
    cold = jnp.tile(jnp.exp((c - 1.0 - idx)[:, None] * log_gamma[None, :]), (n_rep, 1))
    cpow = jnp.broadcast_to(jnp.exp(c * log_gamma)[None, :], (LANES, RET_HEADS))
    return {'dec': decay, 'rowd': rowd, 'cold': cold, 'cpow': cpow}


def _pad_heads(wm, n_heads, width, offset):
    k = wm.shape[0]
    wm = wm.reshape(k, n_heads, width)
    out = jnp.zeros((k, n_heads, LANES), wm.dtype).at[:, :, offset:offset + width].set(wm)
    return out.reshape(k, n_heads * LANES)


def _prep_weights(norm_attn_g, w_in, ret_out_g, q_a_norm_g, w_q_b, kv_a_norm_g, w_uk, w_uv,
                  qk_norm_q_g, qk_norm_k_g, w_out, norm_ffn_g, w_router, b_router):
    splits = [RET_HEADS * RET_DK, RET_HEADS * RET_DK, RET_HEADS * RET_DV, RET_HEADS * RET_DV, Q_LORA, KV_LORA,
              MLA_ROPE]
    offs = [0]
    for s in splits:
        offs.append(offs[-1] + s)
    part = [w_in[:, offs[i]:offs[i + 1]] for i in range(len(splits))]
    kpe_cols = jnp.zeros((D_MODEL, LANES), F32).at[:, :MLA_ROPE].set(part[6])
    win = jnp.concatenate([_pad_heads(part[0], RET_HEADS, RET_DK, 0), _pad_heads(part[1], RET_HEADS, RET_DK, 0),
                           part[2], part[3], part[4], part[5], kpe_cols], axis=1).astype(BF16)

    def mla_cols(wm):
        k = wm.shape[0]
        wm = wm.reshape(k, MLA_HEADS, MLA_QK)
        wm = jnp.concatenate([wm[:, :, MLA_NOPE:], wm[:, :, :MLA_NOPE]], axis=2)
        return _pad_heads(wm.reshape(k, MLA_HEADS * MLA_QK), MLA_HEADS, MLA_QK, 0)

    def mla_gain(g):
        g = jnp.concatenate([g[MLA_NOPE:], g[:MLA_NOPE], jnp.zeros((LANES - MLA_QK,), F32)])
        return g[None, :]

    gk_nope = qk_norm_k_g[:MLA_NOPE]
    uk_heads = w_uk.reshape(KV_LORA, MLA_HEADS, MLA_NOPE)
    wabs = jnp.zeros((MLA_HEADS, HEAD_PAD, KV_LORA), F32).at[:, MLA_ROPE:MLA_ROPE + MLA_NOPE, :].set(
        (uk_heads * gk_nope[None, None, :]).transpose(1, 2, 0))
    head_of_col = jnp.arange(MLA_HEADS * MLA_V) // MLA_V
    uv_blk = jnp.where(head_of_col[None, None, :] == jnp.arange(MLA_HEADS)[:, None, None], w_uv[None], 0.0)
    wr = jnp.zeros((D_MODEL, LANES), F32).at[:, :N_EXPERTS].set(w_router)
    wr_hi = wr.astype(BF16)
    return {
        'gattn': norm_attn_g[None, :], 'win': win, 'gqa': q_a_norm_g[None, :], 'wqb': mla_cols(w_q_b).astype(BF16),
        'gq': mla_gain(qk_norm_q_g), 'gkv': kv_a_norm_g[None, :],
        'wuk': _pad_heads(w_uk, MLA_HEADS, MLA_NOPE, MLA_ROPE).astype(BF16), 'gk': mla_gain(qk_norm_k_g),
        'wuv': w_uv.astype(BF16), 'gret': ret_out_g[None, :],
        'wabs': wabs.astype(BF16), 'uk_t': w_uk.T.astype(BF16), 'uv_blk': uv_blk.astype(BF16),
        'gpe': qk_norm_k_g[MLA_NOPE:, None],
        'wo': w_out.astype(BF16), 'gffn': norm_ffn_g[None, :], 'wr_hi': wr_hi,
        'wr_lo': (wr - wr_hi.astype(F32)).astype(BF16),
        'br': jnp.full((1, LANES), NEG_BIG, F32).at[0, :N_EXPERTS].set(b_router),
    }


def _pad_rows(a, rows):
    return jnp.zeros((rows,) + a.shape[1:], a.dtype).at[:a.shape[0]].set(a)


def _largest_divisor(n, cap):
    d = min(n, cap)
    while n % d:
        d -= 1
    return d


def kernel(x_prompt, x_sample, cache_latent, cache_krope, state_retention, page_table, meta_tokens, norm_attn_g, w_in, ret_out_g, q_a_norm_g, w_q_b, kv_a_norm_g, w_uk, w_uv, qk_norm_q_g, qk_norm_k_g, w_out, norm_ffn_g, w_router, b_router, w_gate_up, b_gate_up, w_down, b_down):
    assert w_in.shape[0] == 1, "single-layer trunk"
    batch, seq, _ = x_prompt.shape
    n_seq, n_tok, _ = x_sample.shape
    n_pages, page = page_table.shape[1], cache_latent.shape[2]
    past = n_pages * page
    assert seq % RET_CHUNK == 0 and LANES % n_tok == 0 and (n_seq * n_tok) % LANES == 0 and page == LANES
    w = _prep_weights(norm_attn_g[0], w_in[0], ret_out_g[0], q_a_norm_g[0], w_q_b[0], kv_a_norm_g[0], w_uk[0],
                      w_uv[0], qk_norm_q_g[0], qk_norm_k_g[0], w_out[0], norm_ffn_g[0], w_router[0], b_router[0])
    log_gamma = jnp.log1p(-jnp.exp2(-5.0 - jnp.arange(RET_HEADS, dtype=F32)))

    rows_p = batch * seq
    rows_s = n_seq * n_tok
    tm = _largest_divisor(min(seq, rows_s), ROW_BLOCK)
    nb_seq = seq // tm
    x_pair = (x_prompt.reshape(rows_p, D_MODEL), x_sample.reshape(rows_s, D_MODEL))
    pos_rows = jnp.concatenate([N_META + jnp.arange(seq), jnp.tile(past + jnp.arange(n_tok), tm // n_tok)])
    tabs = _rope_tables(pos_rows)
    n_pb = rows_p // tm
    rq, rk, rv, zg, mq, ckv, kpe, k, v = _front(
        *x_pair, tabs, lambda i: jnp.where(i < n_pb, i % nb_seq, nb_seq), w, tm)
    _, mrk, mrv, _, _, mckv, mkpe, mk, mv = _front(
        meta_tokens, None, _rope_tables(jnp.arange(N_META)), lambda i: i, w, N_META)

    dt_p = _decay_tables(log_gamma, RET_CHUNK, 1)
    dt_p['mcol'] = _pad_rows(jnp.exp((N_META - 1.0 - jnp.arange(N_META, dtype=F32))[:, None] * log_gamma[None, :]),
                             RET_CHUNK)
    gret = w['gret']
    r_p, st_p = _ret_prompt(rq, rk, rv, zg, _pad_rows(mrk, RET_CHUNK), _pad_rows(mrv, RET_CHUNK), dt_p, gret,
                            batch, seq // RET_CHUNK)
    dt_s = _decay_tables(log_gamma, n_tok, LANES // n_tok)
    r_s, st_s = _ret_sample(rq[rows_p:], rk[rows_p:], rv[rows_p:], zg[rows_p:], state_retention[0], dt_s, gret, n_tok)

    blk = _largest_divisor(seq, ATTN_BLOCK)
    m_p = _attn_prompt(mq, k, v, _pad_rows(mk, LANES), _pad_rows(mv, LANES), batch, seq, blk)
    tm_s = _largest_divisor(rows_s, ROW_BLOCK)
    qt, qf = _absorb(mq, w['wabs'], rows_p // tm_s, rows_s, tm_s)
    n_pages_step = _largest_divisor(n_pages, PAGES_PER_STEP)
    cos_t, sin_t = _rope_tables_t(jnp.arange(past))
    cosn_t, sinn_t = _rope_tables_t(past + jnp.arange(page))
    tabs_s = {'cos_t': cos_t, 'sin_t': sin_t, 'cosn_t': cosn_t, 'sinn_t': sinn_t}
    m_s = _attn_sample(page_table, cache_latent, jnp.swapaxes(cache_krope, 2, 3), tabs_s, qt, qf, ckv[rows_p:],
                       kpe[rows_p:], w, n_tok, n_pages_step, _largest_divisor(n_pages_step, PAGES_PER_CHAIN))

    h1, xn2, top_i, gates = _post((r_p, r_s), (m_p, m_s), x_pair, w, tm)

    slot_tok, dest, block_e, n_used = _route(top_i[:, :TOP_K], EXPERT_BLOCK)
    outs = _moe_blocks(block_e, n_used, _gather_rows(xn2, slot_tok), w_gate_up[0], b_gate_up[0], w_down[0],
                       b_down[0], EXPERT_BLOCK)
    y4 = _gather_rows(outs, dest.T.reshape(-1)).reshape(TOP_K, rows_p + rows_s, D_MODEL)
    y_prompt = _combine(h1, gates, y4, 0, rows_p, tm).reshape(batch, seq, D_MODEL)
    y_sample = _combine(h1, gates, y4, n_pb, rows_s, tm).reshape(n_seq, n_tok, D_MODEL)
    lat_p = jnp.concatenate([jnp.broadcast_to(mckv[None], (batch, N_META, KV_LORA)),
                             ckv[:rows_p].reshape(batch, seq, KV_LORA)], axis=1)[None]
    kpe_p = jnp.concatenate([jnp.broadcast_to(mkpe[None], (batch, N_META, MLA_ROPE)),
                             kpe[:rows_p].reshape(batch, seq, MLA_ROPE)], axis=1)[None]
    return (y_prompt, y_sample, lat_p, kpe_p, st_p[None],
            ckv[rows_p:].reshape(n_seq, n_tok, KV_LORA)[None], kpe[rows_p:].reshape(n_seq, n_tok, MLA_ROPE)[None],
            st_s[None])
```

```python
import functools

import jax
import jax.numpy as jnp
from jax import lax
from jax.experimental import pallas as pl
from jax.experimental.pallas import tpu as pltpu
from jax.experimental.pallas import tpu_sc as plsc

F32 = jnp.float32
BF16 = jnp.bfloat16

D_MODEL = 1024
N_META = 16
RET_HEADS = 4
RET_DK = 64
RET_DV = 128
RET_CHUNK = 128
RET_THETA = 10000.0
MLA_HEADS = 8
MLA_NOPE = 64
MLA_ROPE = 32
MLA_QK = MLA_NOPE + MLA_ROPE
MLA_V = 64
Q_LORA = 384
KV_LORA = 256
MLA_THETA = 10000.0
MLA_SCALE = MLA_QK ** -0.5
LOG2E = 1.4426950408889634
N_EXPERTS = 32
TOP_K = 4
D_FF = 1024
SWIGLU_LIMIT = 7.0
SWIGLU_ALPHA = 1.702
EPS = 1e-6
NEG_BIG = -1e30

LANES = 128
HEAD_PAD = LANES
VMEM_LIMIT = 56 * 1024 * 1024
ROW_BLOCK = 512
ATTN_BLOCK = 1024
ATTN_HEADS_PER_STEP = 4
PAGES_PER_STEP = 64
PAGES_PER_CHAIN = 32
EXPERT_BLOCK = 512
SC_CORES, SC_SUBCORES = 2, 16
GATHER_WINDOW = 32

_OFF_RQ = 0
_OFF_RK = _OFF_RQ + RET_HEADS * LANES
_OFF_RV = _OFF_RK + RET_HEADS * LANES
_OFF_ZG = _OFF_RV + RET_HEADS * RET_DV
_OFF_CQ = _OFF_ZG + RET_HEADS * RET_DV
_OFF_CKV = _OFF_CQ + Q_LORA
_OFF_KPE = _OFF_CKV + KV_LORA
IN_PAD = _OFF_KPE + LANES


def _params(sem):
    return pltpu.CompilerParams(dimension_semantics=sem, vmem_limit_bytes=VMEM_LIMIT)


def _rms(x, g):
    return x * lax.rsqrt(jnp.mean(x * x, axis=-1, keepdims=True) + EPS) * g


def _dot(a, b):
    return jnp.dot(a, b, preferred_element_type=F32)


def _dot_nt(a, b):
    return lax.dot_general(a, b, (((1,), (1,)), ((), ())), preferred_element_type=F32)


def _dot_tn(a, b):
    return lax.dot_general(a, b, (((0,), (0,)), ((), ())), preferred_element_type=F32)


def _front_kernel(xa_ref, xb_ref, gattn_ref, win_ref, cr_ref, sr_ref, cm_ref, sma_ref, smb_ref,
                  gqa_ref, wqb_ref, gq_ref, gkv_ref, wuk_ref, gk_ref, wuv_ref,
                  rq_ref, rk_ref, rv_ref, zg_ref, mq_ref, ckv_ref, kpe_ref, k_ref, v_ref, *, n_a):
    tm = xa_ref.shape[0]
    x = jnp.where(pl.program_id(0) < n_a, xa_ref[...], xb_ref[...])
    xn = _rms(x, gattn_ref[...])
    z = _dot(xn.astype(BF16), win_ref[...])

    lane = lax.broadcasted_iota(jnp.int32, (tm, LANES), 1)
    upper = (lane & (RET_DK // 2)) != 0
    cr, sr = cr_ref[...], sr_ref[...]

    def rope_ret(t):
        partner = jnp.where(upper, pltpu.roll(t, RET_DK // 2, 1), pltpu.roll(t, LANES - RET_DK // 2, 1))
        return t * cr + partner * sr

    for h in range(RET_HEADS):
        sl = slice(h * LANES, (h + 1) * LANES)
        rq_ref[:, sl] = rope_ret(z[:, _OFF_RQ + h * LANES:_OFF_RQ + (h + 1) * LANES])
        rk_ref[:, sl] = rope_ret(z[:, _OFF_RK + h * LANES:_OFF_RK + (h + 1) * LANES]) * (RET_DK ** -0.5)
    rv_ref[...] = z[:, _OFF_RV:_OFF_ZG].astype(BF16)
    zg_ref[...] = z[:, _OFF_ZG:_OFF_CQ]

    cm, sma, smb = cm_ref[...], sma_ref[...], smb_ref[...]

    def rope_mla(t):
        half = MLA_ROPE // 2
        return t * cm + pltpu.roll(t, half, 1) * sma + pltpu.roll(t, LANES - half, 1) * smb

    def head_norm(t, g):
        ms = jnp.sum(t * t, axis=-1, keepdims=True) * (1.0 / MLA_QK)
        return t * lax.rsqrt(ms + EPS) * g

    cq = _rms(z[:, _OFF_CQ:_OFF_CKV], gqa_ref[...])
    q = _dot(cq.astype(BF16), wqb_ref[...])
    gq = gq_ref[...]
    for h in range(MLA_HEADS):
        sl = slice(h * HEAD_PAD, (h + 1) * HEAD_PAD)
        mq_ref[:, sl] = (rope_mla(head_norm(q[:, sl], gq)) * (MLA_SCALE * LOG2E)).astype(BF16)

    ckv = _rms(z[:, _OFF_CKV:_OFF_KPE], gkv_ref[...])
    ckv_ref[...] = ckv
    kpe_slab = z[:, _OFF_KPE:IN_PAD]
    kpe_ref[...] = kpe_slab[:, :MLA_ROPE]
    ckv_b = ckv.astype(BF16)
    kn = _dot(ckv_b, wuk_ref[...])
    gk = gk_ref[...]
    for h in range(MLA_HEADS):
        sl = slice(h * HEAD_PAD, (h + 1) * HEAD_PAD)
        k_ref[:, sl] = rope_mla(head_norm(kn[:, sl] + kpe_slab, gk)).astype(BF16)
    v_ref[...] = _dot(ckv_b, wuv_ref[...]).astype(BF16)


def _front(xa, xb, tabs, tab_index, w, tm):
    n_a, n_b = xa.shape[0] // tm, (0 if xb is None else xb.shape[0] // tm)
    xb = xa if xb is None else xb
    rows = (n_a + n_b) * tm
    grid = (n_a + n_b,)
    row = lambda i: (i, 0)
    const = lambda i: (0, 0)
    tab = lambda i: (tab_index(i), 0)

    def full(a):
        return pl.BlockSpec(a.shape, const)

    in_specs = [pl.BlockSpec((tm, D_MODEL), lambda i: (jnp.minimum(i, n_a - 1), 0)),
                pl.BlockSpec((tm, D_MODEL), lambda i: (jnp.maximum(i - n_a, 0), 0)), full(w['gattn']), full(w['win'])]
    in_specs += [pl.BlockSpec((tm, LANES), tab)] * 5
    in_specs += [full(w[n]) for n in ('gqa', 'wqb', 'gq', 'gkv', 'wuk', 'gk', 'wuv')]
    widths = [(RET_HEADS * LANES, F32), (RET_HEADS * LANES, F32), (RET_HEADS * RET_DV, BF16),
              (RET_HEADS * RET_DV, F32), (MLA_HEADS * HEAD_PAD, BF16), (KV_LORA, F32), (MLA_ROPE, F32),
              (MLA_HEADS * HEAD_PAD, BF16), (MLA_HEADS * MLA_V, BF16)]
    out_shape = [jax.ShapeDtypeStruct((rows, n), dt) for n, dt in widths]
    out_specs = [pl.BlockSpec((tm, n), row) for n, _ in widths]
    return pl.pallas_call(
        functools.partial(_front_kernel, n_a=n_a), grid=grid, in_specs=in_specs, out_specs=out_specs,
        out_shape=out_shape, compiler_params=_params(("arbitrary",)), name="front",
    )(xa, xb, w['gattn'], w['win'], *tabs, w['gqa'], w['wqb'], w['gq'], w['gkv'], w['wuk'], w['gk'], w['wuv'])


def _ret_gate(o, zg, g):
    on = o * lax.rsqrt(jnp.mean(o * o, axis=-1, keepdims=True) + EPS) * g
    return (zg * jax.nn.sigmoid(zg)) * on


def _ret_prompt_kernel(q_ref, k_ref, v_ref, zg_ref, mk_ref, mv_ref, mcol_ref, dec_ref, rowd_ref, cold_ref,
                       cpow_ref, g_ref, r_ref, s_ref, s_scr):
    c = pl.program_id(1)

    @pl.when(c == 0)
    def _():
        for h in range(RET_HEADS):
            sl = slice(h * LANES, (h + 1) * LANES)
            kw = mk_ref[:, sl] * mcol_ref[:, h:h + 1]
            s_scr[h] = _dot_tn(kw.astype(BF16), mv_ref[:, sl])

    for h in range(RET_HEADS):
        sl = slice(h * LANES, (h + 1) * LANES)
        q = q_ref[:, sl]
        k = k_ref[:, sl]
        v = v_ref[:, sl]
        qb = q.astype(BF16)
        s0 = s_scr[h]
        scores = _dot_nt(qb, k.astype(BF16)) * dec_ref[h]
        inner = _dot(scores.astype(BF16), v)
        cross = _dot(qb, s0.astype(BF16)) * rowd_ref[:, h:h + 1]
        kw = k * cold_ref[:, h:h + 1]
        s_new = s0 * cpow_ref[:, h:h + 1] + _dot_tn(kw.astype(BF16), v)
        s_scr[h] = s_new
        r_ref[:, sl] = _ret_gate(inner + cross, zg_ref[:, sl], g_ref[:, sl]).astype(BF16)

    @pl.when(c == pl.num_programs(1) - 1)
    def _():
        for h in range(RET_HEADS):
            s_ref[0, h] = s_scr[h, :RET_DK, :]


def _ret_prompt(rq, rk, rv, zg, mk, mv, tabs, g, batch, n_chunks):
    cs = RET_CHUNK
    row = lambda b, c: (b * n_chunks + c, 0)
    const2 = lambda b, c: (0, 0)
    w4 = RET_HEADS * LANES
    in_specs = [pl.BlockSpec((cs, w4), row), pl.BlockSpec((cs, w4), row), pl.BlockSpec((cs, w4), row),
                pl.BlockSpec((cs, w4), row),
                pl.BlockSpec((cs, w4), const2), pl.BlockSpec((cs, w4), const2),
                pl.BlockSpec((cs, RET_HEADS), const2),
                pl.BlockSpec((RET_HEADS, cs, cs), lambda b, c: (0, 0, 0)),
                pl.BlockSpec((cs, RET_HEADS), const2), pl.BlockSpec((cs, RET_HEADS), const2),
                pl.BlockSpec((LANES, RET_HEADS), const2), pl.BlockSpec((1, w4), const2)]
    out_shape = [jax.ShapeDtypeStruct((batch * n_chunks * cs, w4), BF16),
                 jax.ShapeDtypeStruct((batch, RET_HEADS, RET_DK, RET_DV), F32)]
    out_specs = [pl.BlockSpec((cs, w4), row),
                 pl.BlockSpec((1, RET_HEADS, RET_DK, RET_DV), lambda b, c: (b, 0, 0, 0))]
    return pl.pallas_call(
        _ret_prompt_kernel, grid=(batch, n_chunks), in_specs=in_specs, out_specs=out_specs, out_shape=out_shape,
        scratch_shapes=[pltpu.VMEM((RET_HEADS, LANES, RET_DV), F32)],
        compiler_params=_params(("arbitrary", "arbitrary")), name="ret_prompt",
    )(rq, rk, rv, zg, mk, mv, tabs['mcol'], tabs['dec'], tabs['rowd'], tabs['cold'], tabs['cpow'], g)


def _ret_sample_kernel(q_ref, k_ref, v_ref, zg_ref, s0_ref, dec_ref, rowd_ref, cold_ref, cpow_ref, g_ref,
                       r_ref, s_ref, *, n_seq, n_tok):
    rows = n_seq * n_tok
    ri = lax.broadcasted_iota(jnp.int32, (rows, 1), 0)
    for h in range(RET_HEADS):
        sl = slice(h * LANES, (h + 1) * LANES)
        q = q_ref[:, sl]
        k = k_ref[:, sl]
        v = v_ref[:, sl]
        qb = q.astype(BF16)
        scores = _dot_nt(qb, k.astype(BF16)) * dec_ref[h]
        inner = _dot(scores.astype(BF16), v)
        kw = k * cold_ref[:, h:h + 1]
        cross = jnp.zeros((rows, RET_DV), F32)
        for s in range(n_seq):
            s0 = s0_ref[s, h]
            mine = (ri >= s * n_tok) & (ri < (s + 1) * n_tok)
            cross = cross + jnp.where(mine, _dot(qb[:, :RET_DK], s0.astype(BF16)), 0.0)
            upd = _dot_tn(jnp.where(mine, kw, 0.0).astype(BF16), v)
            s_ref[s, h] = s0 * cpow_ref[:RET_DK, h:h + 1] + upd[:RET_DK]
        cross = cross * rowd_ref[:, h:h + 1]
        r_ref[:, sl] = _ret_gate(inner + cross, zg_ref[:, sl], g_ref[:, sl]).astype(BF16)


def _ret_sample(rq, rk, rv, zg, state, tabs, g, n_tok):
    n_seq_total = state.shape[0]
    n_seq = LANES // n_tok
    rows = n_seq * n_tok
    w4 = RET_HEADS * LANES
    row = lambda i: (i, 0)
    const = lambda i: (0, 0)
    in_specs = [pl.BlockSpec((rows, w4), row)] * 4
    in_specs += [pl.BlockSpec((n_seq, RET_HEADS, RET_DK, RET_DV), lambda i: (i, 0, 0, 0)),
                 pl.BlockSpec((RET_HEADS, rows, rows), lambda i: (0, 0, 0)),
                 pl.BlockSpec((rows, RET_HEADS), const), pl.BlockSpec((rows, RET_HEADS), const),
                 pl.BlockSpec((LANES, RET_HEADS), const), pl.BlockSpec((1, w4), const)]
    out_shape = [jax.ShapeDtypeStruct((n_seq_total * n_tok, w4), BF16),
                 jax.ShapeDtypeStruct(state.shape, F32)]
    out_specs = [pl.BlockSpec((rows, w4), row),
                 pl.BlockSpec((n_seq, RET_HEADS, RET_DK, RET_DV), lambda i: (i, 0, 0, 0))]
    return pl.pallas_call(
        functools.partial(_ret_sample_kernel, n_seq=n_seq, n_tok=n_tok),
        grid=(n_seq_total // n_seq,), in_specs=in_specs, out_specs=out_specs, out_shape=out_shape,
        compiler_params=_params(("arbitrary",)), name="ret_sample",
    )(rq, rk, rv, zg, state, tabs['dec'], tabs['rowd'], tabs['cold'], tabs['cpow'], g)


def _attn_prompt_kernel(qi_ref, ki_ref, q_ref, k_ref, v_ref, km_ref, vm_ref, o_ref, m_scr, l_scr, acc_scr, *,
                        sub_tiles):
    qi = qi_ref[pl.program_id(2)]
    ki = ki_ref[pl.program_id(2)]
    tm, tk = q_ref.shape[0], k_ref.shape[0]
    heads = q_ref.shape[1] // HEAD_PAD

    def pair_lanes(hh):
        return slice((hh // 2) * 2 * MLA_V, (hh // 2 + 1) * 2 * MLA_V)

    @pl.when(ki == 0)
    def _():
        lane = lax.broadcasted_iota(jnp.int32, (tm, km_ref.shape[0]), 1)
        for hh in range(heads):
            sl = slice(hh * HEAD_PAD, (hh + 1) * HEAD_PAD)
            s = jnp.where(lane < N_META, _dot_nt(q_ref[:, sl], km_ref[:, sl]), NEG_BIG)
            m = jnp.max(s, axis=-1, keepdims=True)
            p = jnp.exp2(s - m)
            m_scr[hh] = jnp.broadcast_to(m, (tm, LANES))
            l_scr[hh] = jnp.broadcast_to(jnp.sum(p, axis=-1, keepdims=True), (tm, LANES))
            acc_scr[hh] = _dot(p.astype(BF16), vm_ref[:, pair_lanes(hh)])

    def tile(r0, nr, c0, nc, masked):
        rows = pl.ds(r0, nr)
        if masked:
            keep = (lax.broadcasted_iota(jnp.int32, (nr, nc), 1) <= lax.broadcasted_iota(jnp.int32, (nr, nc), 0))
        for hh in range(heads):
            sl = slice(hh * HEAD_PAD, (hh + 1) * HEAD_PAD)
            s = _dot_nt(q_ref[rows, sl], k_ref[pl.ds(c0, nc), sl])
            if masked:
                s = jnp.where(keep, s, NEG_BIG)
            m_old = m_scr[hh, rows, :]
            m_new = jnp.maximum(m_old, jnp.max(s, axis=-1, keepdims=True))
            alpha = jnp.exp2(m_old - m_new)
            p = jnp.exp2(s - jnp.tile(m_new, (1, nc // LANES)))
            m_scr[hh, rows, :] = m_new
            l_scr[hh, rows, :] = alpha * l_scr[hh, rows, :] + jnp.sum(p, axis=-1, keepdims=True)
            acc_scr[hh, rows, :] = alpha * acc_scr[hh, rows, :] + _dot(p.astype(BF16),
                                                                       v_ref[pl.ds(c0, nc), pair_lanes(hh)])

    @pl.when(ki < qi)
    def _():
        tile(0, tm, 0, tk, False)

    @pl.when(ki == qi)
    def _():
        ns = sub_tiles
        st = tm // ns
        for a in range(ns):
            if a > 0:
                tile(a * st, st, 0, a * st, False)
            tile(a * st, st, a * st, st, True)
        lane = lax.broadcasted_iota(jnp.int32, (tm, 2 * MLA_V), 1)
        for pr in range(heads // 2):
            even = acc_scr[2 * pr] / l_scr[2 * pr]
            odd = acc_scr[2 * pr + 1] / l_scr[2 * pr + 1]
            o_ref[:, pair_lanes(2 * pr)] = jnp.where(lane < MLA_V, even, odd).astype(BF16)


def _attn_prompt(mq, k, v, km, vm, batch, seq, blk):
    nb = seq // blk
    hps = ATTN_HEADS_PER_STEP
    pairs_qk = [(qi, ki) for qi in range(nb) for ki in range(qi + 1)]
    qi_tab = jnp.asarray([p[0] for p in pairs_qk], jnp.int32)
    ki_tab = jnp.asarray([p[1] for p in pairs_qk], jnp.int32)
    in_specs = [pl.BlockSpec((blk, hps * HEAD_PAD), lambda b, h, t, qt, kt: (b * nb + qt[t], h)),
                pl.BlockSpec((blk, hps * HEAD_PAD), lambda b, h, t, qt, kt: (b * nb + kt[t], h)),
                pl.BlockSpec((blk, hps * MLA_V), lambda b, h, t, qt, kt: (b * nb + kt[t], h)),
                pl.BlockSpec((km.shape[0], hps * HEAD_PAD), lambda b, h, t, qt, kt: (0, h)),
                pl.BlockSpec((vm.shape[0], hps * MLA_V), lambda b, h, t, qt, kt: (0, h))]
    grid_spec = pltpu.PrefetchScalarGridSpec(
        num_scalar_prefetch=2, grid=(batch, MLA_HEADS // hps, len(pairs_qk)), in_specs=in_specs,
        out_specs=pl.BlockSpec((blk, hps * MLA_V), lambda b, h, t, qt, kt: (b * nb + qt[t], h)),
        scratch_shapes=[pltpu.VMEM((hps, blk, LANES), F32), pltpu.VMEM((hps, blk, LANES), F32),
                        pltpu.VMEM((hps, blk, 2 * MLA_V), F32)])
    return pl.pallas_call(
        functools.partial(_attn_prompt_kernel, sub_tiles=2 if blk % (2 * LANES) == 0 else 1), grid_spec=grid_spec,
        out_shape=jax.ShapeDtypeStruct((batch * seq, MLA_HEADS * MLA_V), BF16),
        compiler_params=_params(("arbitrary",) * 3), name="attn_prompt",
    )(qi_tab, ki_tab, mq, k, v, km, vm)


def _absorb_kernel(mq_ref, wabs_ref, qt_ref, qf_ref):
    q = mq_ref[...]
    qf_ref[...] = q.astype(F32)
    for h in range(MLA_HEADS):
        qt_ref[:, h * KV_LORA:(h + 1) * KV_LORA] = _dot(q[:, h * HEAD_PAD:(h + 1) * HEAD_PAD], wabs_ref[h])


def _absorb(mq, wabs, row0_blocks, rows, tm):
    return pl.pallas_call(
        _absorb_kernel, grid=(rows // tm,),
        in_specs=[pl.BlockSpec((tm, MLA_HEADS * HEAD_PAD), lambda i: (row0_blocks + i, 0)),
                  pl.BlockSpec(wabs.shape, lambda i: (0, 0, 0))],
        out_specs=[pl.BlockSpec((tm, MLA_HEADS * KV_LORA), lambda i: (i, 0)),
                   pl.BlockSpec((tm, MLA_HEADS * HEAD_PAD), lambda i: (i, 0))],
        out_shape=[jax.ShapeDtypeStruct((rows, MLA_HEADS * KV_LORA), F32),
                   jax.ShapeDtypeStruct((rows, MLA_HEADS * HEAD_PAD), F32)],
        compiler_params=_params(("arbitrary",)), name="absorb",
    )(mq, wabs)


def _attn_sample_kernel(pt_ref, lat_hbm, kpe_hbm, cos_ref, sin_ref, qt_ref, qf_ref, cn_ref, kn_ref, cosn_ref,
                        sinn_ref, lw_ref, wuv_ref, gpe_ref, o_ref, l_scr, qpe_scr, m_scr, d_scr, acc_scr, kpad_scr,
                        cpad_scr, lat_buf, kpe_buf, sem, *, n_pages, sub, n_tok, page):
    s_id = pl.program_id(0)
    j = pl.program_id(1)
    n_steps = pl.num_programs(1)
    step = s_id * n_steps + j
    last = pl.num_programs(0) * n_steps - 1
    slot = lax.rem(step, 2)
    nq = MLA_HEADS * n_tok
    n_up = MLA_HEADS * MLA_NOPE

    def page_copies(pid, sl, p):
        return (pltpu.make_async_copy(lat_hbm.at[0, pid], lat_buf.at[sl, p], sem.at[sl, 0]),
                pltpu.make_async_copy(kpe_hbm.at[0, pid], kpe_buf.at[sl, p], sem.at[sl, 1]))

    def start_pages(seq, st, sl, pages=range(n_pages)):
        for p in pages:
            for c in page_copies(pt_ref[seq, st * n_pages + p], sl, p):
                c.start()

    def wait_pages(sl):
        for p in range(n_pages):
            for c in page_copies(0, sl, p):
                c.wait()

    @pl.when(step == 0)
    def _():
        start_pages(0, 0, 0)

    wait_pages(slot)
    wrap = j == n_steps - 1
    nxt_seq = jnp.where(wrap, jnp.where(step == last, 0, s_id + 1), s_id)
    nxt_j = jnp.where(wrap, 0, j + 1)
    start_pages(nxt_seq, nxt_j, 1 - slot)
    lat_refs = [lat_buf.at[slot, p] for p in range(n_pages)]
    kpe_refs = [kpe_buf.at[slot, p] for p in range(n_pages)]

    @pl.when(j == 0)
    def _():
        l_scr[:n_up, :] = lw_ref[...]
        qt = qt_ref[...]
        qf = qf_ref[...]
        l_scr[n_up:, :] = jnp.concatenate(
            [qt[:, h * KV_LORA:(h + 1) * KV_LORA] for h in range(MLA_HEADS)], axis=0).astype(BF16)
        lane = lax.broadcasted_iota(jnp.int32, (nq, HEAD_PAD), 1)
        qpe = jnp.concatenate([qf[:, h * HEAD_PAD:(h + 1) * HEAD_PAD] for h in range(MLA_HEADS)], axis=0)
        qpe_scr[...] = jnp.where(lane < MLA_ROPE, qpe, 0.0).astype(BF16)
        m_scr[...] = jnp.full(m_scr.shape, NEG_BIG, F32)
        d_scr[...] = jnp.zeros(d_scr.shape, F32)
        acc_scr[...] = jnp.zeros(acc_scr.shape, F32)

    def scores(cb, kpe_t, cos_t, sin_t):
        tk = cb.shape[0]
        big = _dot_nt(l_scr[...], cb)
        k_t = big[:n_up]
        ss = jnp.sum((k_t * k_t).reshape(MLA_HEADS, MLA_NOPE, tk), axis=1)
        ss = ss + jnp.sum(kpe_t * kpe_t, axis=0, keepdims=True)
        rs = lax.rsqrt(ss * (1.0 / MLA_QK) + EPS)
        kg = kpe_t * gpe_ref[...]
        half = MLA_ROPE // 2
        x1, x2 = kg[:half], kg[half:]
        rot = jnp.concatenate([x1 * cos_t - x2 * sin_t, x1 * sin_t + x2 * cos_t], axis=0)
        pe = _dot(qpe_scr[:, :MLA_ROPE], rot.astype(BF16))
        rs_rows = jnp.concatenate([jnp.broadcast_to(rs[h:h + 1, :], (n_tok, tk)) for h in range(MLA_HEADS)], axis=0)
        return (big[n_up:] + pe) * rs_rows

    def update(s_list, cb_list):
        m_old = m_scr[...]
        m_new = m_old
        for s in s_list:
            m_new = jnp.maximum(m_new, jnp.max(s, axis=-1, keepdims=True))
        alpha = jnp.exp2(m_old - m_new)
        d = alpha * d_scr[...]
        acc = alpha * acc_scr[...]
        for s, cb in zip(s_list, cb_list):
            p = jnp.exp2(s - m_new)
            d = d + jnp.sum(p, axis=-1, keepdims=True)
            acc = acc + _dot(p.astype(BF16), cb)
        m_scr[...] = m_new
        d_scr[...] = d
        acc_scr[...] = acc

    s_list, cb_list = [], []
    for g in range(n_pages // sub):
        pages = range(g * sub, (g + 1) * sub)
        cb = jnp.concatenate([lat_refs[b][...] for b in pages], axis=0).astype(BF16)
        kpe_t = jnp.concatenate([kpe_refs[b][...] for b in pages], axis=1)
        cols = slice(g * sub * page, (g + 1) * sub * page)
        s_list.append(scores(cb, kpe_t, cos_ref[:, cols], sin_ref[:, cols]))
        cb_list.append(cb)
    update(s_list, cb_list)

    @pl.when(j == pl.num_programs(1) - 1)
    def _():
        cpad_scr[...] = jnp.zeros(cpad_scr.shape, F32)
        cpad_scr[:n_tok, :] = cn_ref[...]
        kpad_scr[...] = jnp.zeros(kpad_scr.shape, F32)
        kpad_scr[:n_tok, :MLA_ROPE] = kn_ref[...]
        rowi = lax.broadcasted_iota(jnp.int32, (nq, page), 0)
        coli = lax.broadcasted_iota(jnp.int32, (nq, page), 1)
        keep = coli <= (rowi & (n_tok - 1))
        cb = cpad_scr[...].astype(BF16)
        s = scores(cb, kpad_scr[...].T[:MLA_ROPE], cosn_ref[...], sinn_ref[...])
        update([jnp.where(keep, s, NEG_BIG)], [cb])
        ctx = (acc_scr[...] / d_scr[...]).astype(BF16)
        out = jnp.zeros(o_ref.shape, F32)
        for h in range(MLA_HEADS):
            out = out + _dot(ctx, wuv_ref[h])[h * n_tok:(h + 1) * n_tok]
        o_ref[...] = out

    @pl.when(step == last)
    def _():
        wait_pages(1 - slot)


def _attn_sample(page_table, cache_latent, cache_krope_t, tabs, qt, qf, ckv, kpe, w, n_tok, n_pages_step, sub):
    n_seq, n_pages = page_table.shape
    page = cache_latent.shape[2]
    n_steps = n_pages // n_pages_step
    tk = n_pages_step * page
    nq = MLA_HEADS * n_tok

    seq_row = lambda s, j, pt: (s, 0)
    const2 = lambda s, j, pt: (0, 0)
    half = MLA_ROPE // 2
    in_specs = [pl.BlockSpec(memory_space=pl.ANY), pl.BlockSpec(memory_space=pl.ANY)]
    in_specs += [pl.BlockSpec((half, tk), lambda s, j, pt: (0, j)), pl.BlockSpec((half, tk), lambda s, j, pt: (0, j)),
                 pl.BlockSpec((n_tok, MLA_HEADS * KV_LORA), seq_row),
                 pl.BlockSpec((n_tok, MLA_HEADS * HEAD_PAD), seq_row),
                 pl.BlockSpec((n_tok, KV_LORA), seq_row), pl.BlockSpec((n_tok, MLA_ROPE), seq_row),
                 pl.BlockSpec((half, page), const2), pl.BlockSpec((half, page), const2),
                 pl.BlockSpec(w['uk_t'].shape, const2),
                 pl.BlockSpec(w['uv_blk'].shape, lambda s, j, pt: (0, 0, 0)),
                 pl.BlockSpec((MLA_ROPE, 1), const2)]
    grid_spec = pltpu.PrefetchScalarGridSpec(
        num_scalar_prefetch=1, grid=(n_seq, n_steps), in_specs=in_specs,
        out_specs=pl.BlockSpec((n_tok, MLA_HEADS * MLA_V), seq_row),
        scratch_shapes=[pltpu.VMEM((MLA_HEADS * MLA_NOPE + nq, KV_LORA), BF16),
                        pltpu.VMEM((nq, HEAD_PAD), BF16),
                        pltpu.VMEM((nq, 1), F32), pltpu.VMEM((nq, 1), F32), pltpu.VMEM((nq, KV_LORA), F32),
                        pltpu.VMEM((page, LANES), F32), pltpu.VMEM((page, KV_LORA), F32),
                        pltpu.VMEM((2, n_pages_step, page, KV_LORA), F32),
                        pltpu.VMEM((2, n_pages_step, MLA_ROPE, page), F32),
                        pltpu.SemaphoreType.DMA((2, 2))])
    return pl.pallas_call(
        functools.partial(_attn_sample_kernel, n_pages=n_pages_step, sub=sub, n_tok=n_tok, page=page),
        grid_spec=grid_spec, out_shape=jax.ShapeDtypeStruct((n_seq * n_tok, MLA_HEADS * MLA_V), F32),
        compiler_params=_params(("arbitrary", "arbitrary")), name="attn_sample",
    )(page_table, cache_latent, cache_krope_t,
      tabs['cos_t'], tabs['sin_t'], qt, qf, ckv, kpe, tabs['cosn_t'], tabs['sinn_t'],
      w['uk_t'], w['uv_blk'], w['gpe'])


def _post_kernel(ra_ref, rb_ref, ma_ref, mb_ref, xa_ref, xb_ref, wo_ref, g_ref, wrh_ref, wrl_ref, br_ref,
                 h_ref, xn_ref, ti_ref, tg_ref, *, n_a):
    first = pl.program_id(0) < n_a
    r = jnp.where(first, ra_ref[...], rb_ref[...])
    m = jnp.where(first, ma_ref[...], mb_ref[...].astype(BF16))
    half = wo_ref.shape[0] // 2
    mix = _dot(r, wo_ref[:half, :]) + _dot(m, wo_ref[half:, :])
    h = jnp.where(first, xa_ref[...], xb_ref[...]) + mix
    h_ref[...] = h
    xn = _rms(h, g_ref[...])
    xn_ref[...] = xn
    hi = xn.astype(BF16)
    lo = (xn - hi.astype(F32)).astype(BF16)
    wrh = wrh_ref[...]
    work = _dot(hi, wrh) + _dot(hi, wrl_ref[...]) + _dot(lo, wrh) + br_ref[...]
    lane = lax.broadcasted_iota(jnp.int32, work.shape, 1).astype(F32)
    idx = jnp.zeros(work.shape, F32)
    val = jnp.zeros(work.shape, F32)
    vmax = None
    denom = None
    for k in range(TOP_K):
        vk = jnp.max(work, axis=-1, keepdims=True)
        ik = jnp.min(jnp.where(work == vk, lane, float(LANES)), axis=-1, keepdims=True)
        work = jnp.where(lane == ik, -jnp.inf, work)
        if k == 0:
            vmax = vk
        ek = jnp.exp(vk - vmax)
        denom = ek if k == 0 else denom + ek
        idx = jnp.where(lane == float(k), ik, idx)
        val = jnp.where(lane == float(k), ek, val)
    ti_ref[...] = idx.astype(jnp.int32)
    tg_ref[...] = val / denom


def _post(r_pair, m_pair, x_pair, w, tm):
    n_a, n_b = x_pair[0].shape[0] // tm, x_pair[1].shape[0] // tm
    rows = (n_a + n_b) * tm
    row = lambda i: (i, 0)
    const = lambda i: (0, 0)
    in_specs = []
    for a, b in (r_pair, m_pair, x_pair):
        in_specs += [pl.BlockSpec((tm, a.shape[1]), lambda i: (jnp.minimum(i, n_a - 1), 0)),
                     pl.BlockSpec((tm, b.shape[1]), lambda i: (jnp.maximum(i - n_a, 0), 0))]
    in_specs += [pl.BlockSpec(w[n].shape, const) for n in ('wo', 'gffn', 'wr_hi', 'wr_lo', 'br')]
    out_shape = [jax.ShapeDtypeStruct((rows, D_MODEL), F32), jax.ShapeDtypeStruct((rows, D_MODEL), F32),
                 jax.ShapeDtypeStruct((rows, LANES), jnp.int32), jax.ShapeDtypeStruct((rows, LANES), F32)]
    out_specs = [pl.BlockSpec((tm, D_MODEL), row), pl.BlockSpec((tm, D_MODEL), row),
                 pl.BlockSpec((tm, LANES), row), pl.BlockSpec((tm, LANES), row)]
    return pl.pallas_call(
        functools.partial(_post_kernel, n_a=n_a), grid=(rows // tm,), in_specs=in_specs, out_specs=out_specs,
        out_shape=out_shape, compiler_params=_params(("arbitrary",)), name="post",
    )(*r_pair, *m_pair, *x_pair, w['wo'], w['gffn'], w['wr_hi'], w['wr_lo'], w['br'])


def _moe_kernel(be_ref, nu_ref, xs_ref, wgu_ref, bgu_ref, wd_ref, bd_ref, o_ref, wgu_b, wd_b):
    i = pl.program_id(0)
    e = be_ref[i]
    e_prev = be_ref[jnp.maximum(i - 1, 0)]

    @pl.when((i == 0) | (e != e_prev))
    def _():
        wgu_b[...] = wgu_ref[...].astype(BF16)
        wd_b[...] = wd_ref[...].astype(BF16)

    @pl.when(i < nu_ref[0])
    def _():
        hgu = _dot(xs_ref[...].astype(BF16), wgu_b[...]) + bgu_ref[...]
        g = jnp.minimum(hgu[:, :D_FF], SWIGLU_LIMIT)
        u = jnp.clip(hgu[:, D_FF:], -SWIGLU_LIMIT, SWIGLU_LIMIT)
        hid = (u + 1.0) * (g * jax.nn.sigmoid(SWIGLU_ALPHA * g))
        o_ref[...] = _dot(hid.astype(BF16), wd_b[...]) + bd_ref[...]

    @pl.when(i >= nu_ref[0])
    def _():
        o_ref[...] = jnp.zeros(o_ref.shape, F32)


def _moe_blocks(block_e, n_used, xs, w_gate_up, b_gate_up, w_down, b_down, blk):
    n_blocks = block_e.shape[0]
    row = lambda i, be, nu: (i, 0)
    in_specs = [pl.BlockSpec((blk, D_MODEL), row),
                pl.BlockSpec((None, D_MODEL, 2 * D_FF), lambda i, be, nu: (be[i], 0, 0)),
                pl.BlockSpec((None, 1, 2 * D_FF), lambda i, be, nu: (be[i], 0, 0)),
                pl.BlockSpec((None, D_FF, D_MODEL), lambda i, be, nu: (be[i], 0, 0)),
                pl.BlockSpec((None, 1, D_MODEL), lambda i, be, nu: (be[i], 0, 0))]
    grid_spec = pltpu.PrefetchScalarGridSpec(
        num_scalar_prefetch=2, grid=(n_blocks,), in_specs=in_specs,
        out_specs=pl.BlockSpec((blk, D_MODEL), row),
        scratch_shapes=[pltpu.VMEM((D_MODEL, 2 * D_FF), BF16), pltpu.VMEM((D_FF, D_MODEL), BF16)])
    return pl.pallas_call(
        _moe_kernel, grid_spec=grid_spec, out_shape=jax.ShapeDtypeStruct((n_blocks * blk, D_MODEL), F32),
        compiler_params=_params(("arbitrary",)), name="moe",
    )(block_e, n_used, xs, w_gate_up, b_gate_up[:, None, :], w_down, b_down[:, None, :])


def _combine_kernel(h_ref, g_ref, y_ref, o_ref):
    g = g_ref[...]
    acc = y_ref[0] * g[:, 0:1]
    for k in range(1, TOP_K):
        acc = acc + y_ref[k] * g[:, k:k + 1]
    o_ref[...] = h_ref[...] + acc


def _combine(h, gates, y4, block0, rows, tm):
    row = lambda i: (block0 + i, 0)
    return pl.pallas_call(
        _combine_kernel, grid=(rows // tm,),
        in_specs=[pl.BlockSpec((tm, D_MODEL), row), pl.BlockSpec((tm, LANES), row),
                  pl.BlockSpec((TOP_K, tm, D_MODEL), lambda i: (0, block0 + i, 0))],
        out_specs=pl.BlockSpec((tm, D_MODEL), lambda i: (i, 0)),
        out_shape=jax.ShapeDtypeStruct((rows, D_MODEL), F32),
        compiler_params=_params(("arbitrary",)), name="combine",
    )(h, gates, y4)


def _gather_rows(table, idx):
    n, d = idx.shape[0], table.shape[1]
    n_win = n // GATHER_WINDOW
    assert n % (GATHER_WINDOW * SC_CORES * SC_SUBCORES) == 0
    idx = jnp.zeros((n_win, LANES), jnp.int32).at[:, :GATHER_WINDOW].set(idx.reshape(n_win, GATHER_WINDOW))
    mesh = plsc.VectorSubcoreMesh(core_axis_name="core", subcore_axis_name="subcore",
                                  num_cores=SC_CORES, num_subcores=SC_SUBCORES)

    @functools.partial(pl.kernel, out_type=jax.ShapeDtypeStruct((n, d), table.dtype), mesh=mesh)
    def gather(x_hbm, i_hbm, o_hbm):
        def body(i_vmem, o_vmem):
            pltpu.sync_copy(x_hbm.at[i_vmem.at[0, pl.ds(0, GATHER_WINDOW)]], o_vmem)

        pltpu.emit_pipeline(
            body, grid=(n_win,),
            in_specs=[pl.BlockSpec((1, LANES), index_map=lambda i: (i, 0))],
            out_specs=[pl.BlockSpec((GATHER_WINDOW, d), index_map=lambda i: (i, 0))],
            core_axis_name=("core", "subcore"), dimension_semantics=(pltpu.PARALLEL,),
        )(i_hbm, o_hbm)

    return gather(table, idx)


def _route(top_i, blk):
    t = top_i.shape[0]
    a = t * TOP_K
    flat_e = top_i.reshape(a)
    assert N_EXPERTS * a < 2 ** 31
    order = jnp.sort(flat_e.astype(jnp.int32) * a + jnp.arange(a, dtype=jnp.int32)) % a
    tok_sorted = (order // TOP_K).astype(jnp.int32)
    experts = jnp.arange(N_EXPERTS, dtype=jnp.int32)
    hit = (top_i[:, :, None] == experts[None, None, :]).astype(jnp.int32)
    per_tok = hit.sum(axis=1)
    before = jnp.cumsum(per_tok, axis=0) - per_tok
    counts = per_tok.sum(axis=0)
    padded = (counts + blk - 1) // blk * blk
    start = jnp.cumsum(counts) - counts
    pend = jnp.cumsum(padded)
    pstart = pend - padded
    dest = ((before + pstart[None, :])[:, None, :] * hit).sum(axis=2)
    n_blocks = -(-a // blk) + N_EXPERTS
    block_e = jnp.minimum((pend[None, :] <= (jnp.arange(n_blocks, dtype=jnp.int32) * blk)[:, None]).sum(axis=1),
                          N_EXPERTS - 1).astype(jnp.int32)
    slot = jnp.arange(n_blocks * blk, dtype=jnp.int32)
    slot_e = jnp.repeat(block_e, blk)
    r = slot - pstart[slot_e]
    slot_tok = jnp.where(r < counts[slot_e], tok_sorted[jnp.minimum(start[slot_e] + r, a - 1)], slot % t)
    n_used = (pend[-1:] // blk).astype(jnp.int32)
    return slot_tok, dest.astype(jnp.int32), block_e, n_used


def _rope_tables(pos):
    pos = pos.astype(F32)[:, None]
    n = pos.shape[0]
    hr = RET_DK // 2
    ang = pos * (RET_THETA ** (-jnp.arange(hr, dtype=F32) / hr))[None, :]
    cos, sin = jnp.cos(ang), jnp.sin(ang)
    zr = jnp.zeros((n, LANES - RET_DK), F32)
    cr = jnp.concatenate([cos, cos, zr], axis=1)
    sr = jnp.concatenate([-sin, sin, zr], axis=1)
    hm = MLA_ROPE // 2
    ang = pos * (MLA_THETA ** (-jnp.arange(hm, dtype=F32) / hm))[None, :]
    cos, sin = jnp.cos(ang), jnp.sin(ang)
    cm = jnp.concatenate([cos, cos, jnp.ones((n, LANES - MLA_ROPE), F32)], axis=1)
    sma = jnp.concatenate([jnp.zeros((n, hm), F32), sin, jnp.zeros((n, LANES - MLA_ROPE), F32)], axis=1)
    smb = jnp.concatenate([-sin, jnp.zeros((n, LANES - hm), F32)], axis=1)
    return [cr, sr, cm, sma, smb]


def _rope_tables_t(pos):
    hm = MLA_ROPE // 2
    ang = pos.astype(F32)[:, None] * (MLA_THETA ** (-jnp.arange(hm, dtype=F32) / hm))[None, :]
    return jnp.cos(ang).T, jnp.sin(ang).T


def _decay_tables(log_gamma, c, n_rep):
    idx = jnp.arange(c, dtype=F32)
    diff = idx[:, None] - idx[None, :]
    decay = jnp.where(diff >= 0, jnp.exp(jnp.maximum(diff, 0.0)[None] * log_gamma[:, None, None]), 0.0)
    if n_rep > 1:
        eye = jnp.eye(n_rep, dtype=F32)
        decay = (eye[None, :, None, :, None] * decay[:, None, :, None, :]).reshape(RET_HEADS, n_rep * c, n_rep * c)
    rowd = jnp.tile(jnp.exp((idx + 1.0)[:, None] * log_gamma[None, :]), (n_rep, 1))
    cold = jnp.tile(jnp.exp((c - 1.0 - idx)[:, None] * log_gamma[None, :]), (n_rep, 1))
    cpow = jnp.broadcast_to(jnp.exp(c * log_gamma)[None, :], (LANES, RET_HEADS))
    return {'dec': decay, 'rowd': rowd, 'cold': cold, 'cpow': cpow}


def _pad_heads(wm, n_heads, width, offset):
    k = wm.shape[0]
    wm = wm.reshape(k, n_heads, width)
    out = jnp.zeros((k, n_heads, LANES), wm.dtype).at[:, :, offset:offset + width].set(wm)
    return out.reshape(k, n_heads * LANES)


def _prep_weights(norm_attn_g, w_in, ret_out_g, q_a_norm_g, w_q_b, kv_a_norm_g, w_uk, w_uv,
                  qk_norm_q_g, qk_norm_k_g, w_out, norm_ffn_g, w_router, b_router):
    splits = [RET_HEADS * RET_DK, RET_HEADS * RET_DK, RET_HEADS * RET_DV, RET_HEADS * RET_DV, Q_LORA, KV_LORA,
              MLA_ROPE]
    offs = [0]
    for s in splits:
        offs.append(offs[-1] + s)
    part = [w_in[:, offs[i]:offs[i + 1]] for i in range(len(splits))]
    kpe_cols = jnp.zeros((D_MODEL, LANES), F32).at[:, :MLA_ROPE].set(part[6])
    win = jnp.concatenate([_pad_heads(part[0], RET_HEADS, RET_DK, 0), _pad_heads(part[1], RET_HEADS, RET_DK, 0),
                           part[2], part[3], part[4], part[5], kpe_cols], axis=1).astype(BF16)

    def mla_cols(wm):
        k = wm.shape[0]
        wm = wm.reshape(k, MLA_HEADS, MLA_QK)
        wm = jnp.concatenate([wm[:, :, MLA_NOPE:], wm[:, :, :MLA_NOPE]], axis=2)
        return _pad_heads(wm.reshape(k, MLA_HEADS * MLA_QK), MLA_HEADS, MLA_QK, 0)

    def mla_gain(g):
        g = jnp.concatenate([g[MLA_NOPE:], g[:MLA_NOPE], jnp.zeros((LANES - MLA_QK,), F32)])
        return g[None, :]

    gk_nope = qk_norm_k_g[:MLA_NOPE]
    uk_heads = w_uk.reshape(KV_LORA, MLA_HEADS, MLA_NOPE)
    wabs = jnp.zeros((MLA_HEADS, HEAD_PAD, KV_LORA), F32).at[:, MLA_ROPE:MLA_ROPE + MLA_NOPE, :].set(
        (uk_heads * gk_nope[None, None, :]).transpose(1, 2, 0))
    head_of_col = jnp.arange(MLA_HEADS * MLA_V) // MLA_V
    uv_blk = jnp.where(head_of_col[None, None, :] == jnp.arange(MLA_HEADS)[:, None, None], w_uv[None], 0.0)
    wr = jnp.zeros((D_MODEL, LANES), F32).at[:, :N_EXPERTS].set(w_router)
    wr_hi = wr.astype(BF16)
    return {
        'gattn': norm_attn_g[None, :], 'win': win, 'gqa': q_a_norm_g[None, :], 'wqb': mla_cols(w_q_b).astype(BF16),
        'gq': mla_gain(qk_norm_q_g), 'gkv': kv_a_norm_g[None, :],
        'wuk': _pad_heads(w_uk, MLA_HEADS, MLA_NOPE, MLA_ROPE).astype(BF16), 'gk': mla_gain(qk_norm_k_g),
        'wuv': w_uv.astype(BF16), 'gret': ret_out_g[None, :],
        'wabs': wabs.astype(BF16), 'uk_t': w_uk.T.astype(BF16), 'uv_blk': uv_blk.astype(BF16),
        'gpe': qk_norm_k_g[MLA_NOPE:, None],
        'wo': w_out.astype(BF16), 'gffn': norm_ffn_g[None, :], 'wr_hi': wr_hi,
        'wr_lo': (wr - wr_hi.astype(F32)).astype(BF16),
        'br': jnp.full((1, LANES), NEG_BIG, F32).at[0, :N_EXPERTS].set(b_router),
    }


def _pad_rows(a, rows):
    return jnp.zeros((rows,) + a.shape[1:], a.dtype).at[:a.shape[0]].set(a)


def _largest_divisor(n, cap):
    d = min(n, cap)
    while n % d:
        d -= 1
    return d


def kernel(x_prompt, x_sample, cache_latent, cache_krope, state_retention, page_table, meta_tokens, norm_attn_g, w_in, ret_out_g, q_a_norm_g, w_q_b, kv_a_norm_g, w_uk, w_uv, qk_norm_q_g, qk_norm_k_g, w_out, norm_ffn_g, w_router, b_router, w_gate_up, b_gate_up, w_down, b_down):
    assert w_in.shape[0] == 1, "single-layer trunk"
    batch, seq, _ = x_prompt.shape
    n_seq, n_tok, _ = x_sample.shape
    n_pages, page = page_table.shape[1], cache_latent.shape[2]
    past = n_pages * page
    assert seq % RET_CHUNK == 0 and LANES % n_tok == 0 and (n_seq * n_tok) % LANES == 0 and page == LANES
    w = _prep_weights(norm_attn_g[0], w_in[0], ret_out_g[0], q_a_norm_g[0], w_q_b[0], kv_a_norm_g[0], w_uk[0],
                      w_uv[0], qk_norm_q_g[0], qk_norm_k_g[0], w_out[0], norm_ffn_g[0], w_router[0], b_router[0])
    log_gamma = jnp.log1p(-jnp.exp2(-5.0 - jnp.arange(RET_HEADS, dtype=F32)))

    rows_p = batch * seq
    rows_s = n_seq * n_tok
    tm = _largest_divisor(min(seq, rows_s), ROW_BLOCK)
    nb_seq = seq // tm
    x_pair = (x_prompt.reshape(rows_p, D_MODEL), x_sample.reshape(rows_s, D_MODEL))
    pos_rows = jnp.concatenate([N_META + jnp.arange(seq), jnp.tile(past + jnp.arange(n_tok), tm // n_tok)])
    tabs = _rope_tables(pos_rows)
    n_pb = rows_p // tm
    rq, rk, rv, zg, mq, ckv, kpe, k, v = _front(
        *x_pair, tabs, lambda i: jnp.where(i < n_pb, i % nb_seq, nb_seq), w, tm)
    _, mrk, mrv, _, _, mckv, mkpe, mk, mv = _front(
        meta_tokens, None, _rope_tables(jnp.arange(N_META)), lambda i: i, w, N_META)

    dt_p = _decay_tables(log_gamma, RET_CHUNK, 1)
    dt_p['mcol'] = _pad_rows(jnp.exp((N_META - 1.0 - jnp.arange(N_META, dtype=F32))[:, None] * log_gamma[None, :]),
                             RET_CHUNK)
    gret = w['gret']
    r_p, st_p = _ret_prompt(rq, rk, rv, zg, _pad_rows(mrk, RET_CHUNK), _pad_rows(mrv, RET_CHUNK), dt_p, gret,
                            batch, seq // RET_CHUNK)
    dt_s = _decay_tables(log_gamma, n_tok, LANES // n_tok)
    r_s, st_s = _ret_sample(rq[rows_p:], rk[rows_p:], rv[rows_p:], zg[rows_p:], state_retention[0], dt_s, gret, n_tok)

    blk = _largest_divisor(seq, ATTN_BLOCK)
    m_p = _attn_prompt(mq, k, v, _pad_rows(mk, LANES), _pad_rows(mv, LANES), batch, seq, blk)
    tm_s = _largest_divisor(rows_s, ROW_BLOCK)
    qt, qf = _absorb(mq, w['wabs'], rows_p // tm_s, rows_s, tm_s)
    n_pages_step = _largest_divisor(n_pages, PAGES_PER_STEP)
    cos_t, sin_t = _rope_tables_t(jnp.arange(past))
    cosn_t, sinn_t = _rope_tables_t(past + jnp.arange(page))
    tabs_s = {'cos_t': cos_t, 'sin_t': sin_t, 'cosn_t': cosn_t, 'sinn_t': sinn_t}
    m_s = _attn_sample(page_table, cache_latent, jnp.swapaxes(cache_krope, 2, 3), tabs_s, qt, qf, ckv[rows_p:],
                       kpe[rows_p:], w, n_tok, n_pages_step, _largest_divisor(n_pages_step, PAGES_PER_CHAIN))

    h1, xn2, top_i, gates = _post((r_p, r_s), (m_p, m_s), x_pair, w, tm)

    slot_tok, dest, block_e, n_used = _route(top_i[:, :TOP_K], EXPERT_BLOCK)
    outs = _moe_blocks(block_e, n_used, _gather_rows(xn2, slot_tok), w_gate_up[0], b_gate_up[0], w_down[0],
                       b_down[0], EXPERT_BLOCK)
    y4 = _gather_rows(outs, dest.T.reshape(-1)).reshape(TOP_K, rows_p + rows_s, D_MODEL)
    y_prompt = _combine(h1, gates, y4, 0, rows_p, tm).reshape(batch, seq, D_MODEL)
    y_sample = _combine(h1, gates, y4, n_pb, rows_s, tm).reshape(n_seq, n_tok, D_MODEL)
    lat_p = jnp.concatenate([jnp.broadcast_to(mckv[None], (batch, N_META, KV_LORA)),
                             ckv[:rows_p].reshape(batch, seq, KV_LORA)], axis=1)[None]
    kpe_p = jnp.concatenate([jnp.broadcast_to(mkpe[None], (batch, N_META, MLA_ROPE)),
                             kpe[:rows_p].reshape(batch, seq, MLA_ROPE)], axis=1)[None]
    return (y_prompt, y_sample, lat_p, kpe_p, st_p[None],
            ckv[rows_p:].reshape(n_seq, n_tok, KV_LORA)[None], kpe[rows_p:].reshape(n_seq, n_tok, MLA_ROPE)[None],
            st_s[None])
```

```python
import functools

import jax
import jax.numpy as jnp
from jax import lax
from jax.experimental import pallas as pl
from jax.experimental.pallas import tpu as pltpu
from jax.experimental.pallas import tpu_sc as plsc

F32 = jnp.float32
BF16 = jnp.bfloat16

D_MODEL = 1024
N_META = 16
RET_HEADS = 4
RET_DK = 64
RET_DV = 128
RET_CHUNK = 128
RET_THETA = 10000.0
MLA_HEADS = 8
MLA_NOPE = 64
MLA_ROPE = 32
MLA_QK = MLA_NOPE + MLA_ROPE
MLA_V = 64
Q_LORA = 384
KV_LORA = 256
MLA_THETA = 10000.0
MLA_SCALE = MLA_QK ** -0.5
LOG2E = 1.4426950408889634
N_EXPERTS = 32
TOP_K = 4
D_FF = 1024
SWIGLU_LIMIT = 7.0
SWIGLU_ALPHA = 1.702
EPS = 1e-6
NEG_BIG = -1e30

LANES = 128
HEAD_PAD = LANES
VMEM_LIMIT = 56 * 1024 * 1024
ROW_BLOCK = 512
ATTN_BLOCK = 1024
ATTN_HEADS_PER_STEP = 4
RET_CHUNKS_PER_STEP = 4
PAGES_PER_STEP = 64
PAGES_PER_CHAIN = 32
EXPERT_BLOCK = 512
SC_CORES, SC_SUBCORES = 2, 16
GATHER_WINDOW = 32

_OFF_RQ = 0
_OFF_RK = _OFF_RQ + RET_HEADS * LANES
_OFF_RV = _OFF_RK + RET_HEADS * LANES
_OFF_ZG = _OFF_RV + RET_HEADS * RET_DV
_OFF_CQ = _OFF_ZG + RET_HEADS * RET_DV
_OFF_CKV = _OFF_CQ + Q_LORA
_OFF_KPE = _OFF_CKV + KV_LORA
IN_PAD = _OFF_KPE + LANES


def _params(sem):
    return pltpu.CompilerParams(dimension_semantics=sem, vmem_limit_bytes=VMEM_LIMIT)


def _rms(x, g):
    return x * lax.rsqrt(jnp.mean(x * x, axis=-1, keepdims=True) + EPS) * g


def _dot(a, b):
    return jnp.dot(a, b, preferred_element_type=F32)


def _dot_nt(a, b):
    return lax.dot_general(a, b, (((1,), (1,)), ((), ())), preferred_element_type=F32)


def _dot_tn(a, b):
    return lax.dot_general(a, b, (((0,), (0,)), ((), ())), preferred_element_type=F32)


def _front_kernel(xa_ref, xb_ref, gattn_ref, win_ref, cr_ref, sr_ref, cm_ref, sma_ref, smb_ref,
                  gqa_ref, wqb_ref, gq_ref, gkv_ref, wuk_ref, gk_ref, wuv_ref,
                  rq_ref, rk_ref, rv_ref, zg_ref, mq_ref, ckv_ref, kpe_ref, k_ref, v_ref, *, n_a):
    tm = xa_ref.shape[0]
    x = jnp.where(pl.program_id(0) < n_a, xa_ref[...], xb_ref[...])
    xn = _rms(x, gattn_ref[...])
    z = _dot(xn.astype(BF16), win_ref[...])

    lane = lax.broadcasted_iota(jnp.int32, (tm, LANES), 1)
    upper = (lane & (RET_DK // 2)) != 0
    cr, sr = cr_ref[...], sr_ref[...]

    def rope_ret(t):
        partner = jnp.where(upper, pltpu.roll(t, RET_DK // 2, 1), pltpu.roll(t, LANES - RET_DK // 2, 1))
        return t * cr + partner * sr

    for h in range(RET_HEADS):
        sl = slice(h * LANES, (h + 1) * LANES)
        rq_ref[:, sl] = rope_ret(z[:, _OFF_RQ + h * LANES:_OFF_RQ + (h + 1) * LANES])
        rk_ref[:, sl] = rope_ret(z[:, _OFF_RK + h * LANES:_OFF_RK + (h + 1) * LANES]) * (RET_DK ** -0.5)
    rv_ref[...] = z[:, _OFF_RV:_OFF_ZG].astype(BF16)
    zg_ref[...] = z[:, _OFF_ZG:_OFF_CQ]

    cm, sma, smb = cm_ref[...], sma_ref[...], smb_ref[...]

    def rope_mla(t):
        half = MLA_ROPE // 2
        return t * cm + pltpu.roll(t, half, 1) * sma + pltpu.roll(t, LANES - half, 1) * smb

    def head_norm(t, g):
        ms = jnp.sum(t * t, axis=-1, keepdims=True) * (1.0 / MLA_QK)
        return t * lax.rsqrt(ms + EPS) * g

    cq = _rms(z[:, _OFF_CQ:_OFF_CKV], gqa_ref[...])
    q = _dot(cq.astype(BF16), wqb_ref[...])
    gq = gq_ref[...]
    for h in range(MLA_HEADS):
        sl = slice(h * HEAD_PAD, (h + 1) * HEAD_PAD)
        mq_ref[:, sl] = (rope_mla(head_norm(q[:, sl], gq)) * (MLA_SCALE * LOG2E)).astype(BF16)

    ckv = _rms(z[:, _OFF_CKV:_OFF_KPE], gkv_ref[...])
    ckv_ref[...] = ckv
    kpe_slab = z[:, _OFF_KPE:IN_PAD]
    kpe_ref[...] = kpe_slab[:, :MLA_ROPE]
    ckv_b = ckv.astype(BF16)
    kn = _dot(ckv_b, wuk_ref[...])
    gk = gk_ref[...]
    for h in range(MLA_HEADS):
        sl = slice(h * HEAD_PAD, (h + 1) * HEAD_PAD)
        k_ref[:, sl] = rope_mla(head_norm(kn[:, sl] + kpe_slab, gk)).astype(BF16)
    v_ref[...] = _dot(ckv_b, wuv_ref[...]).astype(BF16)


def _front(xa, xb, tabs, tab_index, w, tm):
    n_a, n_b = xa.shape[0] // tm, (0 if xb is None else xb.shape[0] // tm)
    xb = xa if xb is None else xb
    rows = (n_a + n_b) * tm
    grid = (n_a + n_b,)
    row = lambda i: (i, 0)
    const = lambda i: (0, 0)
    tab = lambda i: (tab_index(i), 0)

    def full(a):
        return pl.BlockSpec(a.shape, const)

    in_specs = [pl.BlockSpec((tm, D_MODEL), lambda i: (jnp.minimum(i, n_a - 1), 0)),
                pl.BlockSpec((tm, D_MODEL), lambda i: (jnp.maximum(i - n_a, 0), 0)), full(w['gattn']), full(w['win'])]
    in_specs += [pl.BlockSpec((tm, LANES), tab)] * 5
    in_specs += [full(w[n]) for n in ('gqa', 'wqb', 'gq', 'gkv', 'wuk', 'gk', 'wuv')]
    widths = [(RET_HEADS * LANES, F32), (RET_HEADS * LANES, F32), (RET_HEADS * RET_DV, BF16),
              (RET_HEADS * RET_DV, F32), (MLA_HEADS * HEAD_PAD, BF16), (KV_LORA, F32), (MLA_ROPE, F32),
              (MLA_HEADS * HEAD_PAD, BF16), (MLA_HEADS * MLA_V, BF16)]
    out_shape = [jax.ShapeDtypeStruct((rows, n), dt) for n, dt in widths]
    out_specs = [pl.BlockSpec((tm, n), row) for n, _ in widths]
    return pl.pallas_call(
        functools.partial(_front_kernel, n_a=n_a), grid=grid, in_specs=in_specs, out_specs=out_specs,
        out_shape=out_shape, compiler_params=_params(("arbitrary",)), name="front",
    )(xa, xb, w['gattn'], w['win'], *tabs, w['gqa'], w['wqb'], w['gq'], w['gkv'], w['wuk'], w['gk'], w['wuv'])


def _ret_gate(o, zg, g):
    on = o * lax.rsqrt(jnp.mean(o * o, axis=-1, keepdims=True) + EPS) * g
    return (zg * jax.nn.sigmoid(zg)) * on


def _ret_prompt_kernel(q_ref, k_ref, v_ref, zg_ref, mk_ref, mv_ref, mcol_ref, dec_ref, rowd_ref, cold_ref,
                       cpow_ref, g_ref, r_ref, s_ref, s_scr):
    c = pl.program_id(1)

    @pl.when(c == 0)
    def _():
        for h in range(RET_HEADS):
            sl = slice(h * LANES, (h + 1) * LANES)
            kw = mk_ref[:, sl] * mcol_ref[:, h:h + 1]
            s_scr[h] = _dot_tn(kw.astype(BF16), mv_ref[:, sl])

    for h in range(RET_HEADS):
        sl = slice(h * LANES, (h + 1) * LANES)
        s0 = s_scr[h]
        for cc in range(q_ref.shape[0] // RET_CHUNK):
            rows = slice(cc * RET_CHUNK, (cc + 1) * RET_CHUNK)
            k = k_ref[rows, sl]
            v = v_ref[rows, sl]
            qb = q_ref[rows, sl].astype(BF16)
            scores = _dot_nt(qb, k.astype(BF16)) * dec_ref[h]
            inner = _dot(scores.astype(BF16), v)
            cross = _dot(qb, s0.astype(BF16)) * rowd_ref[:, h:h + 1]
            kw = k * cold_ref[:, h:h + 1]
            s0 = s0 * cpow_ref[:, h:h + 1] + _dot_tn(kw.astype(BF16), v)
            r_ref[rows, sl] = _ret_gate(inner + cross, zg_ref[rows, sl], g_ref[:, sl]).astype(BF16)
        s_scr[h] = s0

    @pl.when(c == pl.num_programs(1) - 1)
    def _():
        for h in range(RET_HEADS):
            s_ref[0, h] = s_scr[h, :RET_DK, :]


def _ret_prompt(rq, rk, rv, zg, mk, mv, tabs, g, batch, n_chunks):
    cs = RET_CHUNK
    cps = _largest_divisor(n_chunks, RET_CHUNKS_PER_STEP)
    steps = n_chunks // cps
    row = lambda b, c: (b * steps + c, 0)
    const2 = lambda b, c: (0, 0)
    w4 = RET_HEADS * LANES
    in_specs = [pl.BlockSpec((cps * cs, w4), row), pl.BlockSpec((cps * cs, w4), row),
                pl.BlockSpec((cps * cs, w4), row), pl.BlockSpec((cps * cs, w4), row),
                pl.BlockSpec((cs, w4), const2), pl.BlockSpec((cs, w4), const2),
                pl.BlockSpec((cs, RET_HEADS), const2),
                pl.BlockSpec((RET_HEADS, cs, cs), lambda b, c: (0, 0, 0)),
                pl.BlockSpec((cs, RET_HEADS), const2), pl.BlockSpec((cs, RET_HEADS), const2),
                pl.BlockSpec((LANES, RET_HEADS), const2), pl.BlockSpec((1, w4), const2)]
    out_shape = [jax.ShapeDtypeStruct((batch * n_chunks * cs, w4), BF16),
                 jax.ShapeDtypeStruct((batch, RET_HEADS, RET_DK, RET_DV), F32)]
    out_specs = [pl.BlockSpec((cps * cs, w4), row),
                 pl.BlockSpec((1, RET_HEADS, RET_DK, RET_DV), lambda b, c: (b, 0, 0, 0))]
    return pl.pallas_call(
        _ret_prompt_kernel, grid=(batch, steps), in_specs=in_specs, out_specs=out_specs, out_shape=out_shape,
        scratch_shapes=[pltpu.VMEM((RET_HEADS, LANES, RET_DV), F32)],
        compiler_params=_params(("arbitrary", "arbitrary")), name="ret_prompt",
    )(rq, rk, rv, zg, mk, mv, tabs['mcol'], tabs['dec'], tabs['rowd'], tabs['cold'], tabs['cpow'], g)


def _ret_sample_kernel(q_ref, k_ref, v_ref, zg_ref, s0_ref, dec_ref, rowd_ref, cold_ref, cpow_ref, g_ref,
                       r_ref, s_ref, *, n_seq, n_tok):
    rows = n_seq * n_tok
    ri = lax.broadcasted_iota(jnp.int32, (rows, 1), 0)
    for h in range(RET_HEADS):
        sl = slice(h * LANES, (h + 1) * LANES)
        q = q_ref[:, sl]
        k = k_ref[:, sl]
        v = v_ref[:, sl]
        qb = q.astype(BF16)
        scores = _dot_nt(qb, k.astype(BF16)) * dec_ref[h]
        inner = _dot(scores.astype(BF16), v)
        kw = k * cold_ref[:, h:h + 1]
        cross = jnp.zeros((rows, RET_DV), F32)
        for s in range(n_seq):
            s0 = s0_ref[s, h]
            mine = (ri >= s * n_tok) & (ri < (s + 1) * n_tok)
            cross = cross + jnp.where(mine, _dot(qb[:, :RET_DK], s0.astype(BF16)), 0.0)
            upd = _dot_tn(jnp.where(mine, kw, 0.0).astype(BF16), v)
            s_ref[s, h] = s0 * cpow_ref[:RET_DK, h:h + 1] + upd[:RET_DK]
        cross = cross * rowd_ref[:, h:h + 1]
        r_ref[:, sl] = _ret_gate(inner + cross, zg_ref[:, sl], g_ref[:, sl]).astype(BF16)


def _ret_sample(rq, rk, rv, zg, state, tabs, g, n_tok):
    n_seq_total = state.shape[0]
    n_seq = LANES // n_tok
    rows = n_seq * n_tok
    w4 = RET_HEADS * LANES
    row = lambda i: (i, 0)
    const = lambda i: (0, 0)
    in_specs = [pl.BlockSpec((rows, w4), row)] * 4
    in_specs += [pl.BlockSpec((n_seq, RET_HEADS, RET_DK, RET_DV), lambda i: (i, 0, 0, 0)),
                 pl.BlockSpec((RET_HEADS, rows, rows), lambda i: (0, 0, 0)),
                 pl.BlockSpec((rows, RET_HEADS), const), pl.BlockSpec((rows, RET_HEADS), const),
                 pl.BlockSpec((LANES, RET_HEADS), const), pl.BlockSpec((1, w4), const)]
    out_shape = [jax.ShapeDtypeStruct((n_seq_total * n_tok, w4), BF16),
                 jax.ShapeDtypeStruct(state.shape, F32)]
    out_specs = [pl.BlockSpec((rows, w4), row),
                 pl.BlockSpec((n_seq, RET_HEADS, RET_DK, RET_DV), lambda i: (i, 0, 0, 0))]
    return pl.pallas_call(
        functools.partial(_ret_sample_kernel, n_seq=n_seq, n_tok=n_tok),
        grid=(n_seq_total // n_seq,), in_specs=in_specs, out_specs=out_specs, out_shape=out_shape,
        compiler_params=_params(("arbitrary",)), name="ret_sample",
    )(rq, rk, rv, zg, state, tabs['dec'], tabs['rowd'], tabs['cold'], tabs['cpow'], g)


def _attn_prompt_kernel(qi_ref, ki_ref, q_ref, k_ref, v_ref, km_ref, vm_ref, o_ref, m_scr, l_scr, acc_scr, *,
                        sub_tiles):
    qi = qi_ref[pl.program_id(2)]
    ki = ki_ref[pl.program_id(2)]
    tm, tk = q_ref.shape[0], k_ref.shape[0]
    heads = q_ref.shape[1] // HEAD_PAD

    def pair_lanes(hh):
        return slice((hh // 2) * 2 * MLA_V, (hh // 2 + 1) * 2 * MLA_V)

    @pl.when(ki == 0)
    def _():
        lane = lax.broadcasted_iota(jnp.int32, (tm, km_ref.shape[0]), 1)
        for hh in range(heads):
            sl = slice(hh * HEAD_PAD, (hh + 1) * HEAD_PAD)
            s = jnp.where(lane < N_META, _dot_nt(q_ref[:, sl], km_ref[:, sl]), NEG_BIG)
            m = jnp.max(s, axis=-1, keepdims=True)
            p = jnp.exp2(s - m)
            m_scr[hh] = jnp.broadcast_to(m, (tm, LANES))
            l_scr[hh] = jnp.broadcast_to(jnp.sum(p, axis=-1, keepdims=True), (tm, LANES))
            acc_scr[hh] = _dot(p.astype(BF16), vm_ref[:, pair_lanes(hh)])

    def tile(r0, nr, c0, nc, masked):
        rows = pl.ds(r0, nr)
        if masked:
            keep = (lax.broadcasted_iota(jnp.int32, (nr, nc), 1) <= lax.broadcasted_iota(jnp.int32, (nr, nc), 0))
        for hh in range(heads):
            sl = slice(hh * HEAD_PAD, (hh + 1) * HEAD_PAD)
            s = _dot_nt(q_ref[rows, sl], k_ref[pl.ds(c0, nc), sl])
            if masked:
                s = jnp.where(keep, s, NEG_BIG)
            m_old = m_scr[hh, rows, :]
            m_new = jnp.maximum(m_old, jnp.max(s, axis=-1, keepdims=True))
            alpha = jnp.exp2(m_old - m_new)
            p = jnp.exp2(s - jnp.tile(m_new, (1, nc // LANES)))
            m_scr[hh, rows, :] = m_new
            l_scr[hh, rows, :] = alpha * l_scr[hh, rows, :] + jnp.sum(p, axis=-1, keepdims=True)
            acc_scr[hh, rows, :] = alpha * acc_scr[hh, rows, :] + _dot(p.astype(BF16),
                                                                       v_ref[pl.ds(c0, nc), pair_lanes(hh)])

    @pl.when(ki < qi)
    def _():
        tile(0, tm, 0, tk, False)

    @pl.when(ki == qi)
    def _():
        ns = sub_tiles
        st = tm // ns
        for a in range(ns):
            if a > 0:
                tile(a * st, st, 0, a * st, False)
            tile(a * st, st, a * st, st, True)
        lane = lax.broadcasted_iota(jnp.int32, (tm, 2 * MLA_V), 1)
        for pr in range(heads // 2):
            even = acc_scr[2 * pr] / l_scr[2 * pr]
            odd = acc_scr[2 * pr + 1] / l_scr[2 * pr + 1]
            o_ref[:, pair_lanes(2 * pr)] = jnp.where(lane < MLA_V, even, odd).astype(BF16)


def _attn_prompt(mq, k, v, km, vm, batch, seq, blk):
    nb = seq // blk
    hps = ATTN_HEADS_PER_STEP
    pairs_qk = [(qi, ki) for qi in range(nb) for ki in range(qi + 1)]
    qi_tab = jnp.asarray([p[0] for p in pairs_qk], jnp.int32)
    ki_tab = jnp.asarray([p[1] for p in pairs_qk], jnp.int32)
    in_specs = [pl.BlockSpec((blk, hps * HEAD_PAD), lambda b, h, t, qt, kt: (b * nb + qt[t], h)),
                pl.BlockSpec((blk, hps * HEAD_PAD), lambda b, h, t, qt, kt: (b * nb + kt[t], h)),
                pl.BlockSpec((blk, hps * MLA_V), lambda b, h, t, qt, kt: (b * nb + kt[t], h)),
                pl.BlockSpec((km.shape[0], hps * HEAD_PAD), lambda b, h, t, qt, kt: (0, h)),
                pl.BlockSpec((vm.shape[0], hps * MLA_V), lambda b, h, t, qt, kt: (0, h))]
    grid_spec = pltpu.PrefetchScalarGridSpec(
        num_scalar_prefetch=2, grid=(batch, MLA_HEADS // hps, len(pairs_qk)), in_specs=in_specs,
        out_specs=pl.BlockSpec((blk, hps * MLA_V), lambda b, h, t, qt, kt: (b * nb + qt[t], h)),
        scratch_shapes=[pltpu.VMEM((hps, blk, LANES), F32), pltpu.VMEM((hps, blk, LANES), F32),
                        pltpu.VMEM((hps, blk, 2 * MLA_V), F32)])
    return pl.pallas_call(
        functools.partial(_attn_prompt_kernel, sub_tiles=2 if blk % (2 * LANES) == 0 else 1), grid_spec=grid_spec,
        out_shape=jax.ShapeDtypeStruct((batch * seq, MLA_HEADS * MLA_V), BF16),
        compiler_params=_params(("arbitrary",) * 3), name="attn_prompt",
    )(qi_tab, ki_tab, mq, k, v, km, vm)


def _absorb_kernel(mq_ref, wabs_ref, qt_ref, qf_ref):
    q = mq_ref[...]
    qf_ref[...] = q.astype(F32)
    for h in range(MLA_HEADS):
        qt_ref[:, h * KV_LORA:(h + 1) * KV_LORA] = _dot(q[:, h * HEAD_PAD:(h + 1) * HEAD_PAD], wabs_ref[h])


def _absorb(mq, wabs, row0_blocks, rows, tm):
    return pl.pallas_call(
        _absorb_kernel, grid=(rows // tm,),
        in_specs=[pl.BlockSpec((tm, MLA_HEADS * HEAD_PAD), lambda i: (row0_blocks + i, 0)),
                  pl.BlockSpec(wabs.shape, lambda i: (0, 0, 0))],
        out_specs=[pl.BlockSpec((tm, MLA_HEADS * KV_LORA), lambda i: (i, 0)),
                   pl.BlockSpec((tm, MLA_HEADS * HEAD_PAD), lambda i: (i, 0))],
        out_shape=[jax.ShapeDtypeStruct((rows, MLA_HEADS * KV_LORA), F32),
                   jax.ShapeDtypeStruct((rows, MLA_HEADS * HEAD_PAD), F32)],
        compiler_params=_params(("arbitrary",)), name="absorb",
    )(mq, wabs)


def _attn_sample_kernel(pt_ref, lat_hbm, kpe_hbm, cos_ref, sin_ref, qt_ref, qf_ref, cn_ref, kn_ref, cosn_ref,
                        sinn_ref, lw_ref, wuv_ref, gpe_ref, o_ref, l_scr, qpe_scr, m_scr, d_scr, acc_scr, kpad_scr,
                        cpad_scr, lat_buf, kpe_buf, sem, *, n_pages, sub, n_tok, page):
    s_id = pl.program_id(0)
    j = pl.program_id(1)
    n_steps = pl.num_programs(1)
    step = s_id * n_steps + j
    last = pl.num_programs(0) * n_steps - 1
    slot = lax.rem(step, 2)
    nq = MLA_HEADS * n_tok
    n_up = MLA_HEADS * MLA_NOPE

    def page_copies(pid, sl, p):
        return (pltpu.make_async_copy(lat_hbm.at[0, pid], lat_buf.at[sl, p], sem.at[sl, 0]),
                pltpu.make_async_copy(kpe_hbm.at[0, pid], kpe_buf.at[sl, p], sem.at[sl, 1]))

    def start_pages(seq, st, sl, pages=range(n_pages)):
        for p in pages:
            for c in page_copies(pt_ref[seq, st * n_pages + p], sl, p):
                c.start()

    def wait_pages(sl):
        for p in range(n_pages):
            for c in page_copies(0, sl, p):
                c.wait()

    @pl.when(step == 0)
    def _():
        start_pages(0, 0, 0)

    wait_pages(slot)
    wrap = j == n_steps - 1
    nxt_seq = jnp.where(wrap, jnp.where(step == last, 0, s_id + 1), s_id)
    nxt_j = jnp.where(wrap, 0, j + 1)
    start_pages(nxt_seq, nxt_j, 1 - slot)
    lat_refs = [lat_buf.at[slot, p] for p in range(n_pages)]
    kpe_refs = [kpe_buf.at[slot, p] for p in range(n_pages)]

    @pl.when(j == 0)
    def _():
        l_scr[:n_up, :] = lw_ref[...]
        qt = qt_ref[...]
        qf = qf_ref[...]
        l_scr[n_up:, :] = jnp.concatenate(
            [qt[:, h * KV_LORA:(h + 1) * KV_LORA] for h in range(MLA_HEADS)], axis=0).astype(BF16)
        lane = lax.broadcasted_iota(jnp.int32, (nq, HEAD_PAD), 1)
        qpe = jnp.concatenate([qf[:, h * HEAD_PAD:(h + 1) * HEAD_PAD] for h in range(MLA_HEADS)], axis=0)
        qpe_scr[...] = jnp.where(lane < MLA_ROPE, qpe, 0.0).astype(BF16)
        m_scr[...] = jnp.full(m_scr.shape, NEG_BIG, F32)
        d_scr[...] = jnp.zeros(d_scr.shape, F32)
        acc_scr[...] = jnp.zeros(acc_scr.shape, F32)

    def scores(cb, kpe_t, cos_t, sin_t):
        tk = cb.shape[0]
        big = _dot_nt(l_scr[...], cb)
        k_t = big[:n_up]
        ss = jnp.sum((k_t * k_t).reshape(MLA_HEADS, MLA_NOPE, tk), axis=1)
        ss = ss + jnp.sum(kpe_t * kpe_t, axis=0, keepdims=True)
        rs = lax.rsqrt(ss * (1.0 / MLA_QK) + EPS)
        kg = kpe_t * gpe_ref[...]
        half = MLA_ROPE // 2
        x1, x2 = kg[:half], kg[half:]
        rot = jnp.concatenate([x1 * cos_t - x2 * sin_t, x1 * sin_t + x2 * cos_t], axis=0)
        pe = _dot(qpe_scr[:, :MLA_ROPE], rot.astype(BF16))
        rs_rows = jnp.concatenate([jnp.broadcast_to(rs[h:h + 1, :], (n_tok, tk)) for h in range(MLA_HEADS)], axis=0)
        return (big[n_up:] + pe) * rs_rows

    def update(s_list, cb_list):
        m_old = m_scr[...]
        m_new = m_old
        for s in s_list:
            m_new = jnp.maximum(m_new, jnp.max(s, axis=-1, keepdims=True))
        alpha = jnp.exp2(m_old - m_new)
        d = alpha * d_scr[...]
        acc = alpha * acc_scr[...]
        for s, cb in zip(s_list, cb_list):
            p = jnp.exp2(s - m_new)
            d = d + jnp.sum(p, axis=-1, keepdims=True)
            acc = acc + _dot(p.astype(BF16), cb)
        m_scr[...] = m_new
        d_scr[...] = d
        acc_scr[...] = acc

    s_list, cb_list = [], []
    for g in range(n_pages // sub):
        pages = range(g * sub, (g + 1) * sub)
        cb = jnp.concatenate([lat_refs[b][...] for b in pages], axis=0).astype(BF16)
        kpe_t = jnp.concatenate([kpe_refs[b][...] for b in pages], axis=1)
        cols = slice(g * sub * page, (g + 1) * sub * page)
        s_list.append(scores(cb, kpe_t, cos_ref[:, cols], sin_ref[:, cols]))
        cb_list.append(cb)
    update(s_list, cb_list)

    @pl.when(j == pl.num_programs(1) - 1)
    def _():
        cpad_scr[...] = jnp.zeros(cpad_scr.shape, F32)
        cpad_scr[:n_tok, :] = cn_ref[...]
        kpad_scr[...] = jnp.zeros(kpad_scr.shape, F32)
        kpad_scr[:n_tok, :MLA_ROPE] = kn_ref[...]
        rowi = lax.broadcasted_iota(jnp.int32, (nq, page), 0)
        coli = lax.broadcasted_iota(jnp.int32, (nq, page), 1)
        keep = coli <= (rowi & (n_tok - 1))
        cb = cpad_scr[...].astype(BF16)
        s = scores(cb, kpad_scr[...].T[:MLA_ROPE], cosn_ref[...], sinn_ref[...])
        update([jnp.where(keep, s, NEG_BIG)], [cb])
        ctx = (acc_scr[...] / d_scr[...]).astype(BF16)
        out = jnp.zeros(o_ref.shape, F32)
        for h in range(MLA_HEADS):
            out = out + _dot(ctx, wuv_ref[h])[h * n_tok:(h + 1) * n_tok]
        o_ref[...] = out

    @pl.when(step == last)
    def _():
        wait_pages(1 - slot)


def _attn_sample(page_table, cache_latent, cache_krope_t, tabs, qt, qf, ckv, kpe, w, n_tok, n_pages_step, sub):
    n_seq, n_pages = page_table.shape
    page = cache_latent.shape[2]
    n_steps = n_pages // n_pages_step
    tk = n_pages_step * page
    nq = MLA_HEADS * n_tok

    seq_row = lambda s, j, pt: (s, 0)
    const2 = lambda s, j, pt: (0, 0)
    half = MLA_ROPE // 2
    in_specs = [pl.BlockSpec(memory_space=pl.ANY), pl.BlockSpec(memory_space=pl.ANY)]
    in_specs += [pl.BlockSpec((half, tk), lambda s, j, pt: (0, j)), pl.BlockSpec((half, tk), lambda s, j, pt: (0, j)),
                 pl.BlockSpec((n_tok, MLA_HEADS * KV_LORA), seq_row),
                 pl.BlockSpec((n_tok, MLA_HEADS * HEAD_PAD), seq_row),
                 pl.BlockSpec((n_tok, KV_LORA), seq_row), pl.BlockSpec((n_tok, MLA_ROPE), seq_row),
                 pl.BlockSpec((half, page), const2), pl.BlockSpec((half, page), const2),
                 pl.BlockSpec(w['uk_t'].shape, const2),
                 pl.BlockSpec(w['uv_blk'].shape, lambda s, j, pt: (0, 0, 0)),
                 pl.BlockSpec((MLA_ROPE, 1), const2)]
    grid_spec = pltpu.PrefetchScalarGridSpec(
        num_scalar_prefetch=1, grid=(n_seq, n_steps), in_specs=in_specs,
        out_specs=pl.BlockSpec((n_tok, MLA_HEADS * MLA_V), seq_row),
        scratch_shapes=[pltpu.VMEM((MLA_HEADS * MLA_NOPE + nq, KV_LORA), BF16),
                        pltpu.VMEM((nq, HEAD_PAD), BF16),
                        pltpu.VMEM((nq, 1), F32), pltpu.VMEM((nq, 1), F32), pltpu.VMEM((nq, KV_LORA), F32),
                        pltpu.VMEM((page, LANES), F32), pltpu.VMEM((page, KV_LORA), F32),
                        pltpu.VMEM((2, n_pages_step, page, KV_LORA), F32),
                        pltpu.VMEM((2, n_pages_step, MLA_ROPE, page), F32),
                        pltpu.SemaphoreType.DMA((2, 2))])
    return pl.pallas_call(
        functools.partial(_attn_sample_kernel, n_pages=n_pages_step, sub=sub, n_tok=n_tok, page=page),
        grid_spec=grid_spec, out_shape=jax.ShapeDtypeStruct((n_seq * n_tok, MLA_HEADS * MLA_V), F32),
        compiler_params=_params(("arbitrary", "arbitrary")), name="attn_sample",
    )(page_table, cache_latent, cache_krope_t,
      tabs['cos_t'], tabs['sin_t'], qt, qf, ckv, kpe, tabs['cosn_t'], tabs['sinn_t'],
      w['uk_t'], w['uv_blk'], w['gpe'])


def _post_kernel(ra_ref, rb_ref, ma_ref, mb_ref, xa_ref, xb_ref, wo_ref, g_ref, wrh_ref, wrl_ref, br_ref,
                 h_ref, xn_ref, ti_ref, tg_ref, *, n_a):
    first = pl.program_id(0) < n_a
    r = jnp.where(first, ra_ref[...], rb_ref[...])
    m = jnp.where(first, ma_ref[...], mb_ref[...].astype(BF16))
    half = wo_ref.shape[0] // 2
    mix = _dot(r, wo_ref[:half, :]) + _dot(m, wo_ref[half:, :])
    h = jnp.where(first, xa_ref[...], xb_ref[...]) + mix
    h_ref[...] = h
    xn = _rms(h, g_ref[...])
    hi = xn.astype(BF16)
    bits = lax.bitcast_convert_type(hi.astype(F32), jnp.uint32)
    xn_ref[...] = (bits[:, :D_MODEL // 2] >> 16) | bits[:, D_MODEL // 2:]
    lo = (xn - hi.astype(F32)).astype(BF16)
    wrh = wrh_ref[...]
    work = _dot(hi, wrh) + _dot(hi, wrl_ref[...]) + _dot(lo, wrh) + br_ref[...]
    lane = lax.broadcasted_iota(jnp.int32, work.shape, 1).astype(F32)
    idx = jnp.zeros(work.shape, F32)
    val = jnp.zeros(work.shape, F32)
    vmax = None
    denom = None
    for k in range(TOP_K):
        vk = jnp.max(work, axis=-1, keepdims=True)
        ik = jnp.min(jnp.where(work == vk, lane, float(LANES)), axis=-1, keepdims=True)
        work = jnp.where(lane == ik, -jnp.inf, work)
        if k == 0:
            vmax = vk
        ek = jnp.exp(vk - vmax)
        denom = ek if k == 0 else denom + ek
        idx = jnp.where(lane == float(k), ik, idx)
        val = jnp.where(lane == float(k), ek, val)
    ti_ref[...] = idx.astype(jnp.int32)
    tg_ref[...] = val / denom


def _post(r_pair, m_pair, x_pair, w, tm):
    n_a, n_b = x_pair[0].shape[0] // tm, x_pair[1].shape[0] // tm
    rows = (n_a + n_b) * tm
    row = lambda i: (i, 0)
    const = lambda i: (0, 0)
    in_specs = []
    for a, b in (r_pair, m_pair, x_pair):
        in_specs += [pl.BlockSpec((tm, a.shape[1]), lambda i: (jnp.minimum(i, n_a - 1), 0)),
                     pl.BlockSpec((tm, b.shape[1]), lambda i: (jnp.maximum(i - n_a, 0), 0))]
    in_specs += [pl.BlockSpec(w[n].shape, const) for n in ('wo', 'gffn', 'wr_hi', 'wr_lo', 'br')]
    out_shape = [jax.ShapeDtypeStruct((rows, D_MODEL), F32), jax.ShapeDtypeStruct((rows, D_MODEL // 2), jnp.uint32),
                 jax.ShapeDtypeStruct((rows, LANES), jnp.int32), jax.ShapeDtypeStruct((rows, LANES), F32)]
    out_specs = [pl.BlockSpec((tm, D_MODEL), row), pl.BlockSpec((tm, D_MODEL // 2), row),
                 pl.BlockSpec((tm, LANES), row), pl.BlockSpec((tm, LANES), row)]
    return pl.pallas_call(
        functools.partial(_post_kernel, n_a=n_a), grid=(rows // tm,), in_specs=in_specs, out_specs=out_specs,
        out_shape=out_shape, compiler_params=_params(("arbitrary",)), name="post",
    )(*r_pair, *m_pair, *x_pair, w['wo'], w['gffn'], w['wr_hi'], w['wr_lo'], w['br'])


def _moe_kernel(be_ref, nu_ref, xs_ref, wgu_ref, bgu_ref, wd_ref, bd_ref, o_ref, wgu_b, wd_b):
    i = pl.program_id(0)
    e = be_ref[i]
    e_prev = be_ref[jnp.maximum(i - 1, 0)]

    @pl.when((i == 0) | (e != e_prev))
    def _():
        wgu_b[...] = wgu_ref[...].astype(BF16)
        wd_b[...] = wd_ref[...].astype(BF16)

    @pl.when(i < nu_ref[0])
    def _():
        packed = xs_ref[...]
        x_lo = lax.bitcast_convert_type(packed << 16, F32).astype(BF16)
        x_hi = lax.bitcast_convert_type(packed & jnp.uint32(0xFFFF0000), F32).astype(BF16)
        half = D_MODEL // 2
        hgu = _dot(x_lo, wgu_b[:half, :]) + _dot(x_hi, wgu_b[half:, :]) + bgu_ref[...]
        g = jnp.minimum(hgu[:, :D_FF], SWIGLU_LIMIT)
        u = jnp.clip(hgu[:, D_FF:], -SWIGLU_LIMIT, SWIGLU_LIMIT)
        hid = (u + 1.0) * (g * jax.nn.sigmoid(SWIGLU_ALPHA * g))
        o_ref[...] = _dot(hid.astype(BF16), wd_b[...]) + bd_ref[...]

    @pl.when(i >= nu_ref[0])
    def _():
        o_ref[...] = jnp.zeros(o_ref.shape, F32)


def _moe_blocks(block_e, n_used, xs, w_gate_up, b_gate_up, w_down, b_down, blk):
    n_blocks = block_e.shape[0]
    row = lambda i, be, nu: (i, 0)
    in_specs = [pl.BlockSpec((blk, D_MODEL // 2), row),
                pl.BlockSpec((None, D_MODEL, 2 * D_FF), lambda i, be, nu: (be[i], 0, 0)),
                pl.BlockSpec((None, 1, 2 * D_FF), lambda i, be, nu: (be[i], 0, 0)),
                pl.BlockSpec((None, D_FF, D_MODEL), lambda i, be, nu: (be[i], 0, 0)),
                pl.BlockSpec((None, 1, D_MODEL), lambda i, be, nu: (be[i], 0, 0))]
    grid_spec = pltpu.PrefetchScalarGridSpec(
        num_scalar_prefetch=2, grid=(n_blocks,), in_specs=in_specs,
        out_specs=pl.BlockSpec((blk, D_MODEL), row),
        scratch_shapes=[pltpu.VMEM((D_MODEL, 2 * D_FF), BF16), pltpu.VMEM((D_FF, D_MODEL), BF16)])
    return pl.pallas_call(
        _moe_kernel, grid_spec=grid_spec, out_shape=jax.ShapeDtypeStruct((n_blocks * blk, D_MODEL), F32),
        compiler_params=_params(("arbitrary",)), name="moe",
    )(block_e, n_used, xs, w_gate_up, b_gate_up[:, None, :], w_down, b_down[:, None, :])


def _combine_kernel(h_ref, g_ref, y_ref, o_ref):
    g = g_ref[...]
    acc = y_ref[0] * g[:, 0:1]
    for k in range(1, TOP_K):
        acc = acc + y_ref[k] * g[:, k:k + 1]
    o_ref[...] = h_ref[...] + acc


def _combine(h, gates, y4, block0, rows, tm):
    row = lambda i: (block0 + i, 0)
    return pl.pallas_call(
        _combine_kernel, grid=(rows // tm,),
        in_specs=[pl.BlockSpec((tm, D_MODEL), row), pl.BlockSpec((tm, LANES), row),
                  pl.BlockSpec((TOP_K, tm, D_MODEL), lambda i: (0, block0 + i, 0))],
        out_specs=pl.BlockSpec((tm, D_MODEL), lambda i: (i, 0)),
        out_shape=jax.ShapeDtypeStruct((rows, D_MODEL), F32),
        compiler_params=_params(("arbitrary",)), name="combine",
    )(h, gates, y4)


def _gather_rows(table, idx):
    n, d = idx.shape[0], table.shape[1]
    n_win = n // GATHER_WINDOW
    assert n % (GATHER_WINDOW * SC_CORES * SC_SUBCORES) == 0
    idx = jnp.zeros((n_win, LANES), jnp.int32).at[:, :GATHER_WINDOW].set(idx.reshape(n_win, GATHER_WINDOW))
    mesh = plsc.VectorSubcoreMesh(core_axis_name="core", subcore_axis_name="subcore",
                                  num_cores=SC_CORES, num_subcores=SC_SUBCORES)

    @functools.partial(pl.kernel, out_type=jax.ShapeDtypeStruct((n, d), table.dtype), mesh=mesh)
    def gather(x_hbm, i_hbm, o_hbm):
        def body(i_vmem, o_vmem):
            pltpu.sync_copy(x_hbm.at[i_vmem.at[0, pl.ds(0, GATHER_WINDOW)]], o_vmem)

        pltpu.emit_pipeline(
            body, grid=(n_win,),
            in_specs=[pl.BlockSpec((1, LANES), index_map=lambda i: (i, 0))],
            out_specs=[pl.BlockSpec((GATHER_WINDOW, d), index_map=lambda i: (i, 0))],
            core_axis_name=("core", "subcore"), dimension_semantics=(pltpu.PARALLEL,),
        )(i_hbm, o_hbm)

    return gather(table, idx)


def _route(top_i, blk):
    t = top_i.shape[0]
    a = t * TOP_K
    flat_e = top_i.reshape(a)
    assert N_EXPERTS * a < 2 ** 31
    order = jnp.sort(flat_e.astype(jnp.int32) * a + jnp.arange(a, dtype=jnp.int32)) % a
    tok_sorted = (order // TOP_K).astype(jnp.int32)
    experts = jnp.arange(N_EXPERTS, dtype=jnp.int32)
    hit = (top_i[:, :, None] == experts[None, None, :]).astype(jnp.int32)
    per_tok = hit.sum(axis=1)
    before = jnp.cumsum(per_tok, axis=0) - per_tok
    counts = per_tok.sum(axis=0)
    padded = (counts + blk - 1) // blk * blk
    start = jnp.cumsum(counts) - counts
    pend = jnp.cumsum(padded)
    pstart = pend - padded
    dest = ((before + pstart[None, :])[:, None, :] * hit).sum(axis=2)
    n_blocks = -(-a // blk) + N_EXPERTS
    block_e = jnp.minimum((pend[None, :] <= (jnp.arange(n_blocks, dtype=jnp.int32) * blk)[:, None]).sum(axis=1),
                          N_EXPERTS - 1).astype(jnp.int32)
    slot = jnp.arange(n_blocks * blk, dtype=jnp.int32)
    slot_e = jnp.repeat(block_e, blk)
    r = slot - pstart[slot_e]
    slot_tok = jnp.where(r < counts[slot_e], tok_sorted[jnp.minimum(start[slot_e] + r, a - 1)], slot % t)
    n_used = (pend[-1:] // blk).astype(jnp.int32)
    return slot_tok, dest.astype(jnp.int32), block_e, n_used


def _rope_tables(pos):
    pos = pos.astype(F32)[:, None]
    n = pos.shape[0]
    hr = RET_DK // 2
    ang = pos * (RET_THETA ** (-jnp.arange(hr, dtype=F32) / hr))[None, :]
    cos, sin = jnp.cos(ang), jnp.sin(ang)
    zr = jnp.zeros((n, LANES - RET_DK), F32)
    cr = jnp.concatenate([cos, cos, zr], axis=1)
    sr = jnp.concatenate([-sin, sin, zr], axis=1)
    hm = MLA_ROPE // 2
    ang = pos * (MLA_THETA ** (-jnp.arange(hm, dtype=F32) / hm))[None, :]
    cos, sin = jnp.cos(ang), jnp.sin(ang)
    cm = jnp.concatenate([cos, cos, jnp.ones((n, LANES - MLA_ROPE), F32)], axis=1)
    sma = jnp.concatenate([jnp.zeros((n, hm), F32), sin, jnp.zeros((n, LANES - MLA_ROPE), F32)], axis=1)
    smb = jnp.concatenate([-sin, jnp.zeros((n, LANES - hm), F32)], axis=1)
    return [cr, sr, cm, sma, smb]


def _rope_tables_t(pos):
    hm = MLA_ROPE // 2
    ang = pos.astype(F32)[:, None] * (MLA_THETA ** (-jnp.arange(hm, dtype=F32) / hm))[None, :]
    return jnp.cos(ang).T, jnp.sin(ang).T


def _decay_tables(log_gamma, c, n_rep):
    idx = jnp.arange(c, dtype=F32)
    diff = idx[:, None] - idx[None, :]
    decay = jnp.where(diff >= 0, jnp.exp(jnp.maximum(diff, 0.0)[None] * log_gamma[:, None, None]), 0.0)
    if n_rep > 1:
        eye = jnp.eye(n_rep, dtype=F32)
        decay = (eye[None, :, None, :, None] * decay[:, None, :, None, :]).reshape(RET_HEADS, n_rep * c, n_rep * c)
    rowd = jnp.tile(jnp.exp((idx + 1.0)[:, None] * log_gamma[None, :]), (n_rep, 1))
    cold = jnp.tile(jnp.exp((c - 1.0 - idx)[:, None] * log_gamma[None, :]), (n_rep, 1))
    cpow = jnp.broadcast_to(jnp.exp(c * log_gamma)[None, :], (LANES, RET_HEADS))
    return {'dec': decay, 'rowd': rowd, 'cold': cold, 'cpow': cpow}


def _pad_heads(wm, n_heads, width, offset):
    k = wm.shape[0]
    wm = wm.reshape(k, n_heads, width)
    out = jnp.zeros((k, n_heads, LANES), wm.dtype).at[:, :, offset:offset + width].set(wm)
    return out.reshape(k, n_heads * LANES)


def _prep_weights(norm_attn_g, w_in, ret_out_g, q_a_norm_g, w_q_b, kv_a_norm_g, w_uk, w_uv,
                  qk_norm_q_g, qk_norm_k_g, w_out, norm_ffn_g, w_router, b_router):
    splits = [RET_HEADS * RET_DK, RET_HEADS * RET_DK, RET_HEADS * RET_DV, RET_HEADS * RET_DV, Q_LORA, KV_LORA,
              MLA_ROPE]
    offs = [0]
    for s in splits:
        offs.append(offs[-1] + s)
    part = [w_in[:, offs[i]:offs[i + 1]] for i in range(len(splits))]
    kpe_cols = jnp.zeros((D_MODEL, LANES), F32).at[:, :MLA_ROPE].set(part[6])
    win = jnp.concatenate([_pad_heads(part[0], RET_HEADS, RET_DK, 0), _pad_heads(part[1], RET_HEADS, RET_DK, 0),
                           part[2], part[3], part[4], part[5], kpe_cols], axis=1).astype(BF16)

    def mla_cols(wm):
        k = wm.shape[0]
        wm = wm.reshape(k, MLA_HEADS, MLA_QK)
        wm = jnp.concatenate([wm[:, :, MLA_NOPE:], wm[:, :, :MLA_NOPE]], axis=2)
        return _pad_heads(wm.reshape(k, MLA_HEADS * MLA_QK), MLA_HEADS, MLA_QK, 0)

    def mla_gain(g):
        g = jnp.concatenate([g[MLA_NOPE:], g[:MLA_NOPE], jnp.zeros((LANES - MLA_QK,), F32)])
        return g[None, :]

    gk_nope = qk_norm_k_g[:MLA_NOPE]
    uk_heads = w_uk.reshape(KV_LORA, MLA_HEADS, MLA_NOPE)
    wabs = jnp.zeros((MLA_HEADS, HEAD_PAD, KV_LORA), F32).at[:, MLA_ROPE:MLA_ROPE + MLA_NOPE, :].set(
        (uk_heads * gk_nope[None, None, :]).transpose(1, 2, 0))
    head_of_col = jnp.arange(MLA_HEADS * MLA_V) // MLA_V
    uv_blk = jnp.where(head_of_col[None, None, :] == jnp.arange(MLA_HEADS)[:, None, None], w_uv[None], 0.0)
    wr = jnp.zeros((D_MODEL, LANES), F32).at[:, :N_EXPERTS].set(w_router)
    wr_hi = wr.astype(BF16)
    return {
        'gattn': norm_attn_g[None, :], 'win': win, 'gqa': q_a_norm_g[None, :], 'wqb': mla_cols(w_q_b).astype(BF16),
        'gq': mla_gain(qk_norm_q_g), 'gkv': kv_a_norm_g[None, :],
        'wuk': _pad_heads(w_uk, MLA_HEADS, MLA_NOPE, MLA_ROPE).astype(BF16), 'gk': mla_gain(qk_norm_k_g),
        'wuv': w_uv.astype(BF16), 'gret': ret_out_g[None, :],
        'wabs': wabs.astype(BF16), 'uk_t': w_uk.T.astype(BF16), 'uv_blk': uv_blk.astype(BF16),
        'gpe': qk_norm_k_g[MLA_NOPE:, None],
        'wo': w_out.astype(BF16), 'gffn': norm_ffn_g[None, :], 'wr_hi': wr_hi,
        'wr_lo': (wr - wr_hi.astype(F32)).astype(BF16),
        'br': jnp.full((1, LANES), NEG_BIG, F32).at[0, :N_EXPERTS].set(b_router),
    }


def _pad_rows(a, rows):
    return jnp.zeros((rows,) + a.shape[1:], a.dtype).at[:a.shape[0]].set(a)


def _largest_divisor(n, cap):
    d = min(n, cap)
    while n % d:
        d -= 1
    return d


def kernel(x_prompt, x_sample, cache_latent, cache_krope, state_retention, page_table, meta_tokens, norm_attn_g, w_in, ret_out_g, q_a_norm_g, w_q_b, kv_a_norm_g, w_uk, w_uv, qk_norm_q_g, qk_norm_k_g, w_out, norm_ffn_g, w_router, b_router, w_gate_up, b_gate_up, w_down, b_down):
    assert w_in.shape[0] == 1, "single-layer trunk"
    batch, seq, _ = x_prompt.shape
    n_seq, n_tok, _ = x_sample.shape
    n_pages, page = page_table.shape[1], cache_latent.shape[2]
    past = n_pages * page
    assert seq % RET_CHUNK == 0 and LANES % n_tok == 0 and (n_seq * n_tok) % LANES == 0 and page == LANES
    w = _prep_weights(norm_attn_g[0], w_in[0], ret_out_g[0], q_a_norm_g[0], w_q_b[0], kv_a_norm_g[0], w_uk[0],
                      w_uv[0], qk_norm_q_g[0], qk_norm_k_g[0], w_out[0], norm_ffn_g[0], w_router[0], b_router[0])
    log_gamma = jnp.log1p(-jnp.exp2(-5.0 - jnp.arange(RET_HEADS, dtype=F32)))

    rows_p = batch * seq
    rows_s = n_seq * n_tok
    tm = _largest_divisor(min(seq, rows_s), ROW_BLOCK)
    nb_seq = seq // tm
    x_pair = (x_prompt.reshape(rows_p, D_MODEL), x_sample.reshape(rows_s, D_MODEL))
    pos_rows = jnp.concatenate([N_META + jnp.arange(seq), jnp.tile(past + jnp.arange(n_tok), tm // n_tok)])
    tabs = _rope_tables(pos_rows)
    n_pb = rows_p // tm
    rq, rk, rv, zg, mq, ckv, kpe, k, v = _front(
        *x_pair, tabs, lambda i: jnp.where(i < n_pb, i % nb_seq, nb_seq), w, tm)
    _, mrk, mrv, _, _, mckv, mkpe, mk, mv = _front(
        meta_tokens, None, _rope_tables(jnp.arange(N_META)), lambda i: i, w, N_META)

    dt_p = _decay_tables(log_gamma, RET_CHUNK, 1)
    dt_p['mcol'] = _pad_rows(jnp.exp((N_META - 1.0 - jnp.arange(N_META, dtype=F32))[:, None] * log_gamma[None, :]),
                             RET_CHUNK)
    gret = w['gret']
    r_p, st_p = _ret_prompt(rq, rk, rv, zg, _pad_rows(mrk, RET_CHUNK), _pad_rows(mrv, RET_CHUNK), dt_p, gret,
                            batch, seq // RET_CHUNK)
    dt_s = _decay_tables(log_gamma, n_tok, LANES // n_tok)
    r_s, st_s = _ret_sample(rq[rows_p:], rk[rows_p:], rv[rows_p:], zg[rows_p:], state_retention[0], dt_s, gret, n_tok)

    blk = _largest_divisor(seq, ATTN_BLOCK)
    m_p = _attn_prompt(mq, k, v, _pad_rows(mk, LANES), _pad_rows(mv, LANES), batch, seq, blk)
    tm_s = _largest_divisor(rows_s, ROW_BLOCK)
    qt, qf = _absorb(mq, w['wabs'], rows_p // tm_s, rows_s, tm_s)
    n_pages_step = _largest_divisor(n_pages, PAGES_PER_STEP)
    cos_t, sin_t = _rope_tables_t(jnp.arange(past))
    cosn_t, sinn_t = _rope_tables_t(past + jnp.arange(page))
    tabs_s = {'cos_t': cos_t, 'sin_t': sin_t, 'cosn_t': cosn_t, 'sinn_t': sinn_t}
    m_s = _attn_sample(page_table, cache_latent, jnp.swapaxes(cache_krope, 2, 3), tabs_s, qt, qf, ckv[rows_p:],
                       kpe[rows_p:], w, n_tok, n_pages_step, _largest_divisor(n_pages_step, PAGES_PER_CHAIN))

    h1, xn2, top_i, gates = _post((r_p, r_s), (m_p, m_s), x_pair, w, tm)

    slot_tok, dest, block_e, n_used = _route(top_i[:, :TOP_K], EXPERT_BLOCK)
    outs = _moe_blocks(block_e, n_used, _gather_rows(xn2, slot_tok), w_gate_up[0], b_gate_up[0], w_down[0],
                       b_down[0], EXPERT_BLOCK)
    y4 = _gather_rows(outs, dest.T.reshape(-1)).reshape(TOP_K, rows_p + rows_s, D_MODEL)
    y_prompt = _combine(h1, gates, y4, 0, rows_p, tm).reshape(batch, seq, D_MODEL)
    y_sample = _combine(h1, gates, y4, n_pb, rows_s, tm).reshape(n_seq, n_tok, D_MODEL)
    lat_p = jnp.concatenate([jnp.broadcast_to(mckv[None], (batch, N_META, KV_LORA)),
                             ckv[:rows_p].reshape(batch, seq, KV_LORA)], axis=1)[None]
    kpe_p = jnp.concatenate([jnp.broadcast_to(mkpe[None], (batch, N_META, MLA_ROPE)),
                             kpe[:rows_p].reshape(batch, seq, MLA_ROPE)], axis=1)[None]
    return (y_prompt, y_sample, lat_p, kpe_p, st_p[None],
            ckv[rows_p:].reshape(n_seq, n_tok, KV_LORA)[None], kpe[rows_p:].reshape(n_seq, n_tok, MLA_ROPE)[None],
            st_s[None])
```

```python
import functools

import jax
import jax.numpy as jnp
from jax import lax
from jax.experimental import pallas as pl
from jax.experimental.pallas import tpu as pltpu
from jax.experimental.pallas import tpu_sc as plsc

F32 = jnp.float32
BF16 = jnp.bfloat16

D_MODEL = 1024
N_META = 16
RET_HEADS = 4
RET_DK = 64
RET_DV = 128
RET_CHUNK = 128
RET_THETA = 10000.0
MLA_HEADS = 8
MLA_NOPE = 64
MLA_ROPE = 32
MLA_QK = MLA_NOPE + MLA_ROPE
MLA_V = 64
Q_LORA = 384
KV_LORA = 256
MLA_THETA = 10000.0
MLA_SCALE = MLA_QK ** -0.5
LOG2E = 1.4426950408889634
N_EXPERTS = 32
TOP_K = 4
D_FF = 1024
SWIGLU_LIMIT = 7.0
SWIGLU_ALPHA = 1.702
EPS = 1e-6
NEG_BIG = -1e30

LANES = 128
HEAD_PAD = LANES
VMEM_LIMIT = 56 * 1024 * 1024
ROW_BLOCK = 512
ATTN_BLOCK = 1024
ATTN_HEADS_PER_STEP = 8
RET_CHUNKS_PER_STEP = 4
PAGES_PER_STEP = 64
PAGES_PER_CHAIN = 32
EXPERT_BLOCK = 512
SC_CORES, SC_SUBCORES = 2, 16
GATHER_BUFFER_BYTES = 128 * 1024

_OFF_RQ = 0
_OFF_RK = _OFF_RQ + RET_HEADS * LANES
_OFF_RV = _OFF_RK + RET_HEADS * LANES
_OFF_ZG = _OFF_RV + RET_HEADS * RET_DV
_OFF_CQ = _OFF_ZG + RET_HEADS * RET_DV
_OFF_CKV = _OFF_CQ + Q_LORA
_OFF_KPE = _OFF_CKV + KV_LORA
IN_PAD = _OFF_KPE + LANES


def _params(sem):
    return pltpu.CompilerParams(dimension_semantics=sem, vmem_limit_bytes=VMEM_LIMIT)


def _rms(x, g):
    return x * lax.rsqrt(jnp.mean(x * x, axis=-1, keepdims=True) + EPS) * g


def _dot(a, b):
    return jnp.dot(a, b, preferred_element_type=F32)


def _dot_nt(a, b):
    return lax.dot_general(a, b, (((1,), (1,)), ((), ())), preferred_element_type=F32)


def _dot_tn(a, b):
    return lax.dot_general(a, b, (((0,), (0,)), ((), ())), preferred_element_type=F32)


def _front_kernel(xa_ref, xb_ref, gattn_ref, win_ref, cr_ref, sr_ref, cm_ref, sma_ref, smb_ref,
                  gqa_ref, wqb_ref, gq_ref, gkv_ref, wuk_ref, gk_ref, wuv_ref,
                  rq_ref, rk_ref, rv_ref, zg_ref, mq_ref, ckv_ref, kpe_ref, k_ref, v_ref, *, n_a):
    tm = xa_ref.shape[0]
    x = jnp.where(pl.program_id(0) < n_a, xa_ref[...], xb_ref[...])
    xn = _rms(x, gattn_ref[...])
    z = _dot(xn.astype(BF16), win_ref[...])

    lane = lax.broadcasted_iota(jnp.int32, (tm, LANES), 1)
    upper = (lane & (RET_DK // 2)) != 0
    cr, sr = cr_ref[...], sr_ref[...]

    def rope_ret(t):
        partner = jnp.where(upper, pltpu.roll(t, RET_DK // 2, 1), pltpu.roll(t, LANES - RET_DK // 2, 1))
        return t * cr + partner * sr

    for h in range(RET_HEADS):
        sl = slice(h * LANES, (h + 1) * LANES)
        rq_ref[:, sl] = rope_ret(z[:, _OFF_RQ + h * LANES:_OFF_RQ + (h + 1) * LANES])
        rk_ref[:, sl] = rope_ret(z[:, _OFF_RK + h * LANES:_OFF_RK + (h + 1) * LANES]) * (RET_DK ** -0.5)
    rv_ref[...] = z[:, _OFF_RV:_OFF_ZG].astype(BF16)
    zg_ref[...] = z[:, _OFF_ZG:_OFF_CQ]

    cm, sma, smb = cm_ref[...], sma_ref[...], smb_ref[...]

    def rope_mla(t):
        half = MLA_ROPE // 2
        return t * cm + pltpu.roll(t, half, 1) * sma + pltpu.roll(t, LANES - half, 1) * smb

    def head_norm(t, g):
        ms = jnp.sum(t * t, axis=-1, keepdims=True) * (1.0 / MLA_QK)
        return t * lax.rsqrt(ms + EPS) * g

    cq = _rms(z[:, _OFF_CQ:_OFF_CKV], gqa_ref[...])
    q = _dot(cq.astype(BF16), wqb_ref[...])
    gq = gq_ref[...]
    for h in range(MLA_HEADS):
        sl = slice(h * HEAD_PAD, (h + 1) * HEAD_PAD)
        mq_ref[:, sl] = (rope_mla(head_norm(q[:, sl], gq)) * (MLA_SCALE * LOG2E)).astype(BF16)

    ckv = _rms(z[:, _OFF_CKV:_OFF_KPE], gkv_ref[...])
    ckv_ref[...] = ckv
    kpe_slab = z[:, _OFF_KPE:IN_PAD]
    kpe_ref[...] = kpe_slab[:, :MLA_ROPE]
    ckv_b = ckv.astype(BF16)
    kn = _dot(ckv_b, wuk_ref[...])
    gk = gk_ref[...]
    for h in range(MLA_HEADS):
        sl = slice(h * HEAD_PAD, (h + 1) * HEAD_PAD)
        k_ref[:, sl] = rope_mla(head_norm(kn[:, sl] + kpe_slab, gk)).astype(BF16)
    v_ref[...] = _dot(ckv_b, wuv_ref[...]).astype(BF16)


def _front(xa, xb, tabs, tab_index, w, tm):
    n_a, n_b = xa.shape[0] // tm, (0 if xb is None else xb.shape[0] // tm)
    xb = xa if xb is None else xb
    rows = (n_a + n_b) * tm
    grid = (n_a + n_b,)
    row = lambda i: (i, 0)
    const = lambda i: (0, 0)
    tab = lambda i: (tab_index(i), 0)

    def full(a):
        return pl.BlockSpec(a.shape, const)

    in_specs = [pl.BlockSpec((tm, D_MODEL), lambda i: (jnp.minimum(i, n_a - 1), 0)),
                pl.BlockSpec((tm, D_MODEL), lambda i: (jnp.maximum(i - n_a, 0), 0)), full(w['gattn']), full(w['win'])]
    in_specs += [pl.BlockSpec((tm, LANES), tab)] * 5
    in_specs += [full(w[n]) for n in ('gqa', 'wqb', 'gq', 'gkv', 'wuk', 'gk', 'wuv')]
    widths = [(RET_HEADS * LANES, F32), (RET_HEADS * LANES, F32), (RET_HEADS * RET_DV, BF16),
              (RET_HEADS * RET_DV, F32), (MLA_HEADS * HEAD_PAD, BF16), (KV_LORA, F32), (MLA_ROPE, F32),
              (MLA_HEADS * HEAD_PAD, BF16), (MLA_HEADS * MLA_V, BF16)]
    out_shape = [jax.ShapeDtypeStruct((rows, n), dt) for n, dt in widths]
    out_specs = [pl.BlockSpec((tm, n), row) for n, _ in widths]
    return pl.pallas_call(
        functools.partial(_front_kernel, n_a=n_a), grid=grid, in_specs=in_specs, out_specs=out_specs,
        out_shape=out_shape, compiler_params=_params(("arbitrary",)), name="front",
    )(xa, xb, w['gattn'], w['win'], *tabs, w['gqa'], w['wqb'], w['gq'], w['gkv'], w['wuk'], w['gk'], w['wuv'])


def _ret_gate(o, zg, g):
    on = o * lax.rsqrt(jnp.mean(o * o, axis=-1, keepdims=True) + EPS) * g
    return (zg * jax.nn.sigmoid(zg)) * on


def _ret_prompt_kernel(q_ref, k_ref, v_ref, zg_ref, mk_ref, mv_ref, mcol_ref, dec_ref, rowd_ref, cold_ref,
                       cpow_ref, g_ref, r_ref, s_ref, s_scr):
    c = pl.program_id(1)

    @pl.when(c == 0)
    def _():
        for h in range(RET_HEADS):
            sl = slice(h * LANES, (h + 1) * LANES)
            kw = mk_ref[:, sl] * mcol_ref[:, h:h + 1]
            s_scr[h] = _dot_tn(kw.astype(BF16), mv_ref[:, sl])

    for h in range(RET_HEADS):
        sl = slice(h * LANES, (h + 1) * LANES)
        s0 = s_scr[h]
        for cc in range(q_ref.shape[0] // RET_CHUNK):
            rows = slice(cc * RET_CHUNK, (cc + 1) * RET_CHUNK)
            k = k_ref[rows, sl]
            v = v_ref[rows, sl]
            qb = q_ref[rows, sl].astype(BF16)
            scores = _dot_nt(qb, k.astype(BF16)) * dec_ref[h]
            inner = _dot(scores.astype(BF16), v)
            cross = _dot(qb, s0.astype(BF16)) * rowd_ref[:, h:h + 1]
            kw = k * cold_ref[:, h:h + 1]
            s0 = s0 * cpow_ref[:, h:h + 1] + _dot_tn(kw.astype(BF16), v)
            r_ref[rows, sl] = _ret_gate(inner + cross, zg_ref[rows, sl], g_ref[:, sl]).astype(BF16)
        s_scr[h] = s0

    @pl.when(c == pl.num_programs(1) - 1)
    def _():
        for h in range(RET_HEADS):
            s_ref[0, h] = s_scr[h, :RET_DK, :]


def _ret_prompt(rq, rk, rv, zg, mk, mv, tabs, g, batch, n_chunks):
    cs = RET_CHUNK
    cps = _largest_divisor(n_chunks, RET_CHUNKS_PER_STEP)
    steps = n_chunks // cps
    row = lambda b, c: (b * steps + c, 0)
    const2 = lambda b, c: (0, 0)
    w4 = RET_HEADS * LANES
    in_specs = [pl.BlockSpec((cps * cs, w4), row), pl.BlockSpec((cps * cs, w4), row),
                pl.BlockSpec((cps * cs, w4), row), pl.BlockSpec((cps * cs, w4), row),
                pl.BlockSpec((cs, w4), const2), pl.BlockSpec((cs, w4), const2),
                pl.BlockSpec((cs, RET_HEADS), const2),
                pl.BlockSpec((RET_HEADS, cs, cs), lambda b, c: (0, 0, 0)),
                pl.BlockSpec((cs, RET_HEADS), const2), pl.BlockSpec((cs, RET_HEADS), const2),
                pl.BlockSpec((LANES, RET_HEADS), const2), pl.BlockSpec((1, w4), const2)]
    out_shape = [jax.ShapeDtypeStruct((batch * n_chunks * cs, w4), BF16),
                 jax.ShapeDtypeStruct((batch, RET_HEADS, RET_DK, RET_DV), F32)]
    out_specs = [pl.BlockSpec((cps * cs, w4), row),
                 pl.BlockSpec((1, RET_HEADS, RET_DK, RET_DV), lambda b, c: (b, 0, 0, 0))]
    return pl.pallas_call(
        _ret_prompt_kernel, grid=(batch, steps), in_specs=in_specs, out_specs=out_specs, out_shape=out_shape,
        scratch_shapes=[pltpu.VMEM((RET_HEADS, LANES, RET_DV), F32)],
        compiler_params=_params(("arbitrary", "arbitrary")), name="ret_prompt",
    )(rq, rk, rv, zg, mk, mv, tabs['mcol'], tabs['dec'], tabs['rowd'], tabs['cold'], tabs['cpow'], g)


def _ret_sample_kernel(q_ref, k_ref, v_ref, zg_ref, s0_ref, dec_ref, rowd_ref, cold_ref, cpow_ref, g_ref,
                       r_ref, s_ref, *, n_seq, n_tok):
    rows = n_seq * n_tok
    ri = lax.broadcasted_iota(jnp.int32, (rows, 1), 0)
    for h in range(RET_HEADS):
        sl = slice(h * LANES, (h + 1) * LANES)
        q = q_ref[:, sl]
        k = k_ref[:, sl]
        v = v_ref[:, sl]
        qb = q.astype(BF16)
        scores = _dot_nt(qb, k.astype(BF16)) * dec_ref[h]
        inner = _dot(scores.astype(BF16), v)
        kw = k * cold_ref[:, h:h + 1]
        cross = jnp.zeros((rows, RET_DV), F32)
        for s in range(n_seq):
            s0 = s0_ref[s, h]
            mine = (ri >= s * n_tok) & (ri < (s + 1) * n_tok)
            cross = cross + jnp.where(mine, _dot(qb[:, :RET_DK], s0.astype(BF16)), 0.0)
            upd = _dot_tn(jnp.where(mine, kw, 0.0).astype(BF16), v)
            s_ref[s, h] = s0 * cpow_ref[:RET_DK, h:h + 1] + upd[:RET_DK]
        cross = cross * rowd_ref[:, h:h + 1]
        r_ref[:, sl] = _ret_gate(inner + cross, zg_ref[:, sl], g_ref[:, sl]).astype(BF16)


def _ret_sample(rq, rk, rv, zg, state, tabs, g, n_tok):
    n_seq_total = state.shape[0]
    n_seq = LANES // n_tok
    rows = n_seq * n_tok
    w4 = RET_HEADS * LANES
    row = lambda i: (i, 0)
    const = lambda i: (0, 0)
    in_specs = [pl.BlockSpec((rows, w4), row)] * 4
    in_specs += [pl.BlockSpec((n_seq, RET_HEADS, RET_DK, RET_DV), lambda i: (i, 0, 0, 0)),
                 pl.BlockSpec((RET_HEADS, rows, rows), lambda i: (0, 0, 0)),
                 pl.BlockSpec((rows, RET_HEADS), const), pl.BlockSpec((rows, RET_HEADS), const),
                 pl.BlockSpec((LANES, RET_HEADS), const), pl.BlockSpec((1, w4), const)]
    out_shape = [jax.ShapeDtypeStruct((n_seq_total * n_tok, w4), BF16),
                 jax.ShapeDtypeStruct(state.shape, F32)]
    out_specs = [pl.BlockSpec((rows, w4), row),
                 pl.BlockSpec((n_seq, RET_HEADS, RET_DK, RET_DV), lambda i: (i, 0, 0, 0))]
    return pl.pallas_call(
        functools.partial(_ret_sample_kernel, n_seq=n_seq, n_tok=n_tok),
        grid=(n_seq_total // n_seq,), in_specs=in_specs, out_specs=out_specs, out_shape=out_shape,
        compiler_params=_params(("arbitrary",)), name="ret_sample",
    )(rq, rk, rv, zg, state, tabs['dec'], tabs['rowd'], tabs['cold'], tabs['cpow'], g)


def _attn_prompt_kernel(qi_ref, ki_ref, q_ref, k_ref, v_ref, km_ref, vm_ref, o_ref, m_scr, l_scr, acc_scr, *,
                        sub_tiles):
    qi = qi_ref[pl.program_id(2)]
    ki = ki_ref[pl.program_id(2)]
    tm, tk = q_ref.shape[0], k_ref.shape[0]
    heads = q_ref.shape[1] // HEAD_PAD

    def pair_lanes(hh):
        return slice((hh // 2) * 2 * MLA_V, (hh // 2 + 1) * 2 * MLA_V)

    @pl.when(ki == 0)
    def _():
        lane = lax.broadcasted_iota(jnp.int32, (tm, km_ref.shape[0]), 1)
        for hh in range(heads):
            sl = slice(hh * HEAD_PAD, (hh + 1) * HEAD_PAD)
            s = jnp.where(lane < N_META, _dot_nt(q_ref[:, sl], km_ref[:, sl]), NEG_BIG)
            m = jnp.max(s, axis=-1, keepdims=True)
            p = jnp.exp2(s - m)
            m_scr[hh] = jnp.broadcast_to(m, (tm, LANES))
            l_scr[hh] = jnp.broadcast_to(jnp.sum(p, axis=-1, keepdims=True), (tm, LANES))
            acc_scr[hh] = _dot(p.astype(BF16), vm_ref[:, pair_lanes(hh)])

    def tile(r0, nr, c0, nc, masked):
        rows = pl.ds(r0, nr)
        if masked:
            keep = (lax.broadcasted_iota(jnp.int32, (nr, nc), 1) <= lax.broadcasted_iota(jnp.int32, (nr, nc), 0))
        for hh in range(heads):
            sl = slice(hh * HEAD_PAD, (hh + 1) * HEAD_PAD)
            s = _dot_nt(q_ref[rows, sl], k_ref[pl.ds(c0, nc), sl])
            if masked:
                s = jnp.where(keep, s, NEG_BIG)
            m_old = m_scr[hh, rows, :]
            m_new = jnp.maximum(m_old, jnp.max(s, axis=-1, keepdims=True))
            alpha = jnp.exp2(m_old - m_new)
            p = jnp.exp2(s - jnp.tile(m_new, (1, nc // LANES)))
            m_scr[hh, rows, :] = m_new
            l_scr[hh, rows, :] = alpha * l_scr[hh, rows, :] + jnp.sum(p, axis=-1, keepdims=True)
            acc_scr[hh, rows, :] = alpha * acc_scr[hh, rows, :] + _dot(p.astype(BF16),
                                                                       v_ref[pl.ds(c0, nc), pair_lanes(hh)])

    @pl.when(ki < qi)
    def _():
        tile(0, tm, 0, tk, False)

    @pl.when(ki == qi)
    def _():
        ns = sub_tiles
        st = tm // ns
        for a in range(ns):
            if a > 0:
                tile(a * st, st, 0, a * st, False)
            tile(a * st, st, a * st, st, True)
        lane = lax.broadcasted_iota(jnp.int32, (tm, 2 * MLA_V), 1)
        for pr in range(heads // 2):
            even = acc_scr[2 * pr] / l_scr[2 * pr]
            odd = acc_scr[2 * pr + 1] / l_scr[2 * pr + 1]
            o_ref[:, pair_lanes(2 * pr)] = jnp.where(lane < MLA_V, even, odd).astype(BF16)


def _attn_prompt(mq, k, v, km, vm, batch, seq, blk):
    nb = seq // blk
    hps = ATTN_HEADS_PER_STEP
    pairs_qk = [(qi, ki) for qi in range(nb) for ki in range(qi + 1)]
    qi_tab = jnp.asarray([p[0] for p in pairs_qk], jnp.int32)
    ki_tab = jnp.asarray([p[1] for p in pairs_qk], jnp.int32)
    in_specs = [pl.BlockSpec((blk, hps * HEAD_PAD), lambda b, h, t, qt, kt: (b * nb + qt[t], h)),
                pl.BlockSpec((blk, hps * HEAD_PAD), lambda b, h, t, qt, kt: (b * nb + kt[t], h)),
                pl.BlockSpec((blk, hps * MLA_V), lambda b, h, t, qt, kt: (b * nb + kt[t], h)),
                pl.BlockSpec((km.shape[0], hps * HEAD_PAD), lambda b, h, t, qt, kt: (0, h)),
                pl.BlockSpec((vm.shape[0], hps * MLA_V), lambda b, h, t, qt, kt: (0, h))]
    grid_spec = pltpu.PrefetchScalarGridSpec(
        num_scalar_prefetch=2, grid=(batch, MLA_HEADS // hps, len(pairs_qk)), in_specs=in_specs,
        out_specs=pl.BlockSpec((blk, hps * MLA_V), lambda b, h, t, qt, kt: (b * nb + qt[t], h)),
        scratch_shapes=[pltpu.VMEM((hps, blk, LANES), F32), pltpu.VMEM((hps, blk, LANES), F32),
                        pltpu.VMEM((hps, blk, 2 * MLA_V), F32)])
    return pl.pallas_call(
        functools.partial(_attn_prompt_kernel, sub_tiles=2 if blk % (2 * LANES) == 0 else 1), grid_spec=grid_spec,
        out_shape=jax.ShapeDtypeStruct((batch * seq, MLA_HEADS * MLA_V), BF16),
        compiler_params=_params(("arbitrary",) * 3), name="attn_prompt",
    )(qi_tab, ki_tab, mq, k, v, km, vm)


def _absorb_kernel(mq_ref, wabs_ref, qt_ref, qf_ref):
    q = mq_ref[...]
    qf_ref[...] = q.astype(F32)
    for h in range(MLA_HEADS):
        qt_ref[:, h * KV_LORA:(h + 1) * KV_LORA] = _dot(q[:, h * HEAD_PAD:(h + 1) * HEAD_PAD], wabs_ref[h])


def _absorb(mq, wabs, row0_blocks, rows, tm):
    return pl.pallas_call(
        _absorb_kernel, grid=(rows // tm,),
        in_specs=[pl.BlockSpec((tm, MLA_HEADS * HEAD_PAD), lambda i: (row0_blocks + i, 0)),
                  pl.BlockSpec(wabs.shape, lambda i: (0, 0, 0))],
        out_specs=[pl.BlockSpec((tm, MLA_HEADS * KV_LORA), lambda i: (i, 0)),
                   pl.BlockSpec((tm, MLA_HEADS * HEAD_PAD), lambda i: (i, 0))],
        out_shape=[jax.ShapeDtypeStruct((rows, MLA_HEADS * KV_LORA), F32),
                   jax.ShapeDtypeStruct((rows, MLA_HEADS * HEAD_PAD), F32)],
        compiler_params=_params(("arbitrary",)), name="absorb",
    )(mq, wabs)


def _attn_sample_kernel(pt_ref, lat_hbm, kpe_hbm, cos_ref, sin_ref, qt_ref, qf_ref, cn_ref, kn_ref, cosn_ref,
                        sinn_ref, lw_ref, wuv_ref, gpe_ref, o_ref, l_scr, qpe_scr, m_scr, d_scr, acc_scr, kpad_scr,
                        cpad_scr, lat_buf, kpe_buf, sem, *, n_pages, sub, n_tok, page):
    s_id = pl.program_id(0)
    j = pl.program_id(1)
    n_steps = pl.num_programs(1)
    step = s_id * n_steps + j
    last = pl.num_programs(0) * n_steps - 1
    slot = lax.rem(step, 2)
    nq = MLA_HEADS * n_tok
    n_up = MLA_HEADS * MLA_NOPE

    def page_copies(pid, sl, p):
        return (pltpu.make_async_copy(lat_hbm.at[0, pid], lat_buf.at[sl, p], sem.at[sl, 0]),
                pltpu.make_async_copy(kpe_hbm.at[0, pid], kpe_buf.at[sl, p], sem.at[sl, 1]))

    def start_pages(seq, st, sl, pages=range(n_pages)):
        for p in pages:
            for c in page_copies(pt_ref[seq, st * n_pages + p], sl, p):
                c.start()

    def wait_pages(sl):
        for p in range(n_pages):
            for c in page_copies(0, sl, p):
                c.wait()

    @pl.when(step == 0)
    def _():
        start_pages(0, 0, 0)

    wait_pages(slot)
    wrap = j == n_steps - 1
    nxt_seq = jnp.where(wrap, jnp.where(step == last, 0, s_id + 1), s_id)
    nxt_j = jnp.where(wrap, 0, j + 1)
    start_pages(nxt_seq, nxt_j, 1 - slot)
    lat_refs = [lat_buf.at[slot, p] for p in range(n_pages)]
    kpe_refs = [kpe_buf.at[slot, p] for p in range(n_pages)]

    @pl.when(j == 0)
    def _():
        l_scr[:n_up, :] = lw_ref[...]
        qt = qt_ref[...]
        qf = qf_ref[...]
        l_scr[n_up:, :] = jnp.concatenate(
            [qt[:, h * KV_LORA:(h + 1) * KV_LORA] for h in range(MLA_HEADS)], axis=0).astype(BF16)
        lane = lax.broadcasted_iota(jnp.int32, (nq, HEAD_PAD), 1)
        qpe = jnp.concatenate([qf[:, h * HEAD_PAD:(h + 1) * HEAD_PAD] for h in range(MLA_HEADS)], axis=0)
        qpe_scr[...] = jnp.where(lane < MLA_ROPE, qpe, 0.0).astype(BF16)
        m_scr[...] = jnp.full(m_scr.shape, NEG_BIG, F32)
        d_scr[...] = jnp.zeros(d_scr.shape, F32)
        acc_scr[...] = jnp.zeros(acc_scr.shape, F32)

    def scores(cb, kpe_t, cos_t, sin_t):
        tk = cb.shape[0]
        big = _dot_nt(l_scr[...], cb)
        k_t = big[:n_up]
        ss = jnp.sum((k_t * k_t).reshape(MLA_HEADS, MLA_NOPE, tk), axis=1)
        ss = ss + jnp.sum(kpe_t * kpe_t, axis=0, keepdims=True)
        rs = lax.rsqrt(ss * (1.0 / MLA_QK) + EPS)
        kg = kpe_t * gpe_ref[...]
        half = MLA_ROPE // 2
        x1, x2 = kg[:half], kg[half:]
        rot = jnp.concatenate([x1 * cos_t - x2 * sin_t, x1 * sin_t + x2 * cos_t], axis=0)
        pe = _dot(qpe_scr[:, :MLA_ROPE], rot.astype(BF16))
        rs_rows = jnp.concatenate([jnp.broadcast_to(rs[h:h + 1, :], (n_tok, tk)) for h in range(MLA_HEADS)], axis=0)
        return (big[n_up:] + pe) * rs_rows

    def update(s_list, cb_list):
        m_old = m_scr[...]
        m_new = m_old
        for s in s_list:
            m_new = jnp.maximum(m_new, jnp.max(s, axis=-1, keepdims=True))
        alpha = jnp.exp2(m_old - m_new)
        d = alpha * d_scr[...]
        acc = alpha * acc_scr[...]
        for s, cb in zip(s_list, cb_list):
            p = jnp.exp2(s - m_new)
            d = d + jnp.sum(p, axis=-1, keepdims=True)
            acc = acc + _dot(p.astype(BF16), cb)
        m_scr[...] = m_new
        d_scr[...] = d
        acc_scr[...] = acc

    s_list, cb_list = [], []
    for g in range(n_pages // sub):
        pages = range(g * sub, (g + 1) * sub)
        cb = jnp.concatenate([lat_refs[b][...] for b in pages], axis=0).astype(BF16)
        kpe_t = jnp.concatenate([kpe_refs[b][...] for b in pages], axis=1)
        cols = slice(g * sub * page, (g + 1) * sub * page)
        s_list.append(scores(cb, kpe_t, cos_ref[:, cols], sin_ref[:, cols]))
        cb_list.append(cb)
    update(s_list, cb_list)

    @pl.when(j == pl.num_programs(1) - 1)
    def _():
        cpad_scr[...] = jnp.zeros(cpad_scr.shape, F32)
        cpad_scr[:n_tok, :] = cn_ref[...]
        kpad_scr[...] = jnp.zeros(kpad_scr.shape, F32)
        kpad_scr[:n_tok, :MLA_ROPE] = kn_ref[...]
        rowi = lax.broadcasted_iota(jnp.int32, (nq, page), 0)
        coli = lax.broadcasted_iota(jnp.int32, (nq, page), 1)
        keep = coli <= (rowi & (n_tok - 1))
        cb = cpad_scr[...].astype(BF16)
        s = scores(cb, kpad_scr[...].T[:MLA_ROPE], cosn_ref[...], sinn_ref[...])
        update([jnp.where(keep, s, NEG_BIG)], [cb])
        ctx = (acc_scr[...] / d_scr[...]).astype(BF16)
        out = jnp.zeros(o_ref.shape, F32)
        for h in range(MLA_HEADS):
            out = out + _dot(ctx, wuv_ref[h])[h * n_tok:(h + 1) * n_tok]
        o_ref[...] = out

    @pl.when(step == last)
    def _():
        wait_pages(1 - slot)


def _attn_sample(page_table, cache_latent, cache_krope_t, tabs, qt, qf, ckv, kpe, w, n_tok, n_pages_step, sub):
    n_seq, n_pages = page_table.shape
    page = cache_latent.shape[2]
    n_steps = n_pages // n_pages_step
    tk = n_pages_step * page
    nq = MLA_HEADS * n_tok

    seq_row = lambda s, j, pt: (s, 0)
    const2 = lambda s, j, pt: (0, 0)
    half = MLA_ROPE // 2
    in_specs = [pl.BlockSpec(memory_space=pl.ANY), pl.BlockSpec(memory_space=pl.ANY)]
    in_specs += [pl.BlockSpec((half, tk), lambda s, j, pt: (0, j)), pl.BlockSpec((half, tk), lambda s, j, pt: (0, j)),
                 pl.BlockSpec((n_tok, MLA_HEADS * KV_LORA), seq_row),
                 pl.BlockSpec((n_tok, MLA_HEADS * HEAD_PAD), seq_row),
                 pl.BlockSpec((n_tok, KV_LORA), seq_row), pl.BlockSpec((n_tok, MLA_ROPE), seq_row),
                 pl.BlockSpec((half, page), const2), pl.BlockSpec((half, page), const2),
                 pl.BlockSpec(w['uk_t'].shape, const2),
                 pl.BlockSpec(w['uv_blk'].shape, lambda s, j, pt: (0, 0, 0)),
                 pl.BlockSpec((MLA_ROPE, 1), const2)]
    grid_spec = pltpu.PrefetchScalarGridSpec(
        num_scalar_prefetch=1, grid=(n_seq, n_steps), in_specs=in_specs,
        out_specs=pl.BlockSpec((n_tok, MLA_HEADS * MLA_V), seq_row),
        scratch_shapes=[pltpu.VMEM((MLA_HEADS * MLA_NOPE + nq, KV_LORA), BF16),
                        pltpu.VMEM((nq, HEAD_PAD), BF16),
                        pltpu.VMEM((nq, 1), F32), pltpu.VMEM((nq, 1), F32), pltpu.VMEM((nq, KV_LORA), F32),
                        pltpu.VMEM((page, LANES), F32), pltpu.VMEM((page, KV_LORA), F32),
                        pltpu.VMEM((2, n_pages_step, page, KV_LORA), F32),
                        pltpu.VMEM((2, n_pages_step, MLA_ROPE, page), F32),
                        pltpu.SemaphoreType.DMA((2, 2))])
    return pl.pallas_call(
        functools.partial(_attn_sample_kernel, n_pages=n_pages_step, sub=sub, n_tok=n_tok, page=page),
        grid_spec=grid_spec, out_shape=jax.ShapeDtypeStruct((n_seq * n_tok, MLA_HEADS * MLA_V), F32),
        compiler_params=_params(("arbitrary", "arbitrary")), name="attn_sample",
    )(page_table, cache_latent, cache_krope_t,
      tabs['cos_t'], tabs['sin_t'], qt, qf, ckv, kpe, tabs['cosn_t'], tabs['sinn_t'],
      w['uk_t'], w['uv_blk'], w['gpe'])


def _post_kernel(ra_ref, rb_ref, ma_ref, mb_ref, xa_ref, xb_ref, wo_ref, g_ref, wrh_ref, wrl_ref, br_ref,
                 h_ref, xn_ref, ti_ref, tg_ref, *, n_a):
    first = pl.program_id(0) < n_a
    r = jnp.where(first, ra_ref[...], rb_ref[...])
    m = jnp.where(first, ma_ref[...], mb_ref[...].astype(BF16))
    half = wo_ref.shape[0] // 2
    mix = _dot(r, wo_ref[:half, :]) + _dot(m, wo_ref[half:, :])
    h = jnp.where(first, xa_ref[...], xb_ref[...]) + mix
    h_ref[...] = h
    xn = _rms(h, g_ref[...])
    hi = xn.astype(BF16)
    bits = lax.bitcast_convert_type(hi.astype(F32), jnp.uint32)
    xn_ref[...] = (bits[:, :D_MODEL // 2] >> 16) | bits[:, D_MODEL // 2:]
    lo = (xn - hi.astype(F32)).astype(BF16)
    wrh = wrh_ref[...]
    work = _dot(hi, wrh) + _dot(hi, wrl_ref[...]) + _dot(lo, wrh) + br_ref[...]
    lane = lax.broadcasted_iota(jnp.int32, work.shape, 1).astype(F32)
    idx = jnp.zeros(work.shape, F32)
    val = jnp.zeros(work.shape, F32)
    vmax = None
    denom = None
    for k in range(TOP_K):
        vk = jnp.max(work, axis=-1, keepdims=True)
        ik = jnp.min(jnp.where(work == vk, lane, float(LANES)), axis=-1, keepdims=True)
        work = jnp.where(lane == ik, -jnp.inf, work)
        if k == 0:
            vmax = vk
        ek = jnp.exp(vk - vmax)
        denom = ek if k == 0 else denom + ek
        idx = jnp.where(lane == float(k), ik, idx)
        val = jnp.where(lane == float(k), ek, val)
    ti_ref[...] = idx.astype(jnp.int32)
    tg_ref[...] = val / denom


def _post(r_pair, m_pair, x_pair, w, tm):
    n_a, n_b = x_pair[0].shape[0] // tm, x_pair[1].shape[0] // tm
    rows = (n_a + n_b) * tm
    row = lambda i: (i, 0)
    const = lambda i: (0, 0)
    in_specs = []
    for a, b in (r_pair, m_pair, x_pair):
        in_specs += [pl.BlockSpec((tm, a.shape[1]), lambda i: (jnp.minimum(i, n_a - 1), 0)),
                     pl.BlockSpec((tm, b.shape[1]), lambda i: (jnp.maximum(i - n_a, 0), 0))]
    in_specs += [pl.BlockSpec(w[n].shape, const) for n in ('wo', 'gffn', 'wr_hi', 'wr_lo', 'br')]
    out_shape = [jax.ShapeDtypeStruct((rows, D_MODEL), F32), jax.ShapeDtypeStruct((rows, D_MODEL // 2), jnp.uint32),
                 jax.ShapeDtypeStruct((rows, LANES), jnp.int32), jax.ShapeDtypeStruct((rows, LANES), F32)]
    out_specs = [pl.BlockSpec((tm, D_MODEL), row), pl.BlockSpec((tm, D_MODEL // 2), row),
                 pl.BlockSpec((tm, LANES), row), pl.BlockSpec((tm, LANES), row)]
    return pl.pallas_call(
        functools.partial(_post_kernel, n_a=n_a), grid=(rows // tm,), in_specs=in_specs, out_specs=out_specs,
        out_shape=out_shape, compiler_params=_params(("arbitrary",)), name="post",
    )(*r_pair, *m_pair, *x_pair, w['wo'], w['gffn'], w['wr_hi'], w['wr_lo'], w['br'])


def _moe_kernel(be_ref, nu_ref, xs_ref, wgu_ref, bgu_ref, wd_ref, bd_ref, o_ref, wgu_b, wd_b):
    i = pl.program_id(0)
    e = be_ref[i]
    e_prev = be_ref[jnp.maximum(i - 1, 0)]

    @pl.when((i == 0) | (e != e_prev))
    def _():
        wgu_b[...] = wgu_ref[...].astype(BF16)
        wd_b[...] = wd_ref[...].astype(BF16)

    @pl.when(i < nu_ref[0])
    def _():
        packed = xs_ref[...]
        x_lo = lax.bitcast_convert_type(packed << 16, F32).astype(BF16)
        x_hi = lax.bitcast_convert_type(packed & jnp.uint32(0xFFFF0000), F32).astype(BF16)
        half = D_MODEL // 2
        hgu = _dot(x_lo, wgu_b[:half, :]) + _dot(x_hi, wgu_b[half:, :]) + bgu_ref[...]
        g = jnp.minimum(hgu[:, :D_FF], SWIGLU_LIMIT)
        u = jnp.clip(hgu[:, D_FF:], -SWIGLU_LIMIT, SWIGLU_LIMIT)
        hid = (u + 1.0) * (g * jax.nn.sigmoid(SWIGLU_ALPHA * g))
        o_ref[...] = _dot(hid.astype(BF16), wd_b[...]) + bd_ref[...]

    @pl.when(i >= nu_ref[0])
    def _():
        o_ref[...] = jnp.zeros(o_ref.shape, F32)


def _moe_blocks(block_e, n_used, xs, w_gate_up, b_gate_up, w_down, b_down, blk):
    n_blocks = block_e.shape[0]
    row = lambda i, be, nu: (i, 0)
    in_specs = [pl.BlockSpec((blk, D_MODEL // 2), row),
                pl.BlockSpec((None, D_MODEL, 2 * D_FF), lambda i, be, nu: (be[i], 0, 0)),
                pl.BlockSpec((None, 1, 2 * D_FF), lambda i, be, nu: (be[i], 0, 0)),
                pl.BlockSpec((None, D_FF, D_MODEL), lambda i, be, nu: (be[i], 0, 0)),
                pl.BlockSpec((None, 1, D_MODEL), lambda i, be, nu: (be[i], 0, 0))]
    grid_spec = pltpu.PrefetchScalarGridSpec(
        num_scalar_prefetch=2, grid=(n_blocks,), in_specs=in_specs,
        out_specs=pl.BlockSpec((blk, D_MODEL), row),
        scratch_shapes=[pltpu.VMEM((D_MODEL, 2 * D_FF), BF16), pltpu.VMEM((D_FF, D_MODEL), BF16)])
    return pl.pallas_call(
        _moe_kernel, grid_spec=grid_spec, out_shape=jax.ShapeDtypeStruct((n_blocks * blk, D_MODEL), F32),
        compiler_params=_params(("arbitrary",)), name="moe",
    )(block_e, n_used, xs, w_gate_up, b_gate_up[:, None, :], w_down, b_down[:, None, :])


def _combine_kernel(h_ref, g_ref, y_ref, o_ref):
    g = g_ref[...]
    acc = y_ref[0] * g[:, 0:1]
    for k in range(1, TOP_K):
        acc = acc + y_ref[k] * g[:, k:k + 1]
    o_ref[...] = h_ref[...] + acc


def _combine(h, gates, y4, block0, rows, tm):
    row = lambda i: (block0 + i, 0)
    return pl.pallas_call(
        _combine_kernel, grid=(rows // tm,),
        in_specs=[pl.BlockSpec((tm, D_MODEL), row), pl.BlockSpec((tm, LANES), row),
                  pl.BlockSpec((TOP_K, tm, D_MODEL), lambda i: (0, block0 + i, 0))],
        out_specs=pl.BlockSpec((tm, D_MODEL), lambda i: (i, 0)),
        out_shape=jax.ShapeDtypeStruct((rows, D_MODEL), F32),
        compiler_params=_params(("arbitrary",)), name="combine",
    )(h, gates, y4)


def _gather_rows(table, idx):
    n, d = idx.shape[0], table.shape[1]
    window = min(LANES, GATHER_BUFFER_BYTES // (d * table.dtype.itemsize))
    n_win = n // window
    assert n % (window * SC_CORES * SC_SUBCORES) == 0
    idx = jnp.zeros((n_win, LANES), jnp.int32).at[:, :window].set(idx.reshape(n_win, window))
    mesh = plsc.VectorSubcoreMesh(core_axis_name="core", subcore_axis_name="subcore",
                                  num_cores=SC_CORES, num_subcores=SC_SUBCORES)

    @functools.partial(pl.kernel, out_type=jax.ShapeDtypeStruct((n, d), table.dtype), mesh=mesh)
    def gather(x_hbm, i_hbm, o_hbm):
        def body(i_vmem, o_vmem):
            pltpu.sync_copy(x_hbm.at[i_vmem.at[0, pl.ds(0, window)]], o_vmem)

        pltpu.emit_pipeline(
            body, grid=(n_win,),
            in_specs=[pl.BlockSpec((1, LANES), index_map=lambda i: (i, 0))],
            out_specs=[pl.BlockSpec((window, d), index_map=lambda i: (i, 0))],
            core_axis_name=("core", "subcore"), dimension_semantics=(pltpu.PARALLEL,),
        )(i_hbm, o_hbm)

    return gather(table, idx)


def _route(top_i, blk):
    t = top_i.shape[0]
    a = t * TOP_K
    flat_e = top_i.reshape(a)
    assert N_EXPERTS * a < 2 ** 31
    order = jnp.sort(flat_e.astype(jnp.int32) * a + jnp.arange(a, dtype=jnp.int32)) % a
    tok_sorted = (order // TOP_K).astype(jnp.int32)
    experts = jnp.arange(N_EXPERTS, dtype=jnp.int32)
    hit = (top_i[:, :, None] == experts[None, None, :]).astype(jnp.int32)
    per_tok = hit.sum(axis=1)
    before = jnp.cumsum(per_tok, axis=0) - per_tok
    counts = per_tok.sum(axis=0)
    padded = (counts + blk - 1) // blk * blk
    start = jnp.cumsum(counts) - counts
    pend = jnp.cumsum(padded)
    pstart = pend - padded
    dest = ((before + pstart[None, :])[:, None, :] * hit).sum(axis=2)
    n_blocks = -(-a // blk) + N_EXPERTS
    block_e = jnp.minimum((pend[None, :] <= (jnp.arange(n_blocks, dtype=jnp.int32) * blk)[:, None]).sum(axis=1),
                          N_EXPERTS - 1).astype(jnp.int32)
    slot = jnp.arange(n_blocks * blk, dtype=jnp.int32)
    slot_e = jnp.repeat(block_e, blk)
    r = slot - pstart[slot_e]
    slot_tok = jnp.where(r < counts[slot_e], tok_sorted[jnp.minimum(start[slot_e] + r, a - 1)], slot % t)
    n_used = (pend[-1:] // blk).astype(jnp.int32)
    return slot_tok, dest.astype(jnp.int32), block_e, n_used


def _rope_tables(pos):
    pos = pos.astype(F32)[:, None]
    n = pos.shape[0]
    hr = RET_DK // 2
    ang = pos * (RET_THETA ** (-jnp.arange(hr, dtype=F32) / hr))[None, :]
    cos, sin = jnp.cos(ang), jnp.sin(ang)
    zr = jnp.zeros((n, LANES - RET_DK), F32)
    cr = jnp.concatenate([cos, cos, zr], axis=1)
    sr = jnp.concatenate([-sin, sin, zr], axis=1)
    hm = MLA_ROPE // 2
    ang = pos * (MLA_THETA ** (-jnp.arange(hm, dtype=F32) / hm))[None, :]
    cos, sin = jnp.cos(ang), jnp.sin(ang)
    cm = jnp.concatenate([cos, cos, jnp.ones((n, LANES - MLA_ROPE), F32)], axis=1)
    sma = jnp.concatenate([jnp.zeros((n, hm), F32), sin, jnp.zeros((n, LANES - MLA_ROPE), F32)], axis=1)
    smb = jnp.concatenate([-sin, jnp.zeros((n, LANES - hm), F32)], axis=1)
    return [cr, sr, cm, sma, smb]


def _rope_tables_t(pos):
    hm = MLA_ROPE // 2
    ang = pos.astype(F32)[:, None] * (MLA_THETA ** (-jnp.arange(hm, dtype=F32) / hm))[None, :]
    return jnp.cos(ang).T, jnp.sin(ang).T


def _decay_tables(log_gamma, c, n_rep):
    idx = jnp.arange(c, dtype=F32)
    diff = idx[:, None] - idx[None, :]
    decay = jnp.where(diff >= 0, jnp.exp(jnp.maximum(diff, 0.0)[None] * log_gamma[:, None, None]), 0.0)
    if n_rep > 1:
        eye = jnp.eye(n_rep, dtype=F32)
        decay = (eye[None, :, None, :, None] * decay[:, None, :, None, :]).reshape(RET_HEADS, n_rep * c, n_rep * c)
    rowd = jnp.tile(jnp.exp((idx + 1.0)[:, None] * log_gamma[None, :]), (n_rep, 1))
    cold = jnp.tile(jnp.exp((c - 1.0 - idx)[:, None] * log_gamma[None, :]), (n_rep, 1))
    cpow = jnp.broadcast_to(jnp.exp(c * log_gamma)[None, :], (LANES, RET_HEADS))
    return {'dec': decay, 'rowd': rowd, 'cold': cold, 'cpow': cpow}


def _pad_heads(wm, n_heads, width, offset):
    k = wm.shape[0]
    wm = wm.reshape(k, n_heads, width)
    out = jnp.zeros((k, n_heads, LANES), wm.dtype).at[:, :, offset:offset + width].set(wm)
    return out.reshape(k, n_heads * LANES)


def _prep_weights(norm_attn_g, w_in, ret_out_g, q_a_norm_g, w_q_b, kv_a_norm_g, w_uk, w_uv,
                  qk_norm_q_g, qk_norm_k_g, w_out, norm_ffn_g, w_router, b_router):
    splits = [RET_HEADS * RET_DK, RET_HEADS * RET_DK, RET_HEADS * RET_DV, RET_HEADS * RET_DV, Q_LORA, KV_LORA,
              MLA_ROPE]
    offs = [0]
    for s in splits:
        offs.append(offs[-1] + s)
    part = [w_in[:, offs[i]:offs[i + 1]] for i in range(len(splits))]
    kpe_cols = jnp.zeros((D_MODEL, LANES), F32).at[:, :MLA_ROPE].set(part[6])
    win = jnp.concatenate([_pad_heads(part[0], RET_HEADS, RET_DK, 0), _pad_heads(part[1], RET_HEADS, RET_DK, 0),
                           part[2], part[3], part[4], part[5], kpe_cols], axis=1).astype(BF16)

    def mla_cols(wm):
        k = wm.shape[0]
        wm = wm.reshape(k, MLA_HEADS, MLA_QK)
        wm = jnp.concatenate([wm[:, :, MLA_NOPE:], wm[:, :, :MLA_NOPE]], axis=2)
        return _pad_heads(wm.reshape(k, MLA_HEADS * MLA_QK), MLA_HEADS, MLA_QK, 0)

    def mla_gain(g):
        g = jnp.concatenate([g[MLA_NOPE:], g[:MLA_NOPE], jnp.zeros((LANES - MLA_QK,), F32)])
        return g[None, :]

    gk_nope = qk_norm_k_g[:MLA_NOPE]
    uk_heads = w_uk.reshape(KV_LORA, MLA_HEADS, MLA_NOPE)
    wabs = jnp.zeros((MLA_HEADS, HEAD_PAD, KV_LORA), F32).at[:, MLA_ROPE:MLA_ROPE + MLA_NOPE, :].set(
        (uk_heads * gk_nope[None, None, :]).transpose(1, 2, 0))
    head_of_col = jnp.arange(MLA_HEADS * MLA_V) // MLA_V
    uv_blk = jnp.where(head_of_col[None, None, :] == jnp.arange(MLA_HEADS)[:, None, None], w_uv[None], 0.0)
    wr = jnp.zeros((D_MODEL, LANES), F32).at[:, :N_EXPERTS].set(w_router)
    wr_hi = wr.astype(BF16)
    return {
        'gattn': norm_attn_g[None, :], 'win': win, 'gqa': q_a_norm_g[None, :], 'wqb': mla_cols(w_q_b).astype(BF16),
        'gq': mla_gain(qk_norm_q_g), 'gkv': kv_a_norm_g[None, :],
        'wuk': _pad_heads(w_uk, MLA_HEADS, MLA_NOPE, MLA_ROPE).astype(BF16), 'gk': mla_gain(qk_norm_k_g),
        'wuv': w_uv.astype(BF16), 'gret': ret_out_g[None, :],
        'wabs': wabs.astype(BF16), 'uk_t': w_uk.T.astype(BF16), 'uv_blk': uv_blk.astype(BF16),
        'gpe': qk_norm_k_g[MLA_NOPE:, None],
        'wo': w_out.astype(BF16), 'gffn': norm_ffn_g[None, :], 'wr_hi': wr_hi,
        'wr_lo': (wr - wr_hi.astype(F32)).astype(BF16),
        'br': jnp.full((1, LANES), NEG_BIG, F32).at[0, :N_EXPERTS].set(b_router),
    }


def _pad_rows(a, rows):
    return jnp.zeros((rows,) + a.shape[1:], a.dtype).at[:a.shape[0]].set(a)


def _largest_divisor(n, cap):
    d = min(n, cap)
    while n % d:
        d -= 1
    return d


def kernel(x_prompt, x_sample, cache_latent, cache_krope, state_retention, page_table, meta_tokens, norm_attn_g, w_in, ret_out_g, q_a_norm_g, w_q_b, kv_a_norm_g, w_uk, w_uv, qk_norm_q_g, qk_norm_k_g, w_out, norm_ffn_g, w_router, b_router, w_gate_up, b_gate_up, w_down, b_down):
    assert w_in.shape[0] == 1, "single-layer trunk"
    batch, seq, _ = x_prompt.shape
    n_seq, n_tok, _ = x_sample.shape
    n_pages, page = page_table.shape[1], cache_latent.shape[2]
    past = n_pages * page
    assert seq % RET_CHUNK == 0 and LANES % n_tok == 0 and (n_seq * n_tok) % LANES == 0 and page == LANES
    w = _prep_weights(norm_attn_g[0], w_in[0], ret_out_g[0], q_a_norm_g[0], w_q_b[0], kv_a_norm_g[0], w_uk[0],
                      w_uv[0], qk_norm_q_g[0], qk_norm_k_g[0], w_out[0], norm_ffn_g[0], w_router[0], b_router[0])
    log_gamma = jnp.log1p(-jnp.exp2(-5.0 - jnp.arange(RET_HEADS, dtype=F32)))

    rows_p = batch * seq
    rows_s = n_seq * n_tok
    tm = _largest_divisor(min(seq, rows_s), ROW_BLOCK)
    nb_seq = seq // tm
    x_pair = (x_prompt.reshape(rows_p, D_MODEL), x_sample.reshape(rows_s, D_MODEL))
    pos_rows = jnp.concatenate([N_META + jnp.arange(seq), jnp.tile(past + jnp.arange(n_tok), tm // n_tok)])
    tabs = _rope_tables(pos_rows)
    n_pb = rows_p // tm
    rq, rk, rv, zg, mq, ckv, kpe, k, v = _front(
        *x_pair, tabs, lambda i: jnp.where(i < n_pb, i % nb_seq, nb_seq), w, tm)
    _, mrk, mrv, _, _, mckv, mkpe, mk, mv = _front(
        meta_tokens, None, _rope_tables(jnp.arange(N_META)), lambda i: i, w, N_META)

    dt_p = _decay_tables(log_gamma, RET_CHUNK, 1)
    dt_p['mcol'] = _pad_rows(jnp.exp((N_META - 1.0 - jnp.arange(N_META, dtype=F32))[:, None] * log_gamma[None, :]),
                             RET_CHUNK)
    gret = w['gret']
    r_p, st_p = _ret_prompt(rq, rk, rv, zg, _pad_rows(mrk, RET_CHUNK), _pad_rows(mrv, RET_CHUNK), dt_p, gret,
                            batch, seq // RET_CHUNK)
    dt_s = _decay_tables(log_gamma, n_tok, LANES // n_tok)
    r_s, st_s = _ret_sample(rq[rows_p:], rk[rows_p:], rv[rows_p:], zg[rows_p:], state_retention[0], dt_s, gret, n_tok)

    blk = _largest_divisor(seq, ATTN_BLOCK)
    m_p = _attn_prompt(mq, k, v, _pad_rows(mk, LANES), _pad_rows(mv, LANES), batch, seq, blk)
    tm_s = _largest_divisor(rows_s, ROW_BLOCK)
    qt, qf = _absorb(mq, w['wabs'], rows_p // tm_s, rows_s, tm_s)
    n_pages_step = _largest_divisor(n_pages, PAGES_PER_STEP)
    cos_t, sin_t = _rope_tables_t(jnp.arange(past))
    cosn_t, sinn_t = _rope_tables_t(past + jnp.arange(page))
    tabs_s = {'cos_t': cos_t, 'sin_t': sin_t, 'cosn_t': cosn_t, 'sinn_t': sinn_t}
    m_s = _attn_sample(page_table, cache_latent, jnp.swapaxes(cache_krope, 2, 3), tabs_s, qt, qf, ckv[rows_p:],
                       kpe[rows_p:], w, n_tok, n_pages_step, _largest_divisor(n_pages_step, PAGES_PER_CHAIN))

    h1, xn2, top_i, gates = _post((r_p, r_s), (m_p, m_s), x_pair, w, tm)

    slot_tok, dest, block_e, n_used = _route(top_i[:, :TOP_K], EXPERT_BLOCK)
    outs = _moe_blocks(block_e, n_used, _gather_rows(xn2, slot_tok), w_gate_up[0], b_gate_up[0], w_down[0],
                       b_down[0], EXPERT_BLOCK)
    y4 = _gather_rows(outs, dest.T.reshape(-1)).reshape(TOP_K, rows_p + rows_s, D_MODEL)
    y_prompt = _combine(h1, gates, y4, 0, rows_p, tm).reshape(batch, seq, D_MODEL)
    y_sample = _combine(h1, gates, y4, n_pb, rows_s, tm).reshape(n_seq, n_tok, D_MODEL)
    lat_p = jnp.concatenate([jnp.broadcast_to(mckv[None], (batch, N_META, KV_LORA)),
                             ckv[:rows_p].reshape(batch, seq, KV_LORA)], axis=1)[None]
    kpe_p = jnp.concatenate([jnp.broadcast_to(mkpe[None], (batch, N_META, MLA_ROPE)),
                             kpe[:rows_p].reshape(batch, seq, MLA_ROPE)], axis=1)[None]
    return (y_prompt, y_sample, lat_p, kpe_p, st_p[None],
            ckv[rows_p:].reshape(n_seq, n_tok, KV_LORA)[None], kpe[rows_p:].reshape(n_seq, n_tok, MLA_ROPE)[None],
            st_s[None])
```

```python
import functools

import jax
import jax.numpy as jnp
from jax import lax
from jax.experimental import pallas as pl
from jax.experimental.pallas import tpu as pltpu
from jax.experimental.pallas import tpu_sc as plsc

F32 = jnp.float32
BF16 = jnp.bfloat16

D_MODEL = 1024
N_META = 16
RET_HEADS = 4
RET_DK = 64
RET_DV = 128
RET_CHUNK = 128
RET_THETA = 10000.0
MLA_HEADS = 8
MLA_NOPE = 64
MLA_ROPE = 32
MLA_QK = MLA_NOPE + MLA_ROPE
MLA_V = 64
Q_LORA = 384
KV_LORA = 256
MLA_THETA = 10000.0
MLA_SCALE = MLA_QK ** -0.5
LOG2E = 1.4426950408889634
N_EXPERTS = 32
TOP_K = 4
D_FF = 1024
SWIGLU_LIMIT = 7.0
SWIGLU_ALPHA = 1.702
EPS = 1e-6
NEG_BIG = -1e30

LANES = 128
HEAD_PAD = LANES
VMEM_LIMIT = 56 * 1024 * 1024
ROW_BLOCK = 512
ATTN_BLOCK = 1024
ATTN_HEADS_PER_STEP = 8
RET_CHUNKS_PER_STEP = 4
PAGES_PER_STEP = 64
PAGES_PER_CHAIN = 32
EXPERT_BLOCK = 512
SC_CORES, SC_SUBCORES = 2, 16
GATHER_BUFFER_BYTES = 128 * 1024

_OFF_RQ = 0
_OFF_RK = _OFF_RQ + RET_HEADS * LANES
_OFF_RV = _OFF_RK + RET_HEADS * LANES
_OFF_ZG = _OFF_RV + RET_HEADS * RET_DV
_OFF_CQ = _OFF_ZG + RET_HEADS * RET_DV
_OFF_CKV = _OFF_CQ + Q_LORA
_OFF_KPE = _OFF_CKV + KV_LORA
IN_PAD = _OFF_KPE + LANES


def _params(sem):
    return pltpu.CompilerParams(dimension_semantics=sem, vmem_limit_bytes=VMEM_LIMIT)


def _rms(x, g):
    return x * lax.rsqrt(jnp.mean(x * x, axis=-1, keepdims=True) + EPS) * g


def _dot(a, b):
    return jnp.dot(a, b, preferred_element_type=F32)


def _dot_nt(a, b):
    return lax.dot_general(a, b, (((1,), (1,)), ((), ())), preferred_element_type=F32)


def _dot_tn(a, b):
    return lax.dot_general(a, b, (((0,), (0,)), ((), ())), preferred_element_type=F32)


def _front_kernel(xa_ref, xb_ref, gattn_ref, win_ref, cr_ref, sr_ref, cm_ref, sma_ref, smb_ref,
                  gqa_ref, wqb_ref, gq_ref, gkv_ref, wuk_ref, gk_ref, wuv_ref,
                  rq_ref, rk_ref, rv_ref, zg_ref, mq_ref, ckv_ref, kpe_ref, k_ref, v_ref, *, n_a):
    tm = xa_ref.shape[0]
    x = jnp.where(pl.program_id(0) < n_a, xa_ref[...], xb_ref[...])
    xn = _rms(x, gattn_ref[...])
    z = _dot(xn.astype(BF16), win_ref[...])

    lane = lax.broadcasted_iota(jnp.int32, (tm, LANES), 1)
    upper = (lane & (RET_DK // 2)) != 0
    cr, sr = cr_ref[...], sr_ref[...]

    def rope_ret(t):
        partner = jnp.where(upper, pltpu.roll(t, RET_DK // 2, 1), pltpu.roll(t, LANES - RET_DK // 2, 1))
        return t * cr + partner * sr

    for h in range(RET_HEADS):
        sl = slice(h * LANES, (h + 1) * LANES)
        rq_ref[:, sl] = rope_ret(z[:, _OFF_RQ + h * LANES:_OFF_RQ + (h + 1) * LANES])
        rk_ref[:, sl] = rope_ret(z[:, _OFF_RK + h * LANES:_OFF_RK + (h + 1) * LANES]) * (RET_DK ** -0.5)
    rv_ref[...] = z[:, _OFF_RV:_OFF_ZG].astype(BF16)
    zg_ref[...] = z[:, _OFF_ZG:_OFF_CQ]

    cm, sma, smb = cm_ref[...], sma_ref[...], smb_ref[...]

    def rope_mla(t):
        half = MLA_ROPE // 2
        return t * cm + pltpu.roll(t, half, 1) * sma + pltpu.roll(t, LANES - half, 1) * smb

    def head_norm(t, g):
        ms = jnp.sum(t * t, axis=-1, keepdims=True) * (1.0 / MLA_QK)
        return t * lax.rsqrt(ms + EPS) * g

    cq = _rms(z[:, _OFF_CQ:_OFF_CKV], gqa_ref[...])
    q = _dot(cq.astype(BF16), wqb_ref[...])
    gq = gq_ref[...]
    for h in range(MLA_HEADS):
        sl = slice(h * HEAD_PAD, (h + 1) * HEAD_PAD)
        mq_ref[:, sl] = (rope_mla(head_norm(q[:, sl], gq)) * (MLA_SCALE * LOG2E)).astype(BF16)

    ckv = _rms(z[:, _OFF_CKV:_OFF_KPE], gkv_ref[...])
    ckv_ref[...] = ckv
    kpe_slab = z[:, _OFF_KPE:IN_PAD]
    kpe_ref[...] = kpe_slab[:, :MLA_ROPE]
    ckv_b = ckv.astype(BF16)
    kn = _dot(ckv_b, wuk_ref[...])
    gk = gk_ref[...]
    for h in range(MLA_HEADS):
        sl = slice(h * HEAD_PAD, (h + 1) * HEAD_PAD)
        k_ref[:, sl] = rope_mla(head_norm(kn[:, sl] + kpe_slab, gk)).astype(BF16)
    v_ref[...] = _dot(ckv_b, wuv_ref[...]).astype(BF16)


def _front(xa, xb, tabs, tab_index, w, tm):
    n_a, n_b = xa.shape[0] // tm, (0 if xb is None else xb.shape[0] // tm)
    xb = xa if xb is None else xb
    rows = (n_a + n_b) * tm
    grid = (n_a + n_b,)
    row = lambda i: (i, 0)
    const = lambda i: (0, 0)
    tab = lambda i: (tab_index(i), 0)

    def full(a):
        return pl.BlockSpec(a.shape, const)

    in_specs = [pl.BlockSpec((tm, D_MODEL), lambda i: (jnp.minimum(i, n_a - 1), 0)),
                pl.BlockSpec((tm, D_MODEL), lambda i: (jnp.maximum(i - n_a, 0), 0)), full(w['gattn']), full(w['win'])]
    in_specs += [pl.BlockSpec((tm, LANES), tab)] * 5
    in_specs += [full(w[n]) for n in ('gqa', 'wqb', 'gq', 'gkv', 'wuk', 'gk', 'wuv')]
    widths = [(RET_HEADS * LANES, F32), (RET_HEADS * LANES, F32), (RET_HEADS * RET_DV, BF16),
              (RET_HEADS * RET_DV, F32), (MLA_HEADS * HEAD_PAD, BF16), (KV_LORA, F32), (MLA_ROPE, F32),
              (MLA_HEADS * HEAD_PAD, BF16), (MLA_HEADS * MLA_V, BF16)]
    out_shape = [jax.ShapeDtypeStruct((rows, n), dt) for n, dt in widths]
    out_specs = [pl.BlockSpec((tm, n), row) for n, _ in widths]
    return pl.pallas_call(
        functools.partial(_front_kernel, n_a=n_a), grid=grid, in_specs=in_specs, out_specs=out_specs,
        out_shape=out_shape, compiler_params=_params(("arbitrary",)), name="front",
    )(xa, xb, w['gattn'], w['win'], *tabs, w['gqa'], w['wqb'], w['gq'], w['gkv'], w['wuk'], w['gk'], w['wuv'])


def _ret_gate(o, zg, g):
    on = o * lax.rsqrt(jnp.mean(o * o, axis=-1, keepdims=True) + EPS) * g
    return (zg * jax.nn.sigmoid(zg)) * on


def _ret_prompt_kernel(q_ref, k_ref, v_ref, zg_ref, mk_ref, mv_ref, mcol_ref, dec_ref, rowd_ref, cold_ref,
                       cpow_ref, g_ref, r_ref, s_ref, s_scr):
    c = pl.program_id(1)

    @pl.when(c == 0)
    def _():
        for h in range(RET_HEADS):
            sl = slice(h * LANES, (h + 1) * LANES)
            kw = mk_ref[:, sl] * mcol_ref[:, h:h + 1]
            s_scr[h] = _dot_tn(kw.astype(BF16), mv_ref[:, sl])

    for h in range(RET_HEADS):
        sl = slice(h * LANES, (h + 1) * LANES)
        s0 = s_scr[h]
        for cc in range(q_ref.shape[0] // RET_CHUNK):
            rows = slice(cc * RET_CHUNK, (cc + 1) * RET_CHUNK)
            k = k_ref[rows, sl]
            v = v_ref[rows, sl]
            qb = q_ref[rows, sl].astype(BF16)
            scores = _dot_nt(qb, k.astype(BF16)) * dec_ref[h]
            inner = _dot(scores.astype(BF16), v)
            cross = _dot(qb, s0.astype(BF16)) * rowd_ref[:, h:h + 1]
            kw = k * cold_ref[:, h:h + 1]
            s0 = s0 * cpow_ref[:, h:h + 1] + _dot_tn(kw.astype(BF16), v)
            r_ref[rows, sl] = _ret_gate(inner + cross, zg_ref[rows, sl], g_ref[:, sl]).astype(BF16)
        s_scr[h] = s0

    @pl.when(c == pl.num_programs(1) - 1)
    def _():
        for h in range(RET_HEADS):
            s_ref[0, h] = s_scr[h, :RET_DK, :]


def _ret_prompt(rq, rk, rv, zg, mk, mv, tabs, g, batch, n_chunks):
    cs = RET_CHUNK
    cps = _largest_divisor(n_chunks, RET_CHUNKS_PER_STEP)
    steps = n_chunks // cps
    row = lambda b, c: (b * steps + c, 0)
    const2 = lambda b, c: (0, 0)
    w4 = RET_HEADS * LANES
    in_specs = [pl.BlockSpec((cps * cs, w4), row), pl.BlockSpec((cps * cs, w4), row),
                pl.BlockSpec((cps * cs, w4), row), pl.BlockSpec((cps * cs, w4), row),
                pl.BlockSpec((cs, w4), const2), pl.BlockSpec((cs, w4), const2),
                pl.BlockSpec((cs, RET_HEADS), const2),
                pl.BlockSpec((RET_HEADS, cs, cs), lambda b, c: (0, 0, 0)),
                pl.BlockSpec((cs, RET_HEADS), const2), pl.BlockSpec((cs, RET_HEADS), const2),
                pl.BlockSpec((LANES, RET_HEADS), const2), pl.BlockSpec((1, w4), const2)]
    out_shape = [jax.ShapeDtypeStruct((batch * n_chunks * cs, w4), BF16),
                 jax.ShapeDtypeStruct((batch, RET_HEADS, RET_DK, RET_DV), F32)]
    out_specs = [pl.BlockSpec((cps * cs, w4), row),
                 pl.BlockSpec((1, RET_HEADS, RET_DK, RET_DV), lambda b, c: (b, 0, 0, 0))]
    return pl.pallas_call(
        _ret_prompt_kernel, grid=(batch, steps), in_specs=in_specs, out_specs=out_specs, out_shape=out_shape,
        scratch_shapes=[pltpu.VMEM((RET_HEADS, LANES, RET_DV), F32)],
        compiler_params=_params(("arbitrary", "arbitrary")), name="ret_prompt",
    )(rq, rk, rv, zg, mk, mv, tabs['mcol'], tabs['dec'], tabs['rowd'], tabs['cold'], tabs['cpow'], g)


def _ret_sample_kernel(q_ref, k_ref, v_ref, zg_ref, s0_ref, dec_ref, rowd_ref, cold_ref, cpow_ref, g_ref,
                       r_ref, s_ref, *, n_seq, n_tok):
    rows = n_seq * n_tok
    ri = lax.broadcasted_iota(jnp.int32, (rows, 1), 0)
    for h in range(RET_HEADS):
        sl = slice(h * LANES, (h + 1) * LANES)
        q = q_ref[:, sl]
        k = k_ref[:, sl]
        v = v_ref[:, sl]
        qb = q.astype(BF16)
        scores = _dot_nt(qb, k.astype(BF16)) * dec_ref[h]
        inner = _dot(scores.astype(BF16), v)
        kw = k * cold_ref[:, h:h + 1]
        cross = jnp.zeros((rows, RET_DV), F32)
        for s in range(n_seq):
            s0 = s0_ref[s, h]
            mine = (ri >= s * n_tok) & (ri < (s + 1) * n_tok)
            cross = cross + jnp.where(mine, _dot(qb[:, :RET_DK], s0.astype(BF16)), 0.0)
            upd = _dot_tn(jnp.where(mine, kw, 0.0).astype(BF16), v)
            s_ref[s, h] = s0 * cpow_ref[:RET_DK, h:h + 1] + upd[:RET_DK]
        cross = cross * rowd_ref[:, h:h + 1]
        r_ref[:, sl] = _ret_gate(inner + cross, zg_ref[:, sl], g_ref[:, sl]).astype(BF16)


def _ret_sample(rq, rk, rv, zg, state, tabs, g, n_tok):
    n_seq_total = state.shape[0]
    n_seq = LANES // n_tok
    rows = n_seq * n_tok
    w4 = RET_HEADS * LANES
    row = lambda i: (i, 0)
    const = lambda i: (0, 0)
    in_specs = [pl.BlockSpec((rows, w4), row)] * 4
    in_specs += [pl.BlockSpec((n_seq, RET_HEADS, RET_DK, RET_DV), lambda i: (i, 0, 0, 0)),
                 pl.BlockSpec((RET_HEADS, rows, rows), lambda i: (0, 0, 0)),
                 pl.BlockSpec((rows, RET_HEADS), const), pl.BlockSpec((rows, RET_HEADS), const),
                 pl.BlockSpec((LANES, RET_HEADS), const), pl.BlockSpec((1, w4), const)]
    out_shape = [jax.ShapeDtypeStruct((n_seq_total * n_tok, w4), BF16),
                 jax.ShapeDtypeStruct(state.shape, F32)]
    out_specs = [pl.BlockSpec((rows, w4), row),
                 pl.BlockSpec((n_seq, RET_HEADS, RET_DK, RET_DV), lambda i: (i, 0, 0, 0))]
    return pl.pallas_call(
        functools.partial(_ret_sample_kernel, n_seq=n_seq, n_tok=n_tok),
        grid=(n_seq_total // n_seq,), in_specs=in_specs, out_specs=out_specs, out_shape=out_shape,
        compiler_params=_params(("arbitrary",)), name="ret_sample",
    )(rq, rk, rv, zg, state, tabs['dec'], tabs['rowd'], tabs['cold'], tabs['cpow'], g)


def _attn_prompt_kernel(qi_ref, ki_ref, q_ref, k_ref, v_ref, km_ref, vm_ref, o_ref, m_scr, l_scr, acc_scr, *,
                        sub_tiles):
    qi = qi_ref[pl.program_id(2)]
    ki = ki_ref[pl.program_id(2)]
    tm, tk = q_ref.shape[0], k_ref.shape[0]
    heads = q_ref.shape[1] // HEAD_PAD

    def pair_lanes(hh):
        return slice((hh // 2) * 2 * MLA_V, (hh // 2 + 1) * 2 * MLA_V)

    @pl.when(ki == 0)
    def _():
        lane = lax.broadcasted_iota(jnp.int32, (tm, km_ref.shape[0]), 1)
        for hh in range(heads):
            sl = slice(hh * HEAD_PAD, (hh + 1) * HEAD_PAD)
            s = jnp.where(lane < N_META, _dot_nt(q_ref[:, sl], km_ref[:, sl]), NEG_BIG)
            m = jnp.max(s, axis=-1, keepdims=True)
            p = jnp.exp2(s - m)
            m_scr[hh] = jnp.broadcast_to(m, (tm, LANES))
            l_scr[hh] = jnp.broadcast_to(jnp.sum(p, axis=-1, keepdims=True), (tm, LANES))
            acc_scr[hh] = _dot(p.astype(BF16), vm_ref[:, pair_lanes(hh)])

    def tile(r0, nr, c0, nc, masked):
        rows = pl.ds(r0, nr)
        if masked:
            keep = (lax.broadcasted_iota(jnp.int32, (nr, nc), 1) <= lax.broadcasted_iota(jnp.int32, (nr, nc), 0))
        for hh in range(heads):
            sl = slice(hh * HEAD_PAD, (hh + 1) * HEAD_PAD)
            s = _dot_nt(q_ref[rows, sl], k_ref[pl.ds(c0, nc), sl])
            if masked:
                s = jnp.where(keep, s, NEG_BIG)
            m_old = m_scr[hh, rows, :]
            m_new = jnp.maximum(m_old, jnp.max(s, axis=-1, keepdims=True))
            alpha = jnp.exp2(m_old - m_new)
            p = jnp.exp2(s - jnp.tile(m_new, (1, nc // LANES)))
            m_scr[hh, rows, :] = m_new
            l_scr[hh, rows, :] = alpha * l_scr[hh, rows, :] + jnp.sum(p, axis=-1, keepdims=True)
            acc_scr[hh, rows, :] = alpha * acc_scr[hh, rows, :] + _dot(p.astype(BF16),
                                                                       v_ref[pl.ds(c0, nc), pair_lanes(hh)])

    @pl.when(ki < qi)
    def _():
        tile(0, tm, 0, tk, False)

    @pl.when(ki == qi)
    def _():
        ns = sub_tiles
        st = tm // ns
        for a in range(ns):
            if a > 0:
                tile(a * st, st, 0, a * st, False)
            tile(a * st, st, a * st, st, True)
        lane = lax.broadcasted_iota(jnp.int32, (tm, 2 * MLA_V), 1)
        for pr in range(heads // 2):
            even = acc_scr[2 * pr] / l_scr[2 * pr]
            odd = acc_scr[2 * pr + 1] / l_scr[2 * pr + 1]
            o_ref[:, pair_lanes(2 * pr)] = jnp.where(lane < MLA_V, even, odd).astype(BF16)


def _attn_prompt(mq, k, v, km, vm, batch, seq, blk):
    nb = seq // blk
    hps = ATTN_HEADS_PER_STEP
    pairs_qk = [(qi, ki) for qi in range(nb) for ki in range(qi + 1)]
    qi_tab = jnp.asarray([p[0] for p in pairs_qk], jnp.int32)
    ki_tab = jnp.asarray([p[1] for p in pairs_qk], jnp.int32)
    in_specs = [pl.BlockSpec((blk, hps * HEAD_PAD), lambda b, h, t, qt, kt: (b * nb + qt[t], h)),
                pl.BlockSpec((blk, hps * HEAD_PAD), lambda b, h, t, qt, kt: (b * nb + kt[t], h)),
                pl.BlockSpec((blk, hps * MLA_V), lambda b, h, t, qt, kt: (b * nb + kt[t], h)),
                pl.BlockSpec((km.shape[0], hps * HEAD_PAD), lambda b, h, t, qt, kt: (0, h)),
                pl.BlockSpec((vm.shape[0], hps * MLA_V), lambda b, h, t, qt, kt: (0, h))]
    grid_spec = pltpu.PrefetchScalarGridSpec(
        num_scalar_prefetch=2, grid=(batch, MLA_HEADS // hps, len(pairs_qk)), in_specs=in_specs,
        out_specs=pl.BlockSpec((blk, hps * MLA_V), lambda b, h, t, qt, kt: (b * nb + qt[t], h)),
        scratch_shapes=[pltpu.VMEM((hps, blk, LANES), F32), pltpu.VMEM((hps, blk, LANES), F32),
                        pltpu.VMEM((hps, blk, 2 * MLA_V), F32)])
    return pl.pallas_call(
        functools.partial(_attn_prompt_kernel, sub_tiles=2 if blk % (2 * LANES) == 0 else 1), grid_spec=grid_spec,
        out_shape=jax.ShapeDtypeStruct((batch * seq, MLA_HEADS * MLA_V), BF16),
        compiler_params=_params(("arbitrary",) * 3), name="attn_prompt",
    )(qi_tab, ki_tab, mq, k, v, km, vm)


def _absorb_kernel(mq_ref, wabs_ref, qt_ref, qf_ref):
    q = mq_ref[...]
    qf_ref[...] = q.astype(F32)
    for h in range(MLA_HEADS):
        qt_ref[:, h * KV_LORA:(h + 1) * KV_LORA] = _dot(q[:, h * HEAD_PAD:(h + 1) * HEAD_PAD], wabs_ref[h])


def _absorb(mq, wabs, row0_blocks, rows, tm):
    return pl.pallas_call(
        _absorb_kernel, grid=(rows // tm,),
        in_specs=[pl.BlockSpec((tm, MLA_HEADS * HEAD_PAD), lambda i: (row0_blocks + i, 0)),
                  pl.BlockSpec(wabs.shape, lambda i: (0, 0, 0))],
        out_specs=[pl.BlockSpec((tm, MLA_HEADS * KV_LORA), lambda i: (i, 0)),
                   pl.BlockSpec((tm, MLA_HEADS * HEAD_PAD), lambda i: (i, 0))],
        out_shape=[jax.ShapeDtypeStruct((rows, MLA_HEADS * KV_LORA), F32),
                   jax.ShapeDtypeStruct((rows, MLA_HEADS * HEAD_PAD), F32)],
        compiler_params=_params(("arbitrary",)), name="absorb",
    )(mq, wabs)


def _attn_sample_kernel(pt_ref, lat_hbm, kpe_hbm, cos_ref, sin_ref, qt_ref, qf_ref, cn_ref, kn_ref, cosn_ref,
                        sinn_ref, lw_ref, wuv_ref, gpe_ref, o_ref, l_scr, qpe_scr, m_scr, d_scr, acc_scr, kpad_scr,
                        cpad_scr, lat_buf, kpe_buf, sem, *, n_pages, sub, n_tok, page):
    s_id = pl.program_id(0)
    j = pl.program_id(1)
    n_steps = pl.num_programs(1)
    step = s_id * n_steps + j
    last = pl.num_programs(0) * n_steps - 1
    slot = lax.rem(step, 2)
    nq = MLA_HEADS * n_tok
    n_up = MLA_HEADS * MLA_NOPE

    def page_copies(pid, sl, p):
        return (pltpu.make_async_copy(lat_hbm.at[0, pid], lat_buf.at[sl, p], sem.at[sl, 0]),
                pltpu.make_async_copy(kpe_hbm.at[0, pid], kpe_buf.at[sl, p], sem.at[sl, 1]))

    def start_pages(seq, st, sl, pages=range(n_pages)):
        for p in pages:
            lat_copy, kpe_copy = page_copies(pt_ref[seq, st * n_pages + p], sl, p)
            lat_copy.start(priority=p % 2)
            kpe_copy.start(priority=(p + 1) % 2)

    def wait_pages(sl):
        for p in range(n_pages):
            for c in page_copies(0, sl, p):
                c.wait()

    @pl.when(step == 0)
    def _():
        start_pages(0, 0, 0)

    wait_pages(slot)
    wrap = j == n_steps - 1
    nxt_seq = jnp.where(wrap, jnp.where(step == last, 0, s_id + 1), s_id)
    nxt_j = jnp.where(wrap, 0, j + 1)
    start_pages(nxt_seq, nxt_j, 1 - slot)
    lat_refs = [lat_buf.at[slot, p] for p in range(n_pages)]
    kpe_refs = [kpe_buf.at[slot, p] for p in range(n_pages)]

    @pl.when(j == 0)
    def _():
        l_scr[:n_up, :] = lw_ref[...]
        qt = qt_ref[...]
        qf = qf_ref[...]
        l_scr[n_up:, :] = jnp.concatenate(
            [qt[:, h * KV_LORA:(h + 1) * KV_LORA] for h in range(MLA_HEADS)], axis=0).astype(BF16)
        lane = lax.broadcasted_iota(jnp.int32, (nq, HEAD_PAD), 1)
        qpe = jnp.concatenate([qf[:, h * HEAD_PAD:(h + 1) * HEAD_PAD] for h in range(MLA_HEADS)], axis=0)
        qpe_scr[...] = jnp.where(lane < MLA_ROPE, qpe, 0.0).astype(BF16)
        m_scr[...] = jnp.full(m_scr.shape, NEG_BIG, F32)
        d_scr[...] = jnp.zeros(d_scr.shape, F32)
        acc_scr[...] = jnp.zeros(acc_scr.shape, F32)

    def scores(cb, kpe_t, cos_t, sin_t):
        tk = cb.shape[0]
        big = _dot_nt(l_scr[...], cb)
        k_t = big[:n_up]
        ss = jnp.sum((k_t * k_t).reshape(MLA_HEADS, MLA_NOPE, tk), axis=1)
        ss = ss + jnp.sum(kpe_t * kpe_t, axis=0, keepdims=True)
        rs = lax.rsqrt(ss * (1.0 / MLA_QK) + EPS)
        kg = kpe_t * gpe_ref[...]
        half = MLA_ROPE // 2
        x1, x2 = kg[:half], kg[half:]
        rot = jnp.concatenate([x1 * cos_t - x2 * sin_t, x1 * sin_t + x2 * cos_t], axis=0)
        pe = _dot(qpe_scr[:, :MLA_ROPE], rot.astype(BF16))
        rs_rows = jnp.concatenate([jnp.broadcast_to(rs[h:h + 1, :], (n_tok, tk)) for h in range(MLA_HEADS)], axis=0)
        return (big[n_up:] + pe) * rs_rows

    def update(s_list, cb_list):
        m_old = m_scr[...]
        m_new = m_old
        for s in s_list:
            m_new = jnp.maximum(m_new, jnp.max(s, axis=-1, keepdims=True))
        alpha = jnp.exp2(m_old - m_new)
        d = alpha * d_scr[...]
        acc = alpha * acc_scr[...]
        for s, cb in zip(s_list, cb_list):
            p = jnp.exp2(s - m_new)
            d = d + jnp.sum(p, axis=-1, keepdims=True)
            acc = acc + _dot(p.astype(BF16), cb)
        m_scr[...] = m_new
        d_scr[...] = d
        acc_scr[...] = acc

    s_list, cb_list = [], []
    for g in range(n_pages // sub):
        pages = range(g * sub, (g + 1) * sub)
        cb = jnp.concatenate([lat_refs[b][...] for b in pages], axis=0).astype(BF16)
        kpe_t = jnp.concatenate([kpe_refs[b][...] for b in pages], axis=1)
        cols = slice(g * sub * page, (g + 1) * sub * page)
        s_list.append(scores(cb, kpe_t, cos_ref[:, cols], sin_ref[:, cols]))
        cb_list.append(cb)
    update(s_list, cb_list)

    @pl.when(j == pl.num_programs(1) - 1)
    def _():
        cpad_scr[...] = jnp.zeros(cpad_scr.shape, F32)
        cpad_scr[:n_tok, :] = cn_ref[...]
        kpad_scr[...] = jnp.zeros(kpad_scr.shape, F32)
        kpad_scr[:n_tok, :MLA_ROPE] = kn_ref[...]
        rowi = lax.broadcasted_iota(jnp.int32, (nq, page), 0)
        coli = lax.broadcasted_iota(jnp.int32, (nq, page), 1)
        keep = coli <= (rowi & (n_tok - 1))
        cb = cpad_scr[...].astype(BF16)
        s = scores(cb, kpad_scr[...].T[:MLA_ROPE], cosn_ref[...], sinn_ref[...])
        update([jnp.where(keep, s, NEG_BIG)], [cb])
        ctx = (acc_scr[...] / d_scr[...]).astype(BF16)
        out = jnp.zeros(o_ref.shape, F32)
        for h in range(MLA_HEADS):
            out = out + _dot(ctx, wuv_ref[h])[h * n_tok:(h + 1) * n_tok]
        o_ref[...] = out

    @pl.when(step == last)
    def _():
        wait_pages(1 - slot)


def _attn_sample(page_table, cache_latent, cache_krope_t, tabs, qt, qf, ckv, kpe, w, n_tok, n_pages_step, sub):
    n_seq, n_pages = page_table.shape
    page = cache_latent.shape[2]
    n_steps = n_pages // n_pages_step
    tk = n_pages_step * page
    nq = MLA_HEADS * n_tok

    seq_row = lambda s, j, pt: (s, 0)
    const2 = lambda s, j, pt: (0, 0)
    half = MLA_ROPE // 2
    in_specs = [pl.BlockSpec(memory_space=pl.ANY), pl.BlockSpec(memory_space=pl.ANY)]
    in_specs += [pl.BlockSpec((half, tk), lambda s, j, pt: (0, j)), pl.BlockSpec((half, tk), lambda s, j, pt: (0, j)),
                 pl.BlockSpec((n_tok, MLA_HEADS * KV_LORA), seq_row),
                 pl.BlockSpec((n_tok, MLA_HEADS * HEAD_PAD), seq_row),
                 pl.BlockSpec((n_tok, KV_LORA), seq_row), pl.BlockSpec((n_tok, MLA_ROPE), seq_row),
                 pl.BlockSpec((half, page), const2), pl.BlockSpec((half, page), const2),
                 pl.BlockSpec(w['uk_t'].shape, const2),
                 pl.BlockSpec(w['uv_blk'].shape, lambda s, j, pt: (0, 0, 0)),
                 pl.BlockSpec((MLA_ROPE, 1), const2)]
    grid_spec = pltpu.PrefetchScalarGridSpec(
        num_scalar_prefetch=1, grid=(n_seq, n_steps), in_specs=in_specs,
        out_specs=pl.BlockSpec((n_tok, MLA_HEADS * MLA_V), seq_row),
        scratch_shapes=[pltpu.VMEM((MLA_HEADS * MLA_NOPE + nq, KV_LORA), BF16),
                        pltpu.VMEM((nq, HEAD_PAD), BF16),
                        pltpu.VMEM((nq, 1), F32), pltpu.VMEM((nq, 1), F32), pltpu.VMEM((nq, KV_LORA), F32),
                        pltpu.VMEM((page, LANES), F32), pltpu.VMEM((page, KV_LORA), F32),
                        pltpu.VMEM((2, n_pages_step, page, KV_LORA), F32),
                        pltpu.VMEM((2, n_pages_step, MLA_ROPE, page), F32),
                        pltpu.SemaphoreType.DMA((2, 2))])
    return pl.pallas_call(
        functools.partial(_attn_sample_kernel, n_pages=n_pages_step, sub=sub, n_tok=n_tok, page=page),
        grid_spec=grid_spec, out_shape=jax.ShapeDtypeStruct((n_seq * n_tok, MLA_HEADS * MLA_V), F32),
        compiler_params=_params(("arbitrary", "arbitrary")), name="attn_sample",
    )(page_table, cache_latent, cache_krope_t,
      tabs['cos_t'], tabs['sin_t'], qt, qf, ckv, kpe, tabs['cosn_t'], tabs['sinn_t'],
      w['uk_t'], w['uv_blk'], w['gpe'])


def _post_kernel(ra_ref, rb_ref, ma_ref, mb_ref, xa_ref, xb_ref, wo_ref, g_ref, wrh_ref, wrl_ref, br_ref,
                 h_ref, xn_ref, ti_ref, tg_ref, *, n_a):
    first = pl.program_id(0) < n_a
    r = jnp.where(first, ra_ref[...], rb_ref[...])
    m = jnp.where(first, ma_ref[...], mb_ref[...].astype(BF16))
    half = wo_ref.shape[0] // 2
    mix = _dot(r, wo_ref[:half, :]) + _dot(m, wo_ref[half:, :])
    h = jnp.where(first, xa_ref[...], xb_ref[...]) + mix
    h_ref[...] = h
    xn = _rms(h, g_ref[...])
    hi = xn.astype(BF16)
    bits = lax.bitcast_convert_type(hi.astype(F32), jnp.uint32)
    xn_ref[...] = (bits[:, :D_MODEL // 2] >> 16) | bits[:, D_MODEL // 2:]
    lo = (xn - hi.astype(F32)).astype(BF16)
    wrh = wrh_ref[...]
    work = _dot(hi, wrh) + _dot(hi, wrl_ref[...]) + _dot(lo, wrh) + br_ref[...]
    lane = lax.broadcasted_iota(jnp.int32, work.shape, 1).astype(F32)
    idx = jnp.zeros(work.shape, F32)
    val = jnp.zeros(work.shape, F32)
    vmax = None
    denom = None
    for k in range(TOP_K):
        vk = jnp.max(work, axis=-1, keepdims=True)
        ik = jnp.min(jnp.where(work == vk, lane, float(LANES)), axis=-1, keepdims=True)
        work = jnp.where(lane == ik, -jnp.inf, work)
        if k == 0:
            vmax = vk
        ek = jnp.exp(vk - vmax)
        denom = ek if k == 0 else denom + ek
        idx = jnp.where(lane == float(k), ik, idx)
        val = jnp.where(lane == float(k), ek, val)
    ti_ref[...] = idx.astype(jnp.int32)
    tg_ref[...] = val / denom


def _post(r_pair, m_pair, x_pair, w, tm):
    n_a, n_b = x_pair[0].shape[0] // tm, x_pair[1].shape[0] // tm
    rows = (n_a + n_b) * tm
    row = lambda i: (i, 0)
    const = lambda i: (0, 0)
    in_specs = []
    for a, b in (r_pair, m_pair, x_pair):
        in_specs += [pl.BlockSpec((tm, a.shape[1]), lambda i: (jnp.minimum(i, n_a - 1), 0)),
                     pl.BlockSpec((tm, b.shape[1]), lambda i: (jnp.maximum(i - n_a, 0), 0))]
    in_specs += [pl.BlockSpec(w[n].shape, const) for n in ('wo', 'gffn', 'wr_hi', 'wr_lo', 'br')]
    out_shape = [jax.ShapeDtypeStruct((rows, D_MODEL), F32), jax.ShapeDtypeStruct((rows, D_MODEL // 2), jnp.uint32),
                 jax.ShapeDtypeStruct((rows, LANES), jnp.int32), jax.ShapeDtypeStruct((rows, LANES), F32)]
    out_specs = [pl.BlockSpec((tm, D_MODEL), row), pl.BlockSpec((tm, D_MODEL // 2), row),
                 pl.BlockSpec((tm, LANES), row), pl.BlockSpec((tm, LANES), row)]
    return pl.pallas_call(
        functools.partial(_post_kernel, n_a=n_a), grid=(rows // tm,), in_specs=in_specs, out_specs=out_specs,
        out_shape=out_shape, compiler_params=_params(("arbitrary",)), name="post",
    )(*r_pair, *m_pair, *x_pair, w['wo'], w['gffn'], w['wr_hi'], w['wr_lo'], w['br'])


def _moe_kernel(be_ref, nu_ref, xs_ref, wgu_ref, bgu_ref, wd_ref, bd_ref, o_ref, wgu_b, wd_b):
    i = pl.program_id(0)
    e = be_ref[i]
    e_prev = be_ref[jnp.maximum(i - 1, 0)]

    @pl.when((i == 0) | (e != e_prev))
    def _():
        wgu_b[...] = wgu_ref[...].astype(BF16)
        wd_b[...] = wd_ref[...].astype(BF16)

    @pl.when(i < nu_ref[0])
    def _():
        packed = xs_ref[...]
        x_lo = lax.bitcast_convert_type(packed << 16, F32).astype(BF16)
        x_hi = lax.bitcast_convert_type(packed & jnp.uint32(0xFFFF0000), F32).astype(BF16)
        half = D_MODEL // 2
        hgu = _dot(x_lo, wgu_b[:half, :]) + _dot(x_hi, wgu_b[half:, :]) + bgu_ref[...]
        g = jnp.minimum(hgu[:, :D_FF], SWIGLU_LIMIT)
        u = jnp.clip(hgu[:, D_FF:], -SWIGLU_LIMIT, SWIGLU_LIMIT)
        hid = (u + 1.0) * (g * jax.nn.sigmoid(SWIGLU_ALPHA * g))
        o_ref[...] = _dot(hid.astype(BF16), wd_b[...]) + bd_ref[...]

    @pl.when(i >= nu_ref[0])
    def _():
        o_ref[...] = jnp.zeros(o_ref.shape, F32)


def _moe_blocks(block_e, n_used, xs, w_gate_up, b_gate_up, w_down, b_down, blk):
    n_blocks = block_e.shape[0]
    row = lambda i, be, nu: (i, 0)
    in_specs = [pl.BlockSpec((blk, D_MODEL // 2), row),
                pl.BlockSpec((None, D_MODEL, 2 * D_FF), lambda i, be, nu: (be[i], 0, 0)),
                pl.BlockSpec((None, 1, 2 * D_FF), lambda i, be, nu: (be[i], 0, 0)),
                pl.BlockSpec((None, D_FF, D_MODEL), lambda i, be, nu: (be[i], 0, 0)),
                pl.BlockSpec((None, 1, D_MODEL), lambda i, be, nu: (be[i], 0, 0))]
    grid_spec = pltpu.PrefetchScalarGridSpec(
        num_scalar_prefetch=2, grid=(n_blocks,), in_specs=in_specs,
        out_specs=pl.BlockSpec((blk, D_MODEL), row),
        scratch_shapes=[pltpu.VMEM((D_MODEL, 2 * D_FF), BF16), pltpu.VMEM((D_FF, D_MODEL), BF16)])
    return pl.pallas_call(
        _moe_kernel, grid_spec=grid_spec, out_shape=jax.ShapeDtypeStruct((n_blocks * blk, D_MODEL), F32),
        compiler_params=_params(("arbitrary",)), name="moe",
    )(block_e, n_used, xs, w_gate_up, b_gate_up[:, None, :], w_down, b_down[:, None, :])


def _combine_kernel(h_ref, g_ref, y_ref, o_ref):
    g = g_ref[...]
    acc = y_ref[0] * g[:, 0:1]
    for k in range(1, TOP_K):
        acc = acc + y_ref[k] * g[:, k:k + 1]
    o_ref[...] = h_ref[...] + acc


def _combine(h, gates, y4, block0, rows, tm):
    row = lambda i: (block0 + i, 0)
    return pl.pallas_call(
        _combine_kernel, grid=(rows // tm,),
        in_specs=[pl.BlockSpec((tm, D_MODEL), row), pl.BlockSpec((tm, LANES), row),
                  pl.BlockSpec((TOP_K, tm, D_MODEL), lambda i: (0, block0 + i, 0))],
        out_specs=pl.BlockSpec((tm, D_MODEL), lambda i: (i, 0)),
        out_shape=jax.ShapeDtypeStruct((rows, D_MODEL), F32),
        compiler_params=_params(("arbitrary",)), name="combine",
    )(h, gates, y4)


def _gather_rows(table, idx):
    n, d = idx.shape[0], table.shape[1]
    window = min(LANES, GATHER_BUFFER_BYTES // (d * table.dtype.itemsize))
    n_win = n // window
    assert n % (window * SC_CORES * SC_SUBCORES) == 0
    idx = jnp.zeros((n_win, LANES), jnp.int32).at[:, :window].set(idx.reshape(n_win, window))
    mesh = plsc.VectorSubcoreMesh(core_axis_name="core", subcore_axis_name="subcore",
                                  num_cores=SC_CORES, num_subcores=SC_SUBCORES)

    @functools.partial(pl.kernel, out_type=jax.ShapeDtypeStruct((n, d), table.dtype), mesh=mesh)
    def gather(x_hbm, i_hbm, o_hbm):
        def body(i_vmem, o_vmem):
            pltpu.sync_copy(x_hbm.at[i_vmem.at[0, pl.ds(0, window)]], o_vmem)

        pltpu.emit_pipeline(
            body, grid=(n_win,),
            in_specs=[pl.BlockSpec((1, LANES), index_map=lambda i: (i, 0))],
            out_specs=[pl.BlockSpec((window, d), index_map=lambda i: (i, 0))],
            core_axis_name=("core", "subcore"), dimension_semantics=(pltpu.PARALLEL,),
        )(i_hbm, o_hbm)

    return gather(table, idx)


def _route(top_i, blk):
    t = top_i.shape[0]
    a = t * TOP_K
    flat_e = top_i.reshape(a)
    assert N_EXPERTS * a < 2 ** 31
    order = jnp.sort(flat_e.astype(jnp.int32) * a + jnp.arange(a, dtype=jnp.int32)) % a
    tok_sorted = (order // TOP_K).astype(jnp.int32)
    experts = jnp.arange(N_EXPERTS, dtype=jnp.int32)
    hit = (top_i[:, :, None] == experts[None, None, :]).astype(jnp.int32)
    per_tok = hit.sum(axis=1)
    before = jnp.cumsum(per_tok, axis=0) - per_tok
    counts = per_tok.sum(axis=0)
    padded = (counts + blk - 1) // blk * blk
    start = jnp.cumsum(counts) - counts
    pend = jnp.cumsum(padded)
    pstart = pend - padded
    dest = ((before + pstart[None, :])[:, None, :] * hit).sum(axis=2)
    n_blocks = -(-a // blk) + N_EXPERTS
    block_e = jnp.minimum((pend[None, :] <= (jnp.arange(n_blocks, dtype=jnp.int32) * blk)[:, None]).sum(axis=1),
                          N_EXPERTS - 1).astype(jnp.int32)
    slot = jnp.arange(n_blocks * blk, dtype=jnp.int32)
    slot_e = jnp.repeat(block_e, blk)
    r = slot - pstart[slot_e]
    slot_tok = jnp.where(r < counts[slot_e], tok_sorted[jnp.minimum(start[slot_e] + r, a - 1)], slot % t)
    n_used = (pend[-1:] // blk).astype(jnp.int32)
    return slot_tok, dest.astype(jnp.int32), block_e, n_used


def _rope_tables(pos):
    pos = pos.astype(F32)[:, None]
    n = pos.shape[0]
    hr = RET_DK // 2
    ang = pos * (RET_THETA ** (-jnp.arange(hr, dtype=F32) / hr))[None, :]
    cos, sin = jnp.cos(ang), jnp.sin(ang)
    zr = jnp.zeros((n, LANES - RET_DK), F32)
    cr = jnp.concatenate([cos, cos, zr], axis=1)
    sr = jnp.concatenate([-sin, sin, zr], axis=1)
    hm = MLA_ROPE // 2
    ang = pos * (MLA_THETA ** (-jnp.arange(hm, dtype=F32) / hm))[None, :]
    cos, sin = jnp.cos(ang), jnp.sin(ang)
    cm = jnp.concatenate([cos, cos, jnp.ones((n, LANES - MLA_ROPE), F32)], axis=1)
    sma = jnp.concatenate([jnp.zeros((n, hm), F32), sin, jnp.zeros((n, LANES - MLA_ROPE), F32)], axis=1)
    smb = jnp.concatenate([-sin, jnp.zeros((n, LANES - hm), F32)], axis=1)
    return [cr, sr, cm, sma, smb]


def _rope_tables_t(pos):
    hm = MLA_ROPE // 2
    ang = pos.astype(F32)[:, None] * (MLA_THETA ** (-jnp.arange(hm, dtype=F32) / hm))[None, :]
    return jnp.cos(ang).T, jnp.sin(ang).T


def _decay_tables(log_gamma, c, n_rep):
    idx = jnp.arange(c, dtype=F32)
    diff = idx[:, None] - idx[None, :]
    decay = jnp.where(diff >= 0, jnp.exp(jnp.maximum(diff, 0.0)[None] * log_gamma[:, None, None]), 0.0)
    if n_rep > 1:
        eye = jnp.eye(n_rep, dtype=F32)
        decay = (eye[None, :, None, :, None] * decay[:, None, :, None, :]).reshape(RET_HEADS, n_rep * c, n_rep * c)
    rowd = jnp.tile(jnp.exp((idx + 1.0)[:, None] * log_gamma[None, :]), (n_rep, 1))
    cold = jnp.tile(jnp.exp((c - 1.0 - idx)[:, None] * log_gamma[None, :]), (n_rep, 1))
    cpow = jnp.broadcast_to(jnp.exp(c * log_gamma)[None, :], (LANES, RET_HEADS))
    return {'dec': decay, 'rowd': rowd, 'cold': cold, 'cpow': cpow}


def _pad_heads(wm, n_heads, width, offset):
    k = wm.shape[0]
    wm = wm.reshape(k, n_heads, width)
    out = jnp.zeros((k, n_heads, LANES), wm.dtype).at[:, :, offset:offset + width].set(wm)
    return out.reshape(k, n_heads * LANES)


def _prep_weights(norm_attn_g, w_in, ret_out_g, q_a_norm_g, w_q_b, kv_a_norm_g, w_uk, w_uv,
                  qk_norm_q_g, qk_norm_k_g, w_out, norm_ffn_g, w_router, b_router):
    splits = [RET_HEADS * RET_DK, RET_HEADS * RET_DK, RET_HEADS * RET_DV, RET_HEADS * RET_DV, Q_LORA, KV_LORA,
              MLA_ROPE]
    offs = [0]
    for s in splits:
        offs.append(offs[-1] + s)
    part = [w_in[:, offs[i]:offs[i + 1]] for i in range(len(splits))]
    kpe_cols = jnp.zeros((D_MODEL, LANES), F32).at[:, :MLA_ROPE].set(part[6])
    win = jnp.concatenate([_pad_heads(part[0], RET_HEADS, RET_DK, 0), _pad_heads(part[1], RET_HEADS, RET_DK, 0),
                           part[2], part[3], part[4], part[5], kpe_cols], axis=1).astype(BF16)

    def mla_cols(wm):
        k = wm.shape[0]
        wm = wm.reshape(k, MLA_HEADS, MLA_QK)
        wm = jnp.concatenate([wm[:, :, MLA_NOPE:], wm[:, :, :MLA_NOPE]], axis=2)
        return _pad_heads(wm.reshape(k, MLA_HEADS * MLA_QK), MLA_HEADS, MLA_QK, 0)

    def mla_gain(g):
        g = jnp.concatenate([g[MLA_NOPE:], g[:MLA_NOPE], jnp.zeros((LANES - MLA_QK,), F32)])
        return g[None, :]

    gk_nope = qk_norm_k_g[:MLA_NOPE]
    uk_heads = w_uk.reshape(KV_LORA, MLA_HEADS, MLA_NOPE)
    wabs = jnp.zeros((MLA_HEADS, HEAD_PAD, KV_LORA), F32).at[:, MLA_ROPE:MLA_ROPE + MLA_NOPE, :].set(
        (uk_heads * gk_nope[None, None, :]).transpose(1, 2, 0))
    head_of_col = jnp.arange(MLA_HEADS * MLA_V) // MLA_V
    uv_blk = jnp.where(head_of_col[None, None, :] == jnp.arange(MLA_HEADS)[:, None, None], w_uv[None], 0.0)
    wr = jnp.zeros((D_MODEL, LANES), F32).at[:, :N_EXPERTS].set(w_router)
    wr_hi = wr.astype(BF16)
    return {
        'gattn': norm_attn_g[None, :], 'win': win, 'gqa': q_a_norm_g[None, :], 'wqb': mla_cols(w_q_b).astype(BF16),
        'gq': mla_gain(qk_norm_q_g), 'gkv': kv_a_norm_g[None, :],
        'wuk': _pad_heads(w_uk, MLA_HEADS, MLA_NOPE, MLA_ROPE).astype(BF16), 'gk': mla_gain(qk_norm_k_g),
        'wuv': w_uv.astype(BF16), 'gret': ret_out_g[None, :],
        'wabs': wabs.astype(BF16), 'uk_t': w_uk.T.astype(BF16), 'uv_blk': uv_blk.astype(BF16),
        'gpe': qk_norm_k_g[MLA_NOPE:, None],
        'wo': w_out.astype(BF16), 'gffn': norm_ffn_g[None, :], 'wr_hi': wr_hi,
        'wr_lo': (wr - wr_hi.astype(F32)).astype(BF16),
        'br': jnp.full((1, LANES), NEG_BIG, F32).at[0, :N_EXPERTS].set(b_router),
    }


def _pad_rows(a, rows):
    return jnp.zeros((rows,) + a.shape[1:], a.dtype).at[:a.shape[0]].set(a)


def _largest_divisor(n, cap):
    d = min(n, cap)
    while n % d:
        d -= 1
    return d


def kernel(x_prompt, x_sample, cache_latent, cache_krope, state_retention, page_table, meta_tokens, norm_attn_g, w_in, ret_out_g, q_a_norm_g, w_q_b, kv_a_norm_g, w_uk, w_uv, qk_norm_q_g, qk_norm_k_g, w_out, norm_ffn_g, w_router, b_router, w_gate_up, b_gate_up, w_down, b_down):
    assert w_in.shape[0] == 1, "single-layer trunk"
    batch, seq, _ = x_prompt.shape
    n_seq, n_tok, _ = x_sample.shape
    n_pages, page = page_table.shape[1], cache_latent.shape[2]
    past = n_pages * page
    assert seq % RET_CHUNK == 0 and LANES % n_tok == 0 and (n_seq * n_tok) % LANES == 0 and page == LANES
    w = _prep_weights(norm_attn_g[0], w_in[0], ret_out_g[0], q_a_norm_g[0], w_q_b[0], kv_a_norm_g[0], w_uk[0],
                      w_uv[0], qk_norm_q_g[0], qk_norm_k_g[0], w_out[0], norm_ffn_g[0], w_router[0], b_router[0])
    log_gamma = jnp.log1p(-jnp.exp2(-5.0 - jnp.arange(RET_HEADS, dtype=F32)))

    rows_p = batch * seq
    rows_s = n_seq * n_tok
    tm = _largest_divisor(min(seq, rows_s), ROW_BLOCK)
    nb_seq = seq // tm
    x_pair = (x_prompt.reshape(rows_p, D_MODEL), x_sample.reshape(rows_s, D_MODEL))
    pos_rows = jnp.concatenate([N_META + jnp.arange(seq), jnp.tile(past + jnp.arange(n_tok), tm // n_tok)])
    tabs = _rope_tables(pos_rows)
    n_pb = rows_p // tm
    rq, rk, rv, zg, mq, ckv, kpe, k, v = _front(
        *x_pair, tabs, lambda i: jnp.where(i < n_pb, i % nb_seq, nb_seq), w, tm)
    _, mrk, mrv, _, _, mckv, mkpe, mk, mv = _front(
        meta_tokens, None, _rope_tables(jnp.arange(N_META)), lambda i: i, w, N_META)

    dt_p = _decay_tables(log_gamma, RET_CHUNK, 1)
    dt_p['mcol'] = _pad_rows(jnp.exp((N_META - 1.0 - jnp.arange(N_META, dtype=F32))[:, None] * log_gamma[None, :]),
                             RET_CHUNK)
    gret = w['gret']
    r_p, st_p = _ret_prompt(rq, rk, rv, zg, _pad_rows(mrk, RET_CHUNK), _pad_rows(mrv, RET_CHUNK), dt_p, gret,
                            batch, seq // RET_CHUNK)
    dt_s = _decay_tables(log_gamma, n_tok, LANES // n_tok)
    r_s, st_s = _ret_sample(rq[rows_p:], rk[rows_p:], rv[rows_p:], zg[rows_p:], state_retention[0], dt_s, gret, n_tok)

    blk = _largest_divisor(seq, ATTN_BLOCK)
    m_p = _attn_prompt(mq, k, v, _pad_rows(mk, LANES), _pad_rows(mv, LANES), batch, seq, blk)
    tm_s = _largest_divisor(rows_s, ROW_BLOCK)
    qt, qf = _absorb(mq, w['wabs'], rows_p // tm_s, rows_s, tm_s)
    n_pages_step = _largest_divisor(n_pages, PAGES_PER_STEP)
    cos_t, sin_t = _rope_tables_t(jnp.arange(past))
    cosn_t, sinn_t = _rope_tables_t(past + jnp.arange(page))
    tabs_s = {'cos_t': cos_t, 'sin_t': sin_t, 'cosn_t': cosn_t, 'sinn_t': sinn_t}
    m_s = _attn_sample(page_table, cache_latent, jnp.swapaxes(cache_krope, 2, 3), tabs_s, qt, qf, ckv[rows_p:],
                       kpe[rows_p:], w, n_tok, n_pages_step, _largest_divisor(n_pages_step, PAGES_PER_CHAIN))

    h1, xn2, top_i, gates = _post((r_p, r_s), (m_p, m_s), x_pair, w, tm)

    slot_tok, dest, block_e, n_used = _route(top_i[:, :TOP_K], EXPERT_BLOCK)
    outs = _moe_blocks(block_e, n_used, _gather_rows(xn2, slot_tok), w_gate_up[0], b_gate_up[0], w_down[0],
                       b_down[0], EXPERT_BLOCK)
    y4 = _gather_rows(outs, dest.T.reshape(-1)).reshape(TOP_K, rows_p + rows_s, D_MODEL)
    y_prompt = _combine(h1, gates, y4, 0, rows_p, tm).reshape(batch, seq, D_MODEL)
    y_sample = _combine(h1, gates, y4, n_pb, rows_s, tm).reshape(n_seq, n_tok, D_MODEL)
    lat_p = jnp.concatenate([jnp.broadcast_to(mckv[None], (batch, N_META, KV_LORA)),
                             ckv[:rows_p].reshape(batch, seq, KV_LORA)], axis=1)[None]
    kpe_p = jnp.concatenate([jnp.broadcast_to(mkpe[None], (batch, N_META, MLA_ROPE)),
                             kpe[:rows_p].reshape(batch, seq, MLA_ROPE)], axis=1)[None]
    return (y_prompt, y_sample, lat_p, kpe_p, st_p[None],
            ckv[rows_p:].reshape(n_seq, n_tok, KV_LORA)[None], kpe[rows_p:].reshape(n_seq, n_tok, MLA_ROPE)[None],
            st_s[None])
```

```python
import functools

import jax
import jax.numpy as jnp
from jax import lax
from jax.experimental import pallas as pl
from jax.experimental.pallas import tpu as pltpu
from jax.experimental.pallas import tpu_sc as plsc

F32 = jnp.float32
BF16 = jnp.bfloat16

D_MODEL = 1024
N_META = 16
RET_HEADS = 4
RET_DK = 64
RET_DV = 128
RET_CHUNK = 128
RET_THETA = 10000.0
MLA_HEADS = 8
MLA_NOPE = 64
MLA_ROPE = 32
MLA_QK = MLA_NOPE + MLA_ROPE
MLA_V = 64
Q_LORA = 384
KV_LORA = 256
MLA_THETA = 10000.0
MLA_SCALE = MLA_QK ** -0.5
LOG2E = 1.4426950408889634
N_EXPERTS = 32
TOP_K = 4
D_FF = 1024
SWIGLU_LIMIT = 7.0
SWIGLU_ALPHA = 1.702
EPS = 1e-6
NEG_BIG = -1e30

LANES = 128
HEAD_PAD = LANES
VMEM_LIMIT = 56 * 1024 * 1024
ROW_BLOCK = 512
ATTN_BLOCK = 1024
ATTN_HEADS_PER_STEP = 8
RET_CHUNKS_PER_STEP = 4
PAGES_PER_STEP = 64
PAGES_PER_CHAIN = 32
EXPERT_BLOCK = 512
SC_CORES, SC_SUBCORES = 2, 16
GATHER_BUFFER_BYTES = 128 * 1024

_OFF_RQ = 0
_OFF_RK = _OFF_RQ + RET_HEADS * LANES
_OFF_RV = _OFF_RK + RET_HEADS * LANES
_OFF_ZG = _OFF_RV + RET_HEADS * RET_DV
_OFF_CQ = _OFF_ZG + RET_HEADS * RET_DV
_OFF_CKV = _OFF_CQ + Q_LORA
_OFF_KPE = _OFF_CKV + KV_LORA
IN_PAD = _OFF_KPE + LANES


def _params(sem):
    return pltpu.CompilerParams(dimension_semantics=sem, vmem_limit_bytes=VMEM_LIMIT)


def _rms(x, g):
    return x * lax.rsqrt(jnp.mean(x * x, axis=-1, keepdims=True) + EPS) * g


def _dot(a, b):
    return jnp.dot(a, b, preferred_element_type=F32)


def _dot_nt(a, b):
    return lax.dot_general(a, b, (((1,), (1,)), ((), ())), preferred_element_type=F32)


def _dot_tn(a, b):
    return lax.dot_general(a, b, (((0,), (0,)), ((), ())), preferred_element_type=F32)


def _front_kernel(xa_ref, xb_ref, gattn_ref, win_ref, cr_ref, sr_ref, cm_ref, sma_ref, smb_ref,
                  gqa_ref, wqb_ref, gq_ref, gkv_ref, wuk_ref, gk_ref, wuv_ref,
                  rq_ref, rk_ref, rv_ref, zg_ref, mq_ref, ckv_ref, kpe_ref, k_ref, v_ref, *, n_a):
    tm = xa_ref.shape[0]
    x = jnp.where(pl.program_id(0) < n_a, xa_ref[...], xb_ref[...])
    xn = _rms(x, gattn_ref[...])
    z = _dot(xn.astype(BF16), win_ref[...])

    lane = lax.broadcasted_iota(jnp.int32, (tm, LANES), 1)
    upper = (lane & (RET_DK // 2)) != 0
    cr, sr = cr_ref[...], sr_ref[...]

    def rope_ret(t):
        partner = jnp.where(upper, pltpu.roll(t, RET_DK // 2, 1), pltpu.roll(t, LANES - RET_DK // 2, 1))
        return t * cr + partner * sr

    for h in range(RET_HEADS):
        sl = slice(h * LANES, (h + 1) * LANES)
        rq_ref[:, sl] = rope_ret(z[:, _OFF_RQ + h * LANES:_OFF_RQ + (h + 1) * LANES])
        rk_ref[:, sl] = rope_ret(z[:, _OFF_RK + h * LANES:_OFF_RK + (h + 1) * LANES]) * (RET_DK ** -0.5)
    rv_ref[...] = z[:, _OFF_RV:_OFF_ZG].astype(BF16)
    zg_ref[...] = z[:, _OFF_ZG:_OFF_CQ]

    cm, sma, smb = cm_ref[...], sma_ref[...], smb_ref[...]

    def rope_mla(t):
        half = MLA_ROPE // 2
        return t * cm + pltpu.roll(t, half, 1) * sma + pltpu.roll(t, LANES - half, 1) * smb

    def head_norm(t, g):
        ms = jnp.sum(t * t, axis=-1, keepdims=True) * (1.0 / MLA_QK)
        return t * lax.rsqrt(ms + EPS) * g

    cq = _rms(z[:, _OFF_CQ:_OFF_CKV], gqa_ref[...])
    q = _dot(cq.astype(BF16), wqb_ref[...])
    gq = gq_ref[...]
    for h in range(MLA_HEADS):
        sl = slice(h * HEAD_PAD, (h + 1) * HEAD_PAD)
        mq_ref[:, sl] = (rope_mla(head_norm(q[:, sl], gq)) * (MLA_SCALE * LOG2E)).astype(BF16)

    ckv = _rms(z[:, _OFF_CKV:_OFF_KPE], gkv_ref[...])
    ckv_ref[...] = ckv
    kpe_slab = z[:, _OFF_KPE:IN_PAD]
    kpe_ref[...] = kpe_slab[:, :MLA_ROPE]
    ckv_b = ckv.astype(BF16)
    kn = _dot(ckv_b, wuk_ref[...])
    gk = gk_ref[...]
    for h in range(MLA_HEADS):
        sl = slice(h * HEAD_PAD, (h + 1) * HEAD_PAD)
        k_ref[:, sl] = rope_mla(head_norm(kn[:, sl] + kpe_slab, gk)).astype(BF16)
    v_ref[...] = _dot(ckv_b, wuv_ref[...]).astype(BF16)


def _front(xa, xb, tabs, tab_index, w, tm):
    n_a, n_b = xa.shape[0] // tm, (0 if xb is None else xb.shape[0] // tm)
    xb = xa if xb is None else xb
    rows = (n_a + n_b) * tm
    grid = (n_a + n_b,)
    row = lambda i: (i, 0)
    const = lambda i: (0, 0)
    tab = lambda i: (tab_index(i), 0)

    def full(a):
        return pl.BlockSpec(a.shape, const)

    in_specs = [pl.BlockSpec((tm, D_MODEL), lambda i: (jnp.minimum(i, n_a - 1), 0)),
                pl.BlockSpec((tm, D_MODEL), lambda i: (jnp.maximum(i - n_a, 0), 0)), full(w['gattn']), full(w['win'])]
    in_specs += [pl.BlockSpec((tm, LANES), tab)] * 5
    in_specs += [full(w[n]) for n in ('gqa', 'wqb', 'gq', 'gkv', 'wuk', 'gk', 'wuv')]
    widths = [(RET_HEADS * LANES, F32), (RET_HEADS * LANES, F32), (RET_HEADS * RET_DV, BF16),
              (RET_HEADS * RET_DV, F32), (MLA_HEADS * HEAD_PAD, BF16), (KV_LORA, F32), (MLA_ROPE, F32),
              (MLA_HEADS * HEAD_PAD, BF16), (MLA_HEADS * MLA_V, BF16)]
    out_shape = [jax.ShapeDtypeStruct((rows, n), dt) for n, dt in widths]
    out_specs = [pl.BlockSpec((tm, n), row) for n, _ in widths]
    return pl.pallas_call(
        functools.partial(_front_kernel, n_a=n_a), grid=grid, in_specs=in_specs, out_specs=out_specs,
        out_shape=out_shape, compiler_params=_params(("arbitrary",)), name="front",
    )(xa, xb, w['gattn'], w['win'], *tabs, w['gqa'], w['wqb'], w['gq'], w['gkv'], w['wuk'], w['gk'], w['wuv'])


def _ret_gate(o, zg, g):
    on = o * lax.rsqrt(jnp.mean(o * o, axis=-1, keepdims=True) + EPS) * g
    return (zg * jax.nn.sigmoid(zg)) * on


def _ret_prompt_kernel(q_ref, k_ref, v_ref, zg_ref, mk_ref, mv_ref, mcol_ref, dec_ref, rowd_ref, cold_ref,
                       cpow_ref, g_ref, r_ref, s_ref, s_scr):
    c = pl.program_id(1)

    @pl.when(c == 0)
    def _():
        for h in range(RET_HEADS):
            sl = slice(h * LANES, (h + 1) * LANES)
            kw = mk_ref[:, sl] * mcol_ref[:, h:h + 1]
            s_scr[h] = _dot_tn(kw.astype(BF16), mv_ref[:, sl])

    for h in range(RET_HEADS):
        sl = slice(h * LANES, (h + 1) * LANES)
        s0 = s_scr[h]
        for cc in range(q_ref.shape[0] // RET_CHUNK):
            rows = slice(cc * RET_CHUNK, (cc + 1) * RET_CHUNK)
            k = k_ref[rows, sl]
            v = v_ref[rows, sl]
            qb = q_ref[rows, sl].astype(BF16)
            scores = _dot_nt(qb, k.astype(BF16)) * dec_ref[h]
            inner = _dot(scores.astype(BF16), v)
            cross = _dot(qb, s0.astype(BF16)) * rowd_ref[:, h:h + 1]
            kw = k * cold_ref[:, h:h + 1]
            s0 = s0 * cpow_ref[:, h:h + 1] + _dot_tn(kw.astype(BF16), v)
            r_ref[rows, sl] = _ret_gate(inner + cross, zg_ref[rows, sl], g_ref[:, sl]).astype(BF16)
        s_scr[h] = s0

    @pl.when(c == pl.num_programs(1) - 1)
    def _():
        for h in range(RET_HEADS):
            s_ref[0, h] = s_scr[h, :RET_DK, :]


def _ret_prompt(rq, rk, rv, zg, mk, mv, tabs, g, batch, n_chunks):
    cs = RET_CHUNK
    cps = _largest_divisor(n_chunks, RET_CHUNKS_PER_STEP)
    steps = n_chunks // cps
    row = lambda b, c: (b * steps + c, 0)
    const2 = lambda b, c: (0, 0)
    w4 = RET_HEADS * LANES
    in_specs = [pl.BlockSpec((cps * cs, w4), row), pl.BlockSpec((cps * cs, w4), row),
                pl.BlockSpec((cps * cs, w4), row), pl.BlockSpec((cps * cs, w4), row),
                pl.BlockSpec((cs, w4), const2), pl.BlockSpec((cs, w4), const2),
                pl.BlockSpec((cs, RET_HEADS), const2),
                pl.BlockSpec((RET_HEADS, cs, cs), lambda b, c: (0, 0, 0)),
                pl.BlockSpec((cs, RET_HEADS), const2), pl.BlockSpec((cs, RET_HEADS), const2),
                pl.BlockSpec((LANES, RET_HEADS), const2), pl.BlockSpec((1, w4), const2)]
    out_shape = [jax.ShapeDtypeStruct((batch * n_chunks * cs, w4), BF16),
                 jax.ShapeDtypeStruct((batch, RET_HEADS, RET_DK, RET_DV), F32)]
    out_specs = [pl.BlockSpec((cps * cs, w4), row),
                 pl.BlockSpec((1, RET_HEADS, RET_DK, RET_DV), lambda b, c: (b, 0, 0, 0))]
    return pl.pallas_call(
        _ret_prompt_kernel, grid=(batch, steps), in_specs=in_specs, out_specs=out_specs, out_shape=out_shape,
        scratch_shapes=[pltpu.VMEM((RET_HEADS, LANES, RET_DV), F32)],
        compiler_params=_params(("arbitrary", "arbitrary")), name="ret_prompt",
    )(rq, rk, rv, zg, mk, mv, tabs['mcol'], tabs['dec'], tabs['rowd'], tabs['cold'], tabs['cpow'], g)


def _ret_sample_kernel(q_ref, k_ref, v_ref, zg_ref, s0_ref, dec_ref, rowd_ref, cold_ref, cpow_ref, g_ref,
                       r_ref, s_ref, *, n_seq, n_tok):
    rows = n_seq * n_tok
    ri = lax.broadcasted_iota(jnp.int32, (rows, 1), 0)
    for h in range(RET_HEADS):
        sl = slice(h * LANES, (h + 1) * LANES)
        q = q_ref[:, sl]
        k = k_ref[:, sl]
        v = v_ref[:, sl]
        qb = q.astype(BF16)
        scores = _dot_nt(qb, k.astype(BF16)) * dec_ref[h]
        inner = _dot(scores.astype(BF16), v)
        kw = k * cold_ref[:, h:h + 1]
        cross = jnp.zeros((rows, RET_DV), F32)
        for s in range(n_seq):
            s0 = s0_ref[s, h]
            mine = (ri >= s * n_tok) & (ri < (s + 1) * n_tok)
            cross = cross + jnp.where(mine, _dot(qb[:, :RET_DK], s0.astype(BF16)), 0.0)
            upd = _dot_tn(jnp.where(mine, kw, 0.0).astype(BF16), v)
            s_ref[s, h] = s0 * cpow_ref[:RET_DK, h:h + 1] + upd[:RET_DK]
        cross = cross * rowd_ref[:, h:h + 1]
        r_ref[:, sl] = _ret_gate(inner + cross, zg_ref[:, sl], g_ref[:, sl]).astype(BF16)


def _ret_sample(rq, rk, rv, zg, state, tabs, g, n_tok):
    n_seq_total = state.shape[0]
    n_seq = LANES // n_tok
    rows = n_seq * n_tok
    w4 = RET_HEADS * LANES
    row = lambda i: (i, 0)
    const = lambda i: (0, 0)
    in_specs = [pl.BlockSpec((rows, w4), row)] * 4
    in_specs += [pl.BlockSpec((n_seq, RET_HEADS, RET_DK, RET_DV), lambda i: (i, 0, 0, 0)),
                 pl.BlockSpec((RET_HEADS, rows, rows), lambda i: (0, 0, 0)),
                 pl.BlockSpec((rows, RET_HEADS), const), pl.BlockSpec((rows, RET_HEADS), const),
                 pl.BlockSpec((LANES, RET_HEADS), const), pl.BlockSpec((1, w4), const)]
    out_shape = [jax.ShapeDtypeStruct((n_seq_total * n_tok, w4), BF16),
                 jax.ShapeDtypeStruct(state.shape, F32)]
    out_specs = [pl.BlockSpec((rows, w4), row),
                 pl.BlockSpec((n_seq, RET_HEADS, RET_DK, RET_DV), lambda i: (i, 0, 0, 0))]
    return pl.pallas_call(
        functools.partial(_ret_sample_kernel, n_seq=n_seq, n_tok=n_tok),
        grid=(n_seq_total // n_seq,), in_specs=in_specs, out_specs=out_specs, out_shape=out_shape,
        compiler_params=_params(("arbitrary",)), name="ret_sample",
    )(rq, rk, rv, zg, state, tabs['dec'], tabs['rowd'], tabs['cold'], tabs['cpow'], g)


def _attn_prompt_kernel(qi_ref, ki_ref, q_ref, k_ref, v_ref, km_ref, vm_ref, o_ref, m_scr, l_scr, acc_scr, *,
                        sub_tiles):
    qi = qi_ref[pl.program_id(2)]
    ki = ki_ref[pl.program_id(2)]
    tm, tk = q_ref.shape[0], k_ref.shape[0]
    heads = q_ref.shape[1] // HEAD_PAD

    def pair_lanes(hh):
        return slice((hh // 2) * 2 * MLA_V, (hh // 2 + 1) * 2 * MLA_V)

    @pl.when(ki == 0)
    def _():
        lane = lax.broadcasted_iota(jnp.int32, (tm, km_ref.shape[0]), 1)
        for hh in range(heads):
            sl = slice(hh * HEAD_PAD, (hh + 1) * HEAD_PAD)
            s = jnp.where(lane < N_META, _dot_nt(q_ref[:, sl], km_ref[:, sl]), NEG_BIG)
            m = jnp.max(s, axis=-1, keepdims=True)
            p = jnp.exp2(s - m)
            m_scr[hh] = jnp.broadcast_to(m, (tm, LANES))
            l_scr[hh] = jnp.broadcast_to(jnp.sum(p, axis=-1, keepdims=True), (tm, LANES))
            acc_scr[hh] = _dot(p.astype(BF16), vm_ref[:, pair_lanes(hh)])

    def tile(r0, nr, c0, nc, masked):
        rows = pl.ds(r0, nr)
        if masked:
            keep = (lax.broadcasted_iota(jnp.int32, (nr, nc), 1) <= lax.broadcasted_iota(jnp.int32, (nr, nc), 0))
        for hh in range(heads):
            sl = slice(hh * HEAD_PAD, (hh + 1) * HEAD_PAD)
            s = _dot_nt(q_ref[rows, sl], k_ref[pl.ds(c0, nc), sl])
            if masked:
                s = jnp.where(keep, s, NEG_BIG)
            m_old = m_scr[hh, rows, :]
            m_new = jnp.maximum(m_old, jnp.max(s, axis=-1, keepdims=True))
            alpha = jnp.exp2(m_old - m_new)
            p = jnp.exp2(s - jnp.tile(m_new, (1, nc // LANES)))
            m_scr[hh, rows, :] = m_new
            l_scr[hh, rows, :] = alpha * l_scr[hh, rows, :] + jnp.sum(p, axis=-1, keepdims=True)
            acc_scr[hh, rows, :] = alpha * acc_scr[hh, rows, :] + _dot(p.astype(BF16),
                                                                       v_ref[pl.ds(c0, nc), pair_lanes(hh)])

    @pl.when(ki < qi)
    def _():
        tile(0, tm, 0, tk, False)

    @pl.when(ki == qi)
    def _():
        ns = sub_tiles
        st = tm // ns
        for a in range(ns):
            if a > 0:
                tile(a * st, st, 0, a * st, False)
            tile(a * st, st, a * st, st, True)
        lane = lax.broadcasted_iota(jnp.int32, (tm, 2 * MLA_V), 1)
        for pr in range(heads // 2):
            even = acc_scr[2 * pr] / l_scr[2 * pr]
            odd = acc_scr[2 * pr + 1] / l_scr[2 * pr + 1]
            o_ref[:, pair_lanes(2 * pr)] = jnp.where(lane < MLA_V, even, odd).astype(BF16)


def _attn_prompt(mq, k, v, km, vm, batch, seq, blk):
    nb = seq // blk
    hps = ATTN_HEADS_PER_STEP
    pairs_qk = [(qi, ki) for qi in range(nb) for ki in range(qi + 1)]
    qi_tab = jnp.asarray([p[0] for p in pairs_qk], jnp.int32)
    ki_tab = jnp.asarray([p[1] for p in pairs_qk], jnp.int32)
    in_specs = [pl.BlockSpec((blk, hps * HEAD_PAD), lambda b, h, t, qt, kt: (b * nb + qt[t], h)),
                pl.BlockSpec((blk, hps * HEAD_PAD), lambda b, h, t, qt, kt: (b * nb + kt[t], h)),
                pl.BlockSpec((blk, hps * MLA_V), lambda b, h, t, qt, kt: (b * nb + kt[t], h)),
                pl.BlockSpec((km.shape[0], hps * HEAD_PAD), lambda b, h, t, qt, kt: (0, h)),
                pl.BlockSpec((vm.shape[0], hps * MLA_V), lambda b, h, t, qt, kt: (0, h))]
    grid_spec = pltpu.PrefetchScalarGridSpec(
        num_scalar_prefetch=2, grid=(batch, MLA_HEADS // hps, len(pairs_qk)), in_specs=in_specs,
        out_specs=pl.BlockSpec((blk, hps * MLA_V), lambda b, h, t, qt, kt: (b * nb + qt[t], h)),
        scratch_shapes=[pltpu.VMEM((hps, blk, LANES), F32), pltpu.VMEM((hps, blk, LANES), F32),
                        pltpu.VMEM((hps, blk, 2 * MLA_V), F32)])
    return pl.pallas_call(
        functools.partial(_attn_prompt_kernel, sub_tiles=2 if blk % (2 * LANES) == 0 else 1), grid_spec=grid_spec,
        out_shape=jax.ShapeDtypeStruct((batch * seq, MLA_HEADS * MLA_V), BF16),
        compiler_params=_params(("arbitrary",) * 3), name="attn_prompt",
    )(qi_tab, ki_tab, mq, k, v, km, vm)


def _absorb_kernel(mq_ref, wabs_ref, qt_ref, qf_ref):
    q = mq_ref[...]
    qf_ref[...] = q.astype(F32)
    for h in range(MLA_HEADS):
        qt_ref[:, h * KV_LORA:(h + 1) * KV_LORA] = _dot(q[:, h * HEAD_PAD:(h + 1) * HEAD_PAD], wabs_ref[h])


def _absorb(mq, wabs, row0_blocks, rows, tm):
    return pl.pallas_call(
        _absorb_kernel, grid=(rows // tm,),
        in_specs=[pl.BlockSpec((tm, MLA_HEADS * HEAD_PAD), lambda i: (row0_blocks + i, 0)),
                  pl.BlockSpec(wabs.shape, lambda i: (0, 0, 0))],
        out_specs=[pl.BlockSpec((tm, MLA_HEADS * KV_LORA), lambda i: (i, 0)),
                   pl.BlockSpec((tm, MLA_HEADS * HEAD_PAD), lambda i: (i, 0))],
        out_shape=[jax.ShapeDtypeStruct((rows, MLA_HEADS * KV_LORA), F32),
                   jax.ShapeDtypeStruct((rows, MLA_HEADS * HEAD_PAD), F32)],
        compiler_params=_params(("arbitrary",)), name="absorb",
    )(mq, wabs)


def _attn_sample_kernel(pt_ref, lat_hbm, kpe_hbm, cos_ref, sin_ref, qt_ref, qf_ref, cn_ref, kn_ref, cosn_ref,
                        sinn_ref, lw_ref, wuv_ref, gpe_ref, o_ref, l_scr, qpe_scr, m_scr, d_scr, acc_scr, kpad_scr,
                        cpad_scr, lat_buf, kpe_buf, sem, *, n_pages, sub, n_tok, page):
    s_id = pl.program_id(0)
    j = pl.program_id(1)
    n_steps = pl.num_programs(1)
    step = s_id * n_steps + j
    last = pl.num_programs(0) * n_steps - 1
    slot = lax.rem(step, 2)
    nq = MLA_HEADS * n_tok
    n_up = MLA_HEADS * MLA_NOPE

    def page_copies(pid, sl, p):
        return (pltpu.make_async_copy(lat_hbm.at[0, pid], lat_buf.at[sl, p], sem.at[sl, 0]),
                pltpu.make_async_copy(kpe_hbm.at[0, pid], kpe_buf.at[sl, p], sem.at[sl, 1]))

    def start_pages(seq, st, sl, pages=range(n_pages)):
        for p in pages:
            lat_copy, kpe_copy = page_copies(pt_ref[seq, st * n_pages + p], sl, p)
            lat_copy.start(priority=p % 2)
            kpe_copy.start(priority=(p + 1) % 2)

    def wait_pages(sl):
        for p in range(n_pages):
            for c in page_copies(0, sl, p):
                c.wait()

    @pl.when(step == 0)
    def _():
        start_pages(0, 0, 0)

    wait_pages(slot)
    wrap = j == n_steps - 1
    nxt_seq = jnp.where(wrap, jnp.where(step == last, 0, s_id + 1), s_id)
    nxt_j = jnp.where(wrap, 0, j + 1)
    start_pages(nxt_seq, nxt_j, 1 - slot)
    lat_refs = [lat_buf.at[slot, p] for p in range(n_pages)]
    kpe_refs = [kpe_buf.at[slot, p] for p in range(n_pages)]

    @pl.when(j == 0)
    def _():
        l_scr[:n_up, :] = lw_ref[...]
        qt = qt_ref[...]
        qf = qf_ref[...]
        l_scr[n_up:, :] = jnp.concatenate(
            [qt[:, h * KV_LORA:(h + 1) * KV_LORA] for h in range(MLA_HEADS)], axis=0).astype(BF16)
        lane = lax.broadcasted_iota(jnp.int32, (nq, HEAD_PAD), 1)
        qpe = jnp.concatenate([qf[:, h * HEAD_PAD:(h + 1) * HEAD_PAD] for h in range(MLA_HEADS)], axis=0)
        qpe_scr[...] = jnp.where(lane < MLA_ROPE, qpe, 0.0).astype(BF16)
        m_scr[...] = jnp.full(m_scr.shape, NEG_BIG, F32)
        d_scr[...] = jnp.zeros(d_scr.shape, F32)
        acc_scr[...] = jnp.zeros(acc_scr.shape, F32)
        cpad_scr[...] = jnp.zeros(cpad_scr.shape, F32)
        cpad_scr[:n_tok, :] = cn_ref[...]
        kpad_scr[...] = jnp.zeros(kpad_scr.shape, F32)
        kpad_scr[:n_tok, :MLA_ROPE] = kn_ref[...]

    def scores(cb, kpe_t, cos_t, sin_t):
        tk = cb.shape[0]
        big = _dot_nt(l_scr[...], cb)
        k_t = big[:n_up]
        ss = jnp.sum((k_t * k_t).reshape(MLA_HEADS, MLA_NOPE, tk), axis=1)
        ss = ss + jnp.sum(kpe_t * kpe_t, axis=0, keepdims=True)
        rs = lax.rsqrt(ss * (1.0 / MLA_QK) + EPS)
        kg = kpe_t * gpe_ref[...]
        half = MLA_ROPE // 2
        x1, x2 = kg[:half], kg[half:]
        rot = jnp.concatenate([x1 * cos_t - x2 * sin_t, x1 * sin_t + x2 * cos_t], axis=0)
        pe = _dot(qpe_scr[:, :MLA_ROPE], rot.astype(BF16))
        rs_rows = jnp.concatenate([jnp.broadcast_to(rs[h:h + 1, :], (n_tok, tk)) for h in range(MLA_HEADS)], axis=0)
        return (big[n_up:] + pe) * rs_rows

    def update(s_list, cb_list):
        m_old = m_scr[...]
        m_new = m_old
        for s in s_list:
            m_new = jnp.maximum(m_new, jnp.max(s, axis=-1, keepdims=True))
        alpha = jnp.exp2(m_old - m_new)
        d = alpha * d_scr[...]
        acc = alpha * acc_scr[...]
        for s, cb in zip(s_list, cb_list):
            p = jnp.exp2(s - m_new)
            d = d + jnp.sum(p, axis=-1, keepdims=True)
            acc = acc + _dot(p.astype(BF16), cb)
        m_scr[...] = m_new
        d_scr[...] = d
        acc_scr[...] = acc

    s_list, cb_list = [], []
    for g in range(n_pages // sub):
        pages = range(g * sub, (g + 1) * sub)
        cb = jnp.concatenate([lat_refs[b][...] for b in pages], axis=0).astype(BF16)
        kpe_t = jnp.concatenate([kpe_refs[b][...] for b in pages], axis=1)
        cols = slice(g * sub * page, (g + 1) * sub * page)
        cos_t, sin_t = cos_ref[:, cols], sin_ref[:, cols]
        if g == n_pages // sub - 1:
            cb = jnp.concatenate([cb, cpad_scr[...].astype(BF16)], axis=0)
            kpe_t = jnp.concatenate([kpe_t, kpad_scr[...].T[:MLA_ROPE]], axis=1)
            cos_t = jnp.concatenate([cos_t, cosn_ref[...]], axis=1)
            sin_t = jnp.concatenate([sin_t, sinn_ref[...]], axis=1)
        s = scores(cb, kpe_t, cos_t, sin_t)
        if g == n_pages // sub - 1:
            tk_all = s.shape[1]
            rowi = lax.broadcasted_iota(jnp.int32, (nq, tk_all), 0)
            coli = lax.broadcasted_iota(jnp.int32, (nq, tk_all), 1) - (tk_all - page)
            keep = (coli < 0) | ((coli <= (rowi & (n_tok - 1))) & (j == pl.num_programs(1) - 1))
            s = jnp.where(keep, s, NEG_BIG)
        s_list.append(s)
        cb_list.append(cb)
    update(s_list, cb_list)

    @pl.when(j == pl.num_programs(1) - 1)
    def _():
        ctx = (acc_scr[...] / d_scr[...]).astype(BF16)
        out = jnp.zeros(o_ref.shape, F32)
        for h in range(MLA_HEADS):
            out = out + _dot(ctx, wuv_ref[h])[h * n_tok:(h + 1) * n_tok]
        o_ref[...] = out

    @pl.when(step == last)
    def _():
        wait_pages(1 - slot)


def _attn_sample(page_table, cache_latent, cache_krope_t, tabs, qt, qf, ckv, kpe, w, n_tok, n_pages_step, sub):
    n_seq, n_pages = page_table.shape
    page = cache_latent.shape[2]
    n_steps = n_pages // n_pages_step
    tk = n_pages_step * page
    nq = MLA_HEADS * n_tok

    seq_row = lambda s, j, pt: (s, 0)
    const2 = lambda s, j, pt: (0, 0)
    half = MLA_ROPE // 2
    in_specs = [pl.BlockSpec(memory_space=pl.ANY), pl.BlockSpec(memory_space=pl.ANY)]
    in_specs += [pl.BlockSpec((half, tk), lambda s, j, pt: (0, j)), pl.BlockSpec((half, tk), lambda s, j, pt: (0, j)),
                 pl.BlockSpec((n_tok, MLA_HEADS * KV_LORA), seq_row),
                 pl.BlockSpec((n_tok, MLA_HEADS * HEAD_PAD), seq_row),
                 pl.BlockSpec((n_tok, KV_LORA), seq_row), pl.BlockSpec((n_tok, MLA_ROPE), seq_row),
                 pl.BlockSpec((half, page), const2), pl.BlockSpec((half, page), const2),
                 pl.BlockSpec(w['uk_t'].shape, const2),
                 pl.BlockSpec(w['uv_blk'].shape, lambda s, j, pt: (0, 0, 0)),
                 pl.BlockSpec((MLA_ROPE, 1), const2)]
    grid_spec = pltpu.PrefetchScalarGridSpec(
        num_scalar_prefetch=1, grid=(n_seq, n_steps), in_specs=in_specs,
        out_specs=pl.BlockSpec((n_tok, MLA_HEADS * MLA_V), seq_row),
        scratch_shapes=[pltpu.VMEM((MLA_HEADS * MLA_NOPE + nq, KV_LORA), BF16),
                        pltpu.VMEM((nq, HEAD_PAD), BF16),
                        pltpu.VMEM((nq, 1), F32), pltpu.VMEM((nq, 1), F32), pltpu.VMEM((nq, KV_LORA), F32),
                        pltpu.VMEM((page, LANES), F32), pltpu.VMEM((page, KV_LORA), F32),
                        pltpu.VMEM((2, n_pages_step, page, KV_LORA), F32),
                        pltpu.VMEM((2, n_pages_step, MLA_ROPE, page), F32),
                        pltpu.SemaphoreType.DMA((2, 2))])
    return pl.pallas_call(
        functools.partial(_attn_sample_kernel, n_pages=n_pages_step, sub=sub, n_tok=n_tok, page=page),
        grid_spec=grid_spec, out_shape=jax.ShapeDtypeStruct((n_seq * n_tok, MLA_HEADS * MLA_V), F32),
        compiler_params=_params(("arbitrary", "arbitrary")), name="attn_sample",
    )(page_table, cache_latent, cache_krope_t,
      tabs['cos_t'], tabs['sin_t'], qt, qf, ckv, kpe, tabs['cosn_t'], tabs['sinn_t'],
      w['uk_t'], w['uv_blk'], w['gpe'])


def _post_kernel(ra_ref, rb_ref, ma_ref, mb_ref, xa_ref, xb_ref, wo_ref, g_ref, wrh_ref, wrl_ref, br_ref,
                 h_ref, xn_ref, ti_ref, tg_ref, *, n_a):
    first = pl.program_id(0) < n_a
    r = jnp.where(first, ra_ref[...], rb_ref[...])
    m = jnp.where(first, ma_ref[...], mb_ref[...].astype(BF16))
    half = wo_ref.shape[0] // 2
    mix = _dot(r, wo_ref[:half, :]) + _dot(m, wo_ref[half:, :])
    h = jnp.where(first, xa_ref[...], xb_ref[...]) + mix
    h_ref[...] = h
    xn = _rms(h, g_ref[...])
    hi = xn.astype(BF16)
    bits = lax.bitcast_convert_type(hi.astype(F32), jnp.uint32)
    xn_ref[...] = (bits[:, :D_MODEL // 2] >> 16) | bits[:, D_MODEL // 2:]
    lo = (xn - hi.astype(F32)).astype(BF16)
    wrh = wrh_ref[...]
    work = _dot(hi, wrh) + _dot(hi, wrl_ref[...]) + _dot(lo, wrh) + br_ref[...]
    lane = lax.broadcasted_iota(jnp.int32, work.shape, 1).astype(F32)
    idx = jnp.zeros(work.shape, F32)
    val = jnp.zeros(work.shape, F32)
    vmax = None
    denom = None
    for k in range(TOP_K):
        vk = jnp.max(work, axis=-1, keepdims=True)
        ik = jnp.min(jnp.where(work == vk, lane, float(LANES)), axis=-1, keepdims=True)
        work = jnp.where(lane == ik, -jnp.inf, work)
        if k == 0:
            vmax = vk
        ek = jnp.exp(vk - vmax)
        denom = ek if k == 0 else denom + ek
        idx = jnp.where(lane == float(k), ik, idx)
        val = jnp.where(lane == float(k), ek, val)
    ti_ref[...] = idx.astype(jnp.int32)
    tg_ref[...] = val / denom


def _post(r_pair, m_pair, x_pair, w, tm):
    n_a, n_b = x_pair[0].shape[0] // tm, x_pair[1].shape[0] // tm
    rows = (n_a + n_b) * tm
    row = lambda i: (i, 0)
    const = lambda i: (0, 0)
    in_specs = []
    for a, b in (r_pair, m_pair, x_pair):
        in_specs += [pl.BlockSpec((tm, a.shape[1]), lambda i: (jnp.minimum(i, n_a - 1), 0)),
                     pl.BlockSpec((tm, b.shape[1]), lambda i: (jnp.maximum(i - n_a, 0), 0))]
    in_specs += [pl.BlockSpec(w[n].shape, const) for n in ('wo', 'gffn', 'wr_hi', 'wr_lo', 'br')]
    out_shape = [jax.ShapeDtypeStruct((rows, D_MODEL), F32), jax.ShapeDtypeStruct((rows, D_MODEL // 2), jnp.uint32),
                 jax.ShapeDtypeStruct((rows, LANES), jnp.int32), jax.ShapeDtypeStruct((rows, LANES), F32)]
    out_specs = [pl.BlockSpec((tm, D_MODEL), row), pl.BlockSpec((tm, D_MODEL // 2), row),
                 pl.BlockSpec((tm, LANES), row), pl.BlockSpec((tm, LANES), row)]
    return pl.pallas_call(
        functools.partial(_post_kernel, n_a=n_a), grid=(rows // tm,), in_specs=in_specs, out_specs=out_specs,
        out_shape=out_shape, compiler_params=_params(("arbitrary",)), name="post",
    )(*r_pair, *m_pair, *x_pair, w['wo'], w['gffn'], w['wr_hi'], w['wr_lo'], w['br'])


def _moe_kernel(be_ref, nu_ref, xs_ref, wgu_ref, bgu_ref, wd_ref, bd_ref, o_ref, wgu_b, wd_b):
    i = pl.program_id(0)
    e = be_ref[i]
    e_prev = be_ref[jnp.maximum(i - 1, 0)]

    @pl.when((i == 0) | (e != e_prev))
    def _():
        wgu_b[...] = wgu_ref[...].astype(BF16)
        wd_b[...] = wd_ref[...].astype(BF16)

    @pl.when(i < nu_ref[0])
    def _():
        packed = xs_ref[...]
        x_lo = lax.bitcast_convert_type(packed << 16, F32).astype(BF16)
        x_hi = lax.bitcast_convert_type(packed & jnp.uint32(0xFFFF0000), F32).astype(BF16)
        half = D_MODEL // 2
        hgu = _dot(x_lo, wgu_b[:half, :]) + _dot(x_hi, wgu_b[half:, :]) + bgu_ref[...]
        g = jnp.minimum(hgu[:, :D_FF], SWIGLU_LIMIT)
        u = jnp.clip(hgu[:, D_FF:], -SWIGLU_LIMIT, SWIGLU_LIMIT)
        hid = (u + 1.0) * (g * jax.nn.sigmoid(SWIGLU_ALPHA * g))
        o_ref[...] = _dot(hid.astype(BF16), wd_b[...]) + bd_ref[...]

    @pl.when(i >= nu_ref[0])
    def _():
        o_ref[...] = jnp.zeros(o_ref.shape, F32)


def _moe_blocks(block_e, n_used, xs, w_gate_up, b_gate_up, w_down, b_down, blk):
    n_blocks = block_e.shape[0]
    row = lambda i, be, nu: (i, 0)
    in_specs = [pl.BlockSpec((blk, D_MODEL // 2), row),
                pl.BlockSpec((None, D_MODEL, 2 * D_FF), lambda i, be, nu: (be[i], 0, 0)),
                pl.BlockSpec((None, 1, 2 * D_FF), lambda i, be, nu: (be[i], 0, 0)),
                pl.BlockSpec((None, D_FF, D_MODEL), lambda i, be, nu: (be[i], 0, 0)),
                pl.BlockSpec((None, 1, D_MODEL), lambda i, be, nu: (be[i], 0, 0))]
    grid_spec = pltpu.PrefetchScalarGridSpec(
        num_scalar_prefetch=2, grid=(n_blocks,), in_specs=in_specs,
        out_specs=pl.BlockSpec((blk, D_MODEL), row),
        scratch_shapes=[pltpu.VMEM((D_MODEL, 2 * D_FF), BF16), pltpu.VMEM((D_FF, D_MODEL), BF16)])
    return pl.pallas_call(
        _moe_kernel, grid_spec=grid_spec, out_shape=jax.ShapeDtypeStruct((n_blocks * blk, D_MODEL), F32),
        compiler_params=_params(("arbitrary",)), name="moe",
    )(block_e, n_used, xs, w_gate_up, b_gate_up[:, None, :], w_down, b_down[:, None, :])


def _combine_kernel(h_ref, g_ref, y_ref, o_ref):
    g = g_ref[...]
    acc = y_ref[0] * g[:, 0:1]
    for k in range(1, TOP_K):
        acc = acc + y_ref[k] * g[:, k:k + 1]
    o_ref[...] = h_ref[...] + acc


def _combine(h, gates, y4, block0, rows, tm):
    row = lambda i: (block0 + i, 0)
    return pl.pallas_call(
        _combine_kernel, grid=(rows // tm,),
        in_specs=[pl.BlockSpec((tm, D_MODEL), row), pl.BlockSpec((tm, LANES), row),
                  pl.BlockSpec((TOP_K, tm, D_MODEL), lambda i: (0, block0 + i, 0))],
        out_specs=pl.BlockSpec((tm, D_MODEL), lambda i: (i, 0)),
        out_shape=jax.ShapeDtypeStruct((rows, D_MODEL), F32),
        compiler_params=_params(("arbitrary",)), name="combine",
    )(h, gates, y4)


def _gather_rows(table, idx):
    n, d = idx.shape[0], table.shape[1]
    window = min(LANES, GATHER_BUFFER_BYTES // (d * table.dtype.itemsize))
    n_win = n // window
    assert n % (window * SC_CORES * SC_SUBCORES) == 0
    idx = jnp.zeros((n_win, LANES), jnp.int32).at[:, :window].set(idx.reshape(n_win, window))
    mesh = plsc.VectorSubcoreMesh(core_axis_name="core", subcore_axis_name="subcore",
                                  num_cores=SC_CORES, num_subcores=SC_SUBCORES)

    @functools.partial(pl.kernel, out_type=jax.ShapeDtypeStruct((n, d), table.dtype), mesh=mesh)
    def gather(x_hbm, i_hbm, o_hbm):
        def body(i_vmem, o_vmem):
            pltpu.sync_copy(x_hbm.at[i_vmem.at[0, pl.ds(0, window)]], o_vmem)

        pltpu.emit_pipeline(
            body, grid=(n_win,),
            in_specs=[pl.BlockSpec((1, LANES), index_map=lambda i: (i, 0))],
            out_specs=[pl.BlockSpec((window, d), index_map=lambda i: (i, 0))],
            core_axis_name=("core", "subcore"), dimension_semantics=(pltpu.PARALLEL,),
        )(i_hbm, o_hbm)

    return gather(table, idx)


def _route(top_i, blk):
    t = top_i.shape[0]
    a = t * TOP_K
    flat_e = top_i.reshape(a)
    assert N_EXPERTS * a < 2 ** 31
    order = jnp.sort(flat_e.astype(jnp.int32) * a + jnp.arange(a, dtype=jnp.int32)) % a
    tok_sorted = (order // TOP_K).astype(jnp.int32)
    experts = jnp.arange(N_EXPERTS, dtype=jnp.int32)
    hit = (top_i[:, :, None] == experts[None, None, :]).astype(jnp.int32)
    per_tok = hit.sum(axis=1)
    before = jnp.cumsum(per_tok, axis=0) - per_tok
    counts = per_tok.sum(axis=0)
    padded = (counts + blk - 1) // blk * blk
    start = jnp.cumsum(counts) - counts
    pend = jnp.cumsum(padded)
    pstart = pend - padded
    dest = ((before + pstart[None, :])[:, None, :] * hit).sum(axis=2)
    n_blocks = -(-a // blk) + N_EXPERTS
    block_e = jnp.minimum((pend[None, :] <= (jnp.arange(n_blocks, dtype=jnp.int32) * blk)[:, None]).sum(axis=1),
                          N_EXPERTS - 1).astype(jnp.int32)
    slot = jnp.arange(n_blocks * blk, dtype=jnp.int32)
    slot_e = jnp.repeat(block_e, blk)
    r = slot - pstart[slot_e]
    slot_tok = jnp.where(r < counts[slot_e], tok_sorted[jnp.minimum(start[slot_e] + r, a - 1)], slot % t)
    n_used = (pend[-1:] // blk).astype(jnp.int32)
    return slot_tok, dest.astype(jnp.int32), block_e, n_used


def _rope_tables(pos):
    pos = pos.astype(F32)[:, None]
    n = pos.shape[0]
    hr = RET_DK // 2
    ang = pos * (RET_THETA ** (-jnp.arange(hr, dtype=F32) / hr))[None, :]
    cos, sin = jnp.cos(ang), jnp.sin(ang)
    zr = jnp.zeros((n, LANES - RET_DK), F32)
    cr = jnp.concatenate([cos, cos, zr], axis=1)
    sr = jnp.concatenate([-sin, sin, zr], axis=1)
    hm = MLA_ROPE // 2
    ang = pos * (MLA_THETA ** (-jnp.arange(hm, dtype=F32) / hm))[None, :]
    cos, sin = jnp.cos(ang), jnp.sin(ang)
    cm = jnp.concatenate([cos, cos, jnp.ones((n, LANES - MLA_ROPE), F32)], axis=1)
    sma = jnp.concatenate([jnp.zeros((n, hm), F32), sin, jnp.zeros((n, LANES - MLA_ROPE), F32)], axis=1)
    smb = jnp.concatenate([-sin, jnp.zeros((n, LANES - hm), F32)], axis=1)
    return [cr, sr, cm, sma, smb]


def _rope_tables_t(pos):
    hm = MLA_ROPE // 2
    ang = pos.astype(F32)[:, None] * (MLA_THETA ** (-jnp.arange(hm, dtype=F32) / hm))[None, :]
    return jnp.cos(ang).T, jnp.sin(ang).T


def _decay_tables(log_gamma, c, n_rep):
    idx = jnp.arange(c, dtype=F32)
    diff = idx[:, None] - idx[None, :]
    decay = jnp.where(diff >= 0, jnp.exp(jnp.maximum(diff, 0.0)[None] * log_gamma[:, None, None]), 0.0)
    if n_rep > 1:
        eye = jnp.eye(n_rep, dtype=F32)
        decay = (eye[None, :, None, :, None] * decay[:, None, :, None, :]).reshape(RET_HEADS, n_rep * c, n_rep * c)
    rowd = jnp.tile(jnp.exp((idx + 1.0)[:, None] * log_gamma[None, :]), (n_rep, 1))
    cold = jnp.tile(jnp.exp((c - 1.0 - idx)[:, None] * log_gamma[None, :]), (n_rep, 1))
    cpow = jnp.broadcast_to(jnp.exp(c * log_gamma)[None, :], (LANES, RET_HEADS))
    return {'dec': decay, 'rowd': rowd, 'cold': cold, 'cpow': cpow}


def _pad_heads(wm, n_heads, width, offset):
    k = wm.shape[0]
    wm = wm.reshape(k, n_heads, width)
    out = jnp.zeros((k, n_heads, LANES), wm.dtype).at[:, :, offset:offset + width].set(wm)
    return out.reshape(k, n_heads * LANES)


def _prep_weights(norm_attn_g, w_in, ret_out_g, q_a_norm_g, w_q_b, kv_a_norm_g, w_uk, w_uv,
                  qk_norm_q_g, qk_norm_k_g, w_out, norm_ffn_g, w_router, b_router):
    splits = [RET_HEADS * RET_DK, RET_HEADS * RET_DK, RET_HEADS * RET_DV, RET_HEADS * RET_DV, Q_LORA, KV_LORA,
              MLA_ROPE]
    offs = [0]
    for s in splits:
        offs.append(offs[-1] + s)
    part = [w_in[:, offs[i]:offs[i + 1]] for i in range(len(splits))]
    kpe_cols = jnp.zeros((D_MODEL, LANES), F32).at[:, :MLA_ROPE].set(part[6])
    win = jnp.concatenate([_pad_heads(part[0], RET_HEADS, RET_DK, 0), _pad_heads(part[1], RET_HEADS, RET_DK, 0),
                           part[2], part[3], part[4], part[5], kpe_cols], axis=1).astype(BF16)

    def mla_cols(wm):
        k = wm.shape[0]
        wm = wm.reshape(k, MLA_HEADS, MLA_QK)
        wm = jnp.concatenate([wm[:, :, MLA_NOPE:], wm[:, :, :MLA_NOPE]], axis=2)
        return _pad_heads(wm.reshape(k, MLA_HEADS * MLA_QK), MLA_HEADS, MLA_QK, 0)

    def mla_gain(g):
        g = jnp.concatenate([g[MLA_NOPE:], g[:MLA_NOPE], jnp.zeros((LANES - MLA_QK,), F32)])
        return g[None, :]

    gk_nope = qk_norm_k_g[:MLA_NOPE]
    uk_heads = w_uk.reshape(KV_LORA, MLA_HEADS, MLA_NOPE)
    wabs = jnp.zeros((MLA_HEADS, HEAD_PAD, KV_LORA), F32).at[:, MLA_ROPE:MLA_ROPE + MLA_NOPE, :].set(
        (uk_heads * gk_nope[None, None, :]).transpose(1, 2, 0))
    head_of_col = jnp.arange(MLA_HEADS * MLA_V) // MLA_V
    uv_blk = jnp.where(head_of_col[None, None, :] == jnp.arange(MLA_HEADS)[:, None, None], w_uv[None], 0.0)
    wr = jnp.zeros((D_MODEL, LANES), F32).at[:, :N_EXPERTS].set(w_router)
    wr_hi = wr.astype(BF16)
    return {
        'gattn': norm_attn_g[None, :], 'win': win, 'gqa': q_a_norm_g[None, :], 'wqb': mla_cols(w_q_b).astype(BF16),
        'gq': mla_gain(qk_norm_q_g), 'gkv': kv_a_norm_g[None, :],
        'wuk': _pad_heads(w_uk, MLA_HEADS, MLA_NOPE, MLA_ROPE).astype(BF16), 'gk': mla_gain(qk_norm_k_g),
        'wuv': w_uv.astype(BF16), 'gret': ret_out_g[None, :],
        'wabs': wabs.astype(BF16), 'uk_t': w_uk.T.astype(BF16), 'uv_blk': uv_blk.astype(BF16),
        'gpe': qk_norm_k_g[MLA_NOPE:, None],
        'wo': w_out.astype(BF16), 'gffn': norm_ffn_g[None, :], 'wr_hi': wr_hi,
        'wr_lo': (wr - wr_hi.astype(F32)).astype(BF16),
        'br': jnp.full((1, LANES), NEG_BIG, F32).at[0, :N_EXPERTS].set(b_router),
    }


def _pad_rows(a, rows):
    return jnp.zeros((rows,) + a.shape[1:], a.dtype).at[:a.shape[0]].set(a)


def _largest_divisor(n, cap):
    d = min(n, cap)
    while n % d:
        d -= 1
    return d


def kernel(x_prompt, x_sample, cache_latent, cache_krope, state_retention, page_table, meta_tokens, norm_attn_g, w_in, ret_out_g, q_a_norm_g, w_q_b, kv_a_norm_g, w_uk, w_uv, qk_norm_q_g, qk_norm_k_g, w_out, norm_ffn_g, w_router, b_router, w_gate_up, b_gate_up, w_down, b_down):
    assert w_in.shape[0] == 1, "single-layer trunk"
    batch, seq, _ = x_prompt.shape
    n_seq, n_tok, _ = x_sample.shape
    n_pages, page = page_table.shape[1], cache_latent.shape[2]
    past = n_pages * page
    assert seq % RET_CHUNK == 0 and LANES % n_tok == 0 and (n_seq * n_tok) % LANES == 0 and page == LANES
    w = _prep_weights(norm_attn_g[0], w_in[0], ret_out_g[0], q_a_norm_g[0], w_q_b[0], kv_a_norm_g[0], w_uk[0],
                      w_uv[0], qk_norm_q_g[0], qk_norm_k_g[0], w_out[0], norm_ffn_g[0], w_router[0], b_router[0])
    log_gamma = jnp.log1p(-jnp.exp2(-5.0 - jnp.arange(RET_HEADS, dtype=F32)))

    rows_p = batch * seq
    rows_s = n_seq * n_tok
    tm = _largest_divisor(min(seq, rows_s), ROW_BLOCK)
    nb_seq = seq // tm
    x_pair = (x_prompt.reshape(rows_p, D_MODEL), x_sample.reshape(rows_s, D_MODEL))
    pos_rows = jnp.concatenate([N_META + jnp.arange(seq), jnp.tile(past + jnp.arange(n_tok), tm // n_tok)])
    tabs = _rope_tables(pos_rows)
    n_pb = rows_p // tm
    rq, rk, rv, zg, mq, ckv, kpe, k, v = _front(
        *x_pair, tabs, lambda i: jnp.where(i < n_pb, i % nb_seq, nb_seq), w, tm)
    _, mrk, mrv, _, _, mckv, mkpe, mk, mv = _front(
        meta_tokens, None, _rope_tables(jnp.arange(N_META)), lambda i: i, w, N_META)

    dt_p = _decay_tables(log_gamma, RET_CHUNK, 1)
    dt_p['mcol'] = _pad_rows(jnp.exp((N_META - 1.0 - jnp.arange(N_META, dtype=F32))[:, None] * log_gamma[None, :]),
                             RET_CHUNK)
    gret = w['gret']
    r_p, st_p = _ret_prompt(rq, rk, rv, zg, _pad_rows(mrk, RET_CHUNK), _pad_rows(mrv, RET_CHUNK), dt_p, gret,
                            batch, seq // RET_CHUNK)
    dt_s = _decay_tables(log_gamma, n_tok, LANES // n_tok)
    r_s, st_s = _ret_sample(rq[rows_p:], rk[rows_p:], rv[rows_p:], zg[rows_p:], state_retention[0], dt_s, gret, n_tok)

    blk = _largest_divisor(seq, ATTN_BLOCK)
    m_p = _attn_prompt(mq, k, v, _pad_rows(mk, LANES), _pad_rows(mv, LANES), batch, seq, blk)
    tm_s = _largest_divisor(rows_s, ROW_BLOCK)
    qt, qf = _absorb(mq, w['wabs'], rows_p // tm_s, rows_s, tm_s)
    n_pages_step = _largest_divisor(n_pages, PAGES_PER_STEP)
    cos_t, sin_t = _rope_tables_t(jnp.arange(past))
    cosn_t, sinn_t = _rope_tables_t(past + jnp.arange(page))
    tabs_s = {'cos_t': cos_t, 'sin_t': sin_t, 'cosn_t': cosn_t, 'sinn_t': sinn_t}
    m_s = _attn_sample(page_table, cache_latent, jnp.swapaxes(cache_krope, 2, 3), tabs_s, qt, qf, ckv[rows_p:],
                       kpe[rows_p:], w, n_tok, n_pages_step, _largest_divisor(n_pages_step, PAGES_PER_CHAIN))

    h1, xn2, top_i, gates = _post((r_p, r_s), (m_p, m_s), x_pair, w, tm)

    slot_tok, dest, block_e, n_used = _route(top_i[:, :TOP_K], EXPERT_BLOCK)
    outs = _moe_blocks(block_e, n_used, _gather_rows(xn2, slot_tok), w_gate_up[0], b_gate_up[0], w_down[0],
                       b_down[0], EXPERT_BLOCK)
    y4 = _gather_rows(outs, dest.T.reshape(-1)).reshape(TOP_K, rows_p + rows_s, D_MODEL)
    y_prompt = _combine(h1, gates, y4, 0, rows_p, tm).reshape(batch, seq, D_MODEL)
    y_sample = _combine(h1, gates, y4, n_pb, rows_s, tm).reshape(n_seq, n_tok, D_MODEL)
    lat_p = jnp.concatenate([jnp.broadcast_to(mckv[None], (batch, N_META, KV_LORA)),
                             ckv[:rows_p].reshape(batch, seq, KV_LORA)], axis=1)[None]
    kpe_p = jnp.concatenate([jnp.broadcast_to(mkpe[None], (batch, N_META, MLA_ROPE)),
                             kpe[:rows_p].reshape(batch, seq, MLA_ROPE)], axis=1)[None]
    return (y_prompt, y_sample, lat_p, kpe_p, st_p[None],
            ckv[rows_p:].reshape(n_seq, n_tok, KV_LORA)[None], kpe[rows_p:].reshape(n_seq, n_tok, MLA_ROPE)[None],
            st_s[None])
```
